```python
import math
import jax
import jax.numpy as jnp
from jax import lax
import numpy as np

D_MODEL = 1024
BATCH = 16
SEQ = 256
DEPTH = 2
DEC_BATCH = 8
DEC_SEQ = 2048
PAST_LEN = 256

GRID_W = 64
ROPE_THETA = 10000.0
Q_BLOCK = 128
EPS = 1e-6
HEAD_DIM = 64
A_HEADS = 4
A_KV_HEADS = 2
A_GROUP = A_HEADS // A_KV_HEADS
B_HEADS = 4
MLA_NOPE = 64
MLA_ROPE = 32
MLA_V = 64
Q_LORA = 192
KV_LORA = 128
C_HEADS = 4
DIFF_D = 32
DIFF_V = 2 * DIFF_D
SUBLN_EPS = 1e-5
D_HEADS = 4
D_HEADDIM = 64
D_INNER = D_HEADS * D_HEADDIM
D_GROUPS = 2
D_STATE = 64
D_CONV = 3
CONV_DIM = D_INNER + 2 * D_GROUPS * D_STATE
SSD_CHUNK = 128
MIX_WIDTH = A_HEADS * HEAD_DIM + B_HEADS * MLA_V + C_HEADS * DIFF_V + D_INNER
IN_SPLITS = (A_HEADS * HEAD_DIM, A_KV_HEADS * HEAD_DIM, A_KV_HEADS * HEAD_DIM,
             Q_LORA, KV_LORA, MLA_ROPE,
             C_HEADS * 2 * DIFF_D, C_HEADS * 2 * DIFF_D, C_HEADS * DIFF_V,
             D_INNER, CONV_DIM, D_HEADS, D_HEADS)
IN_WIDTH = sum(IN_SPLITS)
IN_OFFSETS = tuple(int(v) for v in np.cumsum(IN_SPLITS)[:-1])
N_EXPERTS = 32
TOP_K = 4
D_FF = 1024
SWIGLU_ALPHA = 1.702
SWIGLU_LIMIT = 7.0
MOE_BLOCK = 128
N_MOD = 6
N_CACHE = 8

kernel_name = 'hybrid_prefix_diffusion_step'

F32 = jnp.float32


def rms_norm(x, g, eps=EPS):
    xf = x.astype(F32)
    y = xf * lax.rsqrt(jnp.mean(xf * xf, axis=-1, keepdims=True) + eps)
    return (y * g.astype(F32)).astype(x.dtype)


def axial_angles(n_tokens, rot_dim):
    rows = n_tokens // GRID_W
    row_ids = jnp.repeat(jnp.arange(rows), GRID_W).astype(F32)
    col_ids = jnp.tile(jnp.arange(GRID_W), rows).astype(F32)
    half = rot_dim // 2
    inv_freq = ROPE_THETA ** (-jnp.arange(0, half, 2, dtype=F32) / half)
    return row_ids[:, None] * inv_freq, col_ids[:, None] * inv_freq


def _rotate(x, ang):
    x1, x2 = jnp.split(x, 2, axis=-1)
    cos, sin = jnp.cos(ang), jnp.sin(ang)
    return jnp.concatenate([x1 * cos - x2 * sin, x1 * sin + x2 * cos], axis=-1)


def apply_axial_rope(x, angles):
    ang_row, ang_col = angles
    shape = (1, x.shape[1]) + (1,) * (x.ndim - 3) + (ang_row.shape[-1],)
    xf = x.astype(F32)
    half = x.shape[-1] // 2
    out = jnp.concatenate([_rotate(xf[..., :half], ang_row.reshape(shape)),
                           _rotate(xf[..., half:], ang_col.reshape(shape))], axis=-1)
    return out.astype(x.dtype)


def sweep_query_blocks(block_fn, q):
    bsz, n = q.shape[0], q.shape[1]
    nb = n // Q_BLOCK
    qb = jnp.moveaxis(q.reshape((bsz, nb, Q_BLOCK) + q.shape[2:]), 1, 0)
    out = jnp.moveaxis(lax.map(block_fn, qb), 0, 1)
    return out.reshape((bsz, n) + out.shape[3:])


def gqa_attention(q, k, v):
    scale = HEAD_DIM ** -0.5

    def block(qb):
        s = jnp.einsum('bqkgd,bskd->bkgqs', qb, k).astype(F32) * scale
        p = jax.nn.softmax(s, axis=-1).astype(v.dtype)
        return jnp.einsum('bkgqs,bskd->bqkgd', p, v)
    return sweep_query_blocks(block, q)


def mla_attention(q, k, v):
    scale = (MLA_NOPE + MLA_ROPE) ** -0.5

    def block(qb):
        s = jnp.einsum('bqhd,bshd->bhqs', qb, k).astype(F32) * scale
        p = jax.nn.softmax(s, axis=-1).astype(v.dtype)
        return jnp.einsum('bhqs,bshe->bqhe', p, v)
    return sweep_query_blocks(block, q)


def diff_attention(q, k, v, lam):
    scale = DIFF_D ** -0.5

    def block(qb):
        s = jnp.einsum('bqhtd,bshtd->bthqs', qb, k).astype(F32) * scale
        p = jax.nn.softmax(s, axis=-1)
        pd = (p[:, 0] - lam * p[:, 1]).astype(v.dtype)
        return jnp.einsum('bhqs,bshe->bqhe', pd, v)
    return sweep_query_blocks(block, q)


def centred_depthwise_conv(x, w, b):
    pad = D_CONV // 2
    y = lax.conv_general_dilated(x, w[:, None, :], window_strides=(1,), padding=[(pad, pad)],
                                 dimension_numbers=('NWC', 'WIO', 'NWC'),
                                 feature_group_count=x.shape[-1])
    return y + b


def ssd_chunked(x, dt, a, bh, ch, h0):
    bsz, n, nh, hp = x.shape
    ns = bh.shape[-1]
    nc = n // SSD_CHUNK
    xq = x.astype(F32).reshape(bsz, nc, SSD_CHUNK, nh, hp)
    dq = dt.astype(F32).reshape(bsz, nc, SSD_CHUNK, nh)
    bq = bh.astype(F32).reshape(bsz, nc, SSD_CHUNK, nh, ns)
    cq = ch.astype(F32).reshape(bsz, nc, SSD_CHUNK, nh, ns)
    a_cum = jnp.cumsum(dq * a.astype(F32), axis=2)
    seg = a_cum[:, :, :, None, :] - a_cum[:, :, None, :, :]
    lower = jnp.tril(jnp.ones((SSD_CHUNK, SSD_CHUNK), bool))[None, None, :, :, None]
    decay = jnp.exp(jnp.where(lower, seg, -jnp.inf))
    w = jnp.einsum('bcihn,bcjhn->bcijh', cq, bq) * decay * dq[:, :, None, :, :]
    y_diag = jnp.einsum('bcijh,bcjhp->bcihp', w, xq)
    to_end = jnp.exp(a_cum[:, :, -1:, :] - a_cum) * dq
    chunk_states = jnp.einsum('bcjh,bcjhp,bcjhn->bchpn', to_end, xq, bq)
    chunk_decay = jnp.exp(a_cum[:, :, -1, :])

    def step(h, inp):
        st, dec = inp
        return h * dec[:, :, None, None] + st, h
    h_last, h_in = lax.scan(step, h0.astype(F32),
                            (jnp.moveaxis(chunk_states, 1, 0), jnp.moveaxis(chunk_decay, 1, 0)))
    h_in = jnp.moveaxis(h_in, 0, 1)
    y_off = jnp.einsum('bcihn,bchpn->bcihp', cq, h_in) * jnp.exp(a_cum)[..., None]
    return (y_diag + y_off).reshape(bsz, n, nh, hp), h_last


def token_mixers(h, lp, layer_idx, ctx, angs):
    bsz, n, _ = h.shape
    latent = ctx is not None
    u = h @ lp['w_in']
    (a_q, a_k, a_v, b_cq, b_ckv, b_kr, c_q, c_k, c_v,
     d_z, d_xbc, d_dtf, d_dtb) = jnp.split(u, IN_OFFSETS, axis=-1)

    qa = rms_norm(a_q.reshape(bsz, n, A_KV_HEADS, A_GROUP, HEAD_DIM), lp['gqa_qn_g'])
    ka = rms_norm(a_k.reshape(bsz, n, A_KV_HEADS, HEAD_DIM), lp['gqa_kn_g'])
    va = a_v.reshape(bsz, n, A_KV_HEADS, HEAD_DIM)
    if latent:
        qa = apply_axial_rope(qa, angs[HEAD_DIM])
        ka = apply_axial_rope(ka, angs[HEAD_DIM])
        ka_all = jnp.concatenate([ctx[0], ka], axis=1)
        va_all = jnp.concatenate([ctx[1], va], axis=1)
    else:
        ka_all, va_all = ka, va
    out_a = gqa_attention(qa, ka_all, va_all).reshape(bsz, n, A_HEADS * HEAD_DIM)

    qb = (rms_norm(b_cq, lp['mla_qa_g']) @ lp['mla_wqb']).reshape(bsz, n, B_HEADS, MLA_NOPE + MLA_ROPE)
    qb_nope, qb_rope = qb[..., :MLA_NOPE], qb[..., MLA_NOPE:]
    ckv = rms_norm(b_ckv, lp['mla_kva_g'])
    kr = b_kr
    if latent:
        qb_rope = apply_axial_rope(qb_rope, angs[MLA_ROPE])
        kr = apply_axial_rope(kr, angs[MLA_ROPE])
        ckv_all = jnp.concatenate([ctx[2], ckv], axis=1)
        kr_all = jnp.concatenate([ctx[3], kr], axis=1)
    else:
        ckv_all, kr_all = ckv, kr
    n_keys = ckv_all.shape[1]
    kv = (ckv_all @ lp['mla_wkvb']).reshape(bsz, n_keys, B_HEADS, MLA_NOPE + MLA_V)
    kb = jnp.concatenate([kv[..., :MLA_NOPE],
                          jnp.broadcast_to(kr_all[:, :, None, :], (bsz, n_keys, B_HEADS, MLA_ROPE))], axis=-1)
    vb = kv[..., MLA_NOPE:]
    qb = jnp.concatenate([qb_nope, qb_rope], axis=-1)
    out_b = mla_attention(qb, kb, vb).reshape(bsz, n, B_HEADS * MLA_V)

    qc = c_q.reshape(bsz, n, C_HEADS, 2, DIFF_D)
    kc = c_k.reshape(bsz, n, C_HEADS, 2, DIFF_D)
    vc = c_v.reshape(bsz, n, C_HEADS, DIFF_V)
    if latent:
        qc = apply_axial_rope(qc, angs[DIFF_D])
        kc = apply_axial_rope(kc, angs[DIFF_D])
        kc_all = jnp.concatenate([ctx[4].reshape(bsz, -1, C_HEADS, 2, DIFF_D), kc], axis=1)
        vc_all = jnp.concatenate([ctx[5], vc], axis=1)
    else:
        kc_all, vc_all = kc, vc
    lam_init = 0.8 - 0.6 * math.exp(-0.3 * layer_idx)
    lam = (jnp.exp(jnp.sum(lp['diff_lq1'].astype(F32) * lp['diff_lk1'].astype(F32)))
           - jnp.exp(jnp.sum(lp['diff_lq2'].astype(F32) * lp['diff_lk2'].astype(F32))) + lam_init)
    oc = diff_attention(qc, kc_all, vc_all, lam)
    oc = rms_norm(oc, lp['diff_subln_g'], eps=SUBLN_EPS) * (1.0 - lam_init)
    out_c = oc.reshape(bsz, n, C_HEADS * DIFF_V)

    xbc = jax.nn.silu(centred_depthwise_conv(d_xbc, lp['ssd_conv_w'], lp['ssd_conv_b']))
    xs, bm, cm = jnp.split(xbc, [D_INNER, D_INNER + D_GROUPS * D_STATE], axis=-1)
    xs = xs.reshape(bsz, n, D_HEADS, D_HEADDIM)
    rep = D_HEADS // D_GROUPS
    bh = jnp.repeat(bm.reshape(bsz, n, D_GROUPS, D_STATE), rep, axis=2)
    ch = jnp.repeat(cm.reshape(bsz, n, D_GROUPS, D_STATE), rep, axis=2)
    dt_f = jax.nn.softplus((d_dtf + lp['ssd_dt_bias_f']).astype(F32))
    dt_b = jax.nn.softplus((d_dtb + lp['ssd_dt_bias_b']).astype(F32))
    a_f = -jnp.exp(lp['ssd_a_log_f'].astype(F32))
    a_b = -jnp.exp(lp['ssd_a_log_b'].astype(F32))
    if latent:
        h0_f, h0_b = ctx[6], ctx[7]
    else:
        h0_f = jnp.zeros((bsz, D_HEADS, D_HEADDIM, D_STATE), F32)
        h0_b = h0_f
    y_f, hf = ssd_chunked(xs, dt_f, a_f, bh, ch, h0_f)
    y_b, hb = ssd_chunked(xs[:, ::-1], dt_b[:, ::-1], a_b, bh[:, ::-1], ch[:, ::-1], h0_b)
    y = y_f + y_b[:, ::-1] + lp['ssd_d'].astype(F32)[:, None] * xs.astype(F32)
    y = y.reshape(bsz, n, D_INNER).astype(h.dtype)
    out_d = rms_norm(y * jax.nn.silu(d_z), lp['ssd_norm_g'])

    mixed = jnp.concatenate([out_a, out_b, out_c, out_d], axis=-1) @ lp['w_out']
    if latent:
        return mixed
    return mixed, (ka, va, ckv, kr, kc.reshape(bsz, n, C_HEADS, 2 * DIFF_D), vc, hf, hb)


def moe_ffn(x2d, lp):
    n_tok = x2d.shape[0]
    n_pairs = n_tok * TOP_K
    logits = (x2d @ lp['router_w'] + lp['router_b']).astype(F32)
    top_v, top_e = lax.top_k(logits, TOP_K)
    gates = jax.nn.softmax(top_v, axis=-1)
    flat_e = top_e.reshape(-1)
    order = jnp.argsort(flat_e)
    sorted_e = flat_e[order]
    tok = order // TOP_K
    counts = jnp.bincount(flat_e, length=N_EXPERTS)
    padded = (counts + MOE_BLOCK - 1) // MOE_BLOCK * MOE_BLOCK
    start = jnp.cumsum(counts) - counts
    pad_end = jnp.cumsum(padded)
    pad_start = pad_end - padded
    dest = pad_start[sorted_e] + jnp.arange(n_pairs) - start[sorted_e]
    n_blocks = n_pairs // MOE_BLOCK + N_EXPERTS
    rows = jnp.zeros((n_blocks * MOE_BLOCK,), jnp.int32).at[dest].set(tok)
    block_e = jnp.minimum(jnp.searchsorted(pad_end, jnp.arange(n_blocks) * MOE_BLOCK, side='right'),
                          N_EXPERTS - 1)
    xb = x2d[rows].reshape(n_blocks, MOE_BLOCK, x2d.shape[-1])
    w_gu, b_gu, w_dn, b_dn = lp['moe_w_gu'], lp['moe_b_gu'], lp['moe_w_dn'], lp['moe_b_dn']

    def expert_block(args):
        xblk, e = args
        hgu = xblk @ w_gu[e] + b_gu[e]
        gate = jnp.minimum(hgu[:, :D_FF], SWIGLU_LIMIT)
        up = jnp.clip(hgu[:, D_FF:], -SWIGLU_LIMIT, SWIGLU_LIMIT)
        act = (up + 1.0) * gate * jax.nn.sigmoid(SWIGLU_ALPHA * gate)
        return act @ w_dn[e] + b_dn[e]
    yb = lax.map(expert_block, (xb, block_e)).reshape(n_blocks * MOE_BLOCK, x2d.shape[-1])
    y_pairs = yb[dest] * gates.reshape(-1)[order][:, None].astype(yb.dtype)
    return jax.ops.segment_sum(y_pairs, tok, num_segments=n_tok)


def trunk_layer(x, cvec, lp, layer_idx, ctx, angs):
    mod = jax.nn.silu(cvec) @ lp['w_ada'] + lp['b_ada']
    sh1, sc1, g1, sh2, sc2, g2 = jnp.split(mod, N_MOD, axis=-1)
    h = rms_norm(x, lp['norm1_g']) * (1.0 + sc1) + sh1
    if ctx is None:
        mixed, new_ctx = token_mixers(h, lp, layer_idx, None, None)
    else:
        mixed, new_ctx = token_mixers(h, lp, layer_idx, ctx, angs), None
    x = x + g1 * mixed
    h = rms_norm(x, lp['norm2_g']) * (1.0 + sc2) + sh2
    bsz, n, d = x.shape
    x = x + g2 * moe_ffn(h.reshape(bsz * n, d), lp).reshape(bsz, n, d)
    return x, new_ctx


def setup_inputs(seed: int = 0) -> dict:
    key = jax.random.key(seed)
    ks = iter(jax.random.split(key, 64))

    def nrm(shape, scale=1.0):
        return scale * jax.random.normal(next(ks), shape, F32)

    def gain(shape):
        return 1.0 + nrm(shape, 0.05)
    dt0 = jnp.exp(jax.random.uniform(next(ks), (DEPTH, 2, D_HEADS), F32, math.log(1e-3), math.log(1e-1)))
    dt_bias = dt0 + jnp.log(-jnp.expm1(-dt0))
    a_log = jnp.log(jax.random.uniform(next(ks), (DEPTH, 2, D_HEADS), F32, 1.0, 16.0))
    return {
        'x_prompt': nrm((BATCH, SEQ, D_MODEL)),
        'x_sample': nrm((DEC_BATCH, DEC_SEQ, D_MODEL)),
        'cache_gqa_k': nrm((DEC_BATCH, DEPTH, PAST_LEN, A_KV_HEADS, HEAD_DIM)),
        'cache_gqa_v': nrm((DEC_BATCH, DEPTH, PAST_LEN, A_KV_HEADS, HEAD_DIM)),
        'cache_mla_ckv': nrm((DEC_BATCH, DEPTH, PAST_LEN, KV_LORA)),
        'cache_mla_krope': nrm((DEC_BATCH, DEPTH, PAST_LEN, MLA_ROPE)),
        'cache_diff_k': nrm((DEC_BATCH, DEPTH, PAST_LEN, C_HEADS, 2 * DIFF_D)),
        'cache_diff_v': nrm((DEC_BATCH, DEPTH, PAST_LEN, C_HEADS, DIFF_V)),
        'state_ssd_fwd': nrm((DEC_BATCH, DEPTH, D_HEADS, D_HEADDIM, D_STATE), 0.3),
        'state_ssd_bwd': nrm((DEC_BATCH, DEPTH, D_HEADS, D_HEADDIM, D_STATE), 0.3),
        'c': nrm((DEC_BATCH, D_MODEL)),
        'c_ctx': nrm((D_MODEL,)),
        'norm1_g': gain((DEPTH, D_MODEL)),
        'norm2_g': gain((DEPTH, D_MODEL)),
        'w_ada': nrm((DEPTH, D_MODEL, N_MOD * D_MODEL), 0.5 * D_MODEL ** -0.5),
        'b_ada': nrm((DEPTH, N_MOD * D_MODEL), 0.02),
        'w_in': nrm((DEPTH, D_MODEL, IN_WIDTH), D_MODEL ** -0.5),
        'w_out': nrm((DEPTH, MIX_WIDTH, D_MODEL), MIX_WIDTH ** -0.5),
        'gqa_qn_g': gain((DEPTH, HEAD_DIM)),
        'gqa_kn_g': gain((DEPTH, HEAD_DIM)),
        'mla_qa_g': gain((DEPTH, Q_LORA)),
        'mla_wqb': nrm((DEPTH, Q_LORA, B_HEADS * (MLA_NOPE + MLA_ROPE)), Q_LORA ** -0.5),
        'mla_kva_g': gain((DEPTH, KV_LORA)),
        'mla_wkvb': nrm((DEPTH, KV_LORA, B_HEADS * (MLA_NOPE + MLA_V)), KV_LORA ** -0.5),
        'diff_lq1': nrm((DEPTH, DIFF_D), 0.1),
        'diff_lk1': nrm((DEPTH, DIFF_D), 0.1),
        'diff_lq2': nrm((DEPTH, DIFF_D), 0.1),
        'diff_lk2': nrm((DEPTH, DIFF_D), 0.1),
        'diff_subln_g': gain((DEPTH, DIFF_V)),
        'ssd_conv_w': nrm((DEPTH, D_CONV, CONV_DIM), D_CONV ** -0.5),
        'ssd_conv_b': nrm((DEPTH, CONV_DIM), 0.02),
        'ssd_a_log_f': a_log[:, 0],
        'ssd_a_log_b': a_log[:, 1],
        'ssd_dt_bias_f': dt_bias[:, 0],
        'ssd_dt_bias_b': dt_bias[:, 1],
        'ssd_d': 1.0 + nrm((DEPTH, D_HEADS), 0.1),
        'ssd_norm_g': gain((DEPTH, D_INNER)),
        'router_w': nrm((DEPTH, D_MODEL, N_EXPERTS), D_MODEL ** -0.5),
        'router_b': nrm((DEPTH, N_EXPERTS), 0.01),
        'moe_w_gu': nrm((DEPTH, N_EXPERTS, D_MODEL, 2 * D_FF), D_MODEL ** -0.5),
        'moe_b_gu': nrm((DEPTH, N_EXPERTS, 2 * D_FF), 0.02),
        'moe_w_dn': nrm((DEPTH, N_EXPERTS, D_FF, D_MODEL), D_FF ** -0.5),
        'moe_b_dn': nrm((DEPTH, N_EXPERTS, D_MODEL), 0.02),
        'final_g': gain((D_MODEL,)),
    }


def reference(x_prompt, x_sample, cache_gqa_k, cache_gqa_v, cache_mla_ckv, cache_mla_krope,
              cache_diff_k, cache_diff_v, state_ssd_fwd, state_ssd_bwd, c, c_ctx,
              norm1_g, norm2_g, w_ada, b_ada, w_in, w_out, gqa_qn_g, gqa_kn_g,
              mla_qa_g, mla_wqb, mla_kva_g, mla_wkvb, diff_lq1, diff_lk1, diff_lq2, diff_lk2,
              diff_subln_g, ssd_conv_w, ssd_conv_b, ssd_a_log_f, ssd_a_log_b, ssd_dt_bias_f,
              ssd_dt_bias_b, ssd_d, ssd_norm_g, router_w, router_b, moe_w_gu, moe_b_gu,
              moe_w_dn, moe_b_dn, final_g):
    n_lat = x_sample.shape[1]
    angs = {d: axial_angles(n_lat, d) for d in (HEAD_DIM, MLA_ROPE, DIFF_D)}
    ctx_c = c_ctx[None, None, :]
    lat_c = c[:, None, :]
    x_ctx, x_lat = x_prompt, x_sample
    ctx_layers = []
    for l in range(DEPTH):
        lp = {
            'norm1_g': norm1_g[l], 'norm2_g': norm2_g[l], 'w_ada': w_ada[l], 'b_ada': b_ada[l],
            'w_in': w_in[l], 'w_out': w_out[l], 'gqa_qn_g': gqa_qn_g[l], 'gqa_kn_g': gqa_kn_g[l],
            'mla_qa_g': mla_qa_g[l], 'mla_wqb': mla_wqb[l], 'mla_kva_g': mla_kva_g[l],
            'mla_wkvb': mla_wkvb[l], 'diff_lq1': diff_lq1[l], 'diff_lk1': diff_lk1[l],
            'diff_lq2': diff_lq2[l], 'diff_lk2': diff_lk2[l], 'diff_subln_g': diff_subln_g[l],
            'ssd_conv_w': ssd_conv_w[l], 'ssd_conv_b': ssd_conv_b[l],
            'ssd_a_log_f': ssd_a_log_f[l], 'ssd_a_log_b': ssd_a_log_b[l],
            'ssd_dt_bias_f': ssd_dt_bias_f[l], 'ssd_dt_bias_b': ssd_dt_bias_b[l],
            'ssd_d': ssd_d[l], 'ssd_norm_g': ssd_norm_g[l], 'router_w': router_w[l],
            'router_b': router_b[l], 'moe_w_gu': moe_w_gu[l], 'moe_b_gu': moe_b_gu[l],
            'moe_w_dn': moe_w_dn[l], 'moe_b_dn': moe_b_dn[l],
        }
        x_ctx, ctx_l = trunk_layer(x_ctx, ctx_c, lp, l, None, None)
        ctx_layers.append(ctx_l)
        past = (cache_gqa_k[:, l], cache_gqa_v[:, l], cache_mla_ckv[:, l], cache_mla_krope[:, l],
                cache_diff_k[:, l], cache_diff_v[:, l], state_ssd_fwd[:, l], state_ssd_bwd[:, l])
        x_lat, _ = trunk_layer(x_lat, lat_c, lp, l, past, angs)
    y_prompt = rms_norm(x_ctx, final_g)
    y_sample = rms_norm(x_lat, final_g)
    (new_gqa_k, new_gqa_v, new_mla_ckv, new_mla_krope, new_diff_k, new_diff_v,
     new_ssd_fwd, new_ssd_bwd) = [jnp.stack([cl[i] for cl in ctx_layers], axis=1) for i in range(N_CACHE)]
    return (y_prompt, y_sample, new_gqa_k, new_gqa_v, new_mla_ckv, new_mla_krope,
            new_diff_k, new_diff_v, new_ssd_fwd, new_ssd_bwd)
```

```python
import functools
import math

import numpy as np
import jax
import jax.numpy as jnp
from jax import lax
from jax.experimental import pallas as pl
from jax.experimental.pallas import tpu as pltpu

F32 = jnp.float32
BF16 = jnp.bfloat16

D_MODEL = 1024
BATCH = 16
SEQ = 256
DEPTH = 2
DEC_BATCH = 8
DEC_SEQ = 2048
PAST_LEN = 256
GRID_W = 64
ROPE_THETA = 10000.0
EPS = 1e-6
HEAD_DIM = 64
A_HEADS = 4
A_KV_HEADS = 2
B_HEADS = 4
MLA_NOPE = 64
MLA_ROPE = 32
MLA_V = 64
Q_LORA = 192
KV_LORA = 128
C_HEADS = 4
DIFF_D = 32
DIFF_V = 64
SUBLN_EPS = 1e-5
D_HEADS = 4
D_HEADDIM = 64
D_INNER = 256
D_GROUPS = 2
D_STATE = 64
CONV_DIM = 512
SSD_CHUNK = 128
N_EXPERTS = 32
TOP_K = 4
D_FF = 1024
SWIGLU_ALPHA = 1.702
SWIGLU_LIMIT = 7.0
N_MOD = 6

N_CTX = BATCH * SEQ
N_LAT = DEC_BATCH * DEC_SEQ
N_TOK = N_CTX + N_LAT
LK_LAT = PAST_LEN + DEC_SEQ

LANES = 128
TM = 256
N_TILES = N_TOK // TM
CTX_TILES = N_CTX // TM
LAT_TILES_PER_SEQ = DEC_SEQ // TM
MOD_ROWS = 16
MOE_BM = 256
N_PAIRS = N_TOK * TOP_K
MOE_BLOCKS = N_PAIRS // MOE_BM + N_EXPERTS
VMEM_LIMIT = 56 * 1024 * 1024
NEG_BIG = -1e30

W1_COLS = 2688
IN_OFF = dict(a_q=0, a_k=256, a_v=384, b_cq=512, b_ckv=704, b_kr=832, c_q=864, c_k=1120,
              c_v=1376, d_z=1632, d_xbc=1888, d_dtf=2400, d_dtb=2404)
IN_WIDTH = 2408
N_TAB = 12


def _mod_row(i):
    return jnp.where(i < CTX_TILES, 0, 1 + (i - CTX_TILES) // LAT_TILES_PER_SEQ)


def _tab_block(i):
    return jnp.where(i < CTX_TILES, 0, 1 + (i - CTX_TILES) % LAT_TILES_PER_SEQ)


def _dot(a, b):
    return jnp.dot(a, b, preferred_element_type=F32)


def _dot_nt(a, b):
    return lax.dot_general(a, b, (((1,), (1,)), ((), ())), preferred_element_type=F32)


def _dot_split(x, m):
    hi = x.astype(BF16)
    lo = (x - hi.astype(F32)).astype(BF16)
    return _dot(hi, m) + _dot(lo, m)


def _dot_split_left(m, x):
    hi = x.astype(BF16)
    lo = (x - hi.astype(F32)).astype(BF16)
    return _dot(m, hi) + _dot(m, lo)


def _silu(x):
    return x * jax.nn.sigmoid(x)


MOD_TN = 1536


def _mod_kernel(c_ref, w_ref, b_ref, o_ref):
    c = c_ref[...]
    s = _silu(c).astype(BF16)
    o_ref[0] = _dot(s, w_ref[0].astype(BF16)) + b_ref[0]


def _modulation(cvec, w_ada, b_ada):
    n = N_MOD * D_MODEL
    return pl.pallas_call(
        _mod_kernel,
        out_shape=jax.ShapeDtypeStruct((DEPTH, MOD_ROWS, n), F32),
        grid=(DEPTH, n // MOD_TN),
        in_specs=[pl.BlockSpec((MOD_ROWS, D_MODEL), lambda l, j: (0, 0)),
                  pl.BlockSpec((1, D_MODEL, MOD_TN), lambda l, j: (l, 0, j)),
                  pl.BlockSpec((1, 1, MOD_TN), lambda l, j: (l, 0, j))],
        out_specs=pl.BlockSpec((1, MOD_ROWS, MOD_TN), lambda l, j: (l, 0, j)),
        compiler_params=pltpu.CompilerParams(dimension_semantics=("parallel", "parallel"),
                                             vmem_limit_bytes=VMEM_LIMIT),
        name="adaln_mod",
    )(cvec, w_ada, b_ada.reshape(DEPTH, 1, n))


def _rope(x, cos, sin, quarter):
    lane = lax.broadcasted_iota(jnp.int32, (x.shape[0], LANES), 1)
    first = (lane // quarter) % 2 == 0
    outs = []
    for t in range(x.shape[1] // LANES):
        xt = x[:, t * LANES:(t + 1) * LANES]
        partner = jnp.where(first, pltpu.roll(xt, LANES - quarter, 1), pltpu.roll(xt, quarter, 1))
        outs.append(xt * cos + partner * sin)
    return outs


def _in_proj_kernel(x_ref, mod_ref, g1_ref, w1_ref, bd_ref, gq_ref, gk_ref, gqa_ref, gkva_ref,
                    wqb_ref, tab_ref,
                    qa_ref, kva_ref, qb_ref, ck_ref, qc_ref, kc_ref, vc_ref, z_ref, xbc_ref, dt_ref):
    x = x_ref[...]
    m = mod_ref[0]
    sh1, sc1 = m[0:1], m[1:2]
    ms = jnp.mean(x * x, axis=-1, keepdims=True)
    h = (x * lax.rsqrt(ms + EPS) * g1_ref[...]) * (1.0 + sc1) + sh1
    u = _dot(h.astype(BF16), w1_ref[...])

    def tab(k):
        return tab_ref[:, k * LANES:(k + 1) * LANES]

    bd = bd_ref[...]

    def head_norm(v, gain):
        w = v.shape[1]
        ss = _dot_split(v * v, bd[:w, :w])
        return v * lax.rsqrt(ss * (1.0 / HEAD_DIM) + EPS) * gain

    qa = _rope(head_norm(u[:, 0:256], gq_ref[...]), tab(0), tab(1), HEAD_DIM // 4)
    for t in range(2):
        qa_ref[:, t * LANES:(t + 1) * LANES] = qa[t].astype(BF16)
    ka = _rope(head_norm(u[:, 256:384], gk_ref[...]), tab(2), tab(3), HEAD_DIM // 4)
    kva_ref[:, 0:128] = ka[0]
    kva_ref[:, 128:256] = u[:, 384:512]

    cq = u[:, 512:768]
    msq = jnp.sum(cq * cq, axis=-1, keepdims=True) * (1.0 / Q_LORA)
    yq = cq * lax.rsqrt(msq + EPS) * gqa_ref[...]
    qb = _rope(_dot(yq.astype(BF16), wqb_ref[...]), tab(4), tab(5), MLA_ROPE // 4)
    for t in range(4):
        qb_ref[:, t * LANES:(t + 1) * LANES] = qb[t].astype(BF16)
    ckv = u[:, 768:896]
    msk = jnp.mean(ckv * ckv, axis=-1, keepdims=True)
    ck_ref[:, 0:128] = ckv * lax.rsqrt(msk + EPS) * gkva_ref[...]
    ck_ref[:, 128:256] = _rope(u[:, 896:1024], tab(6), tab(7), MLA_ROPE // 4)[0]

    qc = _rope(u[:, 1024:1280], tab(8), tab(9), DIFF_D // 4)
    kc = _rope(u[:, 1280:1536], tab(10), tab(11), DIFF_D // 4)
    for t in range(2):
        qc_ref[:, t * LANES:(t + 1) * LANES] = qc[t].astype(BF16)
        kc_ref[:, t * LANES:(t + 1) * LANES] = kc[t]
    vc_ref[...] = u[:, 1536:1792]

    z_ref[...] = u[:, 1792:2048]
    xbc_ref[...] = u[:, 2048:2560]
    dt_ref[...] = u[:, 2560:2688]


def _in_proj(x, mod3, g1, w1, bd, gq, gk, gqa, gkva, wqb, tabs):
    row = lambda w: pl.BlockSpec((TM, w), lambda i: (i, 0))
    full = lambda a: pl.BlockSpec(a.shape, lambda i: (0,) * a.ndim)
    outs = [(256, BF16), (256, F32), (512, BF16), (256, F32), (256, BF16), (256, F32), (256, F32),
            (256, F32), (512, F32), (128, F32)]
    return pl.pallas_call(
        _in_proj_kernel,
        out_shape=[jax.ShapeDtypeStruct((N_TOK, w), d) for w, d in outs],
        grid=(N_TILES,),
        in_specs=[row(D_MODEL),
                  pl.BlockSpec((1, N_MOD, D_MODEL), lambda i: (_mod_row(i), 0, 0)),
                  full(g1), full(w1), full(bd), full(gq), full(gk), full(gqa), full(gkva), full(wqb),
                  pl.BlockSpec((TM, N_TAB * LANES), lambda i: (_tab_block(i), 0))],
        out_specs=[row(w) for w, _ in outs],
        compiler_params=pltpu.CompilerParams(dimension_semantics=("parallel",),
                                             vmem_limit_bytes=VMEM_LIMIT),
        name="in_proj",
    )(x, mod3, g1, w1, bd, gq, gk, gqa, gkva, wqb, tabs)


def _softmax_parts(s):
    m = jnp.max(s, axis=-1, keepdims=True)
    e = jnp.exp(s - m)
    return e, jnp.sum(e, axis=-1, keepdims=True)


def _half_mask(rows):
    lane = lax.broadcasted_iota(jnp.int32, (rows, LANES), 1)
    return lane < (LANES // 2)


def _attn_a_kernel(q_ref, kv_ref, o_ref):
    q = q_ref[0]
    k = kv_ref[0, :, 0:128].astype(BF16)
    v = kv_ref[0, :, 128:256].astype(BF16)
    lo = _half_mask(q.shape[0])
    for g in range(2):
        qt = q[:, g * LANES:(g + 1) * LANES].astype(F32)
        res = []
        for half in range(2):
            qm = jnp.where(lo, qt, 0.0) if half == 0 else jnp.where(lo, 0.0, qt)
            e, l = _softmax_parts(_dot_nt(qm.astype(BF16), k))
            res.append(_dot(e.astype(BF16), v) / l)
        o_ref[0, :, g * LANES:(g + 1) * LANES] = jnp.where(lo, res[0], res[1])


def _attn_b_kernel(q_ref, ck_ref, wk_ref, wv_ref, o_ref, k_s, v_s):
    @pl.when(pl.program_id(1) == 0)
    def _():
        ck = ck_ref[0].astype(BF16)
        k_s[...] = _dot(ck, wk_ref[...]).astype(BF16)
        v_s[...] = _dot(ck[:, 0:128], wv_ref[...]).astype(BF16)

    q = q_ref[0]
    lo = _half_mask(q.shape[0])
    for j in range(2):
        v = v_s[:, j * LANES:(j + 1) * LANES]
        res = []
        for half in range(2):
            h = 2 * j + half
            e, l = _softmax_parts(_dot_nt(q[:, h * LANES:(h + 1) * LANES], k_s[:, h * LANES:(h + 1) * LANES]))
            res.append(_dot(e.astype(BF16), v) / l)
        o_ref[0, :, j * LANES:(j + 1) * LANES] = jnp.where(lo, res[0], res[1])


def _attn_c_kernel(lam_init, q_ref, k_ref, v_ref, lq1_ref, lk1_ref, lq2_ref, lk2_ref, g_ref, o_ref):
    lam = (jnp.exp(jnp.sum(lq1_ref[...] * lk1_ref[...], axis=-1, keepdims=True))
           - jnp.exp(jnp.sum(lq2_ref[...] * lk2_ref[...], axis=-1, keepdims=True)) + lam_init)
    q = q_ref[0]
    rows = q.shape[0]
    lane = lax.broadcasted_iota(jnp.int32, (rows, LANES), 1)
    lo = lane < (LANES // 2)
    for j in range(2):
        qt = q[:, j * LANES:(j + 1) * LANES].astype(F32)
        k = k_ref[0, :, j * LANES:(j + 1) * LANES].astype(BF16)
        v = v_ref[0, :, j * LANES:(j + 1) * LANES].astype(BF16)
        res = []
        for half in range(2):
            p = []
            for t in range(2):
                quarter = 2 * half + t
                qm = jnp.where(lane // (LANES // 4) == quarter, qt, 0.0)
                e, l = _softmax_parts(_dot_nt(qm.astype(BF16), k))
                p.append((e, l))
            pd = p[0][0] * (1.0 / p[0][1]) - p[1][0] * (lam / p[1][1])
            res.append(_dot(pd.astype(BF16), v))
        o = jnp.where(lo, res[0], res[1])
        o2 = o * o
        ss_lo = jnp.sum(jnp.where(lo, o2, 0.0), axis=-1, keepdims=True)
        ss_hi = jnp.sum(jnp.where(lo, 0.0, o2), axis=-1, keepdims=True)
        ss = jnp.where(lo, ss_lo, ss_hi) * (1.0 / DIFF_V)
        o_ref[0, :, j * LANES:(j + 1) * LANES] = (o * lax.rsqrt(ss + SUBLN_EPS) * g_ref[...]) * (1.0 - lam_init)


def _seq_spec(tq, w):
    return pl.BlockSpec((1, tq, w), lambda b, i: (b, i, 0))


def _key_spec(lk, w):
    return pl.BlockSpec((1, lk, w), lambda b, i: (b, 0, 0))


def _const_spec(a):
    return pl.BlockSpec(a.shape, lambda b, i: (0,) * a.ndim)


def _attn_a(q, kv):
    bsz, lq, _ = q.shape
    lk = kv.shape[1]
    return pl.pallas_call(
        _attn_a_kernel,
        out_shape=jax.ShapeDtypeStruct((bsz, lq, 256), F32),
        grid=(bsz, lq // TM),
        in_specs=[_seq_spec(TM, 256), _key_spec(lk, 256)],
        out_specs=_seq_spec(TM, 256),
        compiler_params=pltpu.CompilerParams(dimension_semantics=("parallel", "parallel"),
                                             vmem_limit_bytes=VMEM_LIMIT),
        name="attn_gqa",
    )(q, kv)


def _attn_b(q, ck, wk, wv):
    bsz, lq, _ = q.shape
    lk = ck.shape[1]
    return pl.pallas_call(
        _attn_b_kernel,
        out_shape=jax.ShapeDtypeStruct((bsz, lq, 256), F32),
        grid=(bsz, lq // TM),
        in_specs=[_seq_spec(TM, 512), _key_spec(lk, 256), _const_spec(wk), _const_spec(wv)],
        out_specs=_seq_spec(TM, 256),
        scratch_shapes=[pltpu.VMEM((lk, 512), BF16), pltpu.VMEM((lk, 256), BF16)],
        compiler_params=pltpu.CompilerParams(dimension_semantics=("parallel", "arbitrary"),
                                             vmem_limit_bytes=VMEM_LIMIT),
        name="attn_mla",
    )(q, ck, wk, wv)


def _attn_c(q, k, v, lams, gain, lam_init):
    bsz, lq, _ = q.shape
    lk = k.shape[1]
    return pl.pallas_call(
        functools.partial(_attn_c_kernel, lam_init),
        out_shape=jax.ShapeDtypeStruct((bsz, lq, 256), F32),
        grid=(bsz, lq // TM),
        in_specs=[_seq_spec(TM, 256), _key_spec(lk, 256), _key_spec(lk, 256)]
                 + [_const_spec(a) for a in lams] + [_const_spec(gain)],
        out_specs=_seq_spec(TM, 256),
        compiler_params=pltpu.CompilerParams(dimension_semantics=("parallel", "parallel"),
                                             vmem_limit_bytes=VMEM_LIMIT),
        name="attn_diff",
    )(q, k, v, *lams, gain)


Q = SSD_CHUNK


def _ssd_kernel(z_ref, xbc_ref, dt_ref, cw_ref, cb_ref, dtb_ref, alog_ref, dvec_ref, ng_ref,
                h0f_ref, h0b_ref, out_ref, hf_ref, hb_ref, act_s, cum_s, dtv_s, y_s):
    seq = z_ref.shape[1]
    nc = seq // Q
    row = lax.broadcasted_iota(jnp.int32, (Q, Q), 0)
    col = lax.broadcasted_iota(jnp.int32, (Q, Q), 1)
    lower = row >= col
    upper = row <= col
    tril = jnp.where(lower, 1.0, 0.0).astype(BF16)
    triu = jnp.where(upper, 1.0, 0.0).astype(BF16)
    rowc = lax.broadcasted_iota(jnp.int32, (Q, CONV_DIM), 0)
    lane = lax.broadcasted_iota(jnp.int32, (Q, LANES), 1)
    a_neg = -jnp.exp(alog_ref[...])
    cw = cw_ref[...]
    hf_ref[0] = h0f_ref[0]
    hb_ref[0] = h0b_ref[0]

    def fwd_chunk(c, carry):
        base = pl.multiple_of(c * Q, Q)
        x0 = xbc_ref[0, pl.ds(base, Q), :]
        prev = xbc_ref[0, pl.ds(pl.multiple_of(jnp.maximum(base - 8, 0), 8), 8), :][7:8, :]
        nxt = xbc_ref[0, pl.ds(pl.multiple_of(jnp.minimum(base + Q, seq - 8), 8), 8), :][0:1, :]
        prev = jnp.where(c > 0, prev, 0.0)
        nxt = jnp.where(c < nc - 1, nxt, 0.0)
        xm1 = jnp.where(rowc == 0, prev, pltpu.roll(x0, 1, 0))
        xp1 = jnp.where(rowc == Q - 1, nxt, pltpu.roll(x0, Q - 1, 0))
        act = _silu(xm1 * cw[0:1] + x0 * cw[1:2] + xp1 * cw[2:3] + cb_ref[...])
        act_s[pl.ds(base, Q), :] = act
        xs = act[:, 0:256]
        bm = act[:, 256:384]
        cm = act[:, 384:512]

        dtr = dt_ref[0, pl.ds(base, Q), :] + dtb_ref[...]
        dtv = jnp.maximum(dtr, 0.0) + jnp.log1p(jnp.exp(-jnp.abs(dtr)))
        dta = dtv * a_neg
        cum = jnp.where(lane < D_HEADS, _dot_split_left(tril, dta), _dot_split_left(triu, dta))
        cum_s[pl.ds(base, Q), :] = cum
        dtv_s[pl.ds(base, Q), :] = dtv
        cum_t = cum.T
        dtv_t = dtv.T
        bm_t = bm.T
        ys = []
        for h in range(D_HEADS):
            g = h // (D_HEADS // D_GROUPS)
            cg = cm[:, g * D_STATE:(g + 1) * D_STATE].astype(BF16)
            cb_mat = _dot_nt(cg, bm[:, g * D_STATE:(g + 1) * D_STATE].astype(BF16))
            cf = cum[:, h:h + 1]
            cb = cum[:, D_HEADS + h:D_HEADS + h + 1]
            l_f = jnp.exp(jnp.where(lower, cf - cum_t[h:h + 1, :], NEG_BIG))
            l_b = jnp.exp(jnp.where(upper, cb - cum_t[D_HEADS + h:D_HEADS + h + 1, :], NEG_BIG))
            mix = cb_mat * (l_f * dtv_t[h:h + 1, :] + l_b * dtv_t[D_HEADS + h:D_HEADS + h + 1, :])
            xh = xs[:, h * D_HEADDIM:(h + 1) * D_HEADDIM]
            y = _dot(mix.astype(BF16), xh.astype(BF16))
            state = hf_ref[0, h]
            y = y + _dot(cg, state.astype(BF16)) * jnp.exp(cf)
            y = y + dvec_ref[:, h * D_HEADDIM:(h + 1) * D_HEADDIM] * xh
            ys.append(y)
            last = cum[Q - 1:Q, h:h + 1]
            wgt = jnp.exp(last - cf) * dtv[:, h:h + 1]
            st = _dot(bm_t[g * D_STATE:(g + 1) * D_STATE, :].astype(BF16), (xh * wgt).astype(BF16))
            hf_ref[0, h] = state * jnp.exp(last) + st
        y_s[pl.ds(base, Q), :] = jnp.concatenate(ys, axis=-1)
        return carry

    lax.fori_loop(0, nc, fwd_chunk, 0)

    def bwd_chunk(i, carry):
        c = nc - 1 - i
        base = pl.multiple_of(c * Q, Q)
        act = act_s[pl.ds(base, Q), :]
        cum = cum_s[pl.ds(base, Q), :]
        dtv = dtv_s[pl.ds(base, Q), :]
        xs = act[:, 0:256]
        bm_t = act[:, 256:384].T
        cm = act[:, 384:512]
        ys = []
        for h in range(D_HEADS):
            g = h // (D_HEADS // D_GROUPS)
            cg = cm[:, g * D_STATE:(g + 1) * D_STATE].astype(BF16)
            cb = cum[:, D_HEADS + h:D_HEADS + h + 1]
            xh = xs[:, h * D_HEADDIM:(h + 1) * D_HEADDIM]
            state = hb_ref[0, h]
            ys.append(_dot(cg, state.astype(BF16)) * jnp.exp(cb))
            first = cum[0:1, D_HEADS + h:D_HEADS + h + 1]
            wgt = jnp.exp(first - cb) * dtv[:, D_HEADS + h:D_HEADS + h + 1]
            st = _dot(bm_t[g * D_STATE:(g + 1) * D_STATE, :].astype(BF16), (xh * wgt).astype(BF16))
            hb_ref[0, h] = state * jnp.exp(first) + st
        y = y_s[pl.ds(base, Q), :] + jnp.concatenate(ys, axis=-1)
        gated = y * _silu(z_ref[0, pl.ds(base, Q), :])
        ms = jnp.mean(gated * gated, axis=-1, keepdims=True)
        out_ref[0, pl.ds(base, Q), :] = gated * lax.rsqrt(ms + EPS) * ng_ref[...]
        return carry

    lax.fori_loop(0, nc, bwd_chunk, 0)


def _ssd(z, xbc, dt, cw, cb, dtb, alog, dvec, ng, h0f, h0b):
    bsz, seq, _ = z.shape
    per_seq = lambda w: pl.BlockSpec((1, seq, w), lambda b: (b, 0, 0))
    const = lambda a: pl.BlockSpec(a.shape, lambda b: (0,) * a.ndim)
    st_spec = pl.BlockSpec((1, D_HEADS, D_STATE, D_HEADDIM), lambda b: (b, 0, 0, 0))
    st_shape = jax.ShapeDtypeStruct((bsz, D_HEADS, D_STATE, D_HEADDIM), F32)
    return pl.pallas_call(
        _ssd_kernel,
        out_shape=[jax.ShapeDtypeStruct((bsz, seq, D_INNER), F32), st_shape, st_shape],
        grid=(bsz,),
        in_specs=[per_seq(256), per_seq(512), per_seq(128), const(cw), const(cb), const(dtb),
                  const(alog), const(dvec), const(ng), st_spec, st_spec],
        out_specs=[per_seq(D_INNER), st_spec, st_spec],
        scratch_shapes=[pltpu.VMEM((seq, CONV_DIM), F32), pltpu.VMEM((seq, LANES), F32),
                        pltpu.VMEM((seq, LANES), F32), pltpu.VMEM((seq, D_INNER), F32)],
        compiler_params=pltpu.CompilerParams(dimension_semantics=("parallel",),
                                             vmem_limit_bytes=VMEM_LIMIT),
        name="ssd",
    )(z, xbc, dt, cw, cb, dtb, alog, dvec, ng, h0f, h0b)


def _out_proj_kernel(x_ref, oa_ref, ob_ref, oc_ref, od_ref, mod_ref, wo_ref, g2_ref, rwh_ref, rwl_ref,
                     rb_ref, x1_ref, h2_ref, te_ref, gt_ref):
    m = mod_ref[0]
    gate1, sh2, sc2 = m[2:3], m[3:4], m[4:5]
    mixed = (_dot(oa_ref[...].astype(BF16), wo_ref[0:256, :])
             + _dot(ob_ref[...].astype(BF16), wo_ref[256:512, :])
             + _dot(oc_ref[...].astype(BF16), wo_ref[512:768, :])
             + _dot(od_ref[...].astype(BF16), wo_ref[768:1024, :]))
    x1 = x_ref[...] + gate1 * mixed
    x1_ref[...] = x1
    ms = jnp.mean(x1 * x1, axis=-1, keepdims=True)
    h2 = (x1 * lax.rsqrt(ms + EPS) * g2_ref[...]) * (1.0 + sc2) + sh2
    h2_ref[...] = h2.astype(BF16)

    hi = h2.astype(BF16)
    lo = (h2 - hi.astype(F32)).astype(BF16)
    logits = _dot(hi, rwh_ref[...]) + _dot(lo, rwh_ref[...]) + _dot(hi, rwl_ref[...]) + rb_ref[...]
    lane = lax.broadcasted_iota(jnp.int32, logits.shape, 1)
    vals, idxs = [], []
    for _ in range(TOP_K):
        mx = jnp.max(logits, axis=-1, keepdims=True)
        ix = jnp.min(jnp.where(logits == mx, lane, LANES), axis=-1, keepdims=True)
        vals.append(mx)
        idxs.append(ix)
        logits = jnp.where(lane == ix, -3e38, logits)
    es = [jnp.exp(v - vals[0]) for v in vals]
    den = es[0] + es[1] + es[2] + es[3]
    te = jnp.zeros(lane.shape, jnp.int32)
    gt = jnp.zeros(lane.shape, F32)
    for k in range(TOP_K):
        te = jnp.where(lane == k, idxs[k], te)
        gt = jnp.where(lane == k, es[k] / den, gt)
    te_ref[...] = te[:, 0:TOP_K]
    gt_ref[...] = gt[:, 0:TOP_K]


def _out_proj(x, oa, ob, oc, od, mod3, wo, g2, rwh, rwl, rb):
    row = lambda w: pl.BlockSpec((TM, w), lambda i: (i, 0))
    full = lambda a: pl.BlockSpec(a.shape, lambda i: (0,) * a.ndim)
    return pl.pallas_call(
        _out_proj_kernel,
        out_shape=[jax.ShapeDtypeStruct((N_TOK, D_MODEL), F32), jax.ShapeDtypeStruct((N_TOK, D_MODEL), BF16),
                   jax.ShapeDtypeStruct((N_TOK, TOP_K), jnp.int32), jax.ShapeDtypeStruct((N_TOK, TOP_K), F32)],
        grid=(N_TILES,),
        in_specs=[row(D_MODEL), row(256), row(256), row(256), row(256),
                  pl.BlockSpec((1, N_MOD, D_MODEL), lambda i: (_mod_row(i), 0, 0)),
                  full(wo), full(g2), full(rwh), full(rwl), full(rb)],
        out_specs=[row(D_MODEL), row(D_MODEL), row(TOP_K), row(TOP_K)],
        compiler_params=pltpu.CompilerParams(dimension_semantics=("parallel",),
                                             vmem_limit_bytes=VMEM_LIMIT),
        name="out_proj_router",
    )(x, oa, ob, oc, od, mod3, wo, g2, rwh, rwl, rb)


def _expert_kernel(be_ref, nu_ref, x_ref, wgu_ref, bgu_ref, wdn_ref, bdn_ref, o_ref, wgu_s, wdn_s):
    i = pl.program_id(0)
    used = i < nu_ref[0]
    prev = be_ref[jnp.maximum(i - 1, 0)]
    fresh = jnp.logical_or(i == 0, be_ref[i] != prev)

    @pl.when(jnp.logical_and(used, fresh))
    def _():
        wgu_s[...] = wgu_ref[0].astype(BF16)
        wdn_s[...] = wdn_ref[0].astype(BF16)

    @pl.when(used)
    def _():
        hgu = _dot(x_ref[...], wgu_s[...]) + bgu_ref[0]
        gate = jnp.minimum(hgu[:, :D_FF], SWIGLU_LIMIT)
        up = jnp.clip(hgu[:, D_FF:], -SWIGLU_LIMIT, SWIGLU_LIMIT)
        act = (up + 1.0) * gate * jax.nn.sigmoid(SWIGLU_ALPHA * gate)
        o_ref[...] = _dot(act.astype(BF16), wdn_s[...]) + bdn_ref[0]

    @pl.when(jnp.logical_not(used))
    def _():
        o_ref[...] = jnp.zeros_like(o_ref)


def _experts(block_e, n_used, xb, w_gu, b_gu, w_dn, b_dn):
    grid_spec = pltpu.PrefetchScalarGridSpec(
        num_scalar_prefetch=2,
        grid=(MOE_BLOCKS,),
        in_specs=[pl.BlockSpec((MOE_BM, D_MODEL), lambda i, be, nu: (i, 0)),
                  pl.BlockSpec((1, D_MODEL, 2 * D_FF), lambda i, be, nu: (be[i], 0, 0)),
                  pl.BlockSpec((1, 1, 2 * D_FF), lambda i, be, nu: (be[i], 0, 0)),
                  pl.BlockSpec((1, D_FF, D_MODEL), lambda i, be, nu: (be[i], 0, 0)),
                  pl.BlockSpec((1, 1, D_MODEL), lambda i, be, nu: (be[i], 0, 0))],
        out_specs=pl.BlockSpec((MOE_BM, D_MODEL), lambda i, be, nu: (i, 0)),
        scratch_shapes=[pltpu.VMEM((D_MODEL, 2 * D_FF), BF16), pltpu.VMEM((D_FF, D_MODEL), BF16)],
    )
    return pl.pallas_call(
        _expert_kernel,
        out_shape=jax.ShapeDtypeStruct((MOE_BLOCKS * MOE_BM, D_MODEL), F32),
        grid_spec=grid_spec,
        compiler_params=pltpu.CompilerParams(dimension_semantics=("arbitrary",),
                                             vmem_limit_bytes=VMEM_LIMIT),
        name="experts",
    )(block_e, n_used, xb, w_gu, b_gu.reshape(N_EXPERTS, 1, 2 * D_FF), w_dn,
      b_dn.reshape(N_EXPERTS, 1, D_MODEL))


def _combine_kernel(final, x1_ref, yg_ref, gt_ref, mod_ref, fg_ref, o_ref):
    gate2 = mod_ref[0][5:6]
    gt = gt_ref[...]
    y = gt[:, 0:1] * yg_ref[:, 0:D_MODEL]
    for k in range(1, TOP_K):
        y = y + gt[:, k:k + 1] * yg_ref[:, k * D_MODEL:(k + 1) * D_MODEL]
    x2 = x1_ref[...] + gate2 * y
    if final:
        ms = jnp.mean(x2 * x2, axis=-1, keepdims=True)
        x2 = x2 * lax.rsqrt(ms + EPS) * fg_ref[...]
    o_ref[...] = x2


def _combine(x1, yg, gates, mod3, fg, final):
    row = lambda w: pl.BlockSpec((TM, w), lambda i: (i, 0))
    return pl.pallas_call(
        functools.partial(_combine_kernel, final),
        out_shape=jax.ShapeDtypeStruct((N_TOK, D_MODEL), F32),
        grid=(N_TILES,),
        in_specs=[row(D_MODEL), row(TOP_K * D_MODEL), row(TOP_K),
                  pl.BlockSpec((1, N_MOD, D_MODEL), lambda i: (_mod_row(i), 0, 0)),
                  pl.BlockSpec(fg.shape, lambda i: (0, 0))],
        out_specs=row(D_MODEL),
        compiler_params=pltpu.CompilerParams(dimension_semantics=("parallel",),
                                             vmem_limit_bytes=VMEM_LIMIT),
        name="moe_combine",
    )(x1, yg, gates, mod3, fg)


def _w1_index():
    idx = np.full((W1_COLS,), IN_WIDTH, np.int32)
    for g in range(2):
        for kv in range(2):
            idx[g * 128 + kv * 64:g * 128 + kv * 64 + 64] = np.arange(64) + kv * 128 + g * 64
    idx[256:512] = np.arange(256) + IN_OFF['a_k']
    idx[512:704] = np.arange(192) + IN_OFF['b_cq']
    idx[768:896] = np.arange(128) + IN_OFF['b_ckv']
    idx[896:928] = np.arange(32) + IN_OFF['b_kr']
    idx[1024:1792] = np.arange(768) + IN_OFF['c_q']
    idx[1792:2048] = np.arange(256) + IN_OFF['d_z']
    idx[2048:2560] = np.arange(512) + IN_OFF['d_xbc']
    idx[2560:2568] = np.arange(8) + IN_OFF['d_dtf']
    return idx


def _wout_a_index():
    idx = np.zeros((256,), np.int32)
    for g in range(2):
        for kv in range(2):
            idx[g * 128 + kv * 64:g * 128 + kv * 64 + 64] = np.arange(64) + kv * 128 + g * 64
    return idx


def _rope_tables():
    t = np.arange(DEC_SEQ)
    pos = ((t // GRID_W).astype(np.float32), (t % GRID_W).astype(np.float32))

    def unit(rot_dim):
        quarter, half = rot_dim // 4, rot_dim // 2
        inv = ROPE_THETA ** (-np.arange(0, half, 2, dtype=np.float32) / half)
        cos = np.zeros((DEC_SEQ, rot_dim), np.float32)
        sin = np.zeros((DEC_SEQ, rot_dim), np.float32)
        for seg in range(4):
            ang = pos[seg // 2][:, None] * inv[None, :].astype(np.float32)
            cos[:, seg * quarter:(seg + 1) * quarter] = np.cos(ang)
            sin[:, seg * quarter:(seg + 1) * quarter] = np.sin(ang) * (-1.0 if seg % 2 == 0 else 1.0)
        return cos, sin

    specs = [
        (HEAD_DIM, (0, 64), (), HEAD_DIM ** -0.5),
        (HEAD_DIM, (0, 64), (), 1.0),
        (MLA_ROPE, (MLA_NOPE,), (0, MLA_NOPE), (MLA_NOPE + MLA_ROPE) ** -0.5),
        (MLA_ROPE, (0,), (), 1.0),
        (DIFF_D, (0, 32, 64, 96), (), DIFF_D ** -0.5),
        (DIFF_D, (0, 32, 64, 96), (), 1.0),
    ]
    lat_cols, ident_cols = [], []
    for rot_dim, starts, passthrough, scale in specs:
        ucos, usin = unit(rot_dim)
        cos = np.zeros((DEC_SEQ, LANES), np.float32)
        sin = np.zeros((DEC_SEQ, LANES), np.float32)
        ident = np.zeros((1, LANES), np.float32)
        if passthrough:
            cos[:, passthrough[0]:passthrough[1]] = 1.0
            ident[:, passthrough[0]:passthrough[1]] = 1.0
        for s in starts:
            cos[:, s:s + rot_dim] = ucos
            sin[:, s:s + rot_dim] = usin
            ident[:, s:s + rot_dim] = 1.0
        lat_cols += [cos * scale, sin * scale]
        ident_cols += [ident * scale, np.zeros((1, LANES), np.float32)]
    lat = np.concatenate(lat_cols, axis=1)
    ident_blk = np.broadcast_to(np.concatenate(ident_cols, axis=1), (TM, N_TAB * LANES))
    return np.concatenate([ident_blk, lat], axis=0).astype(np.float32)


def _block_diag_ones(n, blk):
    r = np.arange(n)
    return (r[:, None] // blk == r[None, :] // blk).astype(np.float32)


def kernel(x_prompt, x_sample, cache_gqa_k, cache_gqa_v, cache_mla_ckv, cache_mla_krope, cache_diff_k, cache_diff_v, state_ssd_fwd, state_ssd_bwd, c, c_ctx, norm1_g, norm2_g, w_ada, b_ada, w_in, w_out, gqa_qn_g, gqa_kn_g, mla_qa_g, mla_wqb, mla_kva_g, mla_wkvb, diff_lq1, diff_lk1, diff_lq2, diff_lk2, diff_subln_g, ssd_conv_w, ssd_conv_b, ssd_a_log_f, ssd_a_log_b, ssd_dt_bias_f, ssd_dt_bias_b, ssd_d, ssd_norm_g, router_w, router_b, moe_w_gu, moe_b_gu, moe_w_dn, moe_b_dn, final_g):
    tabs = jnp.asarray(_rope_tables())
    bd = jnp.asarray(_block_diag_ones(256, HEAD_DIM), BF16)
    w1_idx = _w1_index()
    woa_idx = _wout_a_index()

    cvec = jnp.zeros((MOD_ROWS, D_MODEL), F32).at[0].set(c_ctx).at[1:1 + DEC_BATCH].set(c)
    mod = _modulation(cvec, w_ada, b_ada).reshape(DEPTH, MOD_ROWS, N_MOD, D_MODEL)

    x = jnp.concatenate([x_prompt.reshape(N_CTX, D_MODEL), x_sample.reshape(N_LAT, D_MODEL)], axis=0)
    new_ctx = []
    for l in range(DEPTH):
        mod3 = mod[l]
        w1 = jnp.concatenate([w_in[l], jnp.zeros((D_MODEL, 1), F32)], axis=1)[:, w1_idx].astype(BF16)
        wqb = mla_wqb[l].reshape(Q_LORA, B_HEADS, MLA_NOPE + MLA_ROPE)
        wqb = jnp.pad(wqb, ((0, 256 - Q_LORA), (0, 0), (0, LANES - MLA_NOPE - MLA_ROPE)))
        wqb = wqb.reshape(256, B_HEADS * LANES).astype(BF16)
        wkvb = mla_wkvb[l].reshape(KV_LORA, B_HEADS, MLA_NOPE + MLA_V)
        wk_nope = jnp.pad(wkvb[:, :, :MLA_NOPE], ((0, 0), (0, 0), (0, LANES - MLA_NOPE)))
        eye_r = jnp.zeros((LANES, B_HEADS, LANES), F32)
        eye_r = eye_r.at[jnp.arange(MLA_ROPE), :, MLA_NOPE + jnp.arange(MLA_ROPE)].set(1.0)
        wk = jnp.concatenate([wk_nope, eye_r], axis=0).reshape(256, B_HEADS * LANES).astype(BF16)
        wv = wkvb[:, :, MLA_NOPE:].reshape(KV_LORA, B_HEADS * MLA_V).astype(BF16)
        wo = jnp.concatenate([w_out[l][woa_idx], w_out[l][256:]], axis=0).astype(BF16)
        gq = jnp.tile(gqa_qn_g[l], 4)[None, :]
        gk = jnp.tile(gqa_kn_g[l], 2)[None, :]
        gqa = jnp.pad(mla_qa_g[l], (0, 256 - Q_LORA))[None, :]
        gkva = mla_kva_g[l][None, :]
        rw = jnp.pad(router_w[l], ((0, 0), (0, LANES - N_EXPERTS)))
        rwh = rw.astype(BF16)
        rwl = (rw - rwh.astype(F32)).astype(BF16)
        rb = jnp.pad(router_b[l], (0, LANES - N_EXPERTS), constant_values=NEG_BIG)[None, :]

        qa, kva, qb, ck, qc, kc, vc, z, xbc, dt = _in_proj(
            x, mod3, norm1_g[l][None, :], w1, bd, gq, gk, gqa, gkva, wqb, tabs)

        def split(a):
            w = a.shape[-1]
            return a[:N_CTX].reshape(BATCH, SEQ, w), a[N_CTX:].reshape(DEC_BATCH, DEC_SEQ, w)

        qa_c, qa_l = split(qa)
        kva_c, kva_l = split(kva)
        qb_c, qb_l = split(qb)
        ck_c, ck_l = split(ck)
        qc_c, qc_l = split(qc)
        kc_c, kc_l = split(kc)
        vc_c, vc_l = split(vc)
        z_c, z_l = split(z)
        xbc_c, xbc_l = split(xbc)
        dt_c, dt_l = split(dt)

        past_kva = jnp.concatenate([cache_gqa_k[:, l].reshape(DEC_BATCH, PAST_LEN, 128),
                                    cache_gqa_v[:, l].reshape(DEC_BATCH, PAST_LEN, 128)], axis=-1)
        past_ck = jnp.concatenate([cache_mla_ckv[:, l], cache_mla_krope[:, l],
                                   jnp.zeros((DEC_BATCH, PAST_LEN, LANES - MLA_ROPE), F32)], axis=-1)
        past_kc = cache_diff_k[:, l].reshape(DEC_BATCH, PAST_LEN, 256)
        past_vc = cache_diff_v[:, l].reshape(DEC_BATCH, PAST_LEN, 256)

        oa_c = _attn_a(qa_c, kva_c)
        oa_l = _attn_a(qa_l, jnp.concatenate([past_kva, kva_l], axis=1))
        ob_c = _attn_b(qb_c, ck_c, wk, wv)
        ob_l = _attn_b(qb_l, jnp.concatenate([past_ck, ck_l], axis=1), wk, wv)
        lams = [a[l][None, :] for a in (diff_lq1, diff_lk1, diff_lq2, diff_lk2)]
        gsub = jnp.tile(diff_subln_g[l], 2)[None, :]
        lam_init = 0.8 - 0.6 * math.exp(-0.3 * l)
        oc_c = _attn_c(qc_c, kc_c, vc_c, lams, gsub, lam_init)
        oc_l = _attn_c(qc_l, jnp.concatenate([past_kc, kc_l], axis=1),
                       jnp.concatenate([past_vc, vc_l], axis=1), lams, gsub, lam_init)

        pad8 = lambda f, b: jnp.pad(jnp.concatenate([f, b]), (0, LANES - 2 * D_HEADS))[None, :]
        dtb = pad8(ssd_dt_bias_f[l], ssd_dt_bias_b[l])
        alog = pad8(ssd_a_log_f[l], ssd_a_log_b[l])
        dvec = jnp.repeat(ssd_d[l], D_HEADDIM)[None, :]
        ssd_args = (ssd_conv_w[l], ssd_conv_b[l][None, :], dtb, alog, dvec, ssd_norm_g[l][None, :])
        zeros_st = jnp.zeros((BATCH, D_HEADS, D_STATE, D_HEADDIM), F32)
        od_c, hf_c, hb_c = _ssd(z_c, xbc_c, dt_c, *ssd_args, zeros_st, zeros_st)
        od_l, _, _ = _ssd(z_l, xbc_l, dt_l, *ssd_args,
                          jnp.swapaxes(state_ssd_fwd[:, l], -1, -2), jnp.swapaxes(state_ssd_bwd[:, l], -1, -2))

        join = lambda a, b: jnp.concatenate([a.reshape(N_CTX, 256), b.reshape(N_LAT, 256)], axis=0)
        x1, h2, te, gates = _out_proj(x, join(oa_c, oa_l), join(ob_c, ob_l), join(oc_c, oc_l),
                                      join(od_c, od_l), mod3, wo, norm2_g[l][None, :], rwh, rwl, rb)

        flat_e = te.reshape(-1)
        onehot = (flat_e[:, None] == jnp.arange(N_EXPERTS, dtype=jnp.int32)[None, :]).astype(jnp.int32)
        csum = jnp.cumsum(onehot, axis=0)
        rank = jnp.take_along_axis(csum, flat_e[:, None], axis=1)[:, 0] - 1
        counts = csum[-1]
        padded = (counts + MOE_BM - 1) // MOE_BM * MOE_BM
        pad_end = jnp.cumsum(padded)
        pad_start = pad_end - padded
        dest = pad_start[flat_e] + rank
        tok = jnp.arange(N_PAIRS, dtype=jnp.int32) // TOP_K
        rows = jnp.zeros((MOE_BLOCKS * MOE_BM,), jnp.int32).at[dest].set(tok)
        n_used = (pad_end[-1] // MOE_BM).astype(jnp.int32).reshape(1)
        block_e = jnp.minimum(jnp.searchsorted(pad_end, jnp.arange(MOE_BLOCKS, dtype=jnp.int32) * MOE_BM,
                                               side='right'), N_EXPERTS - 1).astype(jnp.int32)
        xb = h2[rows]
        yb = _experts(block_e, n_used, xb, moe_w_gu[l], moe_b_gu[l], moe_w_dn[l], moe_b_dn[l])
        yg = yb[dest].reshape(N_TOK, TOP_K * D_MODEL)
        x = _combine(x1, yg, gates, mod3, final_g[None, :], l == DEPTH - 1)

        ka_c = kva_c[..., 0:128].reshape(BATCH, SEQ, A_KV_HEADS, HEAD_DIM)
        va_c = kva_c[..., 128:256].reshape(BATCH, SEQ, A_KV_HEADS, HEAD_DIM)
        new_ctx.append((ka_c, va_c, ck_c[..., 0:KV_LORA], ck_c[..., KV_LORA:KV_LORA + MLA_ROPE],
                        kc_c.reshape(BATCH, SEQ, C_HEADS, 2 * DIFF_D), vc_c.reshape(BATCH, SEQ, C_HEADS, DIFF_V),
                        jnp.swapaxes(hf_c, -1, -2), jnp.swapaxes(hb_c, -1, -2)))

    y_prompt = x[:N_CTX].reshape(BATCH, SEQ, D_MODEL)
    y_sample = x[N_CTX:].reshape(DEC_BATCH, DEC_SEQ, D_MODEL)
    caches = [jnp.stack([cl[i] for cl in new_ctx], axis=1) for i in range(8)]
    return (y_prompt, y_sample, *caches)
```

```python
import functools
import math

import numpy as np
import jax
import jax.numpy as jnp
from jax import lax
from jax.experimental import pallas as pl
from jax.experimental.pallas import tpu as pltpu

F32 = jnp.float32
BF16 = jnp.bfloat16

D_MODEL = 1024
BATCH = 16
SEQ = 256
DEPTH = 2
DEC_BATCH = 8
DEC_SEQ = 2048
PAST_LEN = 256
GRID_W = 64
ROPE_THETA = 10000.0
EPS = 1e-6
HEAD_DIM = 64
A_HEADS = 4
A_KV_HEADS = 2
B_HEADS = 4
MLA_NOPE = 64
MLA_ROPE = 32
MLA_V = 64
Q_LORA = 192
KV_LORA = 128
C_HEADS = 4
DIFF_D = 32
DIFF_V = 64
SUBLN_EPS = 1e-5
D_HEADS = 4
D_HEADDIM = 64
D_INNER = 256
D_GROUPS = 2
D_STATE = 64
CONV_DIM = 512
SSD_CHUNK = 128
N_EXPERTS = 32
TOP_K = 4
D_FF = 1024
SWIGLU_ALPHA = 1.702
SWIGLU_LIMIT = 7.0
N_MOD = 6

N_CTX = BATCH * SEQ
N_LAT = DEC_BATCH * DEC_SEQ
N_TOK = N_CTX + N_LAT

LANES = 128
BF16_ROWS = 16
TM = 256
N_TILES = N_TOK // TM
CTX_TILES = N_CTX // TM
LAT_TILES_PER_SEQ = DEC_SEQ // TM
MOD_ROWS = 16
MOE_BM = 256
N_PAIRS = N_TOK * TOP_K
MOE_BLOCKS = N_PAIRS // MOE_BM + N_EXPERTS + 1
YB_ROWS = MOE_BLOCKS * MOE_BM
SEG_ROWS = 80
SEG_COLS = N_EXPERTS * SEG_ROWS
VMEM_LIMIT = 56 * 1024 * 1024
NEG_BIG = -1e30

W1_COLS = 2688
IN_OFF = dict(a_q=0, a_k=256, a_v=384, b_cq=512, b_ckv=704, b_kr=832, c_q=864, c_k=1120,
              c_v=1376, d_z=1632, d_xbc=1888, d_dtf=2400, d_dtb=2404)
IN_WIDTH = 2408
N_TAB = 12


def _mod_row(i):
    return jnp.where(i < CTX_TILES, 0, 1 + (i - CTX_TILES) // LAT_TILES_PER_SEQ)


def _tab_block(i):
    return jnp.where(i < CTX_TILES, 0, 1 + (i - CTX_TILES) % LAT_TILES_PER_SEQ)


def _dot(a, b):
    return jnp.dot(a, b, preferred_element_type=F32)


def _dot_nt(a, b):
    return lax.dot_general(a, b, (((1,), (1,)), ((), ())), preferred_element_type=F32)


def _dot_split(x, m):
    hi = x.astype(BF16)
    lo = (x - hi.astype(F32)).astype(BF16)
    return _dot(hi, m) + _dot(lo, m)


def _dot_split_left(m, x):
    hi = x.astype(BF16)
    lo = (x - hi.astype(F32)).astype(BF16)
    return _dot(m, hi) + _dot(m, lo)


def _silu(x):
    return x * jax.nn.sigmoid(x)


def _params(*semantics):
    return pltpu.CompilerParams(dimension_semantics=semantics, vmem_limit_bytes=VMEM_LIMIT)


MOD_TN = 1536


def _mod_kernel(c_ref, w_ref, b_ref, o_ref):
    c = c_ref[...]
    s = _silu(c).astype(BF16)
    o_ref[0] = _dot(s, w_ref[0].astype(BF16)) + b_ref[0]


def _modulation(cvec, w_ada, b_ada):
    n = N_MOD * D_MODEL
    return pl.pallas_call(
        _mod_kernel,
        out_shape=jax.ShapeDtypeStruct((DEPTH, MOD_ROWS, n), F32),
        grid=(DEPTH, n // MOD_TN),
        in_specs=[pl.BlockSpec((MOD_ROWS, D_MODEL), lambda l, j: (0, 0)),
                  pl.BlockSpec((1, D_MODEL, MOD_TN), lambda l, j: (l, 0, j)),
                  pl.BlockSpec((1, 1, MOD_TN), lambda l, j: (l, 0, j))],
        out_specs=pl.BlockSpec((1, MOD_ROWS, MOD_TN), lambda l, j: (l, 0, j)),
        compiler_params=_params("parallel", "parallel"),
        name="adaln_mod",
    )(cvec, w_ada, b_ada.reshape(DEPTH, 1, n))


def _rope(x, cos, sin, quarter):
    lane = lax.broadcasted_iota(jnp.int32, (x.shape[0], LANES), 1)
    first = (lane // quarter) % 2 == 0
    outs = []
    for t in range(x.shape[1] // LANES):
        xt = x[:, t * LANES:(t + 1) * LANES]
        partner = jnp.where(first, pltpu.roll(xt, LANES - quarter, 1), pltpu.roll(xt, quarter, 1))
        outs.append(xt * cos + partner * sin)
    return outs


def _in_proj_kernel(x_ref, mod_ref, g1_ref, w1_ref, bd_ref, gq_ref, gk_ref, gqa_ref, gkva_ref,
                    wqb_ref, tab_ref,
                    qa_ref, kva_ref, qb_ref, ck_ref, qc_ref, kc_ref, vc_ref, z_ref, xbc_ref, dt_ref):
    x = x_ref[...]
    m = mod_ref[0]
    sh1, sc1 = m[0:1], m[1:2]
    ms = jnp.mean(x * x, axis=-1, keepdims=True)
    h = (x * lax.rsqrt(ms + EPS) * g1_ref[...]) * (1.0 + sc1) + sh1
    u = _dot(h.astype(BF16), w1_ref[...])

    def tab(k):
        return tab_ref[:, k * LANES:(k + 1) * LANES]

    bd = bd_ref[...]

    def head_norm(v, gain):
        w = v.shape[1]
        ss = _dot_split(v * v, bd[:w, :w])
        return v * lax.rsqrt(ss * (1.0 / HEAD_DIM) + EPS) * gain

    qa = _rope(head_norm(u[:, 0:256], gq_ref[...]), tab(0), tab(1), HEAD_DIM // 4)
    for t in range(2):
        qa_ref[:, t * LANES:(t + 1) * LANES] = qa[t].astype(BF16)
    ka = _rope(head_norm(u[:, 256:384], gk_ref[...]), tab(2), tab(3), HEAD_DIM // 4)
    kva_ref[:, 0:128] = ka[0]
    kva_ref[:, 128:256] = u[:, 384:512]

    cq = u[:, 512:768]
    msq = jnp.sum(cq * cq, axis=-1, keepdims=True) * (1.0 / Q_LORA)
    yq = cq * lax.rsqrt(msq + EPS) * gqa_ref[...]
    qb = _rope(_dot(yq.astype(BF16), wqb_ref[...]), tab(4), tab(5), MLA_ROPE // 4)
    for t in range(4):
        qb_ref[:, t * LANES:(t + 1) * LANES] = qb[t].astype(BF16)
    ckv = u[:, 768:896]
    msk = jnp.mean(ckv * ckv, axis=-1, keepdims=True)
    ck_ref[:, 0:128] = ckv * lax.rsqrt(msk + EPS) * gkva_ref[...]
    ck_ref[:, 128:256] = _rope(u[:, 896:1024], tab(6), tab(7), MLA_ROPE // 4)[0]

    qc = _rope(u[:, 1024:1280], tab(8), tab(9), DIFF_D // 4)
    kc = _rope(u[:, 1280:1536], tab(10), tab(11), DIFF_D // 4)
    for t in range(2):
        qc_ref[:, t * LANES:(t + 1) * LANES] = qc[t].astype(BF16)
        kc_ref[:, t * LANES:(t + 1) * LANES] = kc[t]
    vc_ref[...] = u[:, 1536:1792]

    z_ref[...] = u[:, 1792:2048]
    xbc_ref[...] = u[:, 2048:2560]
    dt_ref[...] = u[:, 2560:2688]


def _in_proj(x, mod3, g1, w1, bd, gq, gk, gqa, gkva, wqb, tabs):
    row = lambda w: pl.BlockSpec((TM, w), lambda i: (i, 0))
    full = lambda a: pl.BlockSpec(a.shape, lambda i: (0,) * a.ndim)
    outs = [(256, BF16), (256, F32), (512, BF16), (256, F32), (256, BF16), (256, F32), (256, F32),
            (256, F32), (512, F32), (128, F32)]
    return pl.pallas_call(
        _in_proj_kernel,
        out_shape=[jax.ShapeDtypeStruct((N_TOK, w), d) for w, d in outs],
        grid=(N_TILES,),
        in_specs=[row(D_MODEL),
                  pl.BlockSpec((1, N_MOD, D_MODEL), lambda i: (_mod_row(i), 0, 0)),
                  full(g1), full(w1), full(bd), full(gq), full(gk), full(gqa), full(gkva), full(wqb),
                  pl.BlockSpec((TM, N_TAB * LANES), lambda i: (_tab_block(i), 0))],
        out_specs=[row(w) for w, _ in outs],
        compiler_params=_params("parallel"),
        name="in_proj",
    )(x, mod3, g1, w1, bd, gq, gk, gqa, gkva, wqb, tabs)


def _softmax_parts(s):
    m = jnp.max(s, axis=-1, keepdims=True)
    e = jnp.exp(s - m)
    return e, jnp.sum(e, axis=-1, keepdims=True)


def _half_mask(rows):
    lane = lax.broadcasted_iota(jnp.int32, (rows, LANES), 1)
    return lane < (LANES // 2)


def _keys(past_ref, new_ref, lo, hi):
    new = new_ref[:, lo:hi].astype(BF16)
    if past_ref is None:
        return new
    return jnp.concatenate([past_ref[0, :, lo:hi].astype(BF16), new], axis=0)


def _attn_a_kernel(latent, *refs):
    if latent:
        q_ref, kv_ref, past_ref, _, o_ref = refs
    else:
        (q_ref, kv_ref, _, o_ref), past_ref = refs, None
    q = q_ref[...]
    k = _keys(past_ref, kv_ref, 0, 128)
    v = _keys(past_ref, kv_ref, 128, 256)
    lo = _half_mask(q.shape[0])
    for g in range(2):
        qt = q[:, g * LANES:(g + 1) * LANES].astype(F32)
        res = []
        for half in range(2):
            qm = jnp.where(lo, qt, 0.0) if half == 0 else jnp.where(lo, 0.0, qt)
            e, l = _softmax_parts(_dot_nt(qm.astype(BF16), k))
            res.append(_dot(e.astype(BF16), v) / l)
        o_ref[:, g * LANES:(g + 1) * LANES] = jnp.where(lo, res[0], res[1])


def _attn_b_kernel(latent, *refs):
    if latent:
        q_ref, ck_ref, past_ref, wk_ref, wv_ref, _, o_ref, k_s, v_s = refs
    else:
        (q_ref, ck_ref, wk_ref, wv_ref, _, o_ref, k_s, v_s), past_ref = refs, None

    @pl.when(pl.program_id(1) == 0)
    def _():
        ck = _keys(past_ref, ck_ref, 0, 256)
        k_s[...] = _dot(ck, wk_ref[...]).astype(BF16)
        v_s[...] = _dot(ck[:, 0:128], wv_ref[...]).astype(BF16)

    q = q_ref[...]
    lo = _half_mask(q.shape[0])
    for j in range(2):
        v = v_s[:, j * LANES:(j + 1) * LANES]
        res = []
        for half in range(2):
            h = 2 * j + half
            e, l = _softmax_parts(_dot_nt(q[:, h * LANES:(h + 1) * LANES], k_s[:, h * LANES:(h + 1) * LANES]))
            res.append(_dot(e.astype(BF16), v) / l)
        o_ref[:, j * LANES:(j + 1) * LANES] = jnp.where(lo, res[0], res[1])


def _attn_c_kernel(lam_init, latent, *refs):
    if latent:
        q_ref, k_ref, v_ref, pk_ref, pv_ref, lq1_ref, lk1_ref, lq2_ref, lk2_ref, g_ref, _, o_ref = refs
    else:
        (q_ref, k_ref, v_ref, lq1_ref, lk1_ref, lq2_ref, lk2_ref, g_ref, _, o_ref), pk_ref, pv_ref = refs, None, None
    lam = (jnp.exp(jnp.sum(lq1_ref[...] * lk1_ref[...], axis=-1, keepdims=True))
           - jnp.exp(jnp.sum(lq2_ref[...] * lk2_ref[...], axis=-1, keepdims=True)) + lam_init)
    q = q_ref[...]
    rows = q.shape[0]
    lane = lax.broadcasted_iota(jnp.int32, (rows, LANES), 1)
    lo = lane < (LANES // 2)
    for j in range(2):
        qt = q[:, j * LANES:(j + 1) * LANES].astype(F32)
        k = _keys(pk_ref, k_ref, j * LANES, (j + 1) * LANES)
        v = _keys(pv_ref, v_ref, j * LANES, (j + 1) * LANES)
        res = []
        for half in range(2):
            p = []
            for t in range(2):
                quarter = 2 * half + t
                qm = jnp.where(lane // (LANES // 4) == quarter, qt, 0.0)
                e, l = _softmax_parts(_dot_nt(qm.astype(BF16), k))
                p.append((e, l))
            pd = p[0][0] * (1.0 / p[0][1]) - p[1][0] * (lam / p[1][1])
            res.append(_dot(pd.astype(BF16), v))
        o = jnp.where(lo, res[0], res[1])
        o2 = o * o
        ss_lo = jnp.sum(jnp.where(lo, o2, 0.0), axis=-1, keepdims=True)
        ss_hi = jnp.sum(jnp.where(lo, 0.0, o2), axis=-1, keepdims=True)
        ss = jnp.where(lo, ss_lo, ss_hi) * (1.0 / DIFF_V)
        o_ref[:, j * LANES:(j + 1) * LANES] = (o * lax.rsqrt(ss + SUBLN_EPS) * g_ref[...]) * (1.0 - lam_init)


def _seq_call(body, name, latent, q, news, pasts, consts, prev_out, scratch=(), q_semantics="parallel"):
    const_specs = [pl.BlockSpec(a.shape, lambda b, i, n=a.ndim: (0,) * n) for a in consts]
    if latent:
        tile = lambda b, i: (CTX_TILES + b * LAT_TILES_PER_SEQ + i, 0)
        grid = (DEC_BATCH, LAT_TILES_PER_SEQ)
        new_specs = [pl.BlockSpec((DEC_SEQ, a.shape[1]), lambda b, i: (N_CTX // DEC_SEQ + b, 0)) for a in news]
        past_specs = [pl.BlockSpec((1, PAST_LEN, a.shape[2]), lambda b, i: (b, 0, 0)) for a in pasts]
    else:
        tile = lambda b, i: (b, 0)
        grid = (BATCH, 1)
        new_specs = [pl.BlockSpec((SEQ, a.shape[1]), tile) for a in news]
        past_specs, pasts = [], ()
    in_specs = ([pl.BlockSpec((TM, q.shape[1]), tile)] + new_specs + past_specs + const_specs
                + [pl.BlockSpec(memory_space=pl.ANY)])
    args = [q, *news, *pasts, *consts, prev_out]
    return pl.pallas_call(
        functools.partial(body, latent),
        out_shape=jax.ShapeDtypeStruct((N_TOK, 256), F32),
        grid=grid,
        in_specs=in_specs,
        out_specs=pl.BlockSpec((TM, 256), tile),
        scratch_shapes=list(scratch),
        input_output_aliases={len(args) - 1: 0},
        compiler_params=_params("parallel", q_semantics),
        name=name + ("_lat" if latent else "_ctx"),
    )(*args)


def _attention_mixers(qa, kva, qb, ck, qc, kc, vc, past_kva, past_ck, past_kc, past_vc, wk, wv,
                      lams, gsub, lam_init):
    def both(body, name, q, news, pasts, consts, scratch_fn=None, q_semantics="parallel"):
        sc = (lambda lk: ()) if scratch_fn is None else scratch_fn
        out = jnp.zeros((N_TOK, 256), F32)
        out = _seq_call(body, name, False, q, news, (), consts, out, sc(SEQ), q_semantics)
        return _seq_call(body, name, True, q, news, pasts, consts, out, sc(PAST_LEN + DEC_SEQ), q_semantics)

    oa = both(_attn_a_kernel, "attn_gqa", qa, [kva], [past_kva], [])
    ob = both(_attn_b_kernel, "attn_mla", qb, [ck], [past_ck], [wk, wv],
              lambda lk: (pltpu.VMEM((lk, 512), BF16), pltpu.VMEM((lk, 256), BF16)), "arbitrary")
    oc = both(functools.partial(_attn_c_kernel, lam_init), "attn_diff", qc, [kc, vc],
              [past_kc, past_vc], [*lams, gsub])
    return oa, ob, oc


Q = SSD_CHUNK


def _ssd_kernel(z_ref, xbc_ref, dt_ref, cw_ref, cb_ref, dtb_ref, alog_ref, dvec_ref, ng_ref, h0f_ref, h0b_ref, _,
                out_ref, hf_ref, hb_ref, act_s, cum_s, dtv_s, y_s):
    seq = z_ref.shape[0]
    nc = seq // Q
    row = lax.broadcasted_iota(jnp.int32, (Q, Q), 0)
    col = lax.broadcasted_iota(jnp.int32, (Q, Q), 1)
    lower = row >= col
    upper = row <= col
    tril = jnp.where(lower, 1.0, 0.0).astype(BF16)
    triu = jnp.where(upper, 1.0, 0.0).astype(BF16)
    rowc = lax.broadcasted_iota(jnp.int32, (Q, CONV_DIM), 0)
    lane = lax.broadcasted_iota(jnp.int32, (Q, LANES), 1)
    a_neg = -jnp.exp(alog_ref[...])
    cw = cw_ref[...]
    hf_ref[0] = h0f_ref[0]
    hb_ref[0] = h0b_ref[0]

    def fwd_chunk(c, carry):
        base = pl.multiple_of(c * Q, Q)
        x0 = xbc_ref[pl.ds(base, Q), :]
        prev = xbc_ref[pl.ds(pl.multiple_of(jnp.maximum(base - 8, 0), 8), 8), :][7:8, :]
        nxt = xbc_ref[pl.ds(pl.multiple_of(jnp.minimum(base + Q, seq - 8), 8), 8), :][0:1, :]
        prev = jnp.where(c > 0, prev, 0.0)
        nxt = jnp.where(c < nc - 1, nxt, 0.0)
        xm1 = jnp.where(rowc == 0, prev, pltpu.roll(x0, 1, 0))
        xp1 = jnp.where(rowc == Q - 1, nxt, pltpu.roll(x0, Q - 1, 0))
        act = _silu(xm1 * cw[0:1] + x0 * cw[1:2] + xp1 * cw[2:3] + cb_ref[...])
        act_s[pl.ds(base, Q), :] = act
        xs = act[:, 0:256]
        bm = act[:, 256:384]
        cm = act[:, 384:512]

        dtr = dt_ref[pl.ds(base, Q), :] + dtb_ref[...]
        dtv = jnp.maximum(dtr, 0.0) + jnp.log1p(jnp.exp(-jnp.abs(dtr)))
        dta = dtv * a_neg
        cum = jnp.where(lane < D_HEADS, _dot_split_left(tril, dta), _dot_split_left(triu, dta))
        cum_s[pl.ds(base, Q), :] = cum
        dtv_s[pl.ds(base, Q), :] = dtv
        cum_t = cum.T
        dtv_t = dtv.T
        bm_t = bm.T
        ys = []
        for h in range(D_HEADS):
            g = h // (D_HEADS // D_GROUPS)
            cg = cm[:, g * D_STATE:(g + 1) * D_STATE].astype(BF16)
            cb_mat = _dot_nt(cg, bm[:, g * D_STATE:(g + 1) * D_STATE].astype(BF16))
            cf = cum[:, h:h + 1]
            cb = cum[:, D_HEADS + h:D_HEADS + h + 1]
            l_f = jnp.exp(jnp.where(lower, cf - cum_t[h:h + 1, :], NEG_BIG))
            l_b = jnp.exp(jnp.where(upper, cb - cum_t[D_HEADS + h:D_HEADS + h + 1, :], NEG_BIG))
            mix = cb_mat * (l_f * dtv_t[h:h + 1, :] + l_b * dtv_t[D_HEADS + h:D_HEADS + h + 1, :])
            xh = xs[:, h * D_HEADDIM:(h + 1) * D_HEADDIM]
            y = _dot(mix.astype(BF16), xh.astype(BF16))
            state = hf_ref[0, h]
            y = y + _dot(cg, state.astype(BF16)) * jnp.exp(cf)
            y = y + dvec_ref[:, h * D_HEADDIM:(h + 1) * D_HEADDIM] * xh
            ys.append(y)
            last = cum[Q - 1:Q, h:h + 1]
            wgt = jnp.exp(last - cf) * dtv[:, h:h + 1]
            st = _dot(bm_t[g * D_STATE:(g + 1) * D_STATE, :].astype(BF16), (xh * wgt).astype(BF16))
            hf_ref[0, h] = state * jnp.exp(last) + st
        y_s[pl.ds(base, Q), :] = jnp.concatenate(ys, axis=-1)
        return carry

    lax.fori_loop(0, nc, fwd_chunk, 0)

    def bwd_chunk(i, carry):
        c = nc - 1 - i
        base = pl.multiple_of(c * Q, Q)
        act = act_s[pl.ds(base, Q), :]
        cum = cum_s[pl.ds(base, Q), :]
        dtv = dtv_s[pl.ds(base, Q), :]
        xs = act[:, 0:256]
        bm_t = act[:, 256:384].T
        cm = act[:, 384:512]
        ys = []
        for h in range(D_HEADS):
            g = h // (D_HEADS // D_GROUPS)
            cg = cm[:, g * D_STATE:(g + 1) * D_STATE].astype(BF16)
            cb = cum[:, D_HEADS + h:D_HEADS + h + 1]
            xh = xs[:, h * D_HEADDIM:(h + 1) * D_HEADDIM]
            state = hb_ref[0, h]
            ys.append(_dot(cg, state.astype(BF16)) * jnp.exp(cb))
            first = cum[0:1, D_HEADS + h:D_HEADS + h + 1]
            wgt = jnp.exp(first - cb) * dtv[:, D_HEADS + h:D_HEADS + h + 1]
            st = _dot(bm_t[g * D_STATE:(g + 1) * D_STATE, :].astype(BF16), (xh * wgt).astype(BF16))
            hb_ref[0, h] = state * jnp.exp(first) + st
        y = y_s[pl.ds(base, Q), :] + jnp.concatenate(ys, axis=-1)
        gated = y * _silu(z_ref[pl.ds(base, Q), :])
        ms = jnp.mean(gated * gated, axis=-1, keepdims=True)
        out_ref[pl.ds(base, Q), :] = gated * lax.rsqrt(ms + EPS) * ng_ref[...]
        return carry

    lax.fori_loop(0, nc, bwd_chunk, 0)


def _ssd(latent, z, xbc, dt, consts, h0f, h0b, prev_out):
    bsz, seq, first = (DEC_BATCH, DEC_SEQ, N_CTX // DEC_SEQ) if latent else (BATCH, SEQ, 0)
    per_seq = lambda w: pl.BlockSpec((seq, w), lambda b: (first + b, 0))
    const_specs = [pl.BlockSpec(a.shape, lambda b, n=a.ndim: (0,) * n) for a in consts]
    st_spec = pl.BlockSpec((1, D_HEADS, D_STATE, D_HEADDIM), lambda b: (b, 0, 0, 0))
    st_shape = jax.ShapeDtypeStruct((bsz, D_HEADS, D_STATE, D_HEADDIM), F32)
    in_specs = ([per_seq(256), per_seq(512), per_seq(128)] + const_specs + [st_spec, st_spec]
                + [pl.BlockSpec(memory_space=pl.ANY)])
    args = [z, xbc, dt, *consts, h0f, h0b, prev_out]
    aliases = {len(args) - 1: 0}
    return pl.pallas_call(
        _ssd_kernel,
        out_shape=[jax.ShapeDtypeStruct((N_TOK, D_INNER), F32), st_shape, st_shape],
        grid=(bsz,),
        in_specs=in_specs,
        out_specs=[per_seq(D_INNER), st_spec, st_spec],
        scratch_shapes=[pltpu.VMEM((seq, CONV_DIM), F32), pltpu.VMEM((seq, LANES), F32),
                        pltpu.VMEM((seq, LANES), F32), pltpu.VMEM((seq, D_INNER), F32)],
        input_output_aliases=aliases,
        compiler_params=_params("parallel"),
        name="ssd_lat" if latent else "ssd_ctx",
    )(*args)


def _out_proj_kernel(x_ref, oa_ref, ob_ref, oc_ref, od_ref, mod_ref, wo_ref, g2_ref, rwh_ref, rwl_ref,
                     rb_ref, x1_ref, h2_ref, te_ref, gt_ref):
    m = mod_ref[0]
    gate1, sh2, sc2 = m[2:3], m[3:4], m[4:5]
    mixed = (_dot(oa_ref[...].astype(BF16), wo_ref[0:256, :])
             + _dot(ob_ref[...].astype(BF16), wo_ref[256:512, :])
             + _dot(oc_ref[...].astype(BF16), wo_ref[512:768, :])
             + _dot(od_ref[...].astype(BF16), wo_ref[768:1024, :]))
    x1 = x_ref[...] + gate1 * mixed
    x1_ref[...] = x1
    ms = jnp.mean(x1 * x1, axis=-1, keepdims=True)
    h2 = (x1 * lax.rsqrt(ms + EPS) * g2_ref[...]) * (1.0 + sc2) + sh2
    h2_ref[...] = h2.astype(BF16)

    hi = h2.astype(BF16)
    lo = (h2 - hi.astype(F32)).astype(BF16)
    logits = _dot(hi, rwh_ref[...]) + _dot(lo, rwh_ref[...]) + _dot(hi, rwl_ref[...]) + rb_ref[...]
    lane = lax.broadcasted_iota(jnp.int32, logits.shape, 1)
    vals, idxs = [], []
    for _ in range(TOP_K):
        mx = jnp.max(logits, axis=-1, keepdims=True)
        ix = jnp.min(jnp.where(logits == mx, lane, LANES), axis=-1, keepdims=True)
        vals.append(mx)
        idxs.append(ix)
        logits = jnp.where(lane == ix, -3e38, logits)
    es = [jnp.exp(v - vals[0]) for v in vals]
    den = es[0] + es[1] + es[2] + es[3]
    te = jnp.zeros(lane.shape, jnp.int32)
    gt = jnp.zeros(lane.shape, F32)
    for k in range(TOP_K):
        te = jnp.where(lane == k, idxs[k], te)
        gt = jnp.where(lane == k, es[k] / den, gt)
    te_ref[...] = te[:, 0:TOP_K]
    gt_ref[...] = gt[:, 0:TOP_K]


def _out_proj(x, oa, ob, oc, od, mod3, wo, g2, rwh, rwl, rb):
    row = lambda w: pl.BlockSpec((TM, w), lambda i: (i, 0))
    full = lambda a: pl.BlockSpec(a.shape, lambda i: (0,) * a.ndim)
    return pl.pallas_call(
        _out_proj_kernel,
        out_shape=[jax.ShapeDtypeStruct((N_TOK, D_MODEL), F32), jax.ShapeDtypeStruct((N_TOK, D_MODEL), BF16),
                   jax.ShapeDtypeStruct((N_TOK, TOP_K), jnp.int32), jax.ShapeDtypeStruct((N_TOK, TOP_K), F32)],
        grid=(N_TILES,),
        in_specs=[row(D_MODEL), row(256), row(256), row(256), row(256),
                  pl.BlockSpec((1, N_MOD, D_MODEL), lambda i: (_mod_row(i), 0, 0)),
                  full(wo), full(g2), full(rwh), full(rwl), full(rb)],
        out_specs=[row(D_MODEL), row(D_MODEL), row(TOP_K), row(TOP_K)],
        compiler_params=_params("parallel"),
        name="out_proj_router",
    )(x, oa, ob, oc, od, mod3, wo, g2, rwh, rwl, rb)


def _expert_kernel(be_ref, nu_ref, x_ref, wgu_ref, bgu_ref, wdn_ref, bdn_ref, o_ref, wgu_s, wdn_s):
    i = pl.program_id(0)
    used = i < nu_ref[0]
    prev = be_ref[jnp.maximum(i - 1, 0)]
    fresh = jnp.logical_or(i == 0, be_ref[i] != prev)

    @pl.when(jnp.logical_and(used, fresh))
    def _():
        wgu_s[...] = wgu_ref[0].astype(BF16)
        wdn_s[...] = wdn_ref[0].astype(BF16)

    @pl.when(used)
    def _():
        hgu = _dot(x_ref[...], wgu_s[...]) + bgu_ref[0]
        gate = jnp.minimum(hgu[:, :D_FF], SWIGLU_LIMIT)
        up = jnp.clip(hgu[:, D_FF:], -SWIGLU_LIMIT, SWIGLU_LIMIT)
        act = (up + 1.0) * gate * jax.nn.sigmoid(SWIGLU_ALPHA * gate)
        o_ref[...] = (_dot(act.astype(BF16), wdn_s[...]) + bdn_ref[0]).astype(BF16)

    @pl.when(jnp.logical_not(used))
    def _():
        o_ref[...] = jnp.zeros_like(o_ref)


def _experts(block_e, n_used, xb, w_gu, b_gu, w_dn, b_dn):
    grid_spec = pltpu.PrefetchScalarGridSpec(
        num_scalar_prefetch=2,
        grid=(MOE_BLOCKS,),
        in_specs=[pl.BlockSpec((MOE_BM, D_MODEL), lambda i, be, nu: (i, 0)),
                  pl.BlockSpec((1, D_MODEL, 2 * D_FF), lambda i, be, nu: (be[i], 0, 0)),
                  pl.BlockSpec((1, 1, 2 * D_FF), lambda i, be, nu: (be[i], 0, 0)),
                  pl.BlockSpec((1, D_FF, D_MODEL), lambda i, be, nu: (be[i], 0, 0)),
                  pl.BlockSpec((1, 1, D_MODEL), lambda i, be, nu: (be[i], 0, 0))],
        out_specs=pl.BlockSpec((MOE_BM, D_MODEL), lambda i, be, nu: (i, 0)),
        scratch_shapes=[pltpu.VMEM((D_MODEL, 2 * D_FF), BF16), pltpu.VMEM((D_FF, D_MODEL), BF16)],
    )
    return pl.pallas_call(
        _expert_kernel,
        out_shape=jax.ShapeDtypeStruct((YB_ROWS, D_MODEL), BF16),
        grid_spec=grid_spec,
        compiler_params=_params("arbitrary"),
        name="experts",
    )(block_e, n_used, xb, w_gu, b_gu.reshape(N_EXPERTS, 1, 2 * D_FF), w_dn,
      b_dn.reshape(N_EXPERTS, 1, D_MODEL))


def _combine_kernel(final, seg_ref, np_ref, x1_ref, te_ref, gt_ref, off_ref, mod_ref, fg_ref, rep_ref,
                    jc_ref, tri_ref, yb_ref, o_ref, buf, acc, sem):
    i = pl.program_id(0)

    def window_copies(p):
        copies = []
        for e in range(N_EXPERTS):
            start = (seg_ref[i * N_EXPERTS + e] // BF16_ROWS) * BF16_ROWS + p * SEG_ROWS
            start = pl.multiple_of(jnp.minimum(start, YB_ROWS - SEG_ROWS), BF16_ROWS)
            copies.append(pltpu.make_async_copy(yb_ref.at[pl.ds(start, SEG_ROWS)],
                                                buf.at[pl.ds(e * SEG_ROWS, SEG_ROWS)], sem))
        return copies

    te = te_ref[...]
    gt = gt_ref[...]
    lane = lax.broadcasted_iota(jnp.int32, (TM, LANES), 1)
    member = jnp.zeros((TM, LANES), F32)
    gate = jnp.zeros((TM, LANES), F32)
    for k in range(TOP_K):
        hit = lane == te[:, k:k + 1]
        member = jnp.where(hit, 1.0, member)
        gate = jnp.where(hit, gt[:, k:k + 1], gate)
    rank = _dot(tri_ref[...], member.astype(BF16))
    pos = rank + off_ref[0][0:1, :].astype(F32)
    gate_cols = _dot(gate.astype(BF16), rep_ref[...])
    acc[...] = jnp.zeros_like(acc)

    def one_pass(p, carry):
        copies = window_copies(p)
        for cp in copies:
            cp.start()
        pos_cols = _dot_split(pos - (p * SEG_ROWS).astype(F32), rep_ref[...])
        place = jnp.where(pos_cols == jc_ref[...], gate_cols, 0.0).astype(BF16)
        for cp in copies:
            cp.wait()
        acc[...] += _dot(place, buf[...])
        return carry

    lax.fori_loop(0, np_ref[i], one_pass, 0)
    x2 = x1_ref[...] + mod_ref[0][5:6] * acc[...]
    if final:
        ms = jnp.mean(x2 * x2, axis=-1, keepdims=True)
        x2 = x2 * lax.rsqrt(ms + EPS) * fg_ref[...]
    o_ref[...] = x2


def _combine(seg_start, n_pass, x1, te, gates, seg_off, mod3, fg, rep, jc, tri, yb, final):
    row = lambda w: pl.BlockSpec((TM, w), lambda i, s, n: (i, 0))
    full = lambda a: pl.BlockSpec(a.shape, lambda i, s, n: (0,) * a.ndim)
    grid_spec = pltpu.PrefetchScalarGridSpec(
        num_scalar_prefetch=2,
        grid=(N_TILES,),
        in_specs=[row(D_MODEL), row(TOP_K), row(TOP_K),
                  pl.BlockSpec((1, 8, LANES), lambda i, s, n: (i, 0, 0)),
                  pl.BlockSpec((1, N_MOD, D_MODEL), lambda i, s, n: (_mod_row(i), 0, 0)),
                  full(fg), full(rep), full(jc), full(tri),
                  pl.BlockSpec(memory_space=pl.ANY)],
        out_specs=row(D_MODEL),
        scratch_shapes=[pltpu.VMEM((SEG_COLS, D_MODEL), BF16), pltpu.VMEM((TM, D_MODEL), F32),
                        pltpu.SemaphoreType.DMA],
    )
    return pl.pallas_call(
        functools.partial(_combine_kernel, final),
        out_shape=jax.ShapeDtypeStruct((N_TOK, D_MODEL), F32),
        grid_spec=grid_spec,
        compiler_params=_params("arbitrary"),
        name="moe_combine",
    )(seg_start, n_pass, x1, te, gates, seg_off, mod3, fg, rep, jc, tri, yb)


def _w1_index():
    idx = np.full((W1_COLS,), IN_WIDTH, np.int32)
    for g in range(2):
        for kv in range(2):
            idx[g * 128 + kv * 64:g * 128 + kv * 64 + 64] = np.arange(64) + kv * 128 + g * 64
    idx[256:512] = np.arange(256) + IN_OFF['a_k']
    idx[512:704] = np.arange(192) + IN_OFF['b_cq']
    idx[768:896] = np.arange(128) + IN_OFF['b_ckv']
    idx[896:928] = np.arange(32) + IN_OFF['b_kr']
    idx[1024:1792] = np.arange(768) + IN_OFF['c_q']
    idx[1792:2048] = np.arange(256) + IN_OFF['d_z']
    idx[2048:2560] = np.arange(512) + IN_OFF['d_xbc']
    idx[2560:2568] = np.arange(8) + IN_OFF['d_dtf']
    return idx


def _wout_a_index():
    idx = np.zeros((256,), np.int32)
    for g in range(2):
        for kv in range(2):
            idx[g * 128 + kv * 64:g * 128 + kv * 64 + 64] = np.arange(64) + kv * 128 + g * 64
    return idx


def _rope_tables():
    t = np.arange(DEC_SEQ)
    pos = ((t // GRID_W).astype(np.float32), (t % GRID_W).astype(np.float32))

    def unit(rot_dim):
        quarter, half = rot_dim // 4, rot_dim // 2
        inv = ROPE_THETA ** (-np.arange(0, half, 2, dtype=np.float32) / half)
        cos = np.zeros((DEC_SEQ, rot_dim), np.float32)
        sin = np.zeros((DEC_SEQ, rot_dim), np.float32)
        for seg in range(4):
            ang = pos[seg // 2][:, None] * inv[None, :].astype(np.float32)
            cos[:, seg * quarter:(seg + 1) * quarter] = np.cos(ang)
            sin[:, seg * quarter:(seg + 1) * quarter] = np.sin(ang) * (-1.0 if seg % 2 == 0 else 1.0)
        return cos, sin

    specs = [
        (HEAD_DIM, (0, 64), (), HEAD_DIM ** -0.5),
        (HEAD_DIM, (0, 64), (), 1.0),
        (MLA_ROPE, (MLA_NOPE,), (0, MLA_NOPE), (MLA_NOPE + MLA_ROPE) ** -0.5),
        (MLA_ROPE, (0,), (), 1.0),
        (DIFF_D, (0, 32, 64, 96), (), DIFF_D ** -0.5),
        (DIFF_D, (0, 32, 64, 96), (), 1.0),
    ]
    lat_cols, ident_cols = [], []
    for rot_dim, starts, passthrough, scale in specs:
        ucos, usin = unit(rot_dim)
        cos = np.zeros((DEC_SEQ, LANES), np.float32)
        sin = np.zeros((DEC_SEQ, LANES), np.float32)
        ident = np.zeros((1, LANES), np.float32)
        if passthrough:
            cos[:, passthrough[0]:passthrough[1]] = 1.0
            ident[:, passthrough[0]:passthrough[1]] = 1.0
        for s in starts:
            cos[:, s:s + rot_dim] = ucos
            sin[:, s:s + rot_dim] = usin
            ident[:, s:s + rot_dim] = 1.0
        lat_cols += [cos * scale, sin * scale]
        ident_cols += [ident * scale, np.zeros((1, LANES), np.float32)]
    lat = np.concatenate(lat_cols, axis=1)
    ident_blk = np.broadcast_to(np.concatenate(ident_cols, axis=1), (TM, N_TAB * LANES))
    return np.concatenate([ident_blk, lat], axis=0).astype(np.float32)


def _block_diag_ones(n, blk):
    r = np.arange(n)
    return (r[:, None] // blk == r[None, :] // blk).astype(np.float32)


def _combine_constants():
    col = np.arange(SEG_COLS)
    rep = (np.arange(LANES)[:, None] == (col // SEG_ROWS)[None, :]).astype(np.float32)
    jc = (col % SEG_ROWS).astype(np.float32)[None, :]
    r = np.arange(TM)
    tri = (r[None, :] < r[:, None]).astype(np.float32)
    return jnp.asarray(rep, BF16), jnp.asarray(jc), jnp.asarray(tri, BF16)


def kernel(x_prompt, x_sample, cache_gqa_k, cache_gqa_v, cache_mla_ckv, cache_mla_krope, cache_diff_k, cache_diff_v, state_ssd_fwd, state_ssd_bwd, c, c_ctx, norm1_g, norm2_g, w_ada, b_ada, w_in, w_out, gqa_qn_g, gqa_kn_g, mla_qa_g, mla_wqb, mla_kva_g, mla_wkvb, diff_lq1, diff_lk1, diff_lq2, diff_lk2, diff_subln_g, ssd_conv_w, ssd_conv_b, ssd_a_log_f, ssd_a_log_b, ssd_dt_bias_f, ssd_dt_bias_b, ssd_d, ssd_norm_g, router_w, router_b, moe_w_gu, moe_b_gu, moe_w_dn, moe_b_dn, final_g):
    tabs = jnp.asarray(_rope_tables())
    bd = jnp.asarray(_block_diag_ones(256, HEAD_DIM), BF16)
    rep, jc, tri = _combine_constants()
    w1_idx = _w1_index()
    woa_idx = _wout_a_index()

    cvec = jnp.zeros((MOD_ROWS, D_MODEL), F32).at[0].set(c_ctx).at[1:1 + DEC_BATCH].set(c)
    mod = _modulation(cvec, w_ada, b_ada).reshape(DEPTH, MOD_ROWS, N_MOD, D_MODEL)

    x = jnp.concatenate([x_prompt.reshape(N_CTX, D_MODEL), x_sample.reshape(N_LAT, D_MODEL)], axis=0)
    new_ctx = []
    for l in range(DEPTH):
        mod3 = mod[l]
        w1 = jnp.concatenate([w_in[l], jnp.zeros((D_MODEL, 1), F32)], axis=1)[:, w1_idx].astype(BF16)
        wqb = mla_wqb[l].reshape(Q_LORA, B_HEADS, MLA_NOPE + MLA_ROPE)
        wqb = jnp.pad(wqb, ((0, 256 - Q_LORA), (0, 0), (0, LANES - MLA_NOPE - MLA_ROPE)))
        wqb = wqb.reshape(256, B_HEADS * LANES).astype(BF16)
        wkvb = mla_wkvb[l].reshape(KV_LORA, B_HEADS, MLA_NOPE + MLA_V)
        wk_nope = jnp.pad(wkvb[:, :, :MLA_NOPE], ((0, 0), (0, 0), (0, LANES - MLA_NOPE)))
        eye_r = jnp.zeros((LANES, B_HEADS, LANES), F32)
        eye_r = eye_r.at[jnp.arange(MLA_ROPE), :, MLA_NOPE + jnp.arange(MLA_ROPE)].set(1.0)
        wk = jnp.concatenate([wk_nope, eye_r], axis=0).reshape(256, B_HEADS * LANES).astype(BF16)
        wv = wkvb[:, :, MLA_NOPE:].reshape(KV_LORA, B_HEADS * MLA_V).astype(BF16)
        wo = jnp.concatenate([w_out[l][woa_idx], w_out[l][256:]], axis=0).astype(BF16)
        gq = jnp.tile(gqa_qn_g[l], 4)[None, :]
        gk = jnp.tile(gqa_kn_g[l], 2)[None, :]
        gqa = jnp.pad(mla_qa_g[l], (0, 256 - Q_LORA))[None, :]
        gkva = mla_kva_g[l][None, :]
        rw = jnp.pad(router_w[l], ((0, 0), (0, LANES - N_EXPERTS)))
        rwh = rw.astype(BF16)
        rwl = (rw - rwh.astype(F32)).astype(BF16)
        rb = jnp.pad(router_b[l], (0, LANES - N_EXPERTS), constant_values=NEG_BIG)[None, :]

        qa, kva, qb, ck, qc, kc, vc, z, xbc, dt = _in_proj(
            x, mod3, norm1_g[l][None, :], w1, bd, gq, gk, gqa, gkva, wqb, tabs)

        past_kva = jnp.concatenate([cache_gqa_k[:, l].reshape(DEC_BATCH, PAST_LEN, 128),
                                    cache_gqa_v[:, l].reshape(DEC_BATCH, PAST_LEN, 128)], axis=-1)
        past_ck = jnp.concatenate([cache_mla_ckv[:, l], cache_mla_krope[:, l],
                                   jnp.zeros((DEC_BATCH, PAST_LEN, LANES - MLA_ROPE), F32)], axis=-1)
        past_kc = cache_diff_k[:, l].reshape(DEC_BATCH, PAST_LEN, 256)
        past_vc = cache_diff_v[:, l].reshape(DEC_BATCH, PAST_LEN, 256)
        lams = [a[l][None, :] for a in (diff_lq1, diff_lk1, diff_lq2, diff_lk2)]
        gsub = jnp.tile(diff_subln_g[l], 2)[None, :]
        lam_init = 0.8 - 0.6 * math.exp(-0.3 * l)
        oa, ob, oc = _attention_mixers(qa, kva, qb, ck, qc, kc, vc, past_kva, past_ck, past_kc, past_vc,
                                       wk, wv, lams, gsub, lam_init)

        pad8 = lambda f, b: jnp.pad(jnp.concatenate([f, b]), (0, LANES - 2 * D_HEADS))[None, :]
        ssd_consts = (ssd_conv_w[l], ssd_conv_b[l][None, :], pad8(ssd_dt_bias_f[l], ssd_dt_bias_b[l]),
                      pad8(ssd_a_log_f[l], ssd_a_log_b[l]), jnp.repeat(ssd_d[l], D_HEADDIM)[None, :],
                      ssd_norm_g[l][None, :])
        zeros_st = jnp.zeros((BATCH, D_HEADS, D_STATE, D_HEADDIM), F32)
        od, hf_c, hb_c = _ssd(False, z, xbc, dt, ssd_consts, zeros_st, zeros_st, jnp.zeros((N_TOK, D_INNER), F32))
        od, _, _ = _ssd(True, z, xbc, dt, ssd_consts, jnp.swapaxes(state_ssd_fwd[:, l], -1, -2),
                        jnp.swapaxes(state_ssd_bwd[:, l], -1, -2), od)

        x1, h2, te, gates = _out_proj(x, oa, ob, oc, od, mod3, wo, norm2_g[l][None, :], rwh, rwl, rb)

        flat_e = te.reshape(-1)
        onehot = (flat_e[:, None] == jnp.arange(N_EXPERTS, dtype=jnp.int32)[None, :]).astype(jnp.int32)
        csum = jnp.cumsum(onehot, axis=0)
        rank = jnp.take_along_axis(csum, flat_e[:, None], axis=1)[:, 0] - 1
        counts = csum[-1]
        padded = (counts + MOE_BM - 1) // MOE_BM * MOE_BM
        pad_end = jnp.cumsum(padded)
        pad_start = pad_end - padded
        dest = pad_start[flat_e] + rank
        tok = jnp.arange(N_PAIRS, dtype=jnp.int32) // TOP_K
        rows = jnp.zeros((YB_ROWS,), jnp.int32).at[dest].set(tok)
        n_used = (pad_end[-1] // MOE_BM).astype(jnp.int32).reshape(1)
        blk_start = jnp.arange(MOE_BLOCKS, dtype=jnp.int32) * MOE_BM
        block_e = jnp.minimum(jnp.sum((pad_end[None, :] <= blk_start[:, None]).astype(jnp.int32), axis=1),
                              N_EXPERTS - 1).astype(jnp.int32)
        before = jnp.concatenate([jnp.zeros((1, N_EXPERTS), jnp.int32),
                                  csum[TM * TOP_K - 1::TM * TOP_K]], axis=0)
        seg_start = (pad_start[None, :] + before[:-1]).astype(jnp.int32)
        seg_cnt = before[1:] - before[:-1]
        seg_off = seg_start % BF16_ROWS
        n_pass = jnp.maximum(jnp.max((seg_cnt + seg_off + SEG_ROWS - 1) // SEG_ROWS, axis=1), 1).astype(jnp.int32)
        seg_off = jnp.broadcast_to(jnp.pad(seg_off, ((0, 0), (0, LANES - N_EXPERTS)))[:, None, :],
                                   (N_TILES, 8, LANES)).astype(jnp.int32)

        xb = h2[rows]
        yb = _experts(block_e, n_used, xb, moe_w_gu[l], moe_b_gu[l], moe_w_dn[l], moe_b_dn[l])
        x = _combine(seg_start.reshape(-1), n_pass, x1, te, gates, seg_off, mod3, final_g[None, :],
                     rep, jc, tri, yb, l == DEPTH - 1)

        kva_c, ck_c = kva[:N_CTX], ck[:N_CTX]
        new_ctx.append((kva_c[:, 0:128].reshape(BATCH, SEQ, A_KV_HEADS, HEAD_DIM),
                        kva_c[:, 128:256].reshape(BATCH, SEQ, A_KV_HEADS, HEAD_DIM),
                        ck_c[:, 0:KV_LORA].reshape(BATCH, SEQ, KV_LORA),
                        ck_c[:, KV_LORA:KV_LORA + MLA_ROPE].reshape(BATCH, SEQ, MLA_ROPE),
                        kc[:N_CTX].reshape(BATCH, SEQ, C_HEADS, 2 * DIFF_D),
                        vc[:N_CTX].reshape(BATCH, SEQ, C_HEADS, DIFF_V),
                        jnp.swapaxes(hf_c, -1, -2), jnp.swapaxes(hb_c, -1, -2)))

    y_prompt = x[:N_CTX].reshape(BATCH, SEQ, D_MODEL)
    y_sample = x[N_CTX:].reshape(DEC_BATCH, DEC_SEQ, D_MODEL)
    caches = [jnp.stack([cl[i] for cl in new_ctx], axis=1) for i in range(8)]
    return (y_prompt, y_sample, *caches)
```

```python
import functools
import math

import numpy as np
import jax
import jax.numpy as jnp
from jax import lax
from jax.experimental import pallas as pl
from jax.experimental.pallas import tpu as pltpu

F32 = jnp.float32
BF16 = jnp.bfloat16

D_MODEL = 1024
BATCH = 16
SEQ = 256
DEPTH = 2
DEC_BATCH = 8
DEC_SEQ = 2048
PAST_LEN = 256
GRID_W = 64
ROPE_THETA = 10000.0
EPS = 1e-6
HEAD_DIM = 64
A_HEADS = 4
A_KV_HEADS = 2
B_HEADS = 4
MLA_NOPE = 64
MLA_ROPE = 32
MLA_V = 64
Q_LORA = 192
KV_LORA = 128
C_HEADS = 4
DIFF_D = 32
DIFF_V = 64
SUBLN_EPS = 1e-5
D_HEADS = 4
D_HEADDIM = 64
D_INNER = 256
D_GROUPS = 2
D_STATE = 64
CONV_DIM = 512
SSD_CHUNK = 128
N_EXPERTS = 32
TOP_K = 4
D_FF = 1024
SWIGLU_ALPHA = 1.702
SWIGLU_LIMIT = 7.0
N_MOD = 6

N_CTX = BATCH * SEQ
N_LAT = DEC_BATCH * DEC_SEQ
N_TOK = N_CTX + N_LAT

LANES = 128
BF16_ROWS = 16
TM = 256
N_TILES = N_TOK // TM
CTX_TILES = N_CTX // TM
LAT_TILES_PER_SEQ = DEC_SEQ // TM
MOD_ROWS = 16
MOE_BM = 256
N_PAIRS = N_TOK * TOP_K
SEG_ALIGN = 8
DISP_ROWS = 64
DISP_COLS = N_EXPERTS * DISP_ROWS
MOE_BLOCKS = -(-(N_PAIRS + N_TILES * N_EXPERTS * (SEG_ALIGN - 1) + N_EXPERTS * (DISP_ROWS + MOE_BM - 1))
               // MOE_BM) + 1
YB_ROWS = MOE_BLOCKS * MOE_BM
SEG_ROWS = 80
SEG_COLS = N_EXPERTS * SEG_ROWS
VMEM_LIMIT = 56 * 1024 * 1024
NEG_BIG = -1e30

W1_COLS = 2688
IN_OFF = dict(a_q=0, a_k=256, a_v=384, b_cq=512, b_ckv=704, b_kr=832, c_q=864, c_k=1120,
              c_v=1376, d_z=1632, d_xbc=1888, d_dtf=2400, d_dtb=2404)
IN_WIDTH = 2408
N_TAB = 12


def _mod_row(i):
    return jnp.where(i < CTX_TILES, 0, 1 + (i - CTX_TILES) // LAT_TILES_PER_SEQ)


def _tab_block(i):
    return jnp.where(i < CTX_TILES, 0, 1 + (i - CTX_TILES) % LAT_TILES_PER_SEQ)


def _dot(a, b):
    return jnp.dot(a, b, preferred_element_type=F32)


def _dot_nt(a, b):
    return lax.dot_general(a, b, (((1,), (1,)), ((), ())), preferred_element_type=F32)


def _dot_split(x, m):
    hi = x.astype(BF16)
    lo = (x - hi.astype(F32)).astype(BF16)
    return _dot(hi, m) + _dot(lo, m)


def _dot_split_left(m, x):
    hi = x.astype(BF16)
    lo = (x - hi.astype(F32)).astype(BF16)
    return _dot(m, hi) + _dot(m, lo)


def _silu(x):
    return x * jax.nn.sigmoid(x)


def _params(*semantics):
    return pltpu.CompilerParams(dimension_semantics=semantics, vmem_limit_bytes=VMEM_LIMIT)


MOD_TN = 1536


def _mod_kernel(c_ref, w_ref, b_ref, o_ref):
    c = c_ref[...]
    s = _silu(c).astype(BF16)
    o_ref[0] = _dot(s, w_ref[0].astype(BF16)) + b_ref[0]


def _modulation(cvec, w_ada, b_ada):
    n = N_MOD * D_MODEL
    return pl.pallas_call(
        _mod_kernel,
        out_shape=jax.ShapeDtypeStruct((DEPTH, MOD_ROWS, n), F32),
        grid=(DEPTH, n // MOD_TN),
        in_specs=[pl.BlockSpec((MOD_ROWS, D_MODEL), lambda l, j: (0, 0)),
                  pl.BlockSpec((1, D_MODEL, MOD_TN), lambda l, j: (l, 0, j)),
                  pl.BlockSpec((1, 1, MOD_TN), lambda l, j: (l, 0, j))],
        out_specs=pl.BlockSpec((1, MOD_ROWS, MOD_TN), lambda l, j: (l, 0, j)),
        compiler_params=_params("parallel", "parallel"),
        name="adaln_mod",
    )(cvec, w_ada, b_ada.reshape(DEPTH, 1, n))


def _rope(x, cos, sin, quarter):
    lane = lax.broadcasted_iota(jnp.int32, (x.shape[0], LANES), 1)
    first = (lane // quarter) % 2 == 0
    outs = []
    for t in range(x.shape[1] // LANES):
        xt = x[:, t * LANES:(t + 1) * LANES]
        partner = jnp.where(first, pltpu.roll(xt, LANES - quarter, 1), pltpu.roll(xt, quarter, 1))
        outs.append(xt * cos + partner * sin)
    return outs


def _in_proj_kernel(x_ref, mod_ref, g1_ref, w1_ref, bd_ref, gq_ref, gk_ref, gqa_ref, gkva_ref,
                    wqb_ref, tab_ref,
                    qa_ref, kva_ref, qb_ref, ck_ref, qc_ref, kc_ref, vc_ref, z_ref, xbc_ref, dt_ref):
    x = x_ref[...]
    m = mod_ref[0]
    sh1, sc1 = m[0:1], m[1:2]
    ms = jnp.mean(x * x, axis=-1, keepdims=True)
    h = (x * lax.rsqrt(ms + EPS) * g1_ref[...]) * (1.0 + sc1) + sh1
    u = _dot(h.astype(BF16), w1_ref[...])

    def tab(k):
        return tab_ref[:, k * LANES:(k + 1) * LANES]

    bd = bd_ref[...]

    def head_norm(v, gain):
        w = v.shape[1]
        ss = _dot_split(v * v, bd[:w, :w])
        return v * lax.rsqrt(ss * (1.0 / HEAD_DIM) + EPS) * gain

    qa = _rope(head_norm(u[:, 0:256], gq_ref[...]), tab(0), tab(1), HEAD_DIM // 4)
    for t in range(2):
        qa_ref[:, t * LANES:(t + 1) * LANES] = qa[t].astype(BF16)
    ka = _rope(head_norm(u[:, 256:384], gk_ref[...]), tab(2), tab(3), HEAD_DIM // 4)
    kva_ref[:, 0:128] = ka[0]
    kva_ref[:, 128:256] = u[:, 384:512]

    cq = u[:, 512:768]
    msq = jnp.sum(cq * cq, axis=-1, keepdims=True) * (1.0 / Q_LORA)
    yq = cq * lax.rsqrt(msq + EPS) * gqa_ref[...]
    qb = _rope(_dot(yq.astype(BF16), wqb_ref[...]), tab(4), tab(5), MLA_ROPE // 4)
    for t in range(4):
        qb_ref[:, t * LANES:(t + 1) * LANES] = qb[t].astype(BF16)
    ckv = u[:, 768:896]
    msk = jnp.mean(ckv * ckv, axis=-1, keepdims=True)
    ck_ref[:, 0:128] = ckv * lax.rsqrt(msk + EPS) * gkva_ref[...]
    ck_ref[:, 128:256] = _rope(u[:, 896:1024], tab(6), tab(7), MLA_ROPE // 4)[0]

    qc = _rope(u[:, 1024:1280], tab(8), tab(9), DIFF_D // 4)
    kc = _rope(u[:, 1280:1536], tab(10), tab(11), DIFF_D // 4)
    for t in range(2):
        qc_ref[:, t * LANES:(t + 1) * LANES] = qc[t].astype(BF16)
        kc_ref[:, t * LANES:(t + 1) * LANES] = kc[t]
    vc_ref[...] = u[:, 1536:1792]

    z_ref[...] = u[:, 1792:2048]
    xbc_ref[...] = u[:, 2048:2560]
    dt_ref[...] = u[:, 2560:2688]


def _in_proj(x, mod3, g1, w1, bd, gq, gk, gqa, gkva, wqb, tabs):
    row = lambda w: pl.BlockSpec((TM, w), lambda i: (i, 0))
    full = lambda a: pl.BlockSpec(a.shape, lambda i: (0,) * a.ndim)
    outs = [(256, BF16), (256, F32), (512, BF16), (256, F32), (256, BF16), (256, F32), (256, F32),
            (256, F32), (512, F32), (128, F32)]
    return pl.pallas_call(
        _in_proj_kernel,
        out_shape=[jax.ShapeDtypeStruct((N_TOK, w), d) for w, d in outs],
        grid=(N_TILES,),
        in_specs=[row(D_MODEL),
                  pl.BlockSpec((1, N_MOD, D_MODEL), lambda i: (_mod_row(i), 0, 0)),
                  full(g1), full(w1), full(bd), full(gq), full(gk), full(gqa), full(gkva), full(wqb),
                  pl.BlockSpec((TM, N_TAB * LANES), lambda i: (_tab_block(i), 0))],
        out_specs=[row(w) for w, _ in outs],
        compiler_params=_params("parallel"),
        name="in_proj",
    )(x, mod3, g1, w1, bd, gq, gk, gqa, gkva, wqb, tabs)


def _softmax_parts(s):
    m = jnp.max(s, axis=-1, keepdims=True)
    e = jnp.exp(s - m)
    return e, jnp.sum(e, axis=-1, keepdims=True)


def _half_mask(rows):
    lane = lax.broadcasted_iota(jnp.int32, (rows, LANES), 1)
    return lane < (LANES // 2)


def _keys(past_ref, new_ref, lo, hi):
    new = new_ref[:, lo:hi].astype(BF16)
    if past_ref is None:
        return new
    return jnp.concatenate([past_ref[0, :, lo:hi].astype(BF16), new], axis=0)


def _attn_a_kernel(latent, *refs):
    if latent:
        q_ref, kv_ref, past_ref, _, o_ref = refs
    else:
        (q_ref, kv_ref, _, o_ref), past_ref = refs, None
    q = q_ref[...]
    k = _keys(past_ref, kv_ref, 0, 128)
    v = _keys(past_ref, kv_ref, 128, 256)
    lo = _half_mask(q.shape[0])
    for g in range(2):
        qt = q[:, g * LANES:(g + 1) * LANES].astype(F32)
        res = []
        for half in range(2):
            qm = jnp.where(lo, qt, 0.0) if half == 0 else jnp.where(lo, 0.0, qt)
            e, l = _softmax_parts(_dot_nt(qm.astype(BF16), k))
            res.append(_dot(e.astype(BF16), v) / l)
        o_ref[:, g * LANES:(g + 1) * LANES] = jnp.where(lo, res[0], res[1])


def _attn_b_kernel(latent, *refs):
    if latent:
        q_ref, ck_ref, past_ref, wk_ref, wv_ref, _, o_ref, k_s, v_s = refs
    else:
        (q_ref, ck_ref, wk_ref, wv_ref, _, o_ref, k_s, v_s), past_ref = refs, None

    @pl.when(pl.program_id(1) == 0)
    def _():
        ck = _keys(past_ref, ck_ref, 0, 256)
        k_s[...] = _dot(ck, wk_ref[...]).astype(BF16)
        v_s[...] = _dot(ck[:, 0:128], wv_ref[...]).astype(BF16)

    q = q_ref[...]
    lo = _half_mask(q.shape[0])
    for j in range(2):
        v = v_s[:, j * LANES:(j + 1) * LANES]
        res = []
        for half in range(2):
            h = 2 * j + half
            e, l = _softmax_parts(_dot_nt(q[:, h * LANES:(h + 1) * LANES], k_s[:, h * LANES:(h + 1) * LANES]))
            res.append(_dot(e.astype(BF16), v) / l)
        o_ref[:, j * LANES:(j + 1) * LANES] = jnp.where(lo, res[0], res[1])


def _attn_c_kernel(lam_init, latent, *refs):
    if latent:
        q_ref, k_ref, v_ref, pk_ref, pv_ref, lq1_ref, lk1_ref, lq2_ref, lk2_ref, g_ref, _, o_ref = refs
    else:
        (q_ref, k_ref, v_ref, lq1_ref, lk1_ref, lq2_ref, lk2_ref, g_ref, _, o_ref), pk_ref, pv_ref = refs, None, None
    lam = (jnp.exp(jnp.sum(lq1_ref[...] * lk1_ref[...], axis=-1, keepdims=True))
           - jnp.exp(jnp.sum(lq2_ref[...] * lk2_ref[...], axis=-1, keepdims=True)) + lam_init)
    q = q_ref[...]
    rows = q.shape[0]
    lane = lax.broadcasted_iota(jnp.int32, (rows, LANES), 1)
    lo = lane < (LANES // 2)
    for j in range(2):
        qt = q[:, j * LANES:(j + 1) * LANES].astype(F32)
        k = _keys(pk_ref, k_ref, j * LANES, (j + 1) * LANES)
        v = _keys(pv_ref, v_ref, j * LANES, (j + 1) * LANES)
        res = []
        for half in range(2):
            p = []
            for t in range(2):
                quarter = 2 * half + t
                qm = jnp.where(lane // (LANES // 4) == quarter, qt, 0.0)
                e, l = _softmax_parts(_dot_nt(qm.astype(BF16), k))
                p.append((e, l))
            pd = p[0][0] * (1.0 / p[0][1]) - p[1][0] * (lam / p[1][1])
            res.append(_dot(pd.astype(BF16), v))
        o = jnp.where(lo, res[0], res[1])
        o2 = o * o
        ss_lo = jnp.sum(jnp.where(lo, o2, 0.0), axis=-1, keepdims=True)
        ss_hi = jnp.sum(jnp.where(lo, 0.0, o2), axis=-1, keepdims=True)
        ss = jnp.where(lo, ss_lo, ss_hi) * (1.0 / DIFF_V)
        o_ref[:, j * LANES:(j + 1) * LANES] = (o * lax.rsqrt(ss + SUBLN_EPS) * g_ref[...]) * (1.0 - lam_init)


def _seq_call(body, name, latent, q, news, pasts, consts, prev_out, scratch=(), q_semantics="parallel"):
    const_specs = [pl.BlockSpec(a.shape, lambda b, i, n=a.ndim: (0,) * n) for a in consts]
    if latent:
        tile = lambda b, i: (CTX_TILES + b * LAT_TILES_PER_SEQ + i, 0)
        grid = (DEC_BATCH, LAT_TILES_PER_SEQ)
        new_specs = [pl.BlockSpec((DEC_SEQ, a.shape[1]), lambda b, i: (N_CTX // DEC_SEQ + b, 0)) for a in news]
        past_specs = [pl.BlockSpec((1, PAST_LEN, a.shape[2]), lambda b, i: (b, 0, 0)) for a in pasts]
    else:
        tile = lambda b, i: (b, 0)
        grid = (BATCH, 1)
        new_specs = [pl.BlockSpec((SEQ, a.shape[1]), tile) for a in news]
        past_specs, pasts = [], ()
    in_specs = ([pl.BlockSpec((TM, q.shape[1]), tile)] + new_specs + past_specs + const_specs
                + [pl.BlockSpec(memory_space=pl.ANY)])
    args = [q, *news, *pasts, *consts, prev_out]
    return pl.pallas_call(
        functools.partial(body, latent),
        out_shape=jax.ShapeDtypeStruct((N_TOK, 256), F32),
        grid=grid,
        in_specs=in_specs,
        out_specs=pl.BlockSpec((TM, 256), tile),
        scratch_shapes=list(scratch),
        input_output_aliases={len(args) - 1: 0},
        compiler_params=_params("parallel", q_semantics),
        name=name + ("_lat" if latent else "_ctx"),
    )(*args)


def _attention_mixers(qa, kva, qb, ck, qc, kc, vc, past_kva, past_ck, past_kc, past_vc, wk, wv,
                      lams, gsub, lam_init):
    def both(body, name, q, news, pasts, consts, scratch_fn=None, q_semantics="parallel"):
        sc = (lambda lk: ()) if scratch_fn is None else scratch_fn
        out = jnp.zeros((N_TOK, 256), F32)
        out = _seq_call(body, name, False, q, news, (), consts, out, sc(SEQ), q_semantics)
        return _seq_call(body, name, True, q, news, pasts, consts, out, sc(PAST_LEN + DEC_SEQ), q_semantics)

    oa = both(_attn_a_kernel, "attn_gqa", qa, [kva], [past_kva], [])
    ob = both(_attn_b_kernel, "attn_mla", qb, [ck], [past_ck], [wk, wv],
              lambda lk: (pltpu.VMEM((lk, 512), BF16), pltpu.VMEM((lk, 256), BF16)), "arbitrary")
    oc = both(functools.partial(_attn_c_kernel, lam_init), "attn_diff", qc, [kc, vc],
              [past_kc, past_vc], [*lams, gsub])
    return oa, ob, oc


Q = SSD_CHUNK


def _ssd_kernel(z_ref, xbc_ref, dt_ref, cw_ref, cb_ref, dtb_ref, alog_ref, dvec_ref, ng_ref, h0f_ref, h0b_ref, _,
                out_ref, hf_ref, hb_ref, act_s, cum_s, dtv_s, y_s):
    seq = z_ref.shape[0]
    nc = seq // Q
    row = lax.broadcasted_iota(jnp.int32, (Q, Q), 0)
    col = lax.broadcasted_iota(jnp.int32, (Q, Q), 1)
    lower = row >= col
    upper = row <= col
    tril = jnp.where(lower, 1.0, 0.0).astype(BF16)
    triu = jnp.where(upper, 1.0, 0.0).astype(BF16)
    rowc = lax.broadcasted_iota(jnp.int32, (Q, CONV_DIM), 0)
    lane = lax.broadcasted_iota(jnp.int32, (Q, LANES), 1)
    a_neg = -jnp.exp(alog_ref[...])
    cw = cw_ref[...]
    hf_ref[0] = h0f_ref[0]
    hb_ref[0] = h0b_ref[0]

    def fwd_chunk(c, carry):
        base = pl.multiple_of(c * Q, Q)
        x0 = xbc_ref[pl.ds(base, Q), :]
        prev = xbc_ref[pl.ds(pl.multiple_of(jnp.maximum(base - 8, 0), 8), 8), :][7:8, :]
        nxt = xbc_ref[pl.ds(pl.multiple_of(jnp.minimum(base + Q, seq - 8), 8), 8), :][0:1, :]
        prev = jnp.where(c > 0, prev, 0.0)
        nxt = jnp.where(c < nc - 1, nxt, 0.0)
        xm1 = jnp.where(rowc == 0, prev, pltpu.roll(x0, 1, 0))
        xp1 = jnp.where(rowc == Q - 1, nxt, pltpu.roll(x0, Q - 1, 0))
        act = _silu(xm1 * cw[0:1] + x0 * cw[1:2] + xp1 * cw[2:3] + cb_ref[...])
        act_s[pl.ds(base, Q), :] = act
        xs = act[:, 0:256]
        bm = act[:, 256:384]
        cm = act[:, 384:512]

        dtr = dt_ref[pl.ds(base, Q), :] + dtb_ref[...]
        dtv = jnp.maximum(dtr, 0.0) + jnp.log1p(jnp.exp(-jnp.abs(dtr)))
        dta = dtv * a_neg
        cum = jnp.where(lane < D_HEADS, _dot_split_left(tril, dta), _dot_split_left(triu, dta))
        cum_s[pl.ds(base, Q), :] = cum
        dtv_s[pl.ds(base, Q), :] = dtv
        cum_t = cum.T
        dtv_t = dtv.T
        bm_t = bm.T
        ys = []
        for h in range(D_HEADS):
            g = h // (D_HEADS // D_GROUPS)
            cg = cm[:, g * D_STATE:(g + 1) * D_STATE].astype(BF16)
            cb_mat = _dot_nt(cg, bm[:, g * D_STATE:(g + 1) * D_STATE].astype(BF16))
            cf = cum[:, h:h + 1]
            cb = cum[:, D_HEADS + h:D_HEADS + h + 1]
            l_f = jnp.exp(jnp.where(lower, cf - cum_t[h:h + 1, :], NEG_BIG))
            l_b = jnp.exp(jnp.where(upper, cb - cum_t[D_HEADS + h:D_HEADS + h + 1, :], NEG_BIG))
            mix = cb_mat * (l_f * dtv_t[h:h + 1, :] + l_b * dtv_t[D_HEADS + h:D_HEADS + h + 1, :])
            xh = xs[:, h * D_HEADDIM:(h + 1) * D_HEADDIM]
            y = _dot(mix.astype(BF16), xh.astype(BF16))
            state = hf_ref[0, h]
            y = y + _dot(cg, state.astype(BF16)) * jnp.exp(cf)
            y = y + dvec_ref[:, h * D_HEADDIM:(h + 1) * D_HEADDIM] * xh
            ys.append(y)
            last = cum[Q - 1:Q, h:h + 1]
            wgt = jnp.exp(last - cf) * dtv[:, h:h + 1]
            st = _dot(bm_t[g * D_STATE:(g + 1) * D_STATE, :].astype(BF16), (xh * wgt).astype(BF16))
            hf_ref[0, h] = state * jnp.exp(last) + st
        y_s[pl.ds(base, Q), :] = jnp.concatenate(ys, axis=-1)
        return carry

    lax.fori_loop(0, nc, fwd_chunk, 0)

    def bwd_chunk(i, carry):
        c = nc - 1 - i
        base = pl.multiple_of(c * Q, Q)
        act = act_s[pl.ds(base, Q), :]
        cum = cum_s[pl.ds(base, Q), :]
        dtv = dtv_s[pl.ds(base, Q), :]
        xs = act[:, 0:256]
        bm_t = act[:, 256:384].T
        cm = act[:, 384:512]
        ys = []
        for h in range(D_HEADS):
            g = h // (D_HEADS // D_GROUPS)
            cg = cm[:, g * D_STATE:(g + 1) * D_STATE].astype(BF16)
            cb = cum[:, D_HEADS + h:D_HEADS + h + 1]
            xh = xs[:, h * D_HEADDIM:(h + 1) * D_HEADDIM]
            state = hb_ref[0, h]
            ys.append(_dot(cg, state.astype(BF16)) * jnp.exp(cb))
            first = cum[0:1, D_HEADS + h:D_HEADS + h + 1]
            wgt = jnp.exp(first - cb) * dtv[:, D_HEADS + h:D_HEADS + h + 1]
            st = _dot(bm_t[g * D_STATE:(g + 1) * D_STATE, :].astype(BF16), (xh * wgt).astype(BF16))
            hb_ref[0, h] = state * jnp.exp(first) + st
        y = y_s[pl.ds(base, Q), :] + jnp.concatenate(ys, axis=-1)
        gated = y * _silu(z_ref[pl.ds(base, Q), :])
        ms = jnp.mean(gated * gated, axis=-1, keepdims=True)
        out_ref[pl.ds(base, Q), :] = gated * lax.rsqrt(ms + EPS) * ng_ref[...]
        return carry

    lax.fori_loop(0, nc, bwd_chunk, 0)


def _ssd(latent, z, xbc, dt, consts, h0f, h0b, prev_out):
    bsz, seq, first = (DEC_BATCH, DEC_SEQ, N_CTX // DEC_SEQ) if latent else (BATCH, SEQ, 0)
    per_seq = lambda w: pl.BlockSpec((seq, w), lambda b: (first + b, 0))
    const_specs = [pl.BlockSpec(a.shape, lambda b, n=a.ndim: (0,) * n) for a in consts]
    st_spec = pl.BlockSpec((1, D_HEADS, D_STATE, D_HEADDIM), lambda b: (b, 0, 0, 0))
    st_shape = jax.ShapeDtypeStruct((bsz, D_HEADS, D_STATE, D_HEADDIM), F32)
    in_specs = ([per_seq(256), per_seq(512), per_seq(128)] + const_specs + [st_spec, st_spec]
                + [pl.BlockSpec(memory_space=pl.ANY)])
    args = [z, xbc, dt, *consts, h0f, h0b, prev_out]
    aliases = {len(args) - 1: 0}
    return pl.pallas_call(
        _ssd_kernel,
        out_shape=[jax.ShapeDtypeStruct((N_TOK, D_INNER), F32), st_shape, st_shape],
        grid=(bsz,),
        in_specs=in_specs,
        out_specs=[per_seq(D_INNER), st_spec, st_spec],
        scratch_shapes=[pltpu.VMEM((seq, CONV_DIM), F32), pltpu.VMEM((seq, LANES), F32),
                        pltpu.VMEM((seq, LANES), F32), pltpu.VMEM((seq, D_INNER), F32)],
        input_output_aliases=aliases,
        compiler_params=_params("parallel"),
        name="ssd_lat" if latent else "ssd_ctx",
    )(*args)


def _out_proj_kernel(x_ref, oa_ref, ob_ref, oc_ref, od_ref, mod_ref, wo_ref, g2_ref, rwh_ref, rwl_ref,
                     rb_ref, x1_ref, h2_ref, te_ref, gt_ref, cnt_ref):
    m = mod_ref[0]
    gate1, sh2, sc2 = m[2:3], m[3:4], m[4:5]
    mixed = (_dot(oa_ref[...].astype(BF16), wo_ref[0:256, :])
             + _dot(ob_ref[...].astype(BF16), wo_ref[256:512, :])
             + _dot(oc_ref[...].astype(BF16), wo_ref[512:768, :])
             + _dot(od_ref[...].astype(BF16), wo_ref[768:1024, :]))
    x1 = x_ref[...] + gate1 * mixed
    x1_ref[...] = x1
    ms = jnp.mean(x1 * x1, axis=-1, keepdims=True)
    h2 = (x1 * lax.rsqrt(ms + EPS) * g2_ref[...]) * (1.0 + sc2) + sh2
    h2_ref[...] = h2.astype(BF16)

    hi = h2.astype(BF16)
    lo = (h2 - hi.astype(F32)).astype(BF16)
    logits = _dot(hi, rwh_ref[...]) + _dot(lo, rwh_ref[...]) + _dot(hi, rwl_ref[...]) + rb_ref[...]
    lane = lax.broadcasted_iota(jnp.int32, logits.shape, 1)
    vals, idxs = [], []
    for _ in range(TOP_K):
        mx = jnp.max(logits, axis=-1, keepdims=True)
        ix = jnp.min(jnp.where(logits == mx, lane, LANES), axis=-1, keepdims=True)
        vals.append(mx)
        idxs.append(ix)
        logits = jnp.where(lane == ix, -3e38, logits)
    es = [jnp.exp(v - vals[0]) for v in vals]
    den = es[0] + es[1] + es[2] + es[3]
    te = jnp.zeros(lane.shape, jnp.int32)
    gt = jnp.zeros(lane.shape, F32)
    member = jnp.zeros(lane.shape, F32)
    for k in range(TOP_K):
        te = jnp.where(lane == k, idxs[k], te)
        gt = jnp.where(lane == k, es[k] / den, gt)
        member = jnp.where(lane == idxs[k], 1.0, member)
    te_ref[...] = te[:, 0:TOP_K]
    gt_ref[...] = gt[:, 0:TOP_K]
    count = jnp.sum(member, axis=0, keepdims=True)
    cnt_ref[0] = jnp.broadcast_to(count, (8, LANES)).astype(jnp.int32)


def _out_proj(x, oa, ob, oc, od, mod3, wo, g2, rwh, rwl, rb):
    row = lambda w: pl.BlockSpec((TM, w), lambda i: (i, 0))
    full = lambda a: pl.BlockSpec(a.shape, lambda i: (0,) * a.ndim)
    return pl.pallas_call(
        _out_proj_kernel,
        out_shape=[jax.ShapeDtypeStruct((N_TOK, D_MODEL), F32), jax.ShapeDtypeStruct((N_TOK, D_MODEL), BF16),
                   jax.ShapeDtypeStruct((N_TOK, TOP_K), jnp.int32), jax.ShapeDtypeStruct((N_TOK, TOP_K), F32),
                   jax.ShapeDtypeStruct((N_TILES, 8, LANES), jnp.int32)],
        grid=(N_TILES,),
        in_specs=[row(D_MODEL), row(256), row(256), row(256), row(256),
                  pl.BlockSpec((1, N_MOD, D_MODEL), lambda i: (_mod_row(i), 0, 0)),
                  full(wo), full(g2), full(rwh), full(rwl), full(rb)],
        out_specs=[row(D_MODEL), row(D_MODEL), row(TOP_K), row(TOP_K),
                   pl.BlockSpec((1, 8, LANES), lambda i: (i, 0, 0))],
        compiler_params=_params("parallel"),
        name="out_proj_router",
    )(x, oa, ob, oc, od, mod3, wo, g2, rwh, rwl, rb)


def _dispatch_kernel(seg_ref, cnt_ref, np_ref, h2_ref, te_ref, rept_ref, triu_ref, _, xb_ref, buf, sems, sem_x):
    i = pl.program_id(0)
    slot = i % 2
    te = te_ref[...]
    lane = lax.broadcasted_iota(jnp.int32, (TM, LANES), 1)
    member = jnp.zeros((TM, LANES), F32)
    for k in range(TOP_K):
        member = jnp.where(lane == te[:, k:k + 1], 1.0, member)
    mem_t = member.T.astype(BF16)
    rank_t = _dot(mem_t, triu_ref[...]).astype(BF16)
    mem_rows = _dot(rept_ref[...], mem_t)
    rank_rows = _dot(rept_ref[...], rank_t)
    row_in_seg = (lax.broadcasted_iota(jnp.int32, (DISP_COLS, TM), 0) % DISP_ROWS).astype(F32)
    h2 = h2_ref[...]

    def compact(p):
        pick = jnp.where(rank_rows - row_in_seg == p * DISP_ROWS, mem_rows, 0.0).astype(BF16)
        return _dot(pick, h2)

    def window_copy(src, e, p, sem):
        start = pl.multiple_of(seg_ref[i * N_EXPERTS + e] + p * DISP_ROWS, 8)
        return pltpu.make_async_copy(src.at[pl.ds(e * DISP_ROWS, DISP_ROWS)],
                                     xb_ref.at[pl.ds(start, DISP_ROWS)], sem)

    def wait_windows(s):
        for e in range(N_EXPERTS):
            window_copy(buf.at[s], e, 0, sems.at[s]).wait()

    buf[slot] = compact(0)

    @pl.when(i > 0)
    def _():
        wait_windows(1 - slot)

    for e in range(N_EXPERTS):
        window_copy(buf.at[slot], e, 0, sems.at[slot]).start()

    def extra_pass(p, carry):
        buf[1 - slot] = compact(p.astype(F32))
        for e in range(N_EXPERTS):
            @pl.when(cnt_ref[i * N_EXPERTS + e] > p * DISP_ROWS)
            def _():
                cp = window_copy(buf.at[1 - slot], e, p, sem_x)
                cp.start()
                cp.wait()
        return carry

    lax.fori_loop(1, np_ref[i], extra_pass, 0)

    @pl.when(i == N_TILES - 1)
    def _():
        wait_windows(slot)


def _dispatch(seg_start, seg_cnt, n_pass, h2, te, rept, triu, xb_init):
    row = lambda w: pl.BlockSpec((TM, w), lambda i, s, c, n: (i, 0))
    full = lambda a: pl.BlockSpec(a.shape, lambda i, s, c, n: (0,) * a.ndim)
    grid_spec = pltpu.PrefetchScalarGridSpec(
        num_scalar_prefetch=3,
        grid=(N_TILES,),
        in_specs=[row(D_MODEL), row(TOP_K), full(rept), full(triu), pl.BlockSpec(memory_space=pl.ANY)],
        out_specs=pl.BlockSpec(memory_space=pl.ANY),
        scratch_shapes=[pltpu.VMEM((2, DISP_COLS, D_MODEL), F32),
                        pltpu.SemaphoreType.DMA((2,)), pltpu.SemaphoreType.DMA],
    )
    return pl.pallas_call(
        _dispatch_kernel,
        out_shape=jax.ShapeDtypeStruct((YB_ROWS, D_MODEL), F32),
        grid_spec=grid_spec,
        input_output_aliases={7: 0},
        compiler_params=_params("arbitrary"),
        name="moe_dispatch",
    )(seg_start, seg_cnt, n_pass, h2, te, rept, triu, xb_init)


def _expert_kernel(be_ref, nu_ref, x_ref, wgu_ref, bgu_ref, wdn_ref, bdn_ref, o_ref, wgu_s, wdn_s):
    i = pl.program_id(0)
    used = i < nu_ref[0]
    prev = be_ref[jnp.maximum(i - 1, 0)]
    fresh = jnp.logical_or(i == 0, be_ref[i] != prev)

    @pl.when(jnp.logical_and(used, fresh))
    def _():
        wgu_s[...] = wgu_ref[0].astype(BF16)
        wdn_s[...] = wdn_ref[0].astype(BF16)

    @pl.when(used)
    def _():
        hgu = _dot(x_ref[...].astype(BF16), wgu_s[...]) + bgu_ref[0]
        gate = jnp.minimum(hgu[:, :D_FF], SWIGLU_LIMIT)
        up = jnp.clip(hgu[:, D_FF:], -SWIGLU_LIMIT, SWIGLU_LIMIT)
        act = (up + 1.0) * gate * jax.nn.sigmoid(SWIGLU_ALPHA * gate)
        o_ref[...] = (_dot(act.astype(BF16), wdn_s[...]) + bdn_ref[0]).astype(BF16)

    @pl.when(jnp.logical_not(used))
    def _():
        o_ref[...] = jnp.zeros_like(o_ref)


def _experts(block_e, n_used, xb, w_gu, b_gu, w_dn, b_dn):
    grid_spec = pltpu.PrefetchScalarGridSpec(
        num_scalar_prefetch=2,
        grid=(MOE_BLOCKS,),
        in_specs=[pl.BlockSpec((MOE_BM, D_MODEL), lambda i, be, nu: (i, 0)),
                  pl.BlockSpec((1, D_MODEL, 2 * D_FF), lambda i, be, nu: (be[i], 0, 0)),
                  pl.BlockSpec((1, 1, 2 * D_FF), lambda i, be, nu: (be[i], 0, 0)),
                  pl.BlockSpec((1, D_FF, D_MODEL), lambda i, be, nu: (be[i], 0, 0)),
                  pl.BlockSpec((1, 1, D_MODEL), lambda i, be, nu: (be[i], 0, 0))],
        out_specs=pl.BlockSpec((MOE_BM, D_MODEL), lambda i, be, nu: (i, 0)),
        scratch_shapes=[pltpu.VMEM((D_MODEL, 2 * D_FF), BF16), pltpu.VMEM((D_FF, D_MODEL), BF16)],
    )
    return pl.pallas_call(
        _expert_kernel,
        out_shape=jax.ShapeDtypeStruct((YB_ROWS, D_MODEL), BF16),
        grid_spec=grid_spec,
        compiler_params=_params("arbitrary"),
        name="experts",
    )(block_e, n_used, xb, w_gu, b_gu.reshape(N_EXPERTS, 1, 2 * D_FF), w_dn,
      b_dn.reshape(N_EXPERTS, 1, D_MODEL))


def _combine_kernel(final, seg_ref, np_ref, x1_ref, te_ref, gt_ref, off_ref, mod_ref, fg_ref, rep_ref,
                    jc_ref, tri_ref, yb_ref, o_ref, buf, acc, sems):
    i = pl.program_id(0)
    slot = i % 2

    def window_copies(tile, p, s):
        copies = []
        for e in range(N_EXPERTS):
            start = (seg_ref[tile * N_EXPERTS + e] // BF16_ROWS) * BF16_ROWS + p * SEG_ROWS
            start = pl.multiple_of(jnp.minimum(start, YB_ROWS - SEG_ROWS), BF16_ROWS)
            copies.append(pltpu.make_async_copy(yb_ref.at[pl.ds(start, SEG_ROWS)],
                                                buf.at[s, pl.ds(e * SEG_ROWS, SEG_ROWS)], sems.at[s]))
        return copies

    @pl.when(i == 0)
    def _():
        for cp in window_copies(0, 0, 0):
            cp.start()

    @pl.when(i + 1 < N_TILES)
    def _():
        for cp in window_copies(i + 1, 0, 1 - slot):
            cp.start()

    te = te_ref[...]
    gt = gt_ref[...]
    lane = lax.broadcasted_iota(jnp.int32, (TM, LANES), 1)
    member = jnp.zeros((TM, LANES), F32)
    gate = jnp.zeros((TM, LANES), F32)
    for k in range(TOP_K):
        hit = lane == te[:, k:k + 1]
        member = jnp.where(hit, 1.0, member)
        gate = jnp.where(hit, gt[:, k:k + 1], gate)
    rank = _dot(tri_ref[...], member.astype(BF16))
    pos = rank + off_ref[0][0:1, :].astype(F32)
    gate_cols = _dot(gate.astype(BF16), rep_ref[...])

    def placement(p):
        pos_cols = _dot_split(pos - p * SEG_ROWS, rep_ref[...])
        return jnp.where(pos_cols == jc_ref[...], gate_cols, 0.0).astype(BF16)

    place = placement(0.0)
    for cp in window_copies(i, 0, slot):
        cp.wait()
    acc[...] = _dot(place, buf[slot])

    def extra_pass(p, carry):
        copies = window_copies(i, p, slot)
        for cp in copies:
            cp.start()
        place = placement(p.astype(F32))
        for cp in copies:
            cp.wait()
        acc[...] += _dot(place, buf[slot])
        return carry

    lax.fori_loop(1, np_ref[i], extra_pass, 0)
    x2 = x1_ref[...] + mod_ref[0][5:6] * acc[...]
    if final:
        ms = jnp.mean(x2 * x2, axis=-1, keepdims=True)
        x2 = x2 * lax.rsqrt(ms + EPS) * fg_ref[...]
    o_ref[...] = x2


def _combine(seg_start, n_pass, x1, te, gates, seg_off, mod3, fg, rep, jc, tri, yb, final):
    row = lambda w: pl.BlockSpec((TM, w), lambda i, s, n: (i, 0))
    full = lambda a: pl.BlockSpec(a.shape, lambda i, s, n: (0,) * a.ndim)
    grid_spec = pltpu.PrefetchScalarGridSpec(
        num_scalar_prefetch=2,
        grid=(N_TILES,),
        in_specs=[row(D_MODEL), row(TOP_K), row(TOP_K),
                  pl.BlockSpec((1, 8, LANES), lambda i, s, n: (i, 0, 0)),
                  pl.BlockSpec((1, N_MOD, D_MODEL), lambda i, s, n: (_mod_row(i), 0, 0)),
                  full(fg), full(rep), full(jc), full(tri),
                  pl.BlockSpec(memory_space=pl.ANY)],
        out_specs=row(D_MODEL),
        scratch_shapes=[pltpu.VMEM((2, SEG_COLS, D_MODEL), BF16), pltpu.VMEM((TM, D_MODEL), F32),
                        pltpu.SemaphoreType.DMA((2,))],
    )
    return pl.pallas_call(
        functools.partial(_combine_kernel, final),
        out_shape=jax.ShapeDtypeStruct((N_TOK, D_MODEL), F32),
        grid_spec=grid_spec,
        compiler_params=_params("arbitrary"),
        name="moe_combine",
    )(seg_start, n_pass, x1, te, gates, seg_off, mod3, fg, rep, jc, tri, yb)


def _w1_index():
    idx = np.full((W1_COLS,), IN_WIDTH, np.int32)
    for g in range(2):
        for kv in range(2):
            idx[g * 128 + kv * 64:g * 128 + kv * 64 + 64] = np.arange(64) + kv * 128 + g * 64
    idx[256:512] = np.arange(256) + IN_OFF['a_k']
    idx[512:704] = np.arange(192) + IN_OFF['b_cq']
    idx[768:896] = np.arange(128) + IN_OFF['b_ckv']
    idx[896:928] = np.arange(32) + IN_OFF['b_kr']
    idx[1024:1792] = np.arange(768) + IN_OFF['c_q']
    idx[1792:2048] = np.arange(256) + IN_OFF['d_z']
    idx[2048:2560] = np.arange(512) + IN_OFF['d_xbc']
    idx[2560:2568] = np.arange(8) + IN_OFF['d_dtf']
    return idx


def _wout_a_index():
    idx = np.zeros((256,), np.int32)
    for g in range(2):
        for kv in range(2):
            idx[g * 128 + kv * 64:g * 128 + kv * 64 + 64] = np.arange(64) + kv * 128 + g * 64
    return idx


def _rope_tables():
    t = np.arange(DEC_SEQ)
    pos = ((t // GRID_W).astype(np.float32), (t % GRID_W).astype(np.float32))

    def unit(rot_dim):
        quarter, half = rot_dim // 4, rot_dim // 2
        inv = ROPE_THETA ** (-np.arange(0, half, 2, dtype=np.float32) / half)
        cos = np.zeros((DEC_SEQ, rot_dim), np.float32)
        sin = np.zeros((DEC_SEQ, rot_dim), np.float32)
        for seg in range(4):
            ang = pos[seg // 2][:, None] * inv[None, :].astype(np.float32)
            cos[:, seg * quarter:(seg + 1) * quarter] = np.cos(ang)
            sin[:, seg * quarter:(seg + 1) * quarter] = np.sin(ang) * (-1.0 if seg % 2 == 0 else 1.0)
        return cos, sin

    specs = [
        (HEAD_DIM, (0, 64), (), HEAD_DIM ** -0.5),
        (HEAD_DIM, (0, 64), (), 1.0),
        (MLA_ROPE, (MLA_NOPE,), (0, MLA_NOPE), (MLA_NOPE + MLA_ROPE) ** -0.5),
        (MLA_ROPE, (0,), (), 1.0),
        (DIFF_D, (0, 32, 64, 96), (), DIFF_D ** -0.5),
        (DIFF_D, (0, 32, 64, 96), (), 1.0),
    ]
    lat_cols, ident_cols = [], []
    for rot_dim, starts, passthrough, scale in specs:
        ucos, usin = unit(rot_dim)
        cos = np.zeros((DEC_SEQ, LANES), np.float32)
        sin = np.zeros((DEC_SEQ, LANES), np.float32)
        ident = np.zeros((1, LANES), np.float32)
        if passthrough:
            cos[:, passthrough[0]:passthrough[1]] = 1.0
            ident[:, passthrough[0]:passthrough[1]] = 1.0
        for s in starts:
            cos[:, s:s + rot_dim] = ucos
            sin[:, s:s + rot_dim] = usin
            ident[:, s:s + rot_dim] = 1.0
        lat_cols += [cos * scale, sin * scale]
        ident_cols += [ident * scale, np.zeros((1, LANES), np.float32)]
    lat = np.concatenate(lat_cols, axis=1)
    ident_blk = np.broadcast_to(np.concatenate(ident_cols, axis=1), (TM, N_TAB * LANES))
    return np.concatenate([ident_blk, lat], axis=0).astype(np.float32)


def _block_diag_ones(n, blk):
    r = np.arange(n)
    return (r[:, None] // blk == r[None, :] // blk).astype(np.float32)


def _combine_constants():
    col = np.arange(SEG_COLS)
    rep = (np.arange(LANES)[:, None] == (col // SEG_ROWS)[None, :]).astype(np.float32)
    jc = (col % SEG_ROWS).astype(np.float32)[None, :]
    r = np.arange(TM)
    tri = (r[None, :] < r[:, None]).astype(np.float32)
    return jnp.asarray(rep, BF16), jnp.asarray(jc), jnp.asarray(tri, BF16)


def _dispatch_constants():
    row = np.arange(DISP_COLS)
    rept = (row[:, None] // DISP_ROWS == np.arange(LANES)[None, :]).astype(np.float32)
    r = np.arange(TM)
    triu = (r[:, None] < r[None, :]).astype(np.float32)
    return jnp.asarray(rept, BF16), jnp.asarray(triu, BF16)


def kernel(x_prompt, x_sample, cache_gqa_k, cache_gqa_v, cache_mla_ckv, cache_mla_krope, cache_diff_k, cache_diff_v, state_ssd_fwd, state_ssd_bwd, c, c_ctx, norm1_g, norm2_g, w_ada, b_ada, w_in, w_out, gqa_qn_g, gqa_kn_g, mla_qa_g, mla_wqb, mla_kva_g, mla_wkvb, diff_lq1, diff_lk1, diff_lq2, diff_lk2, diff_subln_g, ssd_conv_w, ssd_conv_b, ssd_a_log_f, ssd_a_log_b, ssd_dt_bias_f, ssd_dt_bias_b, ssd_d, ssd_norm_g, router_w, router_b, moe_w_gu, moe_b_gu, moe_w_dn, moe_b_dn, final_g):
    tabs = jnp.asarray(_rope_tables())
    bd = jnp.asarray(_block_diag_ones(256, HEAD_DIM), BF16)
    rep, jc, tri = _combine_constants()
    rept, triu = _dispatch_constants()
    w1_idx = _w1_index()
    woa_idx = _wout_a_index()

    cvec = jnp.zeros((MOD_ROWS, D_MODEL), F32).at[0].set(c_ctx).at[1:1 + DEC_BATCH].set(c)
    mod = _modulation(cvec, w_ada, b_ada).reshape(DEPTH, MOD_ROWS, N_MOD, D_MODEL)

    x = jnp.concatenate([x_prompt.reshape(N_CTX, D_MODEL), x_sample.reshape(N_LAT, D_MODEL)], axis=0)
    new_ctx = []
    for l in range(DEPTH):
        mod3 = mod[l]
        w1 = jnp.concatenate([w_in[l], jnp.zeros((D_MODEL, 1), F32)], axis=1)[:, w1_idx].astype(BF16)
        wqb = mla_wqb[l].reshape(Q_LORA, B_HEADS, MLA_NOPE + MLA_ROPE)
        wqb = jnp.pad(wqb, ((0, 256 - Q_LORA), (0, 0), (0, LANES - MLA_NOPE - MLA_ROPE)))
        wqb = wqb.reshape(256, B_HEADS * LANES).astype(BF16)
        wkvb = mla_wkvb[l].reshape(KV_LORA, B_HEADS, MLA_NOPE + MLA_V)
        wk_nope = jnp.pad(wkvb[:, :, :MLA_NOPE], ((0, 0), (0, 0), (0, LANES - MLA_NOPE)))
        eye_r = jnp.zeros((LANES, B_HEADS, LANES), F32)
        eye_r = eye_r.at[jnp.arange(MLA_ROPE), :, MLA_NOPE + jnp.arange(MLA_ROPE)].set(1.0)
        wk = jnp.concatenate([wk_nope, eye_r], axis=0).reshape(256, B_HEADS * LANES).astype(BF16)
        wv = wkvb[:, :, MLA_NOPE:].reshape(KV_LORA, B_HEADS * MLA_V).astype(BF16)
        wo = jnp.concatenate([w_out[l][woa_idx], w_out[l][256:]], axis=0).astype(BF16)
        gq = jnp.tile(gqa_qn_g[l], 4)[None, :]
        gk = jnp.tile(gqa_kn_g[l], 2)[None, :]
        gqa = jnp.pad(mla_qa_g[l], (0, 256 - Q_LORA))[None, :]
        gkva = mla_kva_g[l][None, :]
        rw = jnp.pad(router_w[l], ((0, 0), (0, LANES - N_EXPERTS)))
        rwh = rw.astype(BF16)
        rwl = (rw - rwh.astype(F32)).astype(BF16)
        rb = jnp.pad(router_b[l], (0, LANES - N_EXPERTS), constant_values=NEG_BIG)[None, :]

        qa, kva, qb, ck, qc, kc, vc, z, xbc, dt = _in_proj(
            x, mod3, norm1_g[l][None, :], w1, bd, gq, gk, gqa, gkva, wqb, tabs)

        past_kva = jnp.concatenate([cache_gqa_k[:, l].reshape(DEC_BATCH, PAST_LEN, 128),
                                    cache_gqa_v[:, l].reshape(DEC_BATCH, PAST_LEN, 128)], axis=-1)
        past_ck = jnp.concatenate([cache_mla_ckv[:, l], cache_mla_krope[:, l],
                                   jnp.zeros((DEC_BATCH, PAST_LEN, LANES - MLA_ROPE), F32)], axis=-1)
        past_kc = cache_diff_k[:, l].reshape(DEC_BATCH, PAST_LEN, 256)
        past_vc = cache_diff_v[:, l].reshape(DEC_BATCH, PAST_LEN, 256)
        lams = [a[l][None, :] for a in (diff_lq1, diff_lk1, diff_lq2, diff_lk2)]
        gsub = jnp.tile(diff_subln_g[l], 2)[None, :]
        lam_init = 0.8 - 0.6 * math.exp(-0.3 * l)
        oa, ob, oc = _attention_mixers(qa, kva, qb, ck, qc, kc, vc, past_kva, past_ck, past_kc, past_vc,
                                       wk, wv, lams, gsub, lam_init)

        pad8 = lambda f, b: jnp.pad(jnp.concatenate([f, b]), (0, LANES - 2 * D_HEADS))[None, :]
        ssd_consts = (ssd_conv_w[l], ssd_conv_b[l][None, :], pad8(ssd_dt_bias_f[l], ssd_dt_bias_b[l]),
                      pad8(ssd_a_log_f[l], ssd_a_log_b[l]), jnp.repeat(ssd_d[l], D_HEADDIM)[None, :],
                      ssd_norm_g[l][None, :])
        zeros_st = jnp.zeros((BATCH, D_HEADS, D_STATE, D_HEADDIM), F32)
        od, hf_c, hb_c = _ssd(False, z, xbc, dt, ssd_consts, zeros_st, zeros_st, jnp.zeros((N_TOK, D_INNER), F32))
        od, _, _ = _ssd(True, z, xbc, dt, ssd_consts, jnp.swapaxes(state_ssd_fwd[:, l], -1, -2),
                        jnp.swapaxes(state_ssd_bwd[:, l], -1, -2), od)

        x1, h2, te, gates, tile_cnt = _out_proj(x, oa, ob, oc, od, mod3, wo, norm2_g[l][None, :], rwh, rwl, rb)

        seg_cnt = tile_cnt[:, 0, :N_EXPERTS]
        seg_len = (seg_cnt + SEG_ALIGN - 1) // SEG_ALIGN * SEG_ALIGN
        region = (jnp.sum(seg_len, axis=0) + DISP_ROWS + MOE_BM - 1) // MOE_BM * MOE_BM
        pad_end = jnp.cumsum(region)
        pad_start = pad_end - region
        seg_start = (pad_start[None, :] + jnp.cumsum(seg_len, axis=0) - seg_len).astype(jnp.int32)
        n_used = (pad_end[-1] // MOE_BM).astype(jnp.int32).reshape(1)
        blk_start = jnp.arange(MOE_BLOCKS, dtype=jnp.int32) * MOE_BM
        block_e = jnp.minimum(jnp.sum((pad_end[None, :] <= blk_start[:, None]).astype(jnp.int32), axis=1),
                              N_EXPERTS - 1).astype(jnp.int32)
        disp_pass = jnp.maximum(jnp.max((seg_cnt + DISP_ROWS - 1) // DISP_ROWS, axis=1), 1).astype(jnp.int32)
        seg_off = seg_start % BF16_ROWS
        comb_pass = jnp.maximum(jnp.max((seg_cnt + seg_off + SEG_ROWS - 1) // SEG_ROWS, axis=1), 1).astype(jnp.int32)
        seg_off = jnp.broadcast_to(jnp.pad(seg_off, ((0, 0), (0, LANES - N_EXPERTS)))[:, None, :],
                                   (N_TILES, 8, LANES)).astype(jnp.int32)

        xb = _dispatch(seg_start.reshape(-1), seg_cnt.reshape(-1), disp_pass, h2, te, rept, triu,
                       jnp.zeros((YB_ROWS, D_MODEL), F32))
        yb = _experts(block_e, n_used, xb, moe_w_gu[l], moe_b_gu[l], moe_w_dn[l], moe_b_dn[l])
        x = _combine(seg_start.reshape(-1), comb_pass, x1, te, gates, seg_off, mod3, final_g[None, :],
                     rep, jc, tri, yb, l == DEPTH - 1)

        kva_c, ck_c = kva[:N_CTX], ck[:N_CTX]
        new_ctx.append((kva_c[:, 0:128].reshape(BATCH, SEQ, A_KV_HEADS, HEAD_DIM),
                        kva_c[:, 128:256].reshape(BATCH, SEQ, A_KV_HEADS, HEAD_DIM),
                        ck_c[:, 0:KV_LORA].reshape(BATCH, SEQ, KV_LORA),
                        ck_c[:, KV_LORA:KV_LORA + MLA_ROPE].reshape(BATCH, SEQ, MLA_ROPE),
                        kc[:N_CTX].reshape(BATCH, SEQ, C_HEADS, 2 * DIFF_D),
                        vc[:N_CTX].reshape(BATCH, SEQ, C_HEADS, DIFF_V),
                        jnp.swapaxes(hf_c, -1, -2), jnp.swapaxes(hb_c, -1, -2)))

    y_prompt = x[:N_CTX].reshape(BATCH, SEQ, D_MODEL)
    y_sample = x[N_CTX:].reshape(DEC_BATCH, DEC_SEQ, D_MODEL)
    caches = [jnp.stack([cl[i] for cl in new_ctx], axis=1) for i in range(8)]
    return (y_prompt, y_sample, *caches)
```

```python
import functools
import math

import numpy as np
import jax
import jax.numpy as jnp
from jax import lax
from jax.experimental import pallas as pl
from jax.experimental.pallas import tpu as pltpu

F32 = jnp.float32
BF16 = jnp.bfloat16

D_MODEL = 1024
BATCH = 16
SEQ = 256
DEPTH = 2
DEC_BATCH = 8
DEC_SEQ = 2048
PAST_LEN = 256
GRID_W = 64
ROPE_THETA = 10000.0
EPS = 1e-6
HEAD_DIM = 64
A_HEADS = 4
A_KV_HEADS = 2
B_HEADS = 4
MLA_NOPE = 64
MLA_ROPE = 32
MLA_V = 64
Q_LORA = 192
KV_LORA = 128
C_HEADS = 4
DIFF_D = 32
DIFF_V = 64
SUBLN_EPS = 1e-5
D_HEADS = 4
D_HEADDIM = 64
D_INNER = 256
D_GROUPS = 2
D_STATE = 64
CONV_DIM = 512
SSD_CHUNK = 128
N_EXPERTS = 32
TOP_K = 4
D_FF = 1024
SWIGLU_ALPHA = 1.702
SWIGLU_LIMIT = 7.0
N_MOD = 6

N_CTX = BATCH * SEQ
N_LAT = DEC_BATCH * DEC_SEQ
N_TOK = N_CTX + N_LAT

LANES = 128
TM = 256
N_TILES = N_TOK // TM
CTX_TILES = N_CTX // TM
LAT_TILES_PER_SEQ = DEC_SEQ // TM
MOD_ROWS = 16
MOE_BM = 256
N_PAIRS = N_TOK * TOP_K
SEG_ALIGN = 8
CHUNKS = (32, 16, 8)
MOE_BLOCKS = -(-(N_PAIRS + N_TILES * N_EXPERTS * (SEG_ALIGN - 1) + N_EXPERTS * (MOE_BM - 1)) // MOE_BM)
YB_ROWS = MOE_BLOCKS * MOE_BM
TILE_BUF = -(-(TM * TOP_K + N_EXPERTS * (SEG_ALIGN - 1)) // LANES) * LANES
ZERO_ROWS = MOE_BM
VMEM_LIMIT = 56 * 1024 * 1024
NEG_BIG = -1e30

W1_COLS = 2688
IN_OFF = dict(a_q=0, a_k=256, a_v=384, b_cq=512, b_ckv=704, b_kr=832, c_q=864, c_k=1120,
              c_v=1376, d_z=1632, d_xbc=1888, d_dtf=2400, d_dtb=2404)
IN_WIDTH = 2408
N_TAB = 12


def _mod_row(i):
    return jnp.where(i < CTX_TILES, 0, 1 + (i - CTX_TILES) // LAT_TILES_PER_SEQ)


def _tab_block(i):
    return jnp.where(i < CTX_TILES, 0, 1 + (i - CTX_TILES) % LAT_TILES_PER_SEQ)


def _dot(a, b):
    return jnp.dot(a, b, preferred_element_type=F32)


def _dot_nt(a, b):
    return lax.dot_general(a, b, (((1,), (1,)), ((), ())), preferred_element_type=F32)


def _dot_split(x, m):
    hi = x.astype(BF16)
    lo = (x - hi.astype(F32)).astype(BF16)
    return _dot(hi, m) + _dot(lo, m)


def _dot_split_left(m, x):
    hi = x.astype(BF16)
    lo = (x - hi.astype(F32)).astype(BF16)
    return _dot(m, hi) + _dot(m, lo)


def _silu(x):
    return x * jax.nn.sigmoid(x)


def _params(*semantics):
    return pltpu.CompilerParams(dimension_semantics=semantics, vmem_limit_bytes=VMEM_LIMIT)


MOD_TN = 1536


def _mod_kernel(c_ref, w_ref, b_ref, o_ref):
    c = c_ref[...]
    s = _silu(c).astype(BF16)
    o_ref[0] = _dot(s, w_ref[0].astype(BF16)) + b_ref[0]


def _modulation(cvec, w_ada, b_ada):
    n = N_MOD * D_MODEL
    return pl.pallas_call(
        _mod_kernel,
        out_shape=jax.ShapeDtypeStruct((DEPTH, MOD_ROWS, n), F32),
        grid=(DEPTH, n // MOD_TN),
        in_specs=[pl.BlockSpec((MOD_ROWS, D_MODEL), lambda l, j: (0, 0)),
                  pl.BlockSpec((1, D_MODEL, MOD_TN), lambda l, j: (l, 0, j)),
                  pl.BlockSpec((1, 1, MOD_TN), lambda l, j: (l, 0, j))],
        out_specs=pl.BlockSpec((1, MOD_ROWS, MOD_TN), lambda l, j: (l, 0, j)),
        compiler_params=_params("parallel", "parallel"),
        name="adaln_mod",
    )(cvec, w_ada, b_ada.reshape(DEPTH, 1, n))


def _rope(x, cos, sin, quarter):
    lane = lax.broadcasted_iota(jnp.int32, (x.shape[0], LANES), 1)
    first = (lane // quarter) % 2 == 0
    outs = []
    for t in range(x.shape[1] // LANES):
        xt = x[:, t * LANES:(t + 1) * LANES]
        partner = jnp.where(first, pltpu.roll(xt, LANES - quarter, 1), pltpu.roll(xt, quarter, 1))
        outs.append(xt * cos + partner * sin)
    return outs


def _in_proj_kernel(x_ref, mod_ref, g1_ref, w1_ref, bd_ref, gq_ref, gk_ref, gqa_ref, gkva_ref,
                    wqb_ref, tab_ref,
                    qa_ref, kva_ref, qb_ref, ck_ref, qc_ref, kc_ref, vc_ref, z_ref, xbc_ref, dt_ref):
    x = x_ref[...]
    m = mod_ref[0]
    sh1, sc1 = m[0:1], m[1:2]
    ms = jnp.mean(x * x, axis=-1, keepdims=True)
    h = (x * lax.rsqrt(ms + EPS) * g1_ref[...]) * (1.0 + sc1) + sh1
    u = _dot(h.astype(BF16), w1_ref[...])

    def tab(k):
        return tab_ref[:, k * LANES:(k + 1) * LANES]

    bd = bd_ref[...]

    def head_norm(v, gain):
        w = v.shape[1]
        ss = _dot_split(v * v, bd[:w, :w])
        return v * lax.rsqrt(ss * (1.0 / HEAD_DIM) + EPS) * gain

    qa = _rope(head_norm(u[:, 0:256], gq_ref[...]), tab(0), tab(1), HEAD_DIM // 4)
    for t in range(2):
        qa_ref[:, t * LANES:(t + 1) * LANES] = qa[t].astype(BF16)
    ka = _rope(head_norm(u[:, 256:384], gk_ref[...]), tab(2), tab(3), HEAD_DIM // 4)
    kva_ref[:, 0:128] = ka[0]
    kva_ref[:, 128:256] = u[:, 384:512]

    cq = u[:, 512:768]
    msq = jnp.sum(cq * cq, axis=-1, keepdims=True) * (1.0 / Q_LORA)
    yq = cq * lax.rsqrt(msq + EPS) * gqa_ref[...]
    qb = _rope(_dot(yq.astype(BF16), wqb_ref[...]), tab(4), tab(5), MLA_ROPE // 4)
    for t in range(4):
        qb_ref[:, t * LANES:(t + 1) * LANES] = qb[t].astype(BF16)
    ckv = u[:, 768:896]
    msk = jnp.mean(ckv * ckv, axis=-1, keepdims=True)
    ck_ref[:, 0:128] = ckv * lax.rsqrt(msk + EPS) * gkva_ref[...]
    ck_ref[:, 128:256] = _rope(u[:, 896:1024], tab(6), tab(7), MLA_ROPE // 4)[0]

    qc = _rope(u[:, 1024:1280], tab(8), tab(9), DIFF_D // 4)
    kc = _rope(u[:, 1280:1536], tab(10), tab(11), DIFF_D // 4)
    for t in range(2):
        qc_ref[:, t * LANES:(t + 1) * LANES] = qc[t].astype(BF16)
        kc_ref[:, t * LANES:(t + 1) * LANES] = kc[t]
    vc_ref[...] = u[:, 1536:1792]

    z_ref[...] = u[:, 1792:2048]
    xbc_ref[...] = u[:, 2048:2560]
    dt_ref[...] = u[:, 2560:2688]


def _in_proj(x, mod3, g1, w1, bd, gq, gk, gqa, gkva, wqb, tabs):
    row = lambda w: pl.BlockSpec((TM, w), lambda i: (i, 0))
    full = lambda a: pl.BlockSpec(a.shape, lambda i: (0,) * a.ndim)
    outs = [(256, BF16), (256, F32), (512, BF16), (256, F32), (256, BF16), (256, F32), (256, F32),
            (256, F32), (512, F32), (128, F32)]
    return pl.pallas_call(
        _in_proj_kernel,
        out_shape=[jax.ShapeDtypeStruct((N_TOK, w), d) for w, d in outs],
        grid=(N_TILES,),
        in_specs=[row(D_MODEL),
                  pl.BlockSpec((1, N_MOD, D_MODEL), lambda i: (_mod_row(i), 0, 0)),
                  full(g1), full(w1), full(bd), full(gq), full(gk), full(gqa), full(gkva), full(wqb),
                  pl.BlockSpec((TM, N_TAB * LANES), lambda i: (_tab_block(i), 0))],
        out_specs=[row(w) for w, _ in outs],
        compiler_params=_params("parallel"),
        name="in_proj",
    )(x, mod3, g1, w1, bd, gq, gk, gqa, gkva, wqb, tabs)


def _softmax_parts(s):
    m = jnp.max(s, axis=-1, keepdims=True)
    e = jnp.exp(s - m)
    return e, jnp.sum(e, axis=-1, keepdims=True)


def _half_mask(rows):
    lane = lax.broadcasted_iota(jnp.int32, (rows, LANES), 1)
    return lane < (LANES // 2)


def _keys(past_ref, new_ref, lo, hi):
    new = new_ref[:, lo:hi].astype(BF16)
    if past_ref is None:
        return new
    return jnp.concatenate([past_ref[0, :, lo:hi].astype(BF16), new], axis=0)


def _attn_a_kernel(latent, *refs):
    if latent:
        q_ref, kv_ref, past_ref, _, o_ref = refs
    else:
        (q_ref, kv_ref, _, o_ref), past_ref = refs, None
    q = q_ref[...]
    k = _keys(past_ref, kv_ref, 0, 128)
    v = _keys(past_ref, kv_ref, 128, 256)
    lo = _half_mask(q.shape[0])
    for g in range(2):
        qt = q[:, g * LANES:(g + 1) * LANES].astype(F32)
        res = []
        for half in range(2):
            qm = jnp.where(lo, qt, 0.0) if half == 0 else jnp.where(lo, 0.0, qt)
            e, l = _softmax_parts(_dot_nt(qm.astype(BF16), k))
            res.append(_dot(e.astype(BF16), v) / l)
        o_ref[:, g * LANES:(g + 1) * LANES] = jnp.where(lo, res[0], res[1])


def _attn_b_kernel(latent, *refs):
    if latent:
        q_ref, ck_ref, past_ref, wk_ref, wv_ref, _, o_ref, k_s, v_s = refs
    else:
        (q_ref, ck_ref, wk_ref, wv_ref, _, o_ref, k_s, v_s), past_ref = refs, None

    @pl.when(pl.program_id(1) == 0)
    def _():
        ck = _keys(past_ref, ck_ref, 0, 256)
        k_s[...] = _dot(ck, wk_ref[...]).astype(BF16)
        v_s[...] = _dot(ck[:, 0:128], wv_ref[...]).astype(BF16)

    q = q_ref[...]
    lo = _half_mask(q.shape[0])
    for j in range(2):
        v = v_s[:, j * LANES:(j + 1) * LANES]
        res = []
        for half in range(2):
            h = 2 * j + half
            e, l = _softmax_parts(_dot_nt(q[:, h * LANES:(h + 1) * LANES], k_s[:, h * LANES:(h + 1) * LANES]))
            res.append(_dot(e.astype(BF16), v) / l)
        o_ref[:, j * LANES:(j + 1) * LANES] = jnp.where(lo, res[0], res[1])


def _attn_c_kernel(lam_init, latent, *refs):
    if latent:
        q_ref, k_ref, v_ref, pk_ref, pv_ref, lq1_ref, lk1_ref, lq2_ref, lk2_ref, g_ref, _, o_ref = refs
    else:
        (q_ref, k_ref, v_ref, lq1_ref, lk1_ref, lq2_ref, lk2_ref, g_ref, _, o_ref), pk_ref, pv_ref = refs, None, None
    lam = (jnp.exp(jnp.sum(lq1_ref[...] * lk1_ref[...], axis=-1, keepdims=True))
           - jnp.exp(jnp.sum(lq2_ref[...] * lk2_ref[...], axis=-1, keepdims=True)) + lam_init)
    q = q_ref[...]
    rows = q.shape[0]
    lane = lax.broadcasted_iota(jnp.int32, (rows, LANES), 1)
    lo = lane < (LANES // 2)
    for j in range(2):
        qt = q[:, j * LANES:(j + 1) * LANES].astype(F32)
        k = _keys(pk_ref, k_ref, j * LANES, (j + 1) * LANES)
        v = _keys(pv_ref, v_ref, j * LANES, (j + 1) * LANES)
        res = []
        for half in range(2):
            p = []
            for t in range(2):
                quarter = 2 * half + t
                qm = jnp.where(lane // (LANES // 4) == quarter, qt, 0.0)
                e, l = _softmax_parts(_dot_nt(qm.astype(BF16), k))
                p.append((e, l))
            pd = p[0][0] * (1.0 / p[0][1]) - p[1][0] * (lam / p[1][1])
            res.append(_dot(pd.astype(BF16), v))
        o = jnp.where(lo, res[0], res[1])
        o2 = o * o
        ss_lo = jnp.sum(jnp.where(lo, o2, 0.0), axis=-1, keepdims=True)
        ss_hi = jnp.sum(jnp.where(lo, 0.0, o2), axis=-1, keepdims=True)
        ss = jnp.where(lo, ss_lo, ss_hi) * (1.0 / DIFF_V)
        o_ref[:, j * LANES:(j + 1) * LANES] = (o * lax.rsqrt(ss + SUBLN_EPS) * g_ref[...]) * (1.0 - lam_init)


def _seq_call(body, name, latent, q, news, pasts, consts, prev_out, scratch=(), q_semantics="parallel"):
    const_specs = [pl.BlockSpec(a.shape, lambda b, i, n=a.ndim: (0,) * n) for a in consts]
    if latent:
        tile = lambda b, i: (CTX_TILES + b * LAT_TILES_PER_SEQ + i, 0)
        grid = (DEC_BATCH, LAT_TILES_PER_SEQ)
        new_specs = [pl.BlockSpec((DEC_SEQ, a.shape[1]), lambda b, i: (N_CTX // DEC_SEQ + b, 0)) for a in news]
        past_specs = [pl.BlockSpec((1, PAST_LEN, a.shape[2]), lambda b, i: (b, 0, 0)) for a in pasts]
    else:
        tile = lambda b, i: (b, 0)
        grid = (BATCH, 1)
        new_specs = [pl.BlockSpec((SEQ, a.shape[1]), tile) for a in news]
        past_specs, pasts = [], ()
    in_specs = ([pl.BlockSpec((TM, q.shape[1]), tile)] + new_specs + past_specs + const_specs
                + [pl.BlockSpec(memory_space=pl.ANY)])
    args = [q, *news, *pasts, *consts, prev_out]
    return pl.pallas_call(
        functools.partial(body, latent),
        out_shape=jax.ShapeDtypeStruct((N_TOK, 256), F32),
        grid=grid,
        in_specs=in_specs,
        out_specs=pl.BlockSpec((TM, 256), tile),
        scratch_shapes=list(scratch),
        input_output_aliases={len(args) - 1: 0},
        compiler_params=_params("parallel", q_semantics),
        name=name + ("_lat" if latent else "_ctx"),
    )(*args)


def _attention_mixers(qa, kva, qb, ck, qc, kc, vc, past_kva, past_ck, past_kc, past_vc, wk, wv,
                      lams, gsub, lam_init):
    def both(body, name, q, news, pasts, consts, scratch_fn=None, q_semantics="parallel"):
        sc = (lambda lk: ()) if scratch_fn is None else scratch_fn
        out = jnp.zeros((N_TOK, 256), F32)
        out = _seq_call(body, name, False, q, news, (), consts, out, sc(SEQ), q_semantics)
        return _seq_call(body, name, True, q, news, pasts, consts, out, sc(PAST_LEN + DEC_SEQ), q_semantics)

    oa = both(_attn_a_kernel, "attn_gqa", qa, [kva], [past_kva], [])
    ob = both(_attn_b_kernel, "attn_mla", qb, [ck], [past_ck], [wk, wv],
              lambda lk: (pltpu.VMEM((lk, 512), BF16), pltpu.VMEM((lk, 256), BF16)), "arbitrary")
    oc = both(functools.partial(_attn_c_kernel, lam_init), "attn_diff", qc, [kc, vc],
              [past_kc, past_vc], [*lams, gsub])
    return oa, ob, oc


Q = SSD_CHUNK


def _ssd_kernel(z_ref, xbc_ref, dt_ref, cw_ref, cb_ref, dtb_ref, alog_ref, dvec_ref, ng_ref, h0f_ref, h0b_ref, _,
                out_ref, hf_ref, hb_ref, act_s, cum_s, dtv_s, y_s):
    seq = z_ref.shape[0]
    nc = seq // Q
    row = lax.broadcasted_iota(jnp.int32, (Q, Q), 0)
    col = lax.broadcasted_iota(jnp.int32, (Q, Q), 1)
    lower = row >= col
    upper = row <= col
    tril = jnp.where(lower, 1.0, 0.0).astype(BF16)
    triu = jnp.where(upper, 1.0, 0.0).astype(BF16)
    rowc = lax.broadcasted_iota(jnp.int32, (Q, CONV_DIM), 0)
    lane = lax.broadcasted_iota(jnp.int32, (Q, LANES), 1)
    a_neg = -jnp.exp(alog_ref[...])
    cw = cw_ref[...]
    hf_ref[0] = h0f_ref[0]
    hb_ref[0] = h0b_ref[0]

    def fwd_chunk(c, carry):
        base = pl.multiple_of(c * Q, Q)
        x0 = xbc_ref[pl.ds(base, Q), :]
        prev = xbc_ref[pl.ds(pl.multiple_of(jnp.maximum(base - 8, 0), 8), 8), :][7:8, :]
        nxt = xbc_ref[pl.ds(pl.multiple_of(jnp.minimum(base + Q, seq - 8), 8), 8), :][0:1, :]
        prev = jnp.where(c > 0, prev, 0.0)
        nxt = jnp.where(c < nc - 1, nxt, 0.0)
        xm1 = jnp.where(rowc == 0, prev, pltpu.roll(x0, 1, 0))
        xp1 = jnp.where(rowc == Q - 1, nxt, pltpu.roll(x0, Q - 1, 0))
        act = _silu(xm1 * cw[0:1] + x0 * cw[1:2] + xp1 * cw[2:3] + cb_ref[...])
        act_s[pl.ds(base, Q), :] = act
        xs = act[:, 0:256]
        bm = act[:, 256:384]
        cm = act[:, 384:512]

        dtr = dt_ref[pl.ds(base, Q), :] + dtb_ref[...]
        dtv = jnp.maximum(dtr, 0.0) + jnp.log1p(jnp.exp(-jnp.abs(dtr)))
        dta = dtv * a_neg
        cum = jnp.where(lane < D_HEADS, _dot_split_left(tril, dta), _dot_split_left(triu, dta))
        cum_s[pl.ds(base, Q), :] = cum
        dtv_s[pl.ds(base, Q), :] = dtv
        cum_t = cum.T
        dtv_t = dtv.T
        bm_t = bm.T
        ys = []
        for h in range(D_HEADS):
            g = h // (D_HEADS // D_GROUPS)
            cg = cm[:, g * D_STATE:(g + 1) * D_STATE].astype(BF16)
            cb_mat = _dot_nt(cg, bm[:, g * D_STATE:(g + 1) * D_STATE].astype(BF16))
            cf = cum[:, h:h + 1]
            cb = cum[:, D_HEADS + h:D_HEADS + h + 1]
            l_f = jnp.exp(jnp.where(lower, cf - cum_t[h:h + 1, :], NEG_BIG))
            l_b = jnp.exp(jnp.where(upper, cb - cum_t[D_HEADS + h:D_HEADS + h + 1, :], NEG_BIG))
            mix = cb_mat * (l_f * dtv_t[h:h + 1, :] + l_b * dtv_t[D_HEADS + h:D_HEADS + h + 1, :])
            xh = xs[:, h * D_HEADDIM:(h + 1) * D_HEADDIM]
            y = _dot(mix.astype(BF16), xh.astype(BF16))
            state = hf_ref[0, h]
            y = y + _dot(cg, state.astype(BF16)) * jnp.exp(cf)
            y = y + dvec_ref[:, h * D_HEADDIM:(h + 1) * D_HEADDIM] * xh
            ys.append(y)
            last = cum[Q - 1:Q, h:h + 1]
            wgt = jnp.exp(last - cf) * dtv[:, h:h + 1]
            st = _dot(bm_t[g * D_STATE:(g + 1) * D_STATE, :].astype(BF16), (xh * wgt).astype(BF16))
            hf_ref[0, h] = state * jnp.exp(last) + st
        y_s[pl.ds(base, Q), :] = jnp.concatenate(ys, axis=-1)
        return carry

    lax.fori_loop(0, nc, fwd_chunk, 0)

    def bwd_chunk(i, carry):
        c = nc - 1 - i
        base = pl.multiple_of(c * Q, Q)
        act = act_s[pl.ds(base, Q), :]
        cum = cum_s[pl.ds(base, Q), :]
        dtv = dtv_s[pl.ds(base, Q), :]
        xs = act[:, 0:256]
        bm_t = act[:, 256:384].T
        cm = act[:, 384:512]
        ys = []
        for h in range(D_HEADS):
            g = h // (D_HEADS // D_GROUPS)
            cg = cm[:, g * D_STATE:(g + 1) * D_STATE].astype(BF16)
            cb = cum[:, D_HEADS + h:D_HEADS + h + 1]
            xh = xs[:, h * D_HEADDIM:(h + 1) * D_HEADDIM]
            state = hb_ref[0, h]
            ys.append(_dot(cg, state.astype(BF16)) * jnp.exp(cb))
            first = cum[0:1, D_HEADS + h:D_HEADS + h + 1]
            wgt = jnp.exp(first - cb) * dtv[:, D_HEADS + h:D_HEADS + h + 1]
            st = _dot(bm_t[g * D_STATE:(g + 1) * D_STATE, :].astype(BF16), (xh * wgt).astype(BF16))
            hb_ref[0, h] = state * jnp.exp(first) + st
        y = y_s[pl.ds(base, Q), :] + jnp.concatenate(ys, axis=-1)
        gated = y * _silu(z_ref[pl.ds(base, Q), :])
        ms = jnp.mean(gated * gated, axis=-1, keepdims=True)
        out_ref[pl.ds(base, Q), :] = gated * lax.rsqrt(ms + EPS) * ng_ref[...]
        return carry

    lax.fori_loop(0, nc, bwd_chunk, 0)


def _ssd(latent, z, xbc, dt, consts, h0f, h0b, prev_out):
    bsz, seq, first = (DEC_BATCH, DEC_SEQ, N_CTX // DEC_SEQ) if latent else (BATCH, SEQ, 0)
    per_seq = lambda w: pl.BlockSpec((seq, w), lambda b: (first + b, 0))
    const_specs = [pl.BlockSpec(a.shape, lambda b, n=a.ndim: (0,) * n) for a in consts]
    st_spec = pl.BlockSpec((1, D_HEADS, D_STATE, D_HEADDIM), lambda b: (b, 0, 0, 0))
    st_shape = jax.ShapeDtypeStruct((bsz, D_HEADS, D_STATE, D_HEADDIM), F32)
    in_specs = ([per_seq(256), per_seq(512), per_seq(128)] + const_specs + [st_spec, st_spec]
                + [pl.BlockSpec(memory_space=pl.ANY)])
    args = [z, xbc, dt, *consts, h0f, h0b, prev_out]
    aliases = {len(args) - 1: 0}
    return pl.pallas_call(
        _ssd_kernel,
        out_shape=[jax.ShapeDtypeStruct((N_TOK, D_INNER), F32), st_shape, st_shape],
        grid=(bsz,),
        in_specs=in_specs,
        out_specs=[per_seq(D_INNER), st_spec, st_spec],
        scratch_shapes=[pltpu.VMEM((seq, CONV_DIM), F32), pltpu.VMEM((seq, LANES), F32),
                        pltpu.VMEM((seq, LANES), F32), pltpu.VMEM((seq, D_INNER), F32)],
        input_output_aliases=aliases,
        compiler_params=_params("parallel"),
        name="ssd_lat" if latent else "ssd_ctx",
    )(*args)


def _out_proj_kernel(x_ref, oa_ref, ob_ref, oc_ref, od_ref, mod_ref, wo_ref, g2_ref, rwh_ref, rwl_ref,
                     rb_ref, x1_ref, h2_ref, te_ref, gt_ref, cnt_ref):
    m = mod_ref[0]
    gate1, sh2, sc2 = m[2:3], m[3:4], m[4:5]
    mixed = (_dot(oa_ref[...].astype(BF16), wo_ref[0:256, :])
             + _dot(ob_ref[...].astype(BF16), wo_ref[256:512, :])
             + _dot(oc_ref[...].astype(BF16), wo_ref[512:768, :])
             + _dot(od_ref[...].astype(BF16), wo_ref[768:1024, :]))
    x1 = x_ref[...] + gate1 * mixed
    x1_ref[...] = x1
    ms = jnp.mean(x1 * x1, axis=-1, keepdims=True)
    h2 = (x1 * lax.rsqrt(ms + EPS) * g2_ref[...]) * (1.0 + sc2) + sh2
    h2_ref[...] = h2.astype(BF16)

    hi = h2.astype(BF16)
    lo = (h2 - hi.astype(F32)).astype(BF16)
    logits = _dot(hi, rwh_ref[...]) + _dot(lo, rwh_ref[...]) + _dot(hi, rwl_ref[...]) + rb_ref[...]
    lane = lax.broadcasted_iota(jnp.int32, logits.shape, 1)
    vals, idxs = [], []
    for _ in range(TOP_K):
        mx = jnp.max(logits, axis=-1, keepdims=True)
        ix = jnp.min(jnp.where(logits == mx, lane, LANES), axis=-1, keepdims=True)
        vals.append(mx)
        idxs.append(ix)
        logits = jnp.where(lane == ix, -3e38, logits)
    es = [jnp.exp(v - vals[0]) for v in vals]
    den = es[0] + es[1] + es[2] + es[3]
    te = jnp.zeros(lane.shape, jnp.int32)
    gt = jnp.zeros(lane.shape, F32)
    member = jnp.zeros(lane.shape, F32)
    for k in range(TOP_K):
        te = jnp.where(lane == k, idxs[k], te)
        gt = jnp.where(lane == k, es[k] / den, gt)
        member = jnp.where(lane == idxs[k], 1.0, member)
    te_ref[...] = te[:, 0:TOP_K]
    gt_ref[...] = gt[:, 0:TOP_K]
    count = jnp.sum(member, axis=0, keepdims=True)
    cnt_ref[0] = jnp.broadcast_to(count, (8, LANES)).astype(jnp.int32)


def _out_proj(x, oa, ob, oc, od, mod3, wo, g2, rwh, rwl, rb):
    row = lambda w: pl.BlockSpec((TM, w), lambda i: (i, 0))
    full = lambda a: pl.BlockSpec(a.shape, lambda i: (0,) * a.ndim)
    return pl.pallas_call(
        _out_proj_kernel,
        out_shape=[jax.ShapeDtypeStruct((N_TOK, D_MODEL), F32), jax.ShapeDtypeStruct((N_TOK, D_MODEL), BF16),
                   jax.ShapeDtypeStruct((N_TOK, TOP_K), jnp.int32), jax.ShapeDtypeStruct((N_TOK, TOP_K), F32),
                   jax.ShapeDtypeStruct((N_TILES, 8, LANES), jnp.int32)],
        grid=(N_TILES,),
        in_specs=[row(D_MODEL), row(256), row(256), row(256), row(256),
                  pl.BlockSpec((1, N_MOD, D_MODEL), lambda i: (_mod_row(i), 0, 0)),
                  full(wo), full(g2), full(rwh), full(rwl), full(rb)],
        out_specs=[row(D_MODEL), row(D_MODEL), row(TOP_K), row(TOP_K),
                   pl.BlockSpec((1, 8, LANES), lambda i: (i, 0, 0))],
        compiler_params=_params("parallel"),
        name="out_proj_router",
    )(x, oa, ob, oc, od, mod3, wo, g2, rwh, rwl, rb)


def _tile_rows(te, tri_ref, off_ref):
    lane = lax.broadcasted_iota(jnp.int32, (TM, LANES), 1)
    hits = [lane == te[:, k:k + 1] for k in range(TOP_K)]
    member = jnp.zeros((TM, LANES), F32)
    for hit in hits:
        member = jnp.where(hit, 1.0, member)
    rank = _dot(tri_ref[...], member.astype(BF16))
    pos = rank + off_ref[0][0:1, :].astype(F32)
    return [jnp.sum(jnp.where(hit, pos, 0.0), axis=-1, keepdims=True) for hit in hits]


def _segment_chunks(seg_ref, len_ref, tile, visit):
    def segment(e, row):
        length = len_ref[tile * N_EXPERTS + e]
        start = seg_ref[tile * N_EXPERTS + e]

        def chunk(c, carry):
            visit(CHUNKS[0], pl.multiple_of(row + c * CHUNKS[0], SEG_ALIGN),
                  pl.multiple_of(start + c * CHUNKS[0], SEG_ALIGN))
            return carry

        lax.fori_loop(0, length // CHUNKS[0], chunk, 0)
        for size in CHUNKS[1:]:
            done = length // (2 * size) * (2 * size)

            @pl.when(length - done >= size)
            def _():
                visit(size, pl.multiple_of(row + done, SEG_ALIGN), pl.multiple_of(start + done, SEG_ALIGN))
        return row + length

    lax.fori_loop(0, N_EXPERTS, segment, 0)


def _wait_chunks(nchunk_ref, tile, copy):
    for j, size in enumerate(CHUNKS):
        lax.fori_loop(0, nchunk_ref[tile * len(CHUNKS) + j], lambda c, carry, size=size: (copy(size).wait(), carry)[1], 0)


def _dispatch_kernel(seg_ref, len_ref, nchunk_ref, end_ref, nu_ref, h2_ref, te_ref, off_ref, tri_ref,
                     xb_ref, buf, sems, sem_z):
    i = pl.program_id(0)
    slot = i % 2

    def chunk_copy(size, s, buf_row, xb_row):
        return pltpu.make_async_copy(buf.at[s, pl.ds(buf_row, size)], xb_ref.at[pl.ds(xb_row, size)], sems.at[s])

    def wait_chunks(tile, s):
        _wait_chunks(nchunk_ref, tile, lambda size: chunk_copy(size, s, 0, 0))

    @pl.when(i == 0)
    def _():
        buf[1, 0:ZERO_ROWS, :] = jnp.zeros((ZERO_ROWS, D_MODEL), F32)
        tail = lambda e: pltpu.make_async_copy(
            buf.at[1, pl.ds(0, ZERO_ROWS)],
            xb_ref.at[pl.ds(pl.multiple_of(jnp.maximum(end_ref[e] - ZERO_ROWS, 0), SEG_ALIGN), ZERO_ROWS)], sem_z)
        block = lambda b: pltpu.make_async_copy(
            buf.at[1, pl.ds(0, MOE_BM)], xb_ref.at[pl.ds(pl.multiple_of(b * MOE_BM, MOE_BM), MOE_BM)], sem_z)
        for e in range(N_EXPERTS):
            tail(e).start()
        lax.fori_loop(nu_ref[0], MOE_BLOCKS, lambda b, carry: (block(b).start(), carry)[1], 0)
        for e in range(N_EXPERTS):
            tail(e).wait()
        lax.fori_loop(nu_ref[0], MOE_BLOCKS, lambda b, carry: (block(b).wait(), carry)[1], 0)

    rows = _tile_rows(te_ref[...], tri_ref, off_ref)
    lane = lax.broadcasted_iota(jnp.int32, (TM, LANES), 1)
    packed = jnp.zeros((TM, LANES), F32)
    for k in range(TOP_K):
        packed = jnp.where(lane == k, rows[k], packed)
    rows_t = packed.T
    buf_row = lax.broadcasted_iota(jnp.int32, (TILE_BUF, TM), 0).astype(F32)
    pick = jnp.zeros((TILE_BUF, TM), F32)
    for k in range(TOP_K):
        pick = jnp.where(buf_row == rows_t[k:k + 1, :], 1.0, pick)
    buf[slot] = _dot(pick.astype(BF16), h2_ref[...])

    @pl.when(i > 0)
    def _():
        wait_chunks(i - 1, 1 - slot)

    _segment_chunks(seg_ref, len_ref, i, lambda size, b, x: chunk_copy(size, slot, b, x).start())

    @pl.when(i == N_TILES - 1)
    def _():
        wait_chunks(i, slot)


def _dispatch(seg_start, seg_len, n_chunk, pad_end, n_used, h2, te, seg_off, tri):
    row = lambda w: pl.BlockSpec((TM, w), lambda i, *_: (i, 0))
    grid_spec = pltpu.PrefetchScalarGridSpec(
        num_scalar_prefetch=5,
        grid=(N_TILES,),
        in_specs=[row(D_MODEL), row(TOP_K), pl.BlockSpec((1, 8, LANES), lambda i, *_: (i, 0, 0)),
                  pl.BlockSpec(tri.shape, lambda i, *_: (0, 0))],
        out_specs=pl.BlockSpec(memory_space=pl.ANY),
        scratch_shapes=[pltpu.VMEM((2, TILE_BUF, D_MODEL), F32),
                        pltpu.SemaphoreType.DMA((2,)), pltpu.SemaphoreType.DMA],
    )
    return pl.pallas_call(
        _dispatch_kernel,
        out_shape=jax.ShapeDtypeStruct((YB_ROWS, D_MODEL), F32),
        grid_spec=grid_spec,
        compiler_params=_params("arbitrary"),
        name="moe_dispatch",
    )(seg_start, seg_len, n_chunk, pad_end, n_used, h2, te, seg_off, tri)


def _expert_kernel(be_ref, nu_ref, x_ref, wgu_ref, bgu_ref, wdn_ref, bdn_ref, o_ref, wgu_s, wdn_s):
    i = pl.program_id(0)
    used = i < nu_ref[0]
    prev = be_ref[jnp.maximum(i - 1, 0)]
    fresh = jnp.logical_or(i == 0, be_ref[i] != prev)

    @pl.when(jnp.logical_and(used, fresh))
    def _():
        wgu_s[...] = wgu_ref[0, 0].astype(BF16)
        wdn_s[...] = wdn_ref[0, 0].astype(BF16)

    @pl.when(used)
    def _():
        hgu = _dot(x_ref[...].astype(BF16), wgu_s[...]) + bgu_ref[0, 0]
        gate = jnp.minimum(hgu[:, :D_FF], SWIGLU_LIMIT)
        up = jnp.clip(hgu[:, D_FF:], -SWIGLU_LIMIT, SWIGLU_LIMIT)
        act = (up + 1.0) * gate * jax.nn.sigmoid(SWIGLU_ALPHA * gate)
        o_ref[...] = _dot(act.astype(BF16), wdn_s[...]) + bdn_ref[0, 0]

    @pl.when(jnp.logical_not(used))
    def _():
        o_ref[...] = jnp.zeros_like(o_ref)


def _experts(layer, block_e, n_used, xb, w_gu, b_gu, w_dn, b_dn):
    grid_spec = pltpu.PrefetchScalarGridSpec(
        num_scalar_prefetch=2,
        grid=(MOE_BLOCKS,),
        in_specs=[pl.BlockSpec((MOE_BM, D_MODEL), lambda i, be, nu: (i, 0)),
                  pl.BlockSpec((1, 1, D_MODEL, 2 * D_FF), lambda i, be, nu: (layer, be[i], 0, 0)),
                  pl.BlockSpec((1, 1, 1, 2 * D_FF), lambda i, be, nu: (layer, be[i], 0, 0)),
                  pl.BlockSpec((1, 1, D_FF, D_MODEL), lambda i, be, nu: (layer, be[i], 0, 0)),
                  pl.BlockSpec((1, 1, 1, D_MODEL), lambda i, be, nu: (layer, be[i], 0, 0))],
        out_specs=pl.BlockSpec((MOE_BM, D_MODEL), lambda i, be, nu: (i, 0)),
        scratch_shapes=[pltpu.VMEM((D_MODEL, 2 * D_FF), BF16), pltpu.VMEM((D_FF, D_MODEL), BF16)],
    )
    return pl.pallas_call(
        _expert_kernel,
        out_shape=jax.ShapeDtypeStruct((YB_ROWS, D_MODEL), F32),
        grid_spec=grid_spec,
        compiler_params=_params("arbitrary"),
        name="experts",
    )(block_e, n_used, xb, w_gu, b_gu.reshape(DEPTH, N_EXPERTS, 1, 2 * D_FF), w_dn,
      b_dn.reshape(DEPTH, N_EXPERTS, 1, D_MODEL))


def _combine_kernel(final, seg_ref, len_ref, nchunk_ref, x1_ref, te_ref, gt_ref, off_ref, mod_ref, fg_ref,
                    tri_ref, yb_ref, o_ref, buf, sems):
    i = pl.program_id(0)
    slot = i % 2

    def chunk_copy(size, s, buf_row, yb_row):
        return pltpu.make_async_copy(yb_ref.at[pl.ds(yb_row, size)], buf.at[s, pl.ds(buf_row, size)], sems.at[s])

    def request(tile, s):
        _segment_chunks(seg_ref, len_ref, tile, lambda size, b, y: chunk_copy(size, s, b, y).start())

    @pl.when(i == 0)
    def _():
        buf[...] = jnp.zeros_like(buf)
        request(0, 0)

    @pl.when(i + 1 < N_TILES)
    def _():
        request(i + 1, 1 - slot)

    rows = _tile_rows(te_ref[...], tri_ref, off_ref)
    gt = gt_ref[...]
    buf_row = lax.broadcasted_iota(jnp.int32, (TM, TILE_BUF), 1).astype(F32)
    place = jnp.zeros((TM, TILE_BUF), F32)
    for k in range(TOP_K):
        place = jnp.where(buf_row == rows[k], gt[:, k:k + 1], place)

    _wait_chunks(nchunk_ref, i, lambda size: chunk_copy(size, slot, 0, 0))
    y = _dot(place.astype(BF16), buf[slot].astype(BF16))
    x2 = x1_ref[...] + mod_ref[0][5:6] * y
    if final:
        ms = jnp.mean(x2 * x2, axis=-1, keepdims=True)
        x2 = x2 * lax.rsqrt(ms + EPS) * fg_ref[...]
    o_ref[...] = x2


def _combine(seg_start, seg_len, n_chunk, x1, te, gates, seg_off, mod3, fg, tri, yb, final):
    row = lambda w: pl.BlockSpec((TM, w), lambda i, *_: (i, 0))
    full = lambda a: pl.BlockSpec(a.shape, lambda i, *_: (0,) * a.ndim)
    grid_spec = pltpu.PrefetchScalarGridSpec(
        num_scalar_prefetch=3,
        grid=(N_TILES,),
        in_specs=[row(D_MODEL), row(TOP_K), row(TOP_K),
                  pl.BlockSpec((1, 8, LANES), lambda i, *_: (i, 0, 0)),
                  pl.BlockSpec((1, N_MOD, D_MODEL), lambda i, *_: (_mod_row(i), 0, 0)),
                  full(fg), full(tri),
                  pl.BlockSpec(memory_space=pl.ANY)],
        out_specs=row(D_MODEL),
        scratch_shapes=[pltpu.VMEM((2, TILE_BUF, D_MODEL), F32), pltpu.SemaphoreType.DMA((2,))],
    )
    return pl.pallas_call(
        functools.partial(_combine_kernel, final),
        out_shape=jax.ShapeDtypeStruct((N_TOK, D_MODEL), F32),
        grid_spec=grid_spec,
        compiler_params=_params("arbitrary"),
        name="moe_combine",
    )(seg_start, seg_len, n_chunk, x1, te, gates, seg_off, mod3, fg, tri, yb)


def _w1_index():
    idx = np.full((W1_COLS,), IN_WIDTH, np.int32)
    for g in range(2):
        for kv in range(2):
            idx[g * 128 + kv * 64:g * 128 + kv * 64 + 64] = np.arange(64) + kv * 128 + g * 64
    idx[256:512] = np.arange(256) + IN_OFF['a_k']
    idx[512:704] = np.arange(192) + IN_OFF['b_cq']
    idx[768:896] = np.arange(128) + IN_OFF['b_ckv']
    idx[896:928] = np.arange(32) + IN_OFF['b_kr']
    idx[1024:1792] = np.arange(768) + IN_OFF['c_q']
    idx[1792:2048] = np.arange(256) + IN_OFF['d_z']
    idx[2048:2560] = np.arange(512) + IN_OFF['d_xbc']
    idx[2560:2568] = np.arange(8) + IN_OFF['d_dtf']
    return idx


def _wout_a_index():
    idx = np.zeros((256,), np.int32)
    for g in range(2):
        for kv in range(2):
            idx[g * 128 + kv * 64:g * 128 + kv * 64 + 64] = np.arange(64) + kv * 128 + g * 64
    return idx


def _rope_tables():
    t = np.arange(DEC_SEQ)
    pos = ((t // GRID_W).astype(np.float32), (t % GRID_W).astype(np.float32))

    def unit(rot_dim):
        quarter, half = rot_dim // 4, rot_dim // 2
        inv = ROPE_THETA ** (-np.arange(0, half, 2, dtype=np.float32) / half)
        cos = np.zeros((DEC_SEQ, rot_dim), np.float32)
        sin = np.zeros((DEC_SEQ, rot_dim), np.float32)
        for seg in range(4):
            ang = pos[seg // 2][:, None] * inv[None, :].astype(np.float32)
            cos[:, seg * quarter:(seg + 1) * quarter] = np.cos(ang)
            sin[:, seg * quarter:(seg + 1) * quarter] = np.sin(ang) * (-1.0 if seg % 2 == 0 else 1.0)
        return cos, sin

    specs = [
        (HEAD_DIM, (0, 64), (), HEAD_DIM ** -0.5),
        (HEAD_DIM, (0, 64), (), 1.0),
        (MLA_ROPE, (MLA_NOPE,), (0, MLA_NOPE), (MLA_NOPE + MLA_ROPE) ** -0.5),
        (MLA_ROPE, (0,), (), 1.0),
        (DIFF_D, (0, 32, 64, 96), (), DIFF_D ** -0.5),
        (DIFF_D, (0, 32, 64, 96), (), 1.0),
    ]
    lat_cols, ident_cols = [], []
    for rot_dim, starts, passthrough, scale in specs:
        ucos, usin = unit(rot_dim)
        cos = np.zeros((DEC_SEQ, LANES), np.float32)
        sin = np.zeros((DEC_SEQ, LANES), np.float32)
        ident = np.zeros((1, LANES), np.float32)
        if passthrough:
            cos[:, passthrough[0]:passthrough[1]] = 1.0
            ident[:, passthrough[0]:passthrough[1]] = 1.0
        for s in starts:
            cos[:, s:s + rot_dim] = ucos
            sin[:, s:s + rot_dim] = usin
            ident[:, s:s + rot_dim] = 1.0
        lat_cols += [cos * scale, sin * scale]
        ident_cols += [ident * scale, np.zeros((1, LANES), np.float32)]
    lat = np.concatenate(lat_cols, axis=1)
    ident_blk = np.broadcast_to(np.concatenate(ident_cols, axis=1), (TM, N_TAB * LANES))
    return np.concatenate([ident_blk, lat], axis=0).astype(np.float32)


def _block_diag_ones(n, blk):
    r = np.arange(n)
    return (r[:, None] // blk == r[None, :] // blk).astype(np.float32)


def _chunk_tables(seg_len):
    counts = [jnp.sum(seg_len // CHUNKS[0], axis=1)]
    counts += [jnp.sum(seg_len % (2 * size) // size, axis=1) for size in CHUNKS[1:]]
    return jnp.stack(counts, axis=1).reshape(-1).astype(jnp.int32)


def _strict_lower_ones(n):
    r = np.arange(n)
    return (r[None, :] < r[:, None]).astype(np.float32)


def kernel(x_prompt, x_sample, cache_gqa_k, cache_gqa_v, cache_mla_ckv, cache_mla_krope, cache_diff_k, cache_diff_v, state_ssd_fwd, state_ssd_bwd, c, c_ctx, norm1_g, norm2_g, w_ada, b_ada, w_in, w_out, gqa_qn_g, gqa_kn_g, mla_qa_g, mla_wqb, mla_kva_g, mla_wkvb, diff_lq1, diff_lk1, diff_lq2, diff_lk2, diff_subln_g, ssd_conv_w, ssd_conv_b, ssd_a_log_f, ssd_a_log_b, ssd_dt_bias_f, ssd_dt_bias_b, ssd_d, ssd_norm_g, router_w, router_b, moe_w_gu, moe_b_gu, moe_w_dn, moe_b_dn, final_g):
    tabs = jnp.asarray(_rope_tables())
    bd = jnp.asarray(_block_diag_ones(256, HEAD_DIM), BF16)
    tri = jnp.asarray(_strict_lower_ones(TM), BF16)
    w1_idx = _w1_index()
    woa_idx = _wout_a_index()

    cvec = jnp.zeros((MOD_ROWS, D_MODEL), F32).at[0].set(c_ctx).at[1:1 + DEC_BATCH].set(c)
    mod = _modulation(cvec, w_ada, b_ada).reshape(DEPTH, MOD_ROWS, N_MOD, D_MODEL)

    x = jnp.concatenate([x_prompt.reshape(N_CTX, D_MODEL), x_sample.reshape(N_LAT, D_MODEL)], axis=0)
    new_ctx = []
    for l in range(DEPTH):
        mod3 = mod[l]
        w1 = jnp.concatenate([w_in[l], jnp.zeros((D_MODEL, 1), F32)], axis=1)[:, w1_idx].astype(BF16)
        wqb = mla_wqb[l].reshape(Q_LORA, B_HEADS, MLA_NOPE + MLA_ROPE)
        wqb = jnp.pad(wqb, ((0, 256 - Q_LORA), (0, 0), (0, LANES - MLA_NOPE - MLA_ROPE)))
        wqb = wqb.reshape(256, B_HEADS * LANES).astype(BF16)
        wkvb = mla_wkvb[l].reshape(KV_LORA, B_HEADS, MLA_NOPE + MLA_V)
        wk_nope = jnp.pad(wkvb[:, :, :MLA_NOPE], ((0, 0), (0, 0), (0, LANES - MLA_NOPE)))
        eye_r = jnp.zeros((LANES, B_HEADS, LANES), F32)
        eye_r = eye_r.at[jnp.arange(MLA_ROPE), :, MLA_NOPE + jnp.arange(MLA_ROPE)].set(1.0)
        wk = jnp.concatenate([wk_nope, eye_r], axis=0).reshape(256, B_HEADS * LANES).astype(BF16)
        wv = wkvb[:, :, MLA_NOPE:].reshape(KV_LORA, B_HEADS * MLA_V).astype(BF16)
        wo = jnp.concatenate([w_out[l][woa_idx], w_out[l][256:]], axis=0).astype(BF16)
        gq = jnp.tile(gqa_qn_g[l], 4)[None, :]
        gk = jnp.tile(gqa_kn_g[l], 2)[None, :]
        gqa = jnp.pad(mla_qa_g[l], (0, 256 - Q_LORA))[None, :]
        gkva = mla_kva_g[l][None, :]
        rw = jnp.pad(router_w[l], ((0, 0), (0, LANES - N_EXPERTS)))
        rwh = rw.astype(BF16)
        rwl = (rw - rwh.astype(F32)).astype(BF16)
        rb = jnp.pad(router_b[l], (0, LANES - N_EXPERTS), constant_values=NEG_BIG)[None, :]

        qa, kva, qb, ck, qc, kc, vc, z, xbc, dt = _in_proj(
            x, mod3, norm1_g[l][None, :], w1, bd, gq, gk, gqa, gkva, wqb, tabs)

        past_kva = jnp.concatenate([cache_gqa_k[:, l].reshape(DEC_BATCH, PAST_LEN, 128),
                                    cache_gqa_v[:, l].reshape(DEC_BATCH, PAST_LEN, 128)], axis=-1)
        past_ck = jnp.concatenate([cache_mla_ckv[:, l], cache_mla_krope[:, l],
                                   jnp.zeros((DEC_BATCH, PAST_LEN, LANES - MLA_ROPE), F32)], axis=-1)
        past_kc = cache_diff_k[:, l].reshape(DEC_BATCH, PAST_LEN, 256)
        past_vc = cache_diff_v[:, l].reshape(DEC_BATCH, PAST_LEN, 256)
        lams = [a[l][None, :] for a in (diff_lq1, diff_lk1, diff_lq2, diff_lk2)]
        gsub = jnp.tile(diff_subln_g[l], 2)[None, :]
        lam_init = 0.8 - 0.6 * math.exp(-0.3 * l)
        oa, ob, oc = _attention_mixers(qa, kva, qb, ck, qc, kc, vc, past_kva, past_ck, past_kc, past_vc,
                                       wk, wv, lams, gsub, lam_init)

        pad8 = lambda f, b: jnp.pad(jnp.concatenate([f, b]), (0, LANES - 2 * D_HEADS))[None, :]
        ssd_consts = (ssd_conv_w[l], ssd_conv_b[l][None, :], pad8(ssd_dt_bias_f[l], ssd_dt_bias_b[l]),
                      pad8(ssd_a_log_f[l], ssd_a_log_b[l]), jnp.repeat(ssd_d[l], D_HEADDIM)[None, :],
                      ssd_norm_g[l][None, :])
        zeros_st = jnp.zeros((BATCH, D_HEADS, D_STATE, D_HEADDIM), F32)
        od, hf_c, hb_c = _ssd(False, z, xbc, dt, ssd_consts, zeros_st, zeros_st, jnp.zeros((N_TOK, D_INNER), F32))
        od, _, _ = _ssd(True, z, xbc, dt, ssd_consts, jnp.swapaxes(state_ssd_fwd[:, l], -1, -2),
                        jnp.swapaxes(state_ssd_bwd[:, l], -1, -2), od)

        x1, h2, te, gates, tile_cnt = _out_proj(x, oa, ob, oc, od, mod3, wo, norm2_g[l][None, :], rwh, rwl, rb)

        seg_cnt = tile_cnt[:, 0, :N_EXPERTS]
        seg_len = (seg_cnt + SEG_ALIGN - 1) // SEG_ALIGN * SEG_ALIGN
        region = (jnp.sum(seg_len, axis=0) + MOE_BM - 1) // MOE_BM * MOE_BM
        pad_end = jnp.cumsum(region).astype(jnp.int32)
        pad_start = pad_end - region
        seg_start = (pad_start[None, :] + jnp.cumsum(seg_len, axis=0) - seg_len).astype(jnp.int32)
        n_used = (pad_end[-1] // MOE_BM).astype(jnp.int32).reshape(1)
        blk_start = jnp.arange(MOE_BLOCKS, dtype=jnp.int32) * MOE_BM
        block_e = jnp.minimum(jnp.sum((pad_end[None, :] <= blk_start[:, None]).astype(jnp.int32), axis=1),
                              N_EXPERTS - 1).astype(jnp.int32)
        n_chunk = _chunk_tables(seg_len)
        seg_off = jnp.cumsum(seg_len, axis=1) - seg_len
        seg_off = jnp.broadcast_to(jnp.pad(seg_off, ((0, 0), (0, LANES - N_EXPERTS)))[:, None, :],
                                   (N_TILES, 8, LANES)).astype(jnp.int32)
        seg_start, seg_len = seg_start.reshape(-1), seg_len.reshape(-1).astype(jnp.int32)

        xb = _dispatch(seg_start, seg_len, n_chunk, pad_end, n_used, h2, te, seg_off, tri)
        yb = _experts(l, block_e, n_used, xb, moe_w_gu, moe_b_gu, moe_w_dn, moe_b_dn)
        x = _combine(seg_start, seg_len, n_chunk, x1, te, gates, seg_off, mod3, final_g[None, :], tri, yb,
                     l == DEPTH - 1)

        kva_c, ck_c = kva[:N_CTX], ck[:N_CTX]
        new_ctx.append((kva_c[:, 0:128].reshape(BATCH, SEQ, A_KV_HEADS, HEAD_DIM),
                        kva_c[:, 128:256].reshape(BATCH, SEQ, A_KV_HEADS, HEAD_DIM),
                        ck_c[:, 0:KV_LORA].reshape(BATCH, SEQ, KV_LORA),
                        ck_c[:, KV_LORA:KV_LORA + MLA_ROPE].reshape(BATCH, SEQ, MLA_ROPE),
                        kc[:N_CTX].reshape(BATCH, SEQ, C_HEADS, 2 * DIFF_D),
                        vc[:N_CTX].reshape(BATCH, SEQ, C_HEADS, DIFF_V),
                        jnp.swapaxes(hf_c, -1, -2), jnp.swapaxes(hb_c, -1, -2)))

    y_prompt = x[:N_CTX].reshape(BATCH, SEQ, D_MODEL)
    y_sample = x[N_CTX:].reshape(DEC_BATCH, DEC_SEQ, D_MODEL)
    caches = [jnp.stack([cl[i] for cl in new_ctx], axis=1) for i in range(8)]
    return (y_prompt, y_sample, *caches)
```

```python
import functools
import math

import numpy as np
import jax
import jax.numpy as jnp
from jax import lax
from jax.experimental import pallas as pl
from jax.experimental.pallas import tpu as pltpu

F32 = jnp.float32
BF16 = jnp.bfloat16

D_MODEL = 1024
BATCH = 16
SEQ = 256
DEPTH = 2
DEC_BATCH = 8
DEC_SEQ = 2048
PAST_LEN = 256
GRID_W = 64
ROPE_THETA = 10000.0
EPS = 1e-6
HEAD_DIM = 64
A_HEADS = 4
A_KV_HEADS = 2
B_HEADS = 4
MLA_NOPE = 64
MLA_ROPE = 32
MLA_V = 64
Q_LORA = 192
KV_LORA = 128
C_HEADS = 4
DIFF_D = 32
DIFF_V = 64
SUBLN_EPS = 1e-5
D_HEADS = 4
D_HEADDIM = 64
D_INNER = 256
D_GROUPS = 2
D_STATE = 64
CONV_DIM = 512
SSD_CHUNK = 128
N_EXPERTS = 32
TOP_K = 4
D_FF = 1024
SWIGLU_ALPHA = 1.702
SWIGLU_LIMIT = 7.0
N_MOD = 6

N_CTX = BATCH * SEQ
N_LAT = DEC_BATCH * DEC_SEQ
N_TOK = N_CTX + N_LAT

LANES = 128
TM = 256
N_TILES = N_TOK // TM
CTX_TILES = N_CTX // TM
LAT_TILES_PER_SEQ = DEC_SEQ // TM
MOD_ROWS = 16
MOE_BM = 512
N_PAIRS = N_TOK * TOP_K
SEG_ALIGN = 8
CHUNKS = (32, 16, 8)
MOE_BLOCKS = -(-(N_PAIRS + N_TILES * N_EXPERTS * (SEG_ALIGN - 1) + N_EXPERTS * (MOE_BM - 1)) // MOE_BM)
YB_ROWS = MOE_BLOCKS * MOE_BM
TILE_BUF = -(-(TM * TOP_K + N_EXPERTS * (SEG_ALIGN - 1)) // LANES) * LANES
ZERO_ROWS = MOE_BM
VMEM_LIMIT = 56 * 1024 * 1024
NEG_BIG = -1e30

W1_COLS = 2688
IN_OFF = dict(a_q=0, a_k=256, a_v=384, b_cq=512, b_ckv=704, b_kr=832, c_q=864, c_k=1120,
              c_v=1376, d_z=1632, d_xbc=1888, d_dtf=2400, d_dtb=2404)
IN_WIDTH = 2408
N_TAB = 12


def _mod_row(i):
    return jnp.where(i < CTX_TILES, 0, 1 + (i - CTX_TILES) // LAT_TILES_PER_SEQ)


def _tab_block(i):
    return jnp.where(i < CTX_TILES, 0, 1 + (i - CTX_TILES) % LAT_TILES_PER_SEQ)


def _dot(a, b):
    return jnp.dot(a, b, preferred_element_type=F32)


def _dot_nt(a, b):
    return lax.dot_general(a, b, (((1,), (1,)), ((), ())), preferred_element_type=F32)


def _dot_split(x, m):
    hi = x.astype(BF16)
    lo = (x - hi.astype(F32)).astype(BF16)
    return _dot(hi, m) + _dot(lo, m)


def _dot_split_left(m, x):
    hi = x.astype(BF16)
    lo = (x - hi.astype(F32)).astype(BF16)
    return _dot(m, hi) + _dot(m, lo)


def _silu(x):
    return x * jax.nn.sigmoid(x)


def _params(*semantics):
    return pltpu.CompilerParams(dimension_semantics=semantics, vmem_limit_bytes=VMEM_LIMIT)


MOD_TN = 1536


def _mod_kernel(c_ref, w_ref, b_ref, o_ref):
    c = c_ref[...]
    s = _silu(c).astype(BF16)
    o_ref[0] = _dot(s, w_ref[0].astype(BF16)) + b_ref[0]


def _modulation(cvec, w_ada, b_ada):
    n = N_MOD * D_MODEL
    return pl.pallas_call(
        _mod_kernel,
        out_shape=jax.ShapeDtypeStruct((DEPTH, MOD_ROWS, n), F32),
        grid=(DEPTH, n // MOD_TN),
        in_specs=[pl.BlockSpec((MOD_ROWS, D_MODEL), lambda l, j: (0, 0)),
                  pl.BlockSpec((1, D_MODEL, MOD_TN), lambda l, j: (l, 0, j)),
                  pl.BlockSpec((1, 1, MOD_TN), lambda l, j: (l, 0, j))],
        out_specs=pl.BlockSpec((1, MOD_ROWS, MOD_TN), lambda l, j: (l, 0, j)),
        compiler_params=_params("parallel", "parallel"),
        name="adaln_mod",
    )(cvec, w_ada, b_ada.reshape(DEPTH, 1, n))


def _rope(x, cos, sin, quarter):
    lane = lax.broadcasted_iota(jnp.int32, (x.shape[0], LANES), 1)
    first = (lane // quarter) % 2 == 0
    outs = []
    for t in range(x.shape[1] // LANES):
        xt = x[:, t * LANES:(t + 1) * LANES]
        partner = jnp.where(first, pltpu.roll(xt, LANES - quarter, 1), pltpu.roll(xt, quarter, 1))
        outs.append(xt * cos + partner * sin)
    return outs


def _in_proj_kernel(x_ref, mod_ref, g1_ref, w1_ref, bd_ref, gq_ref, gk_ref, gqa_ref, gkva_ref,
                    wqb_ref, tab_ref,
                    qa_ref, kva_ref, qb_ref, ck_ref, qc_ref, kc_ref, vc_ref, z_ref, xbc_ref, dt_ref):
    x = x_ref[...]
    m = mod_ref[0]
    sh1, sc1 = m[0:1], m[1:2]
    ms = jnp.mean(x * x, axis=-1, keepdims=True)
    h = (x * lax.rsqrt(ms + EPS) * g1_ref[...]) * (1.0 + sc1) + sh1
    u = _dot(h.astype(BF16), w1_ref[...])

    def tab(k):
        return tab_ref[:, k * LANES:(k + 1) * LANES]

    bd = bd_ref[...]

    def head_norm(v, gain):
        w = v.shape[1]
        ss = _dot_split(v * v, bd[:w, :w])
        return v * lax.rsqrt(ss * (1.0 / HEAD_DIM) + EPS) * gain

    qa = _rope(head_norm(u[:, 0:256], gq_ref[...]), tab(0), tab(1), HEAD_DIM // 4)
    for t in range(2):
        qa_ref[:, t * LANES:(t + 1) * LANES] = qa[t].astype(BF16)
    ka = _rope(head_norm(u[:, 256:384], gk_ref[...]), tab(2), tab(3), HEAD_DIM // 4)
    kva_ref[:, 0:128] = ka[0]
    kva_ref[:, 128:256] = u[:, 384:512]

    cq = u[:, 512:768]
    msq = jnp.sum(cq * cq, axis=-1, keepdims=True) * (1.0 / Q_LORA)
    yq = cq * lax.rsqrt(msq + EPS) * gqa_ref[...]
    qb = _rope(_dot(yq.astype(BF16), wqb_ref[...]), tab(4), tab(5), MLA_ROPE // 4)
    for t in range(4):
        qb_ref[:, t * LANES:(t + 1) * LANES] = qb[t].astype(BF16)
    ckv = u[:, 768:896]
    msk = jnp.mean(ckv * ckv, axis=-1, keepdims=True)
    ck_ref[:, 0:128] = ckv * lax.rsqrt(msk + EPS) * gkva_ref[...]
    ck_ref[:, 128:256] = _rope(u[:, 896:1024], tab(6), tab(7), MLA_ROPE // 4)[0]

    qc = _rope(u[:, 1024:1280], tab(8), tab(9), DIFF_D // 4)
    kc = _rope(u[:, 1280:1536], tab(10), tab(11), DIFF_D // 4)
    for t in range(2):
        qc_ref[:, t * LANES:(t + 1) * LANES] = qc[t].astype(BF16)
        kc_ref[:, t * LANES:(t + 1) * LANES] = kc[t]
    vc_ref[...] = u[:, 1536:1792]

    z_ref[...] = u[:, 1792:2048]
    xbc_ref[...] = u[:, 2048:2560]
    dt_ref[...] = u[:, 2560:2688]


def _in_proj(x, mod3, g1, w1, bd, gq, gk, gqa, gkva, wqb, tabs):
    row = lambda w: pl.BlockSpec((TM, w), lambda i: (i, 0))
    full = lambda a: pl.BlockSpec(a.shape, lambda i: (0,) * a.ndim)
    outs = [(256, BF16), (256, F32), (512, BF16), (256, F32), (256, BF16), (256, F32), (256, F32),
            (256, F32), (512, F32), (128, F32)]
    return pl.pallas_call(
        _in_proj_kernel,
        out_shape=[jax.ShapeDtypeStruct((N_TOK, w), d) for w, d in outs],
        grid=(N_TILES,),
        in_specs=[row(D_MODEL),
                  pl.BlockSpec((1, N_MOD, D_MODEL), lambda i: (_mod_row(i), 0, 0)),
                  full(g1), full(w1), full(bd), full(gq), full(gk), full(gqa), full(gkva), full(wqb),
                  pl.BlockSpec((TM, N_TAB * LANES), lambda i: (_tab_block(i), 0))],
        out_specs=[row(w) for w, _ in outs],
        compiler_params=_params("parallel"),
        name="in_proj",
    )(x, mod3, g1, w1, bd, gq, gk, gqa, gkva, wqb, tabs)


def _attend(q, k, v_ones):
    s = _dot_nt(q, k)
    e = jnp.exp2(s - jnp.max(s, axis=-1, keepdims=True))
    out = _dot(e.astype(BF16), v_ones)
    return out[:, :LANES] / out[:, LANES:]


def _with_ones(v):
    return jnp.concatenate([v, jnp.ones_like(v)], axis=-1)


def _half_mask(rows):
    lane = lax.broadcasted_iota(jnp.int32, (rows, LANES), 1)
    return lane < (LANES // 2)


def _keys(past_ref, new_ref, lo, hi):
    new = new_ref[:, lo:hi].astype(BF16)
    if past_ref is None:
        return new
    return jnp.concatenate([past_ref[0, :, lo:hi].astype(BF16), new], axis=0)


def _attn_a_kernel(latent, *refs):
    if latent:
        q_ref, kv_ref, past_ref, _, o_ref = refs
    else:
        (q_ref, kv_ref, _, o_ref), past_ref = refs, None
    q = q_ref[...]
    k = _keys(past_ref, kv_ref, 0, 128)
    v = _with_ones(_keys(past_ref, kv_ref, 128, 256))
    lo = _half_mask(q.shape[0])
    for g in range(2):
        qt = q[:, g * LANES:(g + 1) * LANES].astype(F32)
        res = []
        for half in range(2):
            qm = jnp.where(lo, qt, 0.0) if half == 0 else jnp.where(lo, 0.0, qt)
            res.append(_attend(qm.astype(BF16), k, v))
        o_ref[:, g * LANES:(g + 1) * LANES] = jnp.where(lo, res[0], res[1])


def _attn_b_kernel(latent, *refs):
    if latent:
        q_ref, ck_ref, past_ref, wk_ref, wv_ref, _, o_ref, k_s, v_s = refs
    else:
        (q_ref, ck_ref, wk_ref, wv_ref, _, o_ref, k_s, v_s), past_ref = refs, None

    @pl.when(pl.program_id(1) == 0)
    def _():
        ck = _keys(past_ref, ck_ref, 0, 256)
        k_s[...] = _dot(ck, wk_ref[...]).astype(BF16)
        v = _dot(ck[:, 0:128], wv_ref[...]).astype(BF16)
        for j in range(2):
            v_s[:, 2 * j * LANES:2 * (j + 1) * LANES] = _with_ones(v[:, j * LANES:(j + 1) * LANES])

    q = q_ref[...]
    lo = _half_mask(q.shape[0])
    for j in range(2):
        v = v_s[:, 2 * j * LANES:2 * (j + 1) * LANES]
        res = []
        for half in range(2):
            h = 2 * j + half
            res.append(_attend(q[:, h * LANES:(h + 1) * LANES], k_s[:, h * LANES:(h + 1) * LANES], v))
        o_ref[:, j * LANES:(j + 1) * LANES] = jnp.where(lo, res[0], res[1])


def _attn_c_kernel(lam_init, latent, *refs):
    if latent:
        q_ref, k_ref, v_ref, pk_ref, pv_ref, lq1_ref, lk1_ref, lq2_ref, lk2_ref, g_ref, _, o_ref = refs
    else:
        (q_ref, k_ref, v_ref, lq1_ref, lk1_ref, lq2_ref, lk2_ref, g_ref, _, o_ref), pk_ref, pv_ref = refs, None, None
    lam = (jnp.exp(jnp.sum(lq1_ref[...] * lk1_ref[...], axis=-1, keepdims=True))
           - jnp.exp(jnp.sum(lq2_ref[...] * lk2_ref[...], axis=-1, keepdims=True)) + lam_init)
    q = q_ref[...]
    rows = q.shape[0]
    lane = lax.broadcasted_iota(jnp.int32, (rows, LANES), 1)
    lo = lane < (LANES // 2)
    for j in range(2):
        qt = q[:, j * LANES:(j + 1) * LANES].astype(F32)
        k = _keys(pk_ref, k_ref, j * LANES, (j + 1) * LANES)
        v = _with_ones(_keys(pv_ref, v_ref, j * LANES, (j + 1) * LANES))
        res = []
        for half in range(2):
            parts = []
            for t in range(2):
                quarter = 2 * half + t
                qm = jnp.where(lane // (LANES // 4) == quarter, qt, 0.0)
                parts.append(_attend(qm.astype(BF16), k, v))
            res.append(parts[0] - lam * parts[1])
        o = jnp.where(lo, res[0], res[1])
        o2 = o * o
        ss_lo = jnp.sum(jnp.where(lo, o2, 0.0), axis=-1, keepdims=True)
        ss_hi = jnp.sum(jnp.where(lo, 0.0, o2), axis=-1, keepdims=True)
        ss = jnp.where(lo, ss_lo, ss_hi) * (1.0 / DIFF_V)
        o_ref[:, j * LANES:(j + 1) * LANES] = (o * lax.rsqrt(ss + SUBLN_EPS) * g_ref[...]) * (1.0 - lam_init)


def _seq_call(body, name, latent, q, news, pasts, consts, prev_out, scratch=(), q_semantics="parallel"):
    const_specs = [pl.BlockSpec(a.shape, lambda b, i, n=a.ndim: (0,) * n) for a in consts]
    if latent:
        tile = lambda b, i: (CTX_TILES + b * LAT_TILES_PER_SEQ + i, 0)
        grid = (DEC_BATCH, LAT_TILES_PER_SEQ)
        new_specs = [pl.BlockSpec((DEC_SEQ, a.shape[1]), lambda b, i: (N_CTX // DEC_SEQ + b, 0)) for a in news]
        past_specs = [pl.BlockSpec((1, PAST_LEN, a.shape[2]), lambda b, i: (b, 0, 0)) for a in pasts]
    else:
        tile = lambda b, i: (b, 0)
        grid = (BATCH, 1)
        new_specs = [pl.BlockSpec((SEQ, a.shape[1]), tile) for a in news]
        past_specs, pasts = [], ()
    in_specs = ([pl.BlockSpec((TM, q.shape[1]), tile)] + new_specs + past_specs + const_specs
                + [pl.BlockSpec(memory_space=pl.ANY)])
    args = [q, *news, *pasts, *consts, prev_out]
    return pl.pallas_call(
        functools.partial(body, latent),
        out_shape=jax.ShapeDtypeStruct((N_TOK, 256), F32),
        grid=grid,
        in_specs=in_specs,
        out_specs=pl.BlockSpec((TM, 256), tile),
        scratch_shapes=list(scratch),
        input_output_aliases={len(args) - 1: 0},
        compiler_params=_params("parallel", q_semantics),
        name=name + ("_lat" if latent else "_ctx"),
    )(*args)


def _attention_mixers(qa, kva, qb, ck, qc, kc, vc, past_kva, past_ck, past_kc, past_vc, wk, wv,
                      lams, gsub, lam_init):
    def both(body, name, q, news, pasts, consts, scratch_fn=None, q_semantics="parallel"):
        sc = (lambda lk: ()) if scratch_fn is None else scratch_fn
        out = jnp.zeros((N_TOK, 256), F32)
        out = _seq_call(body, name, False, q, news, (), consts, out, sc(SEQ), q_semantics)
        return _seq_call(body, name, True, q, news, pasts, consts, out, sc(PAST_LEN + DEC_SEQ), q_semantics)

    oa = both(_attn_a_kernel, "attn_gqa", qa, [kva], [past_kva], [])
    ob = both(_attn_b_kernel, "attn_mla", qb, [ck], [past_ck], [wk, wv],
              lambda lk: (pltpu.VMEM((lk, 512), BF16), pltpu.VMEM((lk, 512), BF16)), "arbitrary")
    oc = both(functools.partial(_attn_c_kernel, lam_init), "attn_diff", qc, [kc, vc],
              [past_kc, past_vc], [*lams, gsub])
    return oa, ob, oc


Q = SSD_CHUNK


def _ssd_kernel(z_ref, xbc_ref, dt_ref, cw_ref, cb_ref, dtb_ref, alog_ref, dvec_ref, ng_ref, h0f_ref, h0b_ref, _,
                out_ref, hf_ref, hb_ref, act_s, cum_s, dtv_s, y_s):
    seq = z_ref.shape[0]
    nc = seq // Q
    row = lax.broadcasted_iota(jnp.int32, (Q, Q), 0)
    col = lax.broadcasted_iota(jnp.int32, (Q, Q), 1)
    lower = row >= col
    upper = row <= col
    tril = jnp.where(lower, 1.0, 0.0).astype(BF16)
    triu = jnp.where(upper, 1.0, 0.0).astype(BF16)
    rowc = lax.broadcasted_iota(jnp.int32, (Q, CONV_DIM), 0)
    lane = lax.broadcasted_iota(jnp.int32, (Q, LANES), 1)
    a_neg = -jnp.exp(alog_ref[...])
    cw = cw_ref[...]
    hf_ref[0] = h0f_ref[0]
    hb_ref[0] = h0b_ref[0]

    def fwd_chunk(c, carry):
        base = pl.multiple_of(c * Q, Q)
        x0 = xbc_ref[pl.ds(base, Q), :]
        prev = xbc_ref[pl.ds(pl.multiple_of(jnp.maximum(base - 8, 0), 8), 8), :][7:8, :]
        nxt = xbc_ref[pl.ds(pl.multiple_of(jnp.minimum(base + Q, seq - 8), 8), 8), :][0:1, :]
        prev = jnp.where(c > 0, prev, 0.0)
        nxt = jnp.where(c < nc - 1, nxt, 0.0)
        xm1 = jnp.where(rowc == 0, prev, pltpu.roll(x0, 1, 0))
        xp1 = jnp.where(rowc == Q - 1, nxt, pltpu.roll(x0, Q - 1, 0))
        act = _silu(xm1 * cw[0:1] + x0 * cw[1:2] + xp1 * cw[2:3] + cb_ref[...])
        act_s[pl.ds(base, Q), :] = act
        xs = act[:, 0:256]
        bm = act[:, 256:384]
        cm = act[:, 384:512]

        dtr = dt_ref[pl.ds(base, Q), :] + dtb_ref[...]
        dtv = jnp.maximum(dtr, 0.0) + jnp.log1p(jnp.exp(-jnp.abs(dtr)))
        dta = dtv * a_neg
        cum = jnp.where(lane < D_HEADS, _dot_split_left(tril, dta), _dot_split_left(triu, dta))
        cum_s[pl.ds(base, Q), :] = cum
        dtv_s[pl.ds(base, Q), :] = dtv
        cum_t = cum.T
        dtv_t = dtv.T
        bm_t = bm.T
        ys = []
        for h in range(D_HEADS):
            g = h // (D_HEADS // D_GROUPS)
            cg = cm[:, g * D_STATE:(g + 1) * D_STATE].astype(BF16)
            cb_mat = _dot_nt(cg, bm[:, g * D_STATE:(g + 1) * D_STATE].astype(BF16))
            cf = cum[:, h:h + 1]
            cb = cum[:, D_HEADS + h:D_HEADS + h + 1]
            l_f = jnp.exp(jnp.where(lower, cf - cum_t[h:h + 1, :], NEG_BIG))
            l_b = jnp.exp(jnp.where(upper, cb - cum_t[D_HEADS + h:D_HEADS + h + 1, :], NEG_BIG))
            mix = cb_mat * (l_f * dtv_t[h:h + 1, :] + l_b * dtv_t[D_HEADS + h:D_HEADS + h + 1, :])
            xh = xs[:, h * D_HEADDIM:(h + 1) * D_HEADDIM]
            y = _dot(mix.astype(BF16), xh.astype(BF16))
            state = hf_ref[0, h]
            y = y + _dot(cg, state.astype(BF16)) * jnp.exp(cf)
            y = y + dvec_ref[:, h * D_HEADDIM:(h + 1) * D_HEADDIM] * xh
            ys.append(y)
            last = cum[Q - 1:Q, h:h + 1]
            wgt = jnp.exp(last - cf) * dtv[:, h:h + 1]
            st = _dot(bm_t[g * D_STATE:(g + 1) * D_STATE, :].astype(BF16), (xh * wgt).astype(BF16))
            hf_ref[0, h] = state * jnp.exp(last) + st
        y_s[pl.ds(base, Q), :] = jnp.concatenate(ys, axis=-1)
        return carry

    lax.fori_loop(0, nc, fwd_chunk, 0)

    def bwd_chunk(i, carry):
        c = nc - 1 - i
        base = pl.multiple_of(c * Q, Q)
        act = act_s[pl.ds(base, Q), :]
        cum = cum_s[pl.ds(base, Q), :]
        dtv = dtv_s[pl.ds(base, Q), :]
        xs = act[:, 0:256]
        bm_t = act[:, 256:384].T
        cm = act[:, 384:512]
        ys = []
        for h in range(D_HEADS):
            g = h // (D_HEADS // D_GROUPS)
            cg = cm[:, g * D_STATE:(g + 1) * D_STATE].astype(BF16)
            cb = cum[:, D_HEADS + h:D_HEADS + h + 1]
            xh = xs[:, h * D_HEADDIM:(h + 1) * D_HEADDIM]
            state = hb_ref[0, h]
            ys.append(_dot(cg, state.astype(BF16)) * jnp.exp(cb))
            first = cum[0:1, D_HEADS + h:D_HEADS + h + 1]
            wgt = jnp.exp(first - cb) * dtv[:, D_HEADS + h:D_HEADS + h + 1]
            st = _dot(bm_t[g * D_STATE:(g + 1) * D_STATE, :].astype(BF16), (xh * wgt).astype(BF16))
            hb_ref[0, h] = state * jnp.exp(first) + st
        y = y_s[pl.ds(base, Q), :] + jnp.concatenate(ys, axis=-1)
        gated = y * _silu(z_ref[pl.ds(base, Q), :])
        ms = jnp.mean(gated * gated, axis=-1, keepdims=True)
        out_ref[pl.ds(base, Q), :] = gated * lax.rsqrt(ms + EPS) * ng_ref[...]
        return carry

    lax.fori_loop(0, nc, bwd_chunk, 0)


def _ssd(latent, z, xbc, dt, consts, h0f, h0b, prev_out):
    bsz, seq, first = (DEC_BATCH, DEC_SEQ, N_CTX // DEC_SEQ) if latent else (BATCH, SEQ, 0)
    per_seq = lambda w: pl.BlockSpec((seq, w), lambda b: (first + b, 0))
    const_specs = [pl.BlockSpec(a.shape, lambda b, n=a.ndim: (0,) * n) for a in consts]
    st_spec = pl.BlockSpec((1, D_HEADS, D_STATE, D_HEADDIM), lambda b: (b, 0, 0, 0))
    st_shape = jax.ShapeDtypeStruct((bsz, D_HEADS, D_STATE, D_HEADDIM), F32)
    in_specs = ([per_seq(256), per_seq(512), per_seq(128)] + const_specs + [st_spec, st_spec]
                + [pl.BlockSpec(memory_space=pl.ANY)])
    args = [z, xbc, dt, *consts, h0f, h0b, prev_out]
    aliases = {len(args) - 1: 0}
    return pl.pallas_call(
        _ssd_kernel,
        out_shape=[jax.ShapeDtypeStruct((N_TOK, D_INNER), F32), st_shape, st_shape],
        grid=(bsz,),
        in_specs=in_specs,
        out_specs=[per_seq(D_INNER), st_spec, st_spec],
        scratch_shapes=[pltpu.VMEM((seq, CONV_DIM), F32), pltpu.VMEM((seq, LANES), F32),
                        pltpu.VMEM((seq, LANES), F32), pltpu.VMEM((seq, D_INNER), F32)],
        input_output_aliases=aliases,
        compiler_params=_params("parallel"),
        name="ssd_lat" if latent else "ssd_ctx",
    )(*args)


def _out_proj_kernel(x_ref, oa_ref, ob_ref, oc_ref, od_ref, mod_ref, wo_ref, g2_ref, rwh_ref, rwl_ref,
                     rb_ref, x1_ref, h2_ref, te_ref, gt_ref, cnt_ref):
    m = mod_ref[0]
    gate1, sh2, sc2 = m[2:3], m[3:4], m[4:5]
    mixed = (_dot(oa_ref[...].astype(BF16), wo_ref[0:256, :])
             + _dot(ob_ref[...].astype(BF16), wo_ref[256:512, :])
             + _dot(oc_ref[...].astype(BF16), wo_ref[512:768, :])
             + _dot(od_ref[...].astype(BF16), wo_ref[768:1024, :]))
    x1 = x_ref[...] + gate1 * mixed
    x1_ref[...] = x1
    ms = jnp.mean(x1 * x1, axis=-1, keepdims=True)
    h2 = (x1 * lax.rsqrt(ms + EPS) * g2_ref[...]) * (1.0 + sc2) + sh2
    h2_ref[...] = h2.astype(BF16)

    hi = h2.astype(BF16)
    lo = (h2 - hi.astype(F32)).astype(BF16)
    logits = _dot(hi, rwh_ref[...]) + _dot(lo, rwh_ref[...]) + _dot(hi, rwl_ref[...]) + rb_ref[...]
    lane = lax.broadcasted_iota(jnp.int32, logits.shape, 1)
    vals, idxs = [], []
    for _ in range(TOP_K):
        mx = jnp.max(logits, axis=-1, keepdims=True)
        ix = jnp.min(jnp.where(logits == mx, lane, LANES), axis=-1, keepdims=True)
        vals.append(mx)
        idxs.append(ix)
        logits = jnp.where(lane == ix, -3e38, logits)
    es = [jnp.exp(v - vals[0]) for v in vals]
    den = es[0] + es[1] + es[2] + es[3]
    te = jnp.zeros(lane.shape, jnp.int32)
    gt = jnp.zeros(lane.shape, F32)
    member = jnp.zeros(lane.shape, F32)
    for k in range(TOP_K):
        te = jnp.where(lane == k, idxs[k], te)
        gt = jnp.where(lane == k, es[k] / den, gt)
        member = jnp.where(lane == idxs[k], 1.0, member)
    te_ref[...] = te[:, 0:TOP_K]
    gt_ref[...] = gt[:, 0:TOP_K]
    count = jnp.sum(member, axis=0, keepdims=True)
    cnt_ref[0] = jnp.broadcast_to(count, (8, LANES)).astype(jnp.int32)


def _out_proj(x, oa, ob, oc, od, mod3, wo, g2, rwh, rwl, rb):
    row = lambda w: pl.BlockSpec((TM, w), lambda i: (i, 0))
    full = lambda a: pl.BlockSpec(a.shape, lambda i: (0,) * a.ndim)
    return pl.pallas_call(
        _out_proj_kernel,
        out_shape=[jax.ShapeDtypeStruct((N_TOK, D_MODEL), F32), jax.ShapeDtypeStruct((N_TOK, D_MODEL), BF16),
                   jax.ShapeDtypeStruct((N_TOK, TOP_K), jnp.int32), jax.ShapeDtypeStruct((N_TOK, TOP_K), F32),
                   jax.ShapeDtypeStruct((N_TILES, 8, LANES), jnp.int32)],
        grid=(N_TILES,),
        in_specs=[row(D_MODEL), row(256), row(256), row(256), row(256),
                  pl.BlockSpec((1, N_MOD, D_MODEL), lambda i: (_mod_row(i), 0, 0)),
                  full(wo), full(g2), full(rwh), full(rwl), full(rb)],
        out_specs=[row(D_MODEL), row(D_MODEL), row(TOP_K), row(TOP_K),
                   pl.BlockSpec((1, 8, LANES), lambda i: (i, 0, 0))],
        compiler_params=_params("parallel"),
        name="out_proj_router",
    )(x, oa, ob, oc, od, mod3, wo, g2, rwh, rwl, rb)


def _tile_rows(te, tri_ref, off_ref):
    lane = lax.broadcasted_iota(jnp.int32, (TM, LANES), 1)
    hits = [lane == te[:, k:k + 1] for k in range(TOP_K)]
    member = jnp.zeros((TM, LANES), F32)
    for hit in hits:
        member = jnp.where(hit, 1.0, member)
    rank = _dot(tri_ref[...], member.astype(BF16))
    pos = rank + off_ref[0][0:1, :].astype(F32)
    return [jnp.sum(jnp.where(hit, pos, 0.0), axis=-1, keepdims=True) for hit in hits]


def _segment_chunks(seg_ref, len_ref, tile, visit):
    def segment(e, row):
        length = len_ref[tile * N_EXPERTS + e]
        start = seg_ref[tile * N_EXPERTS + e]

        def chunk(c, carry):
            visit(CHUNKS[0], pl.multiple_of(row + c * CHUNKS[0], SEG_ALIGN),
                  pl.multiple_of(start + c * CHUNKS[0], SEG_ALIGN))
            return carry

        lax.fori_loop(0, length // CHUNKS[0], chunk, 0)
        for size in CHUNKS[1:]:
            done = length // (2 * size) * (2 * size)

            @pl.when(length - done >= size)
            def _():
                visit(size, pl.multiple_of(row + done, SEG_ALIGN), pl.multiple_of(start + done, SEG_ALIGN))
        return row + length

    lax.fori_loop(0, N_EXPERTS, segment, 0)


def _wait_chunks(nchunk_ref, tile, copy):
    for j, size in enumerate(CHUNKS):
        lax.fori_loop(0, nchunk_ref[tile * len(CHUNKS) + j], lambda c, carry, size=size: (copy(size).wait(), carry)[1], 0)


def _dispatch_kernel(seg_ref, len_ref, nchunk_ref, end_ref, nu_ref, h2_ref, te_ref, off_ref, tri_ref,
                     xb_ref, buf, sems, sem_z):
    i = pl.program_id(0)
    slot = i % 2

    def chunk_copy(size, s, buf_row, xb_row):
        return pltpu.make_async_copy(buf.at[s, pl.ds(buf_row, size)], xb_ref.at[pl.ds(xb_row, size)], sems.at[s])

    def wait_chunks(tile, s):
        _wait_chunks(nchunk_ref, tile, lambda size: chunk_copy(size, s, 0, 0))

    @pl.when(i == 0)
    def _():
        buf[1, 0:ZERO_ROWS, :] = jnp.zeros((ZERO_ROWS, D_MODEL), F32)
        tail = lambda e: pltpu.make_async_copy(
            buf.at[1, pl.ds(0, ZERO_ROWS)],
            xb_ref.at[pl.ds(pl.multiple_of(jnp.maximum(end_ref[e] - ZERO_ROWS, 0), SEG_ALIGN), ZERO_ROWS)], sem_z)
        block = lambda b: pltpu.make_async_copy(
            buf.at[1, pl.ds(0, MOE_BM)], xb_ref.at[pl.ds(pl.multiple_of(b * MOE_BM, MOE_BM), MOE_BM)], sem_z)
        for e in range(N_EXPERTS):
            tail(e).start()
        lax.fori_loop(nu_ref[0], MOE_BLOCKS, lambda b, carry: (block(b).start(), carry)[1], 0)
        for e in range(N_EXPERTS):
            tail(e).wait()
        lax.fori_loop(nu_ref[0], MOE_BLOCKS, lambda b, carry: (block(b).wait(), carry)[1], 0)

    rows = _tile_rows(te_ref[...], tri_ref, off_ref)
    lane = lax.broadcasted_iota(jnp.int32, (TM, LANES), 1)
    packed = jnp.zeros((TM, LANES), F32)
    for k in range(TOP_K):
        packed = jnp.where(lane == k, rows[k], packed)
    rows_t = packed.T
    buf_row = lax.broadcasted_iota(jnp.int32, (TILE_BUF, TM), 0).astype(F32)
    pick = jnp.zeros((TILE_BUF, TM), F32)
    for k in range(TOP_K):
        pick = jnp.where(buf_row == rows_t[k:k + 1, :], 1.0, pick)
    buf[slot] = _dot(pick.astype(BF16), h2_ref[...])

    @pl.when(i > 0)
    def _():
        wait_chunks(i - 1, 1 - slot)

    _segment_chunks(seg_ref, len_ref, i, lambda size, b, x: chunk_copy(size, slot, b, x).start())

    @pl.when(i == N_TILES - 1)
    def _():
        wait_chunks(i, slot)


def _dispatch(seg_start, seg_len, n_chunk, pad_end, n_used, h2, te, seg_off, tri):
    row = lambda w: pl.BlockSpec((TM, w), lambda i, *_: (i, 0))
    grid_spec = pltpu.PrefetchScalarGridSpec(
        num_scalar_prefetch=5,
        grid=(N_TILES,),
        in_specs=[row(D_MODEL), row(TOP_K), pl.BlockSpec((1, 8, LANES), lambda i, *_: (i, 0, 0)),
                  pl.BlockSpec(tri.shape, lambda i, *_: (0, 0))],
        out_specs=pl.BlockSpec(memory_space=pl.ANY),
        scratch_shapes=[pltpu.VMEM((2, TILE_BUF, D_MODEL), F32),
                        pltpu.SemaphoreType.DMA((2,)), pltpu.SemaphoreType.DMA],
    )
    return pl.pallas_call(
        _dispatch_kernel,
        out_shape=jax.ShapeDtypeStruct((YB_ROWS, D_MODEL), F32),
        grid_spec=grid_spec,
        compiler_params=_params("arbitrary"),
        name="moe_dispatch",
    )(seg_start, seg_len, n_chunk, pad_end, n_used, h2, te, seg_off, tri)


def _expert_kernel(be_ref, nu_ref, x_ref, wgu_ref, bgu_ref, wdn_ref, bdn_ref, o_ref, wgu_s, wdn_s):
    i = pl.program_id(0)
    used = i < nu_ref[0]
    prev = be_ref[jnp.maximum(i - 1, 0)]
    fresh = jnp.logical_or(i == 0, be_ref[i] != prev)

    @pl.when(jnp.logical_and(used, fresh))
    def _():
        wgu_s[...] = wgu_ref[0, 0].astype(BF16)
        wdn_s[...] = wdn_ref[0, 0].astype(BF16)

    @pl.when(used)
    def _():
        hgu = _dot(x_ref[...].astype(BF16), wgu_s[...]) + bgu_ref[0, 0]
        gate = jnp.minimum(hgu[:, :D_FF], SWIGLU_LIMIT)
        up = jnp.clip(hgu[:, D_FF:], -SWIGLU_LIMIT, SWIGLU_LIMIT)
        act = (up + 1.0) * gate * jax.nn.sigmoid(SWIGLU_ALPHA * gate)
        o_ref[...] = _dot(act.astype(BF16), wdn_s[...]) + bdn_ref[0, 0]

    @pl.when(jnp.logical_not(used))
    def _():
        o_ref[...] = jnp.zeros_like(o_ref)


def _experts(layer, block_e, n_used, xb, w_gu, b_gu, w_dn, b_dn):
    grid_spec = pltpu.PrefetchScalarGridSpec(
        num_scalar_prefetch=2,
        grid=(MOE_BLOCKS,),
        in_specs=[pl.BlockSpec((MOE_BM, D_MODEL), lambda i, be, nu: (i, 0)),
                  pl.BlockSpec((1, 1, D_MODEL, 2 * D_FF), lambda i, be, nu: (layer, be[i], 0, 0)),
                  pl.BlockSpec((1, 1, 1, 2 * D_FF), lambda i, be, nu: (layer, be[i], 0, 0)),
                  pl.BlockSpec((1, 1, D_FF, D_MODEL), lambda i, be, nu: (layer, be[i], 0, 0)),
                  pl.BlockSpec((1, 1, 1, D_MODEL), lambda i, be, nu: (layer, be[i], 0, 0))],
        out_specs=pl.BlockSpec((MOE_BM, D_MODEL), lambda i, be, nu: (i, 0)),
        scratch_shapes=[pltpu.VMEM((D_MODEL, 2 * D_FF), BF16), pltpu.VMEM((D_FF, D_MODEL), BF16)],
    )
    return pl.pallas_call(
        _expert_kernel,
        out_shape=jax.ShapeDtypeStruct((YB_ROWS, D_MODEL), F32),
        grid_spec=grid_spec,
        compiler_params=_params("arbitrary"),
        name="experts",
    )(block_e, n_used, xb, w_gu, b_gu.reshape(DEPTH, N_EXPERTS, 1, 2 * D_FF), w_dn,
      b_dn.reshape(DEPTH, N_EXPERTS, 1, D_MODEL))


def _combine_kernel(final, seg_ref, len_ref, nchunk_ref, x1_ref, te_ref, gt_ref, off_ref, mod_ref, fg_ref,
                    tri_ref, yb_ref, o_ref, buf, sems):
    i = pl.program_id(0)
    slot = i % 2

    def chunk_copy(size, s, buf_row, yb_row):
        return pltpu.make_async_copy(yb_ref.at[pl.ds(yb_row, size)], buf.at[s, pl.ds(buf_row, size)], sems.at[s])

    def request(tile, s):
        _segment_chunks(seg_ref, len_ref, tile, lambda size, b, y: chunk_copy(size, s, b, y).start())

    @pl.when(i == 0)
    def _():
        buf[...] = jnp.zeros_like(buf)
        request(0, 0)

    @pl.when(i + 1 < N_TILES)
    def _():
        request(i + 1, 1 - slot)

    rows = _tile_rows(te_ref[...], tri_ref, off_ref)
    gt = gt_ref[...]
    buf_row = lax.broadcasted_iota(jnp.int32, (TM, TILE_BUF), 1).astype(F32)
    place = jnp.zeros((TM, TILE_BUF), F32)
    for k in range(TOP_K):
        place = jnp.where(buf_row == rows[k], gt[:, k:k + 1], place)

    _wait_chunks(nchunk_ref, i, lambda size: chunk_copy(size, slot, 0, 0))
    y = _dot(place.astype(BF16), buf[slot].astype(BF16))
    x2 = x1_ref[...] + mod_ref[0][5:6] * y
    if final:
        ms = jnp.mean(x2 * x2, axis=-1, keepdims=True)
        x2 = x2 * lax.rsqrt(ms + EPS) * fg_ref[...]
    o_ref[...] = x2


def _combine(seg_start, seg_len, n_chunk, x1, te, gates, seg_off, mod3, fg, tri, yb, final):
    row = lambda w: pl.BlockSpec((TM, w), lambda i, *_: (i, 0))
    full = lambda a: pl.BlockSpec(a.shape, lambda i, *_: (0,) * a.ndim)
    grid_spec = pltpu.PrefetchScalarGridSpec(
        num_scalar_prefetch=3,
        grid=(N_TILES,),
        in_specs=[row(D_MODEL), row(TOP_K), row(TOP_K),
                  pl.BlockSpec((1, 8, LANES), lambda i, *_: (i, 0, 0)),
                  pl.BlockSpec((1, N_MOD, D_MODEL), lambda i, *_: (_mod_row(i), 0, 0)),
                  full(fg), full(tri),
                  pl.BlockSpec(memory_space=pl.ANY)],
        out_specs=row(D_MODEL),
        scratch_shapes=[pltpu.VMEM((2, TILE_BUF, D_MODEL), F32), pltpu.SemaphoreType.DMA((2,))],
    )
    return pl.pallas_call(
        functools.partial(_combine_kernel, final),
        out_shape=jax.ShapeDtypeStruct((N_TOK, D_MODEL), F32),
        grid_spec=grid_spec,
        compiler_params=_params("arbitrary"),
        name="moe_combine",
    )(seg_start, seg_len, n_chunk, x1, te, gates, seg_off, mod3, fg, tri, yb)


def _w1_index():
    idx = np.full((W1_COLS,), IN_WIDTH, np.int32)
    for g in range(2):
        for kv in range(2):
            idx[g * 128 + kv * 64:g * 128 + kv * 64 + 64] = np.arange(64) + kv * 128 + g * 64
    idx[256:512] = np.arange(256) + IN_OFF['a_k']
    idx[512:704] = np.arange(192) + IN_OFF['b_cq']
    idx[768:896] = np.arange(128) + IN_OFF['b_ckv']
    idx[896:928] = np.arange(32) + IN_OFF['b_kr']
    idx[1024:1792] = np.arange(768) + IN_OFF['c_q']
    idx[1792:2048] = np.arange(256) + IN_OFF['d_z']
    idx[2048:2560] = np.arange(512) + IN_OFF['d_xbc']
    idx[2560:2568] = np.arange(8) + IN_OFF['d_dtf']
    return idx


def _wout_a_index():
    idx = np.zeros((256,), np.int32)
    for g in range(2):
        for kv in range(2):
            idx[g * 128 + kv * 64:g * 128 + kv * 64 + 64] = np.arange(64) + kv * 128 + g * 64
    return idx


def _rope_tables():
    t = np.arange(DEC_SEQ)
    pos = ((t // GRID_W).astype(np.float32), (t % GRID_W).astype(np.float32))

    def unit(rot_dim):
        quarter, half = rot_dim // 4, rot_dim // 2
        inv = ROPE_THETA ** (-np.arange(0, half, 2, dtype=np.float32) / half)
        cos = np.zeros((DEC_SEQ, rot_dim), np.float32)
        sin = np.zeros((DEC_SEQ, rot_dim), np.float32)
        for seg in range(4):
            ang = pos[seg // 2][:, None] * inv[None, :].astype(np.float32)
            cos[:, seg * quarter:(seg + 1) * quarter] = np.cos(ang)
            sin[:, seg * quarter:(seg + 1) * quarter] = np.sin(ang) * (-1.0 if seg % 2 == 0 else 1.0)
        return cos, sin

    log2e = math.log2(math.e)
    specs = [
        (HEAD_DIM, (0, 64), (), HEAD_DIM ** -0.5 * log2e),
        (HEAD_DIM, (0, 64), (), 1.0),
        (MLA_ROPE, (MLA_NOPE,), (0, MLA_NOPE), (MLA_NOPE + MLA_ROPE) ** -0.5 * log2e),
        (MLA_ROPE, (0,), (), 1.0),
        (DIFF_D, (0, 32, 64, 96), (), DIFF_D ** -0.5 * log2e),
        (DIFF_D, (0, 32, 64, 96), (), 1.0),
    ]
    lat_cols, ident_cols = [], []
    for rot_dim, starts, passthrough, scale in specs:
        ucos, usin = unit(rot_dim)
        cos = np.zeros((DEC_SEQ, LANES), np.float32)
        sin = np.zeros((DEC_SEQ, LANES), np.float32)
        ident = np.zeros((1, LANES), np.float32)
        if passthrough:
            cos[:, passthrough[0]:passthrough[1]] = 1.0
            ident[:, passthrough[0]:passthrough[1]] = 1.0
        for s in starts:
            cos[:, s:s + rot_dim] = ucos
            sin[:, s:s + rot_dim] = usin
            ident[:, s:s + rot_dim] = 1.0
        lat_cols += [cos * scale, sin * scale]
        ident_cols += [ident * scale, np.zeros((1, LANES), np.float32)]
    lat = np.concatenate(lat_cols, axis=1)
    ident_blk = np.broadcast_to(np.concatenate(ident_cols, axis=1), (TM, N_TAB * LANES))
    return np.concatenate([ident_blk, lat], axis=0).astype(np.float32)


def _block_diag_ones(n, blk):
    r = np.arange(n)
    return (r[:, None] // blk == r[None, :] // blk).astype(np.float32)


def _chunk_tables(seg_len):
    counts = [jnp.sum(seg_len // CHUNKS[0], axis=1)]
    counts += [jnp.sum(seg_len % (2 * size) // size, axis=1) for size in CHUNKS[1:]]
    return jnp.stack(counts, axis=1).reshape(-1).astype(jnp.int32)


def _strict_lower_ones(n):
    r = np.arange(n)
    return (r[None, :] < r[:, None]).astype(np.float32)


def kernel(x_prompt, x_sample, cache_gqa_k, cache_gqa_v, cache_mla_ckv, cache_mla_krope, cache_diff_k, cache_diff_v, state_ssd_fwd, state_ssd_bwd, c, c_ctx, norm1_g, norm2_g, w_ada, b_ada, w_in, w_out, gqa_qn_g, gqa_kn_g, mla_qa_g, mla_wqb, mla_kva_g, mla_wkvb, diff_lq1, diff_lk1, diff_lq2, diff_lk2, diff_subln_g, ssd_conv_w, ssd_conv_b, ssd_a_log_f, ssd_a_log_b, ssd_dt_bias_f, ssd_dt_bias_b, ssd_d, ssd_norm_g, router_w, router_b, moe_w_gu, moe_b_gu, moe_w_dn, moe_b_dn, final_g):
    tabs = jnp.asarray(_rope_tables())
    bd = jnp.asarray(_block_diag_ones(256, HEAD_DIM), BF16)
    tri = jnp.asarray(_strict_lower_ones(TM), BF16)
    w1_idx = _w1_index()
    woa_idx = _wout_a_index()

    cvec = jnp.zeros((MOD_ROWS, D_MODEL), F32).at[0].set(c_ctx).at[1:1 + DEC_BATCH].set(c)
    mod = _modulation(cvec, w_ada, b_ada).reshape(DEPTH, MOD_ROWS, N_MOD, D_MODEL)

    x = jnp.concatenate([x_prompt.reshape(N_CTX, D_MODEL), x_sample.reshape(N_LAT, D_MODEL)], axis=0)
    new_ctx = []
    for l in range(DEPTH):
        mod3 = mod[l]
        w1 = jnp.concatenate([w_in[l], jnp.zeros((D_MODEL, 1), F32)], axis=1)[:, w1_idx].astype(BF16)
        wqb = mla_wqb[l].reshape(Q_LORA, B_HEADS, MLA_NOPE + MLA_ROPE)
        wqb = jnp.pad(wqb, ((0, 256 - Q_LORA), (0, 0), (0, LANES - MLA_NOPE - MLA_ROPE)))
        wqb = wqb.reshape(256, B_HEADS * LANES).astype(BF16)
        wkvb = mla_wkvb[l].reshape(KV_LORA, B_HEADS, MLA_NOPE + MLA_V)
        wk_nope = jnp.pad(wkvb[:, :, :MLA_NOPE], ((0, 0), (0, 0), (0, LANES - MLA_NOPE)))
        eye_r = jnp.zeros((LANES, B_HEADS, LANES), F32)
        eye_r = eye_r.at[jnp.arange(MLA_ROPE), :, MLA_NOPE + jnp.arange(MLA_ROPE)].set(1.0)
        wk = jnp.concatenate([wk_nope, eye_r], axis=0).reshape(256, B_HEADS * LANES).astype(BF16)
        wv = wkvb[:, :, MLA_NOPE:].reshape(KV_LORA, B_HEADS * MLA_V).astype(BF16)
        wo = jnp.concatenate([w_out[l][woa_idx], w_out[l][256:]], axis=0).astype(BF16)
        gq = jnp.tile(gqa_qn_g[l], 4)[None, :]
        gk = jnp.tile(gqa_kn_g[l], 2)[None, :]
        gqa = jnp.pad(mla_qa_g[l], (0, 256 - Q_LORA))[None, :]
        gkva = mla_kva_g[l][None, :]
        rw = jnp.pad(router_w[l], ((0, 0), (0, LANES - N_EXPERTS)))
        rwh = rw.astype(BF16)
        rwl = (rw - rwh.astype(F32)).astype(BF16)
        rb = jnp.pad(router_b[l], (0, LANES - N_EXPERTS), constant_values=NEG_BIG)[None, :]

        qa, kva, qb, ck, qc, kc, vc, z, xbc, dt = _in_proj(
            x, mod3, norm1_g[l][None, :], w1, bd, gq, gk, gqa, gkva, wqb, tabs)

        past_kva = jnp.concatenate([cache_gqa_k[:, l].reshape(DEC_BATCH, PAST_LEN, 128),
                                    cache_gqa_v[:, l].reshape(DEC_BATCH, PAST_LEN, 128)], axis=-1)
        past_ck = jnp.concatenate([cache_mla_ckv[:, l], cache_mla_krope[:, l],
                                   jnp.zeros((DEC_BATCH, PAST_LEN, LANES - MLA_ROPE), F32)], axis=-1)
        past_kc = cache_diff_k[:, l].reshape(DEC_BATCH, PAST_LEN, 256)
        past_vc = cache_diff_v[:, l].reshape(DEC_BATCH, PAST_LEN, 256)
        lams = [a[l][None, :] for a in (diff_lq1, diff_lk1, diff_lq2, diff_lk2)]
        gsub = jnp.tile(diff_subln_g[l], 2)[None, :]
        lam_init = 0.8 - 0.6 * math.exp(-0.3 * l)
        oa, ob, oc = _attention_mixers(qa, kva, qb, ck, qc, kc, vc, past_kva, past_ck, past_kc, past_vc,
                                       wk, wv, lams, gsub, lam_init)

        pad8 = lambda f, b: jnp.pad(jnp.concatenate([f, b]), (0, LANES - 2 * D_HEADS))[None, :]
        ssd_consts = (ssd_conv_w[l], ssd_conv_b[l][None, :], pad8(ssd_dt_bias_f[l], ssd_dt_bias_b[l]),
                      pad8(ssd_a_log_f[l], ssd_a_log_b[l]), jnp.repeat(ssd_d[l], D_HEADDIM)[None, :],
                      ssd_norm_g[l][None, :])
        zeros_st = jnp.zeros((BATCH, D_HEADS, D_STATE, D_HEADDIM), F32)
        od, hf_c, hb_c = _ssd(False, z, xbc, dt, ssd_consts, zeros_st, zeros_st, jnp.zeros((N_TOK, D_INNER), F32))
        od, _, _ = _ssd(True, z, xbc, dt, ssd_consts, jnp.swapaxes(state_ssd_fwd[:, l], -1, -2),
                        jnp.swapaxes(state_ssd_bwd[:, l], -1, -2), od)

        x1, h2, te, gates, tile_cnt = _out_proj(x, oa, ob, oc, od, mod3, wo, norm2_g[l][None, :], rwh, rwl, rb)

        seg_cnt = tile_cnt[:, 0, :N_EXPERTS]
        seg_len = (seg_cnt + SEG_ALIGN - 1) // SEG_ALIGN * SEG_ALIGN
        region = (jnp.sum(seg_len, axis=0) + MOE_BM - 1) // MOE_BM * MOE_BM
        pad_end = jnp.cumsum(region).astype(jnp.int32)
        pad_start = pad_end - region
        seg_start = (pad_start[None, :] + jnp.cumsum(seg_len, axis=0) - seg_len).astype(jnp.int32)
        n_used = (pad_end[-1] // MOE_BM).astype(jnp.int32).reshape(1)
        blk_start = jnp.arange(MOE_BLOCKS, dtype=jnp.int32) * MOE_BM
        block_e = jnp.minimum(jnp.sum((pad_end[None, :] <= blk_start[:, None]).astype(jnp.int32), axis=1),
                              N_EXPERTS - 1).astype(jnp.int32)
        n_chunk = _chunk_tables(seg_len)
        seg_off = jnp.cumsum(seg_len, axis=1) - seg_len
        seg_off = jnp.broadcast_to(jnp.pad(seg_off, ((0, 0), (0, LANES - N_EXPERTS)))[:, None, :],
                                   (N_TILES, 8, LANES)).astype(jnp.int32)
        seg_start, seg_len = seg_start.reshape(-1), seg_len.reshape(-1).astype(jnp.int32)

        xb = _dispatch(seg_start, seg_len, n_chunk, pad_end, n_used, h2, te, seg_off, tri)
        yb = _experts(l, block_e, n_used, xb, moe_w_gu, moe_b_gu, moe_w_dn, moe_b_dn)
        x = _combine(seg_start, seg_len, n_chunk, x1, te, gates, seg_off, mod3, final_g[None, :], tri, yb,
                     l == DEPTH - 1)

        kva_c, ck_c = kva[:N_CTX], ck[:N_CTX]
        new_ctx.append((kva_c[:, 0:128].reshape(BATCH, SEQ, A_KV_HEADS, HEAD_DIM),
                        kva_c[:, 128:256].reshape(BATCH, SEQ, A_KV_HEADS, HEAD_DIM),
                        ck_c[:, 0:KV_LORA].reshape(BATCH, SEQ, KV_LORA),
                        ck_c[:, KV_LORA:KV_LORA + MLA_ROPE].reshape(BATCH, SEQ, MLA_ROPE),
                        kc[:N_CTX].reshape(BATCH, SEQ, C_HEADS, 2 * DIFF_D),
                        vc[:N_CTX].reshape(BATCH, SEQ, C_HEADS, DIFF_V),
                        jnp.swapaxes(hf_c, -1, -2), jnp.swapaxes(hb_c, -1, -2)))

    y_prompt = x[:N_CTX].reshape(BATCH, SEQ, D_MODEL)
    y_sample = x[N_CTX:].reshape(DEC_BATCH, DEC_SEQ, D_MODEL)
    caches = [jnp.stack([cl[i] for cl in new_ctx], axis=1) for i in range(8)]
    return (y_prompt, y_sample, *caches)
```

```python
import functools
import math

import numpy as np
import jax
import jax.numpy as jnp
from jax import lax
from jax.experimental import pallas as pl
from jax.experimental.pallas import tpu as pltpu

F32 = jnp.float32
BF16 = jnp.bfloat16

D_MODEL = 1024
BATCH = 16
SEQ = 256
DEPTH = 2
DEC_BATCH = 8
DEC_SEQ = 2048
PAST_LEN = 256
GRID_W = 64
ROPE_THETA = 10000.0
EPS = 1e-6
HEAD_DIM = 64
A_HEADS = 4
A_KV_HEADS = 2
B_HEADS = 4
MLA_NOPE = 64
MLA_ROPE = 32
MLA_V = 64
Q_LORA = 192
KV_LORA = 128
C_HEADS = 4
DIFF_D = 32
DIFF_V = 64
SUBLN_EPS = 1e-5
D_HEADS = 4
D_HEADDIM = 64
D_INNER = 256
D_GROUPS = 2
D_STATE = 64
CONV_DIM = 512
SSD_CHUNK = 128
N_EXPERTS = 32
TOP_K = 4
D_FF = 1024
SWIGLU_ALPHA = 1.702
SWIGLU_LIMIT = 7.0
N_MOD = 6

N_CTX = BATCH * SEQ
N_LAT = DEC_BATCH * DEC_SEQ
N_TOK = N_CTX + N_LAT

LANES = 128
TM = 256
N_TILES = N_TOK // TM
CTX_TILES = N_CTX // TM
LAT_TQ = 512
PROJ_TM = 512
MOD_ROWS = 16
MOE_BM = 512
N_PAIRS = N_TOK * TOP_K
SEG_ALIGN = 8
CHUNKS = (32, 16, 8)
CHUNK_SHIFT = 5
MOE_BLOCKS = -(-(N_PAIRS + N_TILES * N_EXPERTS * (SEG_ALIGN - 1) + N_EXPERTS * (MOE_BM - 1)) // MOE_BM)
YB_ROWS = MOE_BLOCKS * MOE_BM
TILE_BUF = -(-(TM * TOP_K + N_EXPERTS * (SEG_ALIGN - 1)) // LANES) * LANES
ZERO_ROWS = MOE_BM
VMEM_LIMIT = 56 * 1024 * 1024
NEG_BIG = -1e30

W1_COLS = 2688
IN_OFF = dict(a_q=0, a_k=256, a_v=384, b_cq=512, b_ckv=704, b_kr=832, c_q=864, c_k=1120,
              c_v=1376, d_z=1632, d_xbc=1888, d_dtf=2400, d_dtb=2404)
IN_WIDTH = 2408
N_TAB = 12


def _mod_row(i, tile=TM):
    return jnp.where(i < N_CTX // tile, 0, 1 + (i - N_CTX // tile) // (DEC_SEQ // tile))


def _tab_block(i, tile):
    return jnp.where(i < N_CTX // tile, 0, 1 + (i - N_CTX // tile) % (DEC_SEQ // tile))


def _dot(a, b):
    return jnp.dot(a, b, preferred_element_type=F32)


def _dot_nt(a, b):
    return lax.dot_general(a, b, (((1,), (1,)), ((), ())), preferred_element_type=F32)


def _dot_split(x, m):
    hi = x.astype(BF16)
    lo = (x - hi.astype(F32)).astype(BF16)
    return _dot(hi, m) + _dot(lo, m)


def _dot_split_left(m, x):
    hi = x.astype(BF16)
    lo = (x - hi.astype(F32)).astype(BF16)
    return _dot(m, hi) + _dot(m, lo)


def _silu(x):
    return x * jax.nn.sigmoid(x)


def _params(*semantics):
    return pltpu.CompilerParams(dimension_semantics=semantics, vmem_limit_bytes=VMEM_LIMIT)


MOD_TN = 1536


def _mod_kernel(c_ref, w_ref, b_ref, o_ref):
    c = c_ref[...]
    s = _silu(c).astype(BF16)
    o_ref[0] = _dot(s, w_ref[0].astype(BF16)) + b_ref[0]


def _modulation(cvec, w_ada, b_ada):
    n = N_MOD * D_MODEL
    return pl.pallas_call(
        _mod_kernel,
        out_shape=jax.ShapeDtypeStruct((DEPTH, MOD_ROWS, n), F32),
        grid=(DEPTH, n // MOD_TN),
        in_specs=[pl.BlockSpec((MOD_ROWS, D_MODEL), lambda l, j: (0, 0)),
                  pl.BlockSpec((1, D_MODEL, MOD_TN), lambda l, j: (l, 0, j)),
                  pl.BlockSpec((1, 1, MOD_TN), lambda l, j: (l, 0, j))],
        out_specs=pl.BlockSpec((1, MOD_ROWS, MOD_TN), lambda l, j: (l, 0, j)),
        compiler_params=_params("parallel", "parallel"),
        name="adaln_mod",
    )(cvec, w_ada, b_ada.reshape(DEPTH, 1, n))


def _rope(x, cos, sin, quarter):
    lane = lax.broadcasted_iota(jnp.int32, (x.shape[0], LANES), 1)
    first = (lane // quarter) % 2 == 0
    outs = []
    for t in range(x.shape[1] // LANES):
        xt = x[:, t * LANES:(t + 1) * LANES]
        partner = jnp.where(first, pltpu.roll(xt, LANES - quarter, 1), pltpu.roll(xt, quarter, 1))
        outs.append(xt * cos + partner * sin)
    return outs


def _in_proj_kernel(x_ref, mod_ref, g1_ref, w1_ref, bd_ref, gq_ref, gk_ref, gqa_ref, gkva_ref,
                    wqb_ref, tab_ref,
                    qa_ref, kva_ref, qb_ref, ck_ref, qc_ref, kc_ref, vc_ref, z_ref, xbc_ref, dt_ref):
    x = x_ref[...]
    m = mod_ref[0]
    sh1, sc1 = m[0:1], m[1:2]
    ms = jnp.mean(x * x, axis=-1, keepdims=True)
    h = (x * lax.rsqrt(ms + EPS) * g1_ref[...]) * (1.0 + sc1) + sh1
    u = _dot(h.astype(BF16), w1_ref[...])

    def tab(k):
        return tab_ref[:, k * LANES:(k + 1) * LANES]

    bd = bd_ref[...]

    def head_norm(v, gain):
        w = v.shape[1]
        ss = _dot_split(v * v, bd[:w, :w])
        return v * lax.rsqrt(ss * (1.0 / HEAD_DIM) + EPS) * gain

    qa = _rope(head_norm(u[:, 0:256], gq_ref[...]), tab(0), tab(1), HEAD_DIM // 4)
    for t in range(2):
        qa_ref[:, t * LANES:(t + 1) * LANES] = qa[t].astype(BF16)
    ka = _rope(head_norm(u[:, 256:384], gk_ref[...]), tab(2), tab(3), HEAD_DIM // 4)
    kva_ref[:, 0:128] = ka[0]
    kva_ref[:, 128:256] = u[:, 384:512]

    cq = u[:, 512:768]
    msq = jnp.sum(cq * cq, axis=-1, keepdims=True) * (1.0 / Q_LORA)
    yq = cq * lax.rsqrt(msq + EPS) * gqa_ref[...]
    qb = _rope(_dot(yq.astype(BF16), wqb_ref[...]), tab(4), tab(5), MLA_ROPE // 4)
    for t in range(4):
        qb_ref[:, t * LANES:(t + 1) * LANES] = qb[t].astype(BF16)
    ckv = u[:, 768:896]
    msk = jnp.mean(ckv * ckv, axis=-1, keepdims=True)
    ck_ref[:, 0:128] = ckv * lax.rsqrt(msk + EPS) * gkva_ref[...]
    ck_ref[:, 128:256] = _rope(u[:, 896:1024], tab(6), tab(7), MLA_ROPE // 4)[0]

    qc = _rope(u[:, 1024:1280], tab(8), tab(9), DIFF_D // 4)
    kc = _rope(u[:, 1280:1536], tab(10), tab(11), DIFF_D // 4)
    for t in range(2):
        qc_ref[:, t * LANES:(t + 1) * LANES] = qc[t].astype(BF16)
        kc_ref[:, t * LANES:(t + 1) * LANES] = kc[t]
    vc_ref[...] = u[:, 1536:1792]

    z_ref[...] = u[:, 1792:2048]
    xbc_ref[...] = u[:, 2048:2560]
    dt_ref[...] = u[:, 2560:2688]


def _in_proj(x, mod3, g1, w1, bd, gq, gk, gqa, gkva, wqb, tabs):
    row = lambda w: pl.BlockSpec((PROJ_TM, w), lambda i: (i, 0))
    full = lambda a: pl.BlockSpec(a.shape, lambda i: (0,) * a.ndim)
    outs = [(256, BF16), (256, F32), (512, BF16), (256, F32), (256, BF16), (256, F32), (256, F32),
            (256, F32), (512, F32), (128, F32)]
    return pl.pallas_call(
        _in_proj_kernel,
        out_shape=[jax.ShapeDtypeStruct((N_TOK, w), d) for w, d in outs],
        grid=(N_TOK // PROJ_TM,),
        in_specs=[row(D_MODEL),
                  pl.BlockSpec((1, N_MOD, D_MODEL), lambda i: (_mod_row(i, PROJ_TM), 0, 0)),
                  full(g1), full(w1), full(bd), full(gq), full(gk), full(gqa), full(gkva), full(wqb),
                  pl.BlockSpec((PROJ_TM, N_TAB * LANES), lambda i: (_tab_block(i, PROJ_TM), 0))],
        out_specs=[row(w) for w, _ in outs],
        compiler_params=_params("parallel"),
        name="in_proj",
    )(x, mod3, g1, w1, bd, gq, gk, gqa, gkva, wqb, tabs)


def _attend(q, k, v_ones):
    s = _dot_nt(q, k)
    e = jnp.exp2(s - jnp.max(s, axis=-1, keepdims=True))
    out = _dot(e.astype(BF16), v_ones)
    return out[:, :LANES] / out[:, LANES:]


def _with_ones(v):
    return jnp.concatenate([v, jnp.ones_like(v)], axis=-1)


def _half_mask(rows):
    lane = lax.broadcasted_iota(jnp.int32, (rows, LANES), 1)
    return lane < (LANES // 2)


def _keys(past_ref, new_ref, lo, hi):
    new = new_ref[:, lo:hi].astype(BF16)
    if past_ref is None:
        return new
    return jnp.concatenate([past_ref[0, :, lo:hi].astype(BF16), new], axis=0)


def _attn_a_kernel(latent, *refs):
    if latent:
        q_ref, kv_ref, past_ref, _, o_ref = refs
    else:
        (q_ref, kv_ref, _, o_ref), past_ref = refs, None
    q = q_ref[...]
    k = _keys(past_ref, kv_ref, 0, 128)
    v = _with_ones(_keys(past_ref, kv_ref, 128, 256))
    lo = _half_mask(q.shape[0])
    for g in range(2):
        qt = q[:, g * LANES:(g + 1) * LANES].astype(F32)
        res = []
        for half in range(2):
            qm = jnp.where(lo, qt, 0.0) if half == 0 else jnp.where(lo, 0.0, qt)
            res.append(_attend(qm.astype(BF16), k, v))
        o_ref[:, g * LANES:(g + 1) * LANES] = jnp.where(lo, res[0], res[1])


def _attn_b_kernel(latent, *refs):
    if latent:
        q_ref, ck_ref, past_ref, wk_ref, wv_ref, _, o_ref, k_s, v_s = refs
    else:
        (q_ref, ck_ref, wk_ref, wv_ref, _, o_ref, k_s, v_s), past_ref = refs, None

    @pl.when(pl.program_id(1) == 0)
    def _():
        ck = _keys(past_ref, ck_ref, 0, 256)
        k_s[...] = _dot(ck, wk_ref[...]).astype(BF16)
        v = _dot(ck[:, 0:128], wv_ref[...]).astype(BF16)
        for j in range(2):
            v_s[:, 2 * j * LANES:2 * (j + 1) * LANES] = _with_ones(v[:, j * LANES:(j + 1) * LANES])

    q = q_ref[...]
    lo = _half_mask(q.shape[0])
    for j in range(2):
        v = v_s[:, 2 * j * LANES:2 * (j + 1) * LANES]
        res = []
        for half in range(2):
            h = 2 * j + half
            res.append(_attend(q[:, h * LANES:(h + 1) * LANES], k_s[:, h * LANES:(h + 1) * LANES], v))
        o_ref[:, j * LANES:(j + 1) * LANES] = jnp.where(lo, res[0], res[1])


def _attn_c_kernel(lam_init, latent, *refs):
    if latent:
        q_ref, k_ref, v_ref, pk_ref, pv_ref, lq1_ref, lk1_ref, lq2_ref, lk2_ref, g_ref, _, o_ref = refs
    else:
        (q_ref, k_ref, v_ref, lq1_ref, lk1_ref, lq2_ref, lk2_ref, g_ref, _, o_ref), pk_ref, pv_ref = refs, None, None
    lam = (jnp.exp(jnp.sum(lq1_ref[...] * lk1_ref[...], axis=-1, keepdims=True))
           - jnp.exp(jnp.sum(lq2_ref[...] * lk2_ref[...], axis=-1, keepdims=True)) + lam_init)
    q = q_ref[...]
    rows = q.shape[0]
    lane = lax.broadcasted_iota(jnp.int32, (rows, LANES), 1)
    lo = lane < (LANES // 2)
    for j in range(2):
        qt = q[:, j * LANES:(j + 1) * LANES].astype(F32)
        k = _keys(pk_ref, k_ref, j * LANES, (j + 1) * LANES)
        v = _with_ones(_keys(pv_ref, v_ref, j * LANES, (j + 1) * LANES))
        res = []
        for half in range(2):
            parts = []
            for t in range(2):
                quarter = 2 * half + t
                qm = jnp.where(lane // (LANES // 4) == quarter, qt, 0.0)
                parts.append(_attend(qm.astype(BF16), k, v))
            res.append(parts[0] - lam * parts[1])
        o = jnp.where(lo, res[0], res[1])
        o2 = o * o
        ss_lo = jnp.sum(jnp.where(lo, o2, 0.0), axis=-1, keepdims=True)
        ss_hi = jnp.sum(jnp.where(lo, 0.0, o2), axis=-1, keepdims=True)
        ss = jnp.where(lo, ss_lo, ss_hi) * (1.0 / DIFF_V)
        o_ref[:, j * LANES:(j + 1) * LANES] = (o * lax.rsqrt(ss + SUBLN_EPS) * g_ref[...]) * (1.0 - lam_init)


def _seq_call(body, name, latent, q, news, pasts, consts, prev_out, scratch=(), q_semantics="parallel"):
    const_specs = [pl.BlockSpec(a.shape, lambda b, i, n=a.ndim: (0,) * n) for a in consts]
    if latent:
        tq = LAT_TQ
        tile = lambda b, i: (N_CTX // LAT_TQ + b * (DEC_SEQ // LAT_TQ) + i, 0)
        grid = (DEC_BATCH, DEC_SEQ // LAT_TQ)
        new_specs = [pl.BlockSpec((DEC_SEQ, a.shape[1]), lambda b, i: (N_CTX // DEC_SEQ + b, 0)) for a in news]
        past_specs = [pl.BlockSpec((1, PAST_LEN, a.shape[2]), lambda b, i: (b, 0, 0)) for a in pasts]
    else:
        tq = SEQ
        tile = lambda b, i: (b, 0)
        grid = (BATCH, 1)
        new_specs = [pl.BlockSpec((SEQ, a.shape[1]), tile) for a in news]
        past_specs, pasts = [], ()
    in_specs = ([pl.BlockSpec((tq, q.shape[1]), tile)] + new_specs + past_specs + const_specs
                + [pl.BlockSpec(memory_space=pl.ANY)])
    args = [q, *news, *pasts, *consts, prev_out]
    return pl.pallas_call(
        functools.partial(body, latent),
        out_shape=jax.ShapeDtypeStruct((N_TOK, 256), F32),
        grid=grid,
        in_specs=in_specs,
        out_specs=pl.BlockSpec((tq, 256), tile),
        scratch_shapes=list(scratch),
        input_output_aliases={len(args) - 1: 0},
        compiler_params=_params("parallel", q_semantics),
        name=name + ("_lat" if latent else "_ctx"),
    )(*args)


def _attention_mixers(qa, kva, qb, ck, qc, kc, vc, past_kva, past_ck, past_kc, past_vc, wk, wv,
                      lams, gsub, lam_init):
    def both(body, name, q, news, pasts, consts, scratch_fn=None, q_semantics="parallel"):
        sc = (lambda lk: ()) if scratch_fn is None else scratch_fn
        out = jnp.zeros((N_TOK, 256), F32)
        out = _seq_call(body, name, False, q, news, (), consts, out, sc(SEQ), q_semantics)
        return _seq_call(body, name, True, q, news, pasts, consts, out, sc(PAST_LEN + DEC_SEQ), q_semantics)

    oa = both(_attn_a_kernel, "attn_gqa", qa, [kva], [past_kva], [])
    ob = both(_attn_b_kernel, "attn_mla", qb, [ck], [past_ck], [wk, wv],
              lambda lk: (pltpu.VMEM((lk, 512), BF16), pltpu.VMEM((lk, 512), BF16)), "arbitrary")
    oc = both(functools.partial(_attn_c_kernel, lam_init), "attn_diff", qc, [kc, vc],
              [past_kc, past_vc], [*lams, gsub])
    return oa, ob, oc


Q = SSD_CHUNK


def _ssd_kernel(z_ref, xbc_ref, dt_ref, cw_ref, cb_ref, dtb_ref, alog_ref, dvec_ref, ng_ref, h0f_ref, h0b_ref, _,
                out_ref, hf_ref, hb_ref, act_s, cum_s, dtv_s, y_s):
    seq = z_ref.shape[0]
    nc = seq // Q
    row = lax.broadcasted_iota(jnp.int32, (Q, Q), 0)
    col = lax.broadcasted_iota(jnp.int32, (Q, Q), 1)
    lower = row >= col
    upper = row <= col
    tril = jnp.where(lower, 1.0, 0.0).astype(BF16)
    triu = jnp.where(upper, 1.0, 0.0).astype(BF16)
    rowc = lax.broadcasted_iota(jnp.int32, (Q, CONV_DIM), 0)
    lane = lax.broadcasted_iota(jnp.int32, (Q, LANES), 1)
    a_neg = -jnp.exp(alog_ref[...])
    cw = cw_ref[...]
    hf_ref[0] = h0f_ref[0]
    hb_ref[0] = h0b_ref[0]

    def fwd_chunk(c, carry):
        base = pl.multiple_of(c * Q, Q)
        x0 = xbc_ref[pl.ds(base, Q), :]
        prev = xbc_ref[pl.ds(pl.multiple_of(jnp.maximum(base - 8, 0), 8), 8), :][7:8, :]
        nxt = xbc_ref[pl.ds(pl.multiple_of(jnp.minimum(base + Q, seq - 8), 8), 8), :][0:1, :]
        prev = jnp.where(c > 0, prev, 0.0)
        nxt = jnp.where(c < nc - 1, nxt, 0.0)
        xm1 = jnp.where(rowc == 0, prev, pltpu.roll(x0, 1, 0))
        xp1 = jnp.where(rowc == Q - 1, nxt, pltpu.roll(x0, Q - 1, 0))
        act = _silu(xm1 * cw[0:1] + x0 * cw[1:2] + xp1 * cw[2:3] + cb_ref[...])
        act_s[pl.ds(base, Q), :] = act
        xs = act[:, 0:256]
        bm = act[:, 256:384]
        cm = act[:, 384:512]

        dtr = dt_ref[pl.ds(base, Q), :] + dtb_ref[...]
        dtv = jnp.maximum(dtr, 0.0) + jnp.log1p(jnp.exp(-jnp.abs(dtr)))
        dta = dtv * a_neg
        cum = jnp.where(lane < D_HEADS, _dot_split_left(tril, dta), _dot_split_left(triu, dta))
        cum_s[pl.ds(base, Q), :] = cum
        dtv_s[pl.ds(base, Q), :] = dtv
        cum_t = cum.T
        dtv_t = dtv.T
        bm_t = bm.T
        ys = []
        for h in range(D_HEADS):
            g = h // (D_HEADS // D_GROUPS)
            cg = cm[:, g * D_STATE:(g + 1) * D_STATE].astype(BF16)
            cb_mat = _dot_nt(cg, bm[:, g * D_STATE:(g + 1) * D_STATE].astype(BF16))
            cf = cum[:, h:h + 1]
            cb = cum[:, D_HEADS + h:D_HEADS + h + 1]
            l_f = jnp.exp(jnp.where(lower, cf - cum_t[h:h + 1, :], NEG_BIG))
            l_b = jnp.exp(jnp.where(upper, cb - cum_t[D_HEADS + h:D_HEADS + h + 1, :], NEG_BIG))
            mix = cb_mat * (l_f * dtv_t[h:h + 1, :] + l_b * dtv_t[D_HEADS + h:D_HEADS + h + 1, :])
            xh = xs[:, h * D_HEADDIM:(h + 1) * D_HEADDIM]
            y = _dot(mix.astype(BF16), xh.astype(BF16))
            state = hf_ref[0, h]
            y = y + _dot(cg, state.astype(BF16)) * jnp.exp(cf)
            y = y + dvec_ref[:, h * D_HEADDIM:(h + 1) * D_HEADDIM] * xh
            ys.append(y)
            last = cum[Q - 1:Q, h:h + 1]
            wgt = jnp.exp(last - cf) * dtv[:, h:h + 1]
            st = _dot(bm_t[g * D_STATE:(g + 1) * D_STATE, :].astype(BF16), (xh * wgt).astype(BF16))
            hf_ref[0, h] = state * jnp.exp(last) + st
        y_s[pl.ds(base, Q), :] = jnp.concatenate(ys, axis=-1)
        return carry

    lax.fori_loop(0, nc, fwd_chunk, 0)

    def bwd_chunk(i, carry):
        c = nc - 1 - i
        base = pl.multiple_of(c * Q, Q)
        act = act_s[pl.ds(base, Q), :]
        cum = cum_s[pl.ds(base, Q), :]
        dtv = dtv_s[pl.ds(base, Q), :]
        xs = act[:, 0:256]
        bm_t = act[:, 256:384].T
        cm = act[:, 384:512]
        ys = []
        for h in range(D_HEADS):
            g = h // (D_HEADS // D_GROUPS)
            cg = cm[:, g * D_STATE:(g + 1) * D_STATE].astype(BF16)
            cb = cum[:, D_HEADS + h:D_HEADS + h + 1]
            xh = xs[:, h * D_HEADDIM:(h + 1) * D_HEADDIM]
            state = hb_ref[0, h]
            ys.append(_dot(cg, state.astype(BF16)) * jnp.exp(cb))
            first = cum[0:1, D_HEADS + h:D_HEADS + h + 1]
            wgt = jnp.exp(first - cb) * dtv[:, D_HEADS + h:D_HEADS + h + 1]
            st = _dot(bm_t[g * D_STATE:(g + 1) * D_STATE, :].astype(BF16), (xh * wgt).astype(BF16))
            hb_ref[0, h] = state * jnp.exp(first) + st
        y = y_s[pl.ds(base, Q), :] + jnp.concatenate(ys, axis=-1)
        gated = y * _silu(z_ref[pl.ds(base, Q), :])
        ms = jnp.mean(gated * gated, axis=-1, keepdims=True)
        out_ref[pl.ds(base, Q), :] = gated * lax.rsqrt(ms + EPS) * ng_ref[...]
        return carry

    lax.fori_loop(0, nc, bwd_chunk, 0)


def _ssd(latent, z, xbc, dt, consts, h0f, h0b, prev_out):
    bsz, seq, first = (DEC_BATCH, DEC_SEQ, N_CTX // DEC_SEQ) if latent else (BATCH, SEQ, 0)
    per_seq = lambda w: pl.BlockSpec((seq, w), lambda b: (first + b, 0))
    const_specs = [pl.BlockSpec(a.shape, lambda b, n=a.ndim: (0,) * n) for a in consts]
    st_spec = pl.BlockSpec((1, D_HEADS, D_STATE, D_HEADDIM), lambda b: (b, 0, 0, 0))
    st_shape = jax.ShapeDtypeStruct((bsz, D_HEADS, D_STATE, D_HEADDIM), F32)
    in_specs = ([per_seq(256), per_seq(512), per_seq(128)] + const_specs + [st_spec, st_spec]
                + [pl.BlockSpec(memory_space=pl.ANY)])
    args = [z, xbc, dt, *consts, h0f, h0b, prev_out]
    aliases = {len(args) - 1: 0}
    return pl.pallas_call(
        _ssd_kernel,
        out_shape=[jax.ShapeDtypeStruct((N_TOK, D_INNER), F32), st_shape, st_shape],
        grid=(bsz,),
        in_specs=in_specs,
        out_specs=[per_seq(D_INNER), st_spec, st_spec],
        scratch_shapes=[pltpu.VMEM((seq, CONV_DIM), F32), pltpu.VMEM((seq, LANES), F32),
                        pltpu.VMEM((seq, LANES), F32), pltpu.VMEM((seq, D_INNER), F32)],
        input_output_aliases=aliases,
        compiler_params=_params("parallel"),
        name="ssd_lat" if latent else "ssd_ctx",
    )(*args)


def _out_proj_kernel(x_ref, oa_ref, ob_ref, oc_ref, od_ref, mod_ref, wo_ref, g2_ref, rwh_ref, rwl_ref,
                     rb_ref, x1_ref, h2_ref, te_ref, gt_ref, cnt_ref):
    m = mod_ref[0]
    gate1, sh2, sc2 = m[2:3], m[3:4], m[4:5]
    mixed = (_dot(oa_ref[...].astype(BF16), wo_ref[0:256, :])
             + _dot(ob_ref[...].astype(BF16), wo_ref[256:512, :])
             + _dot(oc_ref[...].astype(BF16), wo_ref[512:768, :])
             + _dot(od_ref[...].astype(BF16), wo_ref[768:1024, :]))
    x1 = x_ref[...] + gate1 * mixed
    x1_ref[...] = x1
    ms = jnp.mean(x1 * x1, axis=-1, keepdims=True)
    h2 = (x1 * lax.rsqrt(ms + EPS) * g2_ref[...]) * (1.0 + sc2) + sh2
    h2_ref[...] = h2.astype(BF16)

    hi = h2.astype(BF16)
    lo = (h2 - hi.astype(F32)).astype(BF16)
    logits = _dot(hi, rwh_ref[...]) + _dot(lo, rwh_ref[...]) + _dot(hi, rwl_ref[...]) + rb_ref[...]
    lane = lax.broadcasted_iota(jnp.int32, logits.shape, 1)
    vals, idxs = [], []
    for _ in range(TOP_K):
        mx = jnp.max(logits, axis=-1, keepdims=True)
        ix = jnp.min(jnp.where(logits == mx, lane, LANES), axis=-1, keepdims=True)
        vals.append(mx)
        idxs.append(ix)
        logits = jnp.where(lane == ix, -3e38, logits)
    es = [jnp.exp(v - vals[0]) for v in vals]
    den = es[0] + es[1] + es[2] + es[3]
    te = jnp.zeros(lane.shape, jnp.int32)
    gt = jnp.zeros(lane.shape, F32)
    member = jnp.zeros(lane.shape, F32)
    for k in range(TOP_K):
        te = jnp.where(lane == k, idxs[k], te)
        gt = jnp.where(lane == k, es[k] / den, gt)
        member = jnp.where(lane == idxs[k], 1.0, member)
    te_ref[...] = te[:, 0:TOP_K]
    gt_ref[...] = gt[:, 0:TOP_K]
    for t in range(PROJ_TM // TM):
        count = jnp.sum(member[t * TM:(t + 1) * TM], axis=0, keepdims=True)
        cnt_ref[t] = jnp.broadcast_to(count, (8, LANES)).astype(jnp.int32)


def _out_proj(x, oa, ob, oc, od, mod3, wo, g2, rwh, rwl, rb):
    row = lambda w: pl.BlockSpec((PROJ_TM, w), lambda i: (i, 0))
    full = lambda a: pl.BlockSpec(a.shape, lambda i: (0,) * a.ndim)
    return pl.pallas_call(
        _out_proj_kernel,
        out_shape=[jax.ShapeDtypeStruct((N_TOK, D_MODEL), F32), jax.ShapeDtypeStruct((N_TOK, D_MODEL), BF16),
                   jax.ShapeDtypeStruct((N_TOK, TOP_K), jnp.int32), jax.ShapeDtypeStruct((N_TOK, TOP_K), F32),
                   jax.ShapeDtypeStruct((N_TILES, 8, LANES), jnp.int32)],
        grid=(N_TOK // PROJ_TM,),
        in_specs=[row(D_MODEL), row(256), row(256), row(256), row(256),
                  pl.BlockSpec((1, N_MOD, D_MODEL), lambda i: (_mod_row(i, PROJ_TM), 0, 0)),
                  full(wo), full(g2), full(rwh), full(rwl), full(rb)],
        out_specs=[row(D_MODEL), row(D_MODEL), row(TOP_K), row(TOP_K),
                   pl.BlockSpec((PROJ_TM // TM, 8, LANES), lambda i: (i, 0, 0))],
        compiler_params=_params("parallel"),
        name="out_proj_router",
    )(x, oa, ob, oc, od, mod3, wo, g2, rwh, rwl, rb)


def _tile_rows(te, tri_ref, off_ref):
    lane = lax.broadcasted_iota(jnp.int32, (TM, LANES), 1)
    hits = [lane == te[:, k:k + 1] for k in range(TOP_K)]
    member = jnp.zeros((TM, LANES), F32)
    for hit in hits:
        member = jnp.where(hit, 1.0, member)
    rank = _dot(tri_ref[...], member.astype(BF16))
    pos = rank + off_ref[0][0:1, :].astype(F32)
    return [jnp.sum(jnp.where(hit, pos, 0.0), axis=-1, keepdims=True) for hit in hits]


def _segment_chunks(seg_ref, len_ref, tile, visit):
    def segment(e, row):
        length = len_ref[tile * N_EXPERTS + e]
        start = seg_ref[tile * N_EXPERTS + e]

        def chunk(c, carry):
            visit(CHUNKS[0], pl.multiple_of(row + c * CHUNKS[0], SEG_ALIGN),
                  pl.multiple_of(start + c * CHUNKS[0], SEG_ALIGN))
            return carry

        lax.fori_loop(0, lax.shift_right_logical(length, CHUNK_SHIFT), chunk, 0)
        for size in CHUNKS[1:]:
            done = jnp.bitwise_and(length, -2 * size)

            @pl.when(jnp.bitwise_and(length, size) != 0)
            def _():
                visit(size, pl.multiple_of(row + done, SEG_ALIGN), pl.multiple_of(start + done, SEG_ALIGN))
        return row + length

    lax.fori_loop(0, N_EXPERTS, segment, 0)


def _wait_chunks(nchunk_ref, tile, copy):
    for j, size in enumerate(CHUNKS):
        lax.fori_loop(0, nchunk_ref[tile * len(CHUNKS) + j], lambda c, carry, size=size: (copy(size).wait(), carry)[1], 0)


def _dispatch_kernel(seg_ref, len_ref, nchunk_ref, end_ref, nu_ref, h2_ref, te_ref, off_ref, tri_ref,
                     xb_ref, buf, sems, sem_z):
    i = pl.program_id(0)
    slot = i % 2

    def chunk_copy(size, s, buf_row, xb_row):
        return pltpu.make_async_copy(buf.at[s, pl.ds(buf_row, size)], xb_ref.at[pl.ds(xb_row, size)], sems.at[s])

    def wait_chunks(tile, s):
        _wait_chunks(nchunk_ref, tile, lambda size: chunk_copy(size, s, 0, 0))

    @pl.when(i == 0)
    def _():
        buf[1, 0:ZERO_ROWS, :] = jnp.zeros((ZERO_ROWS, D_MODEL), F32)
        tail = lambda e: pltpu.make_async_copy(
            buf.at[1, pl.ds(0, ZERO_ROWS)],
            xb_ref.at[pl.ds(pl.multiple_of(jnp.maximum(end_ref[e] - ZERO_ROWS, 0), SEG_ALIGN), ZERO_ROWS)], sem_z)
        block = lambda b: pltpu.make_async_copy(
            buf.at[1, pl.ds(0, MOE_BM)], xb_ref.at[pl.ds(pl.multiple_of(b * MOE_BM, MOE_BM), MOE_BM)], sem_z)
        for e in range(N_EXPERTS):
            tail(e).start()
        lax.fori_loop(nu_ref[0], MOE_BLOCKS, lambda b, carry: (block(b).start(), carry)[1], 0)
        for e in range(N_EXPERTS):
            tail(e).wait()
        lax.fori_loop(nu_ref[0], MOE_BLOCKS, lambda b, carry: (block(b).wait(), carry)[1], 0)

    rows = _tile_rows(te_ref[...], tri_ref, off_ref)
    lane = lax.broadcasted_iota(jnp.int32, (TM, LANES), 1)
    packed = jnp.zeros((TM, LANES), F32)
    for k in range(TOP_K):
        packed = jnp.where(lane == k, rows[k], packed)
    rows_t = packed.T
    buf_row = lax.broadcasted_iota(jnp.int32, (TILE_BUF, TM), 0).astype(F32)
    pick = jnp.zeros((TILE_BUF, TM), F32)
    for k in range(TOP_K):
        pick = jnp.where(buf_row == rows_t[k:k + 1, :], 1.0, pick)
    buf[slot] = _dot(pick.astype(BF16), h2_ref[...])

    @pl.when(i > 0)
    def _():
        wait_chunks(i - 1, 1 - slot)

    _segment_chunks(seg_ref, len_ref, i, lambda size, b, x: chunk_copy(size, slot, b, x).start())

    @pl.when(i == N_TILES - 1)
    def _():
        wait_chunks(i, slot)


def _dispatch(seg_start, seg_len, n_chunk, pad_end, n_used, h2, te, seg_off, tri):
    row = lambda w: pl.BlockSpec((TM, w), lambda i, *_: (i, 0))
    grid_spec = pltpu.PrefetchScalarGridSpec(
        num_scalar_prefetch=5,
        grid=(N_TILES,),
        in_specs=[row(D_MODEL), row(TOP_K), pl.BlockSpec((1, 8, LANES), lambda i, *_: (i, 0, 0)),
                  pl.BlockSpec(tri.shape, lambda i, *_: (0, 0))],
        out_specs=pl.BlockSpec(memory_space=pl.ANY),
        scratch_shapes=[pltpu.VMEM((2, TILE_BUF, D_MODEL), F32),
                        pltpu.SemaphoreType.DMA((2,)), pltpu.SemaphoreType.DMA],
    )
    return pl.pallas_call(
        _dispatch_kernel,
        out_shape=jax.ShapeDtypeStruct((YB_ROWS, D_MODEL), F32),
        grid_spec=grid_spec,
        compiler_params=_params("arbitrary"),
        name="moe_dispatch",
    )(seg_start, seg_len, n_chunk, pad_end, n_used, h2, te, seg_off, tri)


def _expert_kernel(be_ref, nu_ref, x_ref, wgu_ref, bgu_ref, wdn_ref, bdn_ref, o_ref, wgu_s, wdn_s):
    i = pl.program_id(0)
    used = i < nu_ref[0]
    prev = be_ref[jnp.maximum(i - 1, 0)]
    fresh = jnp.logical_or(i == 0, be_ref[i] != prev)

    @pl.when(jnp.logical_and(used, fresh))
    def _():
        wgu_s[...] = wgu_ref[0, 0].astype(BF16)
        wdn_s[...] = wdn_ref[0, 0].astype(BF16)

    @pl.when(used)
    def _():
        hgu = _dot(x_ref[...].astype(BF16), wgu_s[...]) + bgu_ref[0, 0]
        gate = jnp.minimum(hgu[:, :D_FF], SWIGLU_LIMIT)
        up = jnp.clip(hgu[:, D_FF:], -SWIGLU_LIMIT, SWIGLU_LIMIT)
        act = (up + 1.0) * gate * jax.nn.sigmoid(SWIGLU_ALPHA * gate)
        o_ref[...] = _dot(act.astype(BF16), wdn_s[...]) + bdn_ref[0, 0]

    @pl.when(jnp.logical_not(used))
    def _():
        o_ref[...] = jnp.zeros_like(o_ref)


def _experts(layer, block_e, n_used, xb, w_gu, b_gu, w_dn, b_dn):
    grid_spec = pltpu.PrefetchScalarGridSpec(
        num_scalar_prefetch=2,
        grid=(MOE_BLOCKS,),
        in_specs=[pl.BlockSpec((MOE_BM, D_MODEL), lambda i, be, nu: (i, 0)),
                  pl.BlockSpec((1, 1, D_MODEL, 2 * D_FF), lambda i, be, nu: (layer, be[i], 0, 0)),
                  pl.BlockSpec((1, 1, 1, 2 * D_FF), lambda i, be, nu: (layer, be[i], 0, 0)),
                  pl.BlockSpec((1, 1, D_FF, D_MODEL), lambda i, be, nu: (layer, be[i], 0, 0)),
                  pl.BlockSpec((1, 1, 1, D_MODEL), lambda i, be, nu: (layer, be[i], 0, 0))],
        out_specs=pl.BlockSpec((MOE_BM, D_MODEL), lambda i, be, nu: (i, 0)),
        scratch_shapes=[pltpu.VMEM((D_MODEL, 2 * D_FF), BF16), pltpu.VMEM((D_FF, D_MODEL), BF16)],
    )
    return pl.pallas_call(
        _expert_kernel,
        out_shape=jax.ShapeDtypeStruct((YB_ROWS, D_MODEL), F32),
        grid_spec=grid_spec,
        compiler_params=_params("arbitrary"),
        name="experts",
    )(block_e, n_used, xb, w_gu, b_gu.reshape(DEPTH, N_EXPERTS, 1, 2 * D_FF), w_dn,
      b_dn.reshape(DEPTH, N_EXPERTS, 1, D_MODEL))


def _combine_kernel(final, seg_ref, len_ref, nchunk_ref, x1_ref, te_ref, gt_ref, off_ref, mod_ref, fg_ref,
                    tri_ref, yb_ref, *rest):
    if final:
        o_ctx_ref, o_lat_ref, buf, sems = rest
    else:
        o_ref, buf, sems = rest
    i = pl.program_id(0)
    slot = i % 2

    def chunk_copy(size, s, buf_row, yb_row):
        return pltpu.make_async_copy(yb_ref.at[pl.ds(yb_row, size)], buf.at[s, pl.ds(buf_row, size)], sems.at[s])

    def request(tile, s):
        _segment_chunks(seg_ref, len_ref, tile, lambda size, b, y: chunk_copy(size, s, b, y).start())

    @pl.when(i == 0)
    def _():
        buf[...] = jnp.zeros_like(buf)
        request(0, 0)

    @pl.when(i + 1 < N_TILES)
    def _():
        request(i + 1, 1 - slot)

    rows = _tile_rows(te_ref[...], tri_ref, off_ref)
    gt = gt_ref[...]
    buf_row = lax.broadcasted_iota(jnp.int32, (TM, TILE_BUF), 1).astype(F32)
    place = jnp.zeros((TM, TILE_BUF), F32)
    for k in range(TOP_K):
        place = jnp.where(buf_row == rows[k], gt[:, k:k + 1], place)

    _wait_chunks(nchunk_ref, i, lambda size: chunk_copy(size, slot, 0, 0))
    y = _dot(place.astype(BF16), buf[slot].astype(BF16))
    x2 = x1_ref[...] + mod_ref[0][5:6] * y
    if not final:
        o_ref[...] = x2
        return
    ms = jnp.mean(x2 * x2, axis=-1, keepdims=True)
    x2 = x2 * lax.rsqrt(ms + EPS) * fg_ref[...]

    @pl.when(i < CTX_TILES)
    def _():
        o_ctx_ref[...] = x2

    @pl.when(i >= CTX_TILES)
    def _():
        o_lat_ref[...] = x2


def _combine(seg_start, seg_len, n_chunk, x1, te, gates, seg_off, mod3, fg, tri, yb, final):
    row = lambda w: pl.BlockSpec((TM, w), lambda i, *_: (i, 0))
    full = lambda a: pl.BlockSpec(a.shape, lambda i, *_: (0,) * a.ndim)
    if final:
        out_shape = [jax.ShapeDtypeStruct((N_CTX, D_MODEL), F32), jax.ShapeDtypeStruct((N_LAT, D_MODEL), F32)]
        out_specs = [pl.BlockSpec((TM, D_MODEL), lambda i, *_: (jnp.minimum(i, CTX_TILES - 1), 0)),
                     pl.BlockSpec((TM, D_MODEL), lambda i, *_: (jnp.maximum(i - CTX_TILES, 0), 0))]
    else:
        out_shape = jax.ShapeDtypeStruct((N_TOK, D_MODEL), F32)
        out_specs = row(D_MODEL)
    grid_spec = pltpu.PrefetchScalarGridSpec(
        num_scalar_prefetch=3,
        grid=(N_TILES,),
        in_specs=[row(D_MODEL), row(TOP_K), row(TOP_K),
                  pl.BlockSpec((1, 8, LANES), lambda i, *_: (i, 0, 0)),
                  pl.BlockSpec((1, N_MOD, D_MODEL), lambda i, *_: (_mod_row(i), 0, 0)),
                  full(fg), full(tri),
                  pl.BlockSpec(memory_space=pl.ANY)],
        out_specs=out_specs,
        scratch_shapes=[pltpu.VMEM((2, TILE_BUF, D_MODEL), F32), pltpu.SemaphoreType.DMA((2,))],
    )
    return pl.pallas_call(
        functools.partial(_combine_kernel, final),
        out_shape=out_shape,
        grid_spec=grid_spec,
        compiler_params=_params("arbitrary"),
        name="moe_combine",
    )(seg_start, seg_len, n_chunk, x1, te, gates, seg_off, mod3, fg, tri, yb)


def _w1_index():
    idx = np.full((W1_COLS,), IN_WIDTH, np.int32)
    for g in range(2):
        for kv in range(2):
            idx[g * 128 + kv * 64:g * 128 + kv * 64 + 64] = np.arange(64) + kv * 128 + g * 64
    idx[256:512] = np.arange(256) + IN_OFF['a_k']
    idx[512:704] = np.arange(192) + IN_OFF['b_cq']
    idx[768:896] = np.arange(128) + IN_OFF['b_ckv']
    idx[896:928] = np.arange(32) + IN_OFF['b_kr']
    idx[1024:1792] = np.arange(768) + IN_OFF['c_q']
    idx[1792:2048] = np.arange(256) + IN_OFF['d_z']
    idx[2048:2560] = np.arange(512) + IN_OFF['d_xbc']
    idx[2560:2568] = np.arange(8) + IN_OFF['d_dtf']
    return idx


def _wout_a_index():
    idx = np.zeros((256,), np.int32)
    for g in range(2):
        for kv in range(2):
            idx[g * 128 + kv * 64:g * 128 + kv * 64 + 64] = np.arange(64) + kv * 128 + g * 64
    return idx


def _rope_tables():
    t = np.arange(DEC_SEQ)
    pos = ((t // GRID_W).astype(np.float32), (t % GRID_W).astype(np.float32))

    def unit(rot_dim):
        quarter, half = rot_dim // 4, rot_dim // 2
        inv = ROPE_THETA ** (-np.arange(0, half, 2, dtype=np.float32) / half)
        cos = np.zeros((DEC_SEQ, rot_dim), np.float32)
        sin = np.zeros((DEC_SEQ, rot_dim), np.float32)
        for seg in range(4):
            ang = pos[seg // 2][:, None] * inv[None, :].astype(np.float32)
            cos[:, seg * quarter:(seg + 1) * quarter] = np.cos(ang)
            sin[:, seg * quarter:(seg + 1) * quarter] = np.sin(ang) * (-1.0 if seg % 2 == 0 else 1.0)
        return cos, sin

    log2e = math.log2(math.e)
    specs = [
        (HEAD_DIM, (0, 64), (), HEAD_DIM ** -0.5 * log2e),
        (HEAD_DIM, (0, 64), (), 1.0),
        (MLA_ROPE, (MLA_NOPE,), (0, MLA_NOPE), (MLA_NOPE + MLA_ROPE) ** -0.5 * log2e),
        (MLA_ROPE, (0,), (), 1.0),
        (DIFF_D, (0, 32, 64, 96), (), DIFF_D ** -0.5 * log2e),
        (DIFF_D, (0, 32, 64, 96), (), 1.0),
    ]
    lat_cols, ident_cols = [], []
    for rot_dim, starts, passthrough, scale in specs:
        ucos, usin = unit(rot_dim)
        cos = np.zeros((DEC_SEQ, LANES), np.float32)
        sin = np.zeros((DEC_SEQ, LANES), np.float32)
        ident = np.zeros((1, LANES), np.float32)
        if passthrough:
            cos[:, passthrough[0]:passthrough[1]] = 1.0
            ident[:, passthrough[0]:passthrough[1]] = 1.0
        for s in starts:
            cos[:, s:s + rot_dim] = ucos
            sin[:, s:s + rot_dim] = usin
            ident[:, s:s + rot_dim] = 1.0
        lat_cols += [cos * scale, sin * scale]
        ident_cols += [ident * scale, np.zeros((1, LANES), np.float32)]
    lat = np.concatenate(lat_cols, axis=1)
    ident_blk = np.broadcast_to(np.concatenate(ident_cols, axis=1), (PROJ_TM, N_TAB * LANES))
    return np.concatenate([ident_blk, lat], axis=0).astype(np.float32)


def _block_diag_ones(n, blk):
    r = np.arange(n)
    return (r[:, None] // blk == r[None, :] // blk).astype(np.float32)


def _chunk_tables(seg_len):
    counts = [jnp.sum(seg_len // CHUNKS[0], axis=1)]
    counts += [jnp.sum(seg_len % (2 * size) // size, axis=1) for size in CHUNKS[1:]]
    return jnp.stack(counts, axis=1).reshape(-1).astype(jnp.int32)


def _strict_lower_ones(n):
    r = np.arange(n)
    return (r[None, :] < r[:, None]).astype(np.float32)


def kernel(x_prompt, x_sample, cache_gqa_k, cache_gqa_v, cache_mla_ckv, cache_mla_krope, cache_diff_k, cache_diff_v, state_ssd_fwd, state_ssd_bwd, c, c_ctx, norm1_g, norm2_g, w_ada, b_ada, w_in, w_out, gqa_qn_g, gqa_kn_g, mla_qa_g, mla_wqb, mla_kva_g, mla_wkvb, diff_lq1, diff_lk1, diff_lq2, diff_lk2, diff_subln_g, ssd_conv_w, ssd_conv_b, ssd_a_log_f, ssd_a_log_b, ssd_dt_bias_f, ssd_dt_bias_b, ssd_d, ssd_norm_g, router_w, router_b, moe_w_gu, moe_b_gu, moe_w_dn, moe_b_dn, final_g):
    tabs = jnp.asarray(_rope_tables())
    bd = jnp.asarray(_block_diag_ones(256, HEAD_DIM), BF16)
    tri = jnp.asarray(_strict_lower_ones(TM), BF16)
    w1_idx = _w1_index()
    woa_idx = _wout_a_index()

    cvec = jnp.zeros((MOD_ROWS, D_MODEL), F32).at[0].set(c_ctx).at[1:1 + DEC_BATCH].set(c)
    mod = _modulation(cvec, w_ada, b_ada).reshape(DEPTH, MOD_ROWS, N_MOD, D_MODEL)

    x = jnp.concatenate([x_prompt.reshape(N_CTX, D_MODEL), x_sample.reshape(N_LAT, D_MODEL)], axis=0)
    new_ctx = []
    for l in range(DEPTH):
        mod3 = mod[l]
        w1 = jnp.concatenate([w_in[l], jnp.zeros((D_MODEL, 1), F32)], axis=1)[:, w1_idx].astype(BF16)
        wqb = mla_wqb[l].reshape(Q_LORA, B_HEADS, MLA_NOPE + MLA_ROPE)
        wqb = jnp.pad(wqb, ((0, 256 - Q_LORA), (0, 0), (0, LANES - MLA_NOPE - MLA_ROPE)))
        wqb = wqb.reshape(256, B_HEADS * LANES).astype(BF16)
        wkvb = mla_wkvb[l].reshape(KV_LORA, B_HEADS, MLA_NOPE + MLA_V)
        wk_nope = jnp.pad(wkvb[:, :, :MLA_NOPE], ((0, 0), (0, 0), (0, LANES - MLA_NOPE)))
        eye_r = jnp.zeros((LANES, B_HEADS, LANES), F32)
        eye_r = eye_r.at[jnp.arange(MLA_ROPE), :, MLA_NOPE + jnp.arange(MLA_ROPE)].set(1.0)
        wk = jnp.concatenate([wk_nope, eye_r], axis=0).reshape(256, B_HEADS * LANES).astype(BF16)
        wv = wkvb[:, :, MLA_NOPE:].reshape(KV_LORA, B_HEADS * MLA_V).astype(BF16)
        wo = jnp.concatenate([w_out[l][woa_idx], w_out[l][256:]], axis=0).astype(BF16)
        gq = jnp.tile(gqa_qn_g[l], 4)[None, :]
        gk = jnp.tile(gqa_kn_g[l], 2)[None, :]
        gqa = jnp.pad(mla_qa_g[l], (0, 256 - Q_LORA))[None, :]
        gkva = mla_kva_g[l][None, :]
        rw = jnp.pad(router_w[l], ((0, 0), (0, LANES - N_EXPERTS)))
        rwh = rw.astype(BF16)
        rwl = (rw - rwh.astype(F32)).astype(BF16)
        rb = jnp.pad(router_b[l], (0, LANES - N_EXPERTS), constant_values=NEG_BIG)[None, :]

        qa, kva, qb, ck, qc, kc, vc, z, xbc, dt = _in_proj(
            x, mod3, norm1_g[l][None, :], w1, bd, gq, gk, gqa, gkva, wqb, tabs)

        past_kva = jnp.concatenate([cache_gqa_k[:, l].reshape(DEC_BATCH, PAST_LEN, 128),
                                    cache_gqa_v[:, l].reshape(DEC_BATCH, PAST_LEN, 128)], axis=-1)
        past_ck = jnp.concatenate([cache_mla_ckv[:, l], cache_mla_krope[:, l],
                                   jnp.zeros((DEC_BATCH, PAST_LEN, LANES - MLA_ROPE), F32)], axis=-1)
        past_kc = cache_diff_k[:, l].reshape(DEC_BATCH, PAST_LEN, 256)
        past_vc = cache_diff_v[:, l].reshape(DEC_BATCH, PAST_LEN, 256)
        lams = [a[l][None, :] for a in (diff_lq1, diff_lk1, diff_lq2, diff_lk2)]
        gsub = jnp.tile(diff_subln_g[l], 2)[None, :]
        lam_init = 0.8 - 0.6 * math.exp(-0.3 * l)
        oa, ob, oc = _attention_mixers(qa, kva, qb, ck, qc, kc, vc, past_kva, past_ck, past_kc, past_vc,
                                       wk, wv, lams, gsub, lam_init)

        pad8 = lambda f, b: jnp.pad(jnp.concatenate([f, b]), (0, LANES - 2 * D_HEADS))[None, :]
        ssd_consts = (ssd_conv_w[l], ssd_conv_b[l][None, :], pad8(ssd_dt_bias_f[l], ssd_dt_bias_b[l]),
                      pad8(ssd_a_log_f[l], ssd_a_log_b[l]), jnp.repeat(ssd_d[l], D_HEADDIM)[None, :],
                      ssd_norm_g[l][None, :])
        zeros_st = jnp.zeros((BATCH, D_HEADS, D_STATE, D_HEADDIM), F32)
        od, hf_c, hb_c = _ssd(False, z, xbc, dt, ssd_consts, zeros_st, zeros_st, jnp.zeros((N_TOK, D_INNER), F32))
        od, _, _ = _ssd(True, z, xbc, dt, ssd_consts, jnp.swapaxes(state_ssd_fwd[:, l], -1, -2),
                        jnp.swapaxes(state_ssd_bwd[:, l], -1, -2), od)

        x1, h2, te, gates, tile_cnt = _out_proj(x, oa, ob, oc, od, mod3, wo, norm2_g[l][None, :], rwh, rwl, rb)

        seg_cnt = tile_cnt[:, 0, :N_EXPERTS]
        seg_len = (seg_cnt + SEG_ALIGN - 1) // SEG_ALIGN * SEG_ALIGN
        region = (jnp.sum(seg_len, axis=0) + MOE_BM - 1) // MOE_BM * MOE_BM
        pad_end = jnp.cumsum(region).astype(jnp.int32)
        pad_start = pad_end - region
        seg_start = (pad_start[None, :] + jnp.cumsum(seg_len, axis=0) - seg_len).astype(jnp.int32)
        n_used = (pad_end[-1] // MOE_BM).astype(jnp.int32).reshape(1)
        blk_start = jnp.arange(MOE_BLOCKS, dtype=jnp.int32) * MOE_BM
        block_e = jnp.minimum(jnp.sum((pad_end[None, :] <= blk_start[:, None]).astype(jnp.int32), axis=1),
                              N_EXPERTS - 1).astype(jnp.int32)
        n_chunk = _chunk_tables(seg_len)
        seg_off = jnp.cumsum(seg_len, axis=1) - seg_len
        seg_off = jnp.broadcast_to(jnp.pad(seg_off, ((0, 0), (0, LANES - N_EXPERTS)))[:, None, :],
                                   (N_TILES, 8, LANES)).astype(jnp.int32)
        seg_start, seg_len = seg_start.reshape(-1), seg_len.reshape(-1).astype(jnp.int32)

        xb = _dispatch(seg_start, seg_len, n_chunk, pad_end, n_used, h2, te, seg_off, tri)
        yb = _experts(l, block_e, n_used, xb, moe_w_gu, moe_b_gu, moe_w_dn, moe_b_dn)
        x = _combine(seg_start, seg_len, n_chunk, x1, te, gates, seg_off, mod3, final_g[None, :], tri, yb,
                     l == DEPTH - 1)

        kva_c, ck_c = kva[:N_CTX], ck[:N_CTX]
        new_ctx.append((kva_c[:, 0:128].reshape(BATCH, SEQ, A_KV_HEADS, HEAD_DIM),
                        kva_c[:, 128:256].reshape(BATCH, SEQ, A_KV_HEADS, HEAD_DIM),
                        ck_c[:, 0:KV_LORA].reshape(BATCH, SEQ, KV_LORA),
                        ck_c[:, KV_LORA:KV_LORA + MLA_ROPE].reshape(BATCH, SEQ, MLA_ROPE),
                        kc[:N_CTX].reshape(BATCH, SEQ, C_HEADS, 2 * DIFF_D),
                        vc[:N_CTX].reshape(BATCH, SEQ, C_HEADS, DIFF_V),
                        jnp.swapaxes(hf_c, -1, -2), jnp.swapaxes(hb_c, -1, -2)))

    y_prompt = x[0].reshape(BATCH, SEQ, D_MODEL)
    y_sample = x[1].reshape(DEC_BATCH, DEC_SEQ, D_MODEL)
    caches = [jnp.stack([cl[i] for cl in new_ctx], axis=1) for i in range(8)]
    return (y_prompt, y_sample, *caches)
```

```python
import functools
import math

import numpy as np
import jax
import jax.numpy as jnp
from jax import lax
from jax.experimental import pallas as pl
from jax.experimental.pallas import tpu as pltpu

F32 = jnp.float32
BF16 = jnp.bfloat16

D_MODEL = 1024
BATCH = 16
SEQ = 256
DEPTH = 2
DEC_BATCH = 8
DEC_SEQ = 2048
PAST_LEN = 256
GRID_W = 64
ROPE_THETA = 10000.0
EPS = 1e-6
HEAD_DIM = 64
A_HEADS = 4
A_KV_HEADS = 2
B_HEADS = 4
MLA_NOPE = 64
MLA_ROPE = 32
MLA_V = 64
Q_LORA = 192
KV_LORA = 128
C_HEADS = 4
DIFF_D = 32
DIFF_V = 64
SUBLN_EPS = 1e-5
D_HEADS = 4
D_HEADDIM = 64
D_INNER = 256
D_GROUPS = 2
D_STATE = 64
CONV_DIM = 512
SSD_CHUNK = 128
N_EXPERTS = 32
TOP_K = 4
D_FF = 1024
SWIGLU_ALPHA = 1.702
SWIGLU_LIMIT = 7.0
N_MOD = 6

N_CTX = BATCH * SEQ
N_LAT = DEC_BATCH * DEC_SEQ
N_TOK = N_CTX + N_LAT

LANES = 128
TM = 256
N_TILES = N_TOK // TM
CTX_TILES = N_CTX // TM
LAT_TQ = 1024
PROJ_TM = 512
MOD_ROWS = 16
MOE_BM = 512
N_PAIRS = N_TOK * TOP_K
SEG_ALIGN = 8
CHUNKS = (32, 16, 8)
CHUNK_SHIFT = 5
MOE_BLOCKS = -(-(N_PAIRS + N_TILES * N_EXPERTS * (SEG_ALIGN - 1) + N_EXPERTS * (MOE_BM - 1)) // MOE_BM)
YB_ROWS = MOE_BLOCKS * MOE_BM
TILE_BUF = -(-(TM * TOP_K + N_EXPERTS * (SEG_ALIGN - 1)) // LANES) * LANES
ZERO_ROWS = MOE_BM
VMEM_LIMIT = 56 * 1024 * 1024
NEG_BIG = -1e30

W1_COLS = 2688
IN_OFF = dict(a_q=0, a_k=256, a_v=384, b_cq=512, b_ckv=704, b_kr=832, c_q=864, c_k=1120,
              c_v=1376, d_z=1632, d_xbc=1888, d_dtf=2400, d_dtb=2404)
IN_WIDTH = 2408
N_TAB = 12


def _mod_row(i, tile=TM):
    return jnp.where(i < N_CTX // tile, 0, 1 + (i - N_CTX // tile) // (DEC_SEQ // tile))


def _tab_block(i, tile):
    return jnp.where(i < N_CTX // tile, 0, 1 + (i - N_CTX // tile) % (DEC_SEQ // tile))


def _dot(a, b):
    return jnp.dot(a, b, preferred_element_type=F32)


def _dot_nt(a, b):
    return lax.dot_general(a, b, (((1,), (1,)), ((), ())), preferred_element_type=F32)


def _dot_split(x, m):
    hi = x.astype(BF16)
    lo = (x - hi.astype(F32)).astype(BF16)
    return _dot(hi, m) + _dot(lo, m)


def _dot_split_left(m, x):
    hi = x.astype(BF16)
    lo = (x - hi.astype(F32)).astype(BF16)
    return _dot(m, hi) + _dot(m, lo)


def _silu(x):
    return x * jax.nn.sigmoid(x)


def _params(*semantics):
    return pltpu.CompilerParams(dimension_semantics=semantics, vmem_limit_bytes=VMEM_LIMIT)


MOD_TN = 1536


def _mod_kernel(c_ref, w_ref, b_ref, o_ref):
    c = c_ref[...]
    s = _silu(c).astype(BF16)
    o_ref[0] = _dot(s, w_ref[0].astype(BF16)) + b_ref[0]


def _modulation(cvec, w_ada, b_ada):
    n = N_MOD * D_MODEL
    return pl.pallas_call(
        _mod_kernel,
        out_shape=jax.ShapeDtypeStruct((DEPTH, MOD_ROWS, n), F32),
        grid=(DEPTH, n // MOD_TN),
        in_specs=[pl.BlockSpec((MOD_ROWS, D_MODEL), lambda l, j: (0, 0)),
                  pl.BlockSpec((1, D_MODEL, MOD_TN), lambda l, j: (l, 0, j)),
                  pl.BlockSpec((1, 1, MOD_TN), lambda l, j: (l, 0, j))],
        out_specs=pl.BlockSpec((1, MOD_ROWS, MOD_TN), lambda l, j: (l, 0, j)),
        compiler_params=_params("parallel", "parallel"),
        name="adaln_mod",
    )(cvec, w_ada, b_ada.reshape(DEPTH, 1, n))


def _rope(x, cos, sin, quarter):
    lane = lax.broadcasted_iota(jnp.int32, (x.shape[0], LANES), 1)
    first = (lane // quarter) % 2 == 0
    outs = []
    for t in range(x.shape[1] // LANES):
        xt = x[:, t * LANES:(t + 1) * LANES]
        partner = jnp.where(first, pltpu.roll(xt, LANES - quarter, 1), pltpu.roll(xt, quarter, 1))
        outs.append(xt * cos + partner * sin)
    return outs


def _in_proj_kernel(x_ref, mod_ref, g1_ref, w1_ref, bd_ref, gq_ref, gk_ref, gqa_ref, gkva_ref,
                    wqb_ref, tab_ref,
                    qa_ref, kva_ref, qb_ref, ck_ref, qc_ref, kc_ref, vc_ref, z_ref, xbc_ref, dt_ref):
    x = x_ref[...]
    m = mod_ref[0]
    sh1, sc1 = m[0:1], m[1:2]
    ms = jnp.mean(x * x, axis=-1, keepdims=True)
    h = (x * lax.rsqrt(ms + EPS) * g1_ref[...]) * (1.0 + sc1) + sh1
    u = _dot(h.astype(BF16), w1_ref[...])

    def tab(k):
        return tab_ref[:, k * LANES:(k + 1) * LANES]

    bd = bd_ref[...]

    def head_norm(v, gain):
        w = v.shape[1]
        ss = _dot_split(v * v, bd[:w, :w])
        return v * lax.rsqrt(ss * (1.0 / HEAD_DIM) + EPS) * gain

    qa = _rope(head_norm(u[:, 0:256], gq_ref[...]), tab(0), tab(1), HEAD_DIM // 4)
    for t in range(2):
        qa_ref[:, t * LANES:(t + 1) * LANES] = qa[t].astype(BF16)
    ka = _rope(head_norm(u[:, 256:384], gk_ref[...]), tab(2), tab(3), HEAD_DIM // 4)
    kva_ref[:, 0:128] = ka[0]
    kva_ref[:, 128:256] = u[:, 384:512]

    cq = u[:, 512:768]
    msq = jnp.sum(cq * cq, axis=-1, keepdims=True) * (1.0 / Q_LORA)
    yq = cq * lax.rsqrt(msq + EPS) * gqa_ref[...]
    qb = _rope(_dot(yq.astype(BF16), wqb_ref[...]), tab(4), tab(5), MLA_ROPE // 4)
    for t in range(4):
        qb_ref[:, t * LANES:(t + 1) * LANES] = qb[t].astype(BF16)
    ckv = u[:, 768:896]
    msk = jnp.mean(ckv * ckv, axis=-1, keepdims=True)
    ck_ref[:, 0:128] = ckv * lax.rsqrt(msk + EPS) * gkva_ref[...]
    ck_ref[:, 128:256] = _rope(u[:, 896:1024], tab(6), tab(7), MLA_ROPE // 4)[0]

    qc = _rope(u[:, 1024:1280], tab(8), tab(9), DIFF_D // 4)
    kc = _rope(u[:, 1280:1536], tab(10), tab(11), DIFF_D // 4)
    for t in range(2):
        qc_ref[:, t * LANES:(t + 1) * LANES] = qc[t].astype(BF16)
        kc_ref[:, t * LANES:(t + 1) * LANES] = kc[t]
    vc_ref[...] = u[:, 1536:1792]

    z_ref[...] = u[:, 1792:2048]
    xbc_ref[...] = u[:, 2048:2560]
    dt_ref[...] = u[:, 2560:2688]


def _in_proj(x, mod3, g1, w1, bd, gq, gk, gqa, gkva, wqb, tabs):
    row = lambda w: pl.BlockSpec((PROJ_TM, w), lambda i: (i, 0))
    full = lambda a: pl.BlockSpec(a.shape, lambda i: (0,) * a.ndim)
    outs = [(256, BF16), (256, F32), (512, BF16), (256, F32), (256, BF16), (256, F32), (256, F32),
            (256, F32), (512, F32), (128, F32)]
    return pl.pallas_call(
        _in_proj_kernel,
        out_shape=[jax.ShapeDtypeStruct((N_TOK, w), d) for w, d in outs],
        grid=(N_TOK // PROJ_TM,),
        in_specs=[row(D_MODEL),
                  pl.BlockSpec((1, N_MOD, D_MODEL), lambda i: (_mod_row(i, PROJ_TM), 0, 0)),
                  full(g1), full(w1), full(bd), full(gq), full(gk), full(gqa), full(gkva), full(wqb),
                  pl.BlockSpec((PROJ_TM, N_TAB * LANES), lambda i: (_tab_block(i, PROJ_TM), 0))],
        out_specs=[row(w) for w, _ in outs],
        compiler_params=_params("parallel"),
        name="in_proj",
    )(x, mod3, g1, w1, bd, gq, gk, gqa, gkva, wqb, tabs)


def _attend(q, k, v_ones):
    s = _dot_nt(q, k)
    e = jnp.exp2(s - jnp.max(s, axis=-1, keepdims=True))
    out = _dot(e.astype(BF16), v_ones)
    return out[:, :LANES] / out[:, LANES:]


def _with_ones(v):
    return jnp.concatenate([v, jnp.ones_like(v)], axis=-1)


def _half_mask(rows):
    lane = lax.broadcasted_iota(jnp.int32, (rows, LANES), 1)
    return lane < (LANES // 2)


def _keys(past_ref, new_ref, lo, hi):
    new = new_ref[:, lo:hi].astype(BF16)
    if past_ref is None:
        return new
    return jnp.concatenate([past_ref[0, :, lo:hi].astype(BF16), new], axis=0)


def _attn_a_kernel(latent, *refs):
    if latent:
        q_ref, kv_ref, past_ref, _, o_ref = refs
    else:
        (q_ref, kv_ref, _, o_ref), past_ref = refs, None
    q = q_ref[...]
    k = _keys(past_ref, kv_ref, 0, 128)
    v = _with_ones(_keys(past_ref, kv_ref, 128, 256))
    lo = _half_mask(q.shape[0])
    for g in range(2):
        qt = q[:, g * LANES:(g + 1) * LANES].astype(F32)
        res = []
        for half in range(2):
            qm = jnp.where(lo, qt, 0.0) if half == 0 else jnp.where(lo, 0.0, qt)
            res.append(_attend(qm.astype(BF16), k, v))
        o_ref[:, g * LANES:(g + 1) * LANES] = jnp.where(lo, res[0], res[1])


def _attn_b_kernel(latent, *refs):
    if latent:
        q_ref, ck_ref, past_ref, wk_ref, wv_ref, _, o_ref, k_s, v_s = refs
    else:
        (q_ref, ck_ref, wk_ref, wv_ref, _, o_ref, k_s, v_s), past_ref = refs, None

    @pl.when(pl.program_id(1) == 0)
    def _():
        ck = _keys(past_ref, ck_ref, 0, 256)
        k_s[...] = _dot(ck, wk_ref[...]).astype(BF16)
        v = _dot(ck[:, 0:128], wv_ref[...]).astype(BF16)
        for j in range(2):
            v_s[:, 2 * j * LANES:2 * (j + 1) * LANES] = _with_ones(v[:, j * LANES:(j + 1) * LANES])

    q = q_ref[...]
    lo = _half_mask(q.shape[0])
    for j in range(2):
        v = v_s[:, 2 * j * LANES:2 * (j + 1) * LANES]
        res = []
        for half in range(2):
            h = 2 * j + half
            res.append(_attend(q[:, h * LANES:(h + 1) * LANES], k_s[:, h * LANES:(h + 1) * LANES], v))
        o_ref[:, j * LANES:(j + 1) * LANES] = jnp.where(lo, res[0], res[1])


def _attn_c_kernel(lam_init, latent, *refs):
    if latent:
        q_ref, k_ref, v_ref, pk_ref, pv_ref, lq1_ref, lk1_ref, lq2_ref, lk2_ref, g_ref, _, o_ref = refs
    else:
        (q_ref, k_ref, v_ref, lq1_ref, lk1_ref, lq2_ref, lk2_ref, g_ref, _, o_ref), pk_ref, pv_ref = refs, None, None
    lam = (jnp.exp(jnp.sum(lq1_ref[...] * lk1_ref[...], axis=-1, keepdims=True))
           - jnp.exp(jnp.sum(lq2_ref[...] * lk2_ref[...], axis=-1, keepdims=True)) + lam_init)
    q = q_ref[...]
    rows = q.shape[0]
    lane = lax.broadcasted_iota(jnp.int32, (rows, LANES), 1)
    lo = lane < (LANES // 2)
    for j in range(2):
        qt = q[:, j * LANES:(j + 1) * LANES].astype(F32)
        k = _keys(pk_ref, k_ref, j * LANES, (j + 1) * LANES)
        v = _with_ones(_keys(pv_ref, v_ref, j * LANES, (j + 1) * LANES))
        res = []
        for half in range(2):
            parts = []
            for t in range(2):
                quarter = 2 * half + t
                qm = jnp.where(lane // (LANES // 4) == quarter, qt, 0.0)
                parts.append(_attend(qm.astype(BF16), k, v))
            res.append(parts[0] - lam * parts[1])
        o = jnp.where(lo, res[0], res[1])
        o2 = o * o
        ss_lo = jnp.sum(jnp.where(lo, o2, 0.0), axis=-1, keepdims=True)
        ss_hi = jnp.sum(jnp.where(lo, 0.0, o2), axis=-1, keepdims=True)
        ss = jnp.where(lo, ss_lo, ss_hi) * (1.0 / DIFF_V)
        o_ref[:, j * LANES:(j + 1) * LANES] = (o * lax.rsqrt(ss + SUBLN_EPS) * g_ref[...]) * (1.0 - lam_init)


def _seq_call(body, name, latent, q, news, pasts, consts, prev_out, scratch=(), q_semantics="parallel"):
    const_specs = [pl.BlockSpec(a.shape, lambda b, i, n=a.ndim: (0,) * n) for a in consts]
    if latent:
        tq = LAT_TQ
        tile = lambda b, i: (N_CTX // LAT_TQ + b * (DEC_SEQ // LAT_TQ) + i, 0)
        grid = (DEC_BATCH, DEC_SEQ // LAT_TQ)
        new_specs = [pl.BlockSpec((DEC_SEQ, a.shape[1]), lambda b, i: (N_CTX // DEC_SEQ + b, 0)) for a in news]
        past_specs = [pl.BlockSpec((1, PAST_LEN, a.shape[2]), lambda b, i: (b, 0, 0)) for a in pasts]
    else:
        tq = SEQ
        tile = lambda b, i: (b, 0)
        grid = (BATCH, 1)
        new_specs = [pl.BlockSpec((SEQ, a.shape[1]), tile) for a in news]
        past_specs, pasts = [], ()
    in_specs = ([pl.BlockSpec((tq, q.shape[1]), tile)] + new_specs + past_specs + const_specs
                + [pl.BlockSpec(memory_space=pl.ANY)])
    args = [q, *news, *pasts, *consts, prev_out]
    return pl.pallas_call(
        functools.partial(body, latent),
        out_shape=jax.ShapeDtypeStruct((N_TOK, 256), F32),
        grid=grid,
        in_specs=in_specs,
        out_specs=pl.BlockSpec((tq, 256), tile),
        scratch_shapes=list(scratch),
        input_output_aliases={len(args) - 1: 0},
        compiler_params=_params("parallel", q_semantics),
        name=name + ("_lat" if latent else "_ctx"),
    )(*args)


def _attention_mixers(qa, kva, qb, ck, qc, kc, vc, past_kva, past_ck, past_kc, past_vc, wk, wv,
                      lams, gsub, lam_init):
    def both(body, name, q, news, pasts, consts, scratch_fn=None, q_semantics="parallel"):
        sc = (lambda lk: ()) if scratch_fn is None else scratch_fn
        out = jnp.zeros((N_TOK, 256), F32)
        out = _seq_call(body, name, False, q, news, (), consts, out, sc(SEQ), q_semantics)
        return _seq_call(body, name, True, q, news, pasts, consts, out, sc(PAST_LEN + DEC_SEQ), q_semantics)

    oa = both(_attn_a_kernel, "attn_gqa", qa, [kva], [past_kva], [])
    ob = both(_attn_b_kernel, "attn_mla", qb, [ck], [past_ck], [wk, wv],
              lambda lk: (pltpu.VMEM((lk, 512), BF16), pltpu.VMEM((lk, 512), BF16)), "arbitrary")
    oc = both(functools.partial(_attn_c_kernel, lam_init), "attn_diff", qc, [kc, vc],
              [past_kc, past_vc], [*lams, gsub])
    return oa, ob, oc


Q = SSD_CHUNK
SSD_UNROLL = 2


def _ssd_kernel(z_ref, xbc_ref, dt_ref, cw_ref, cb_ref, dtb_ref, alog_ref, dvec_ref, ng_ref, h0f_ref, h0b_ref, _,
                out_ref, hf_ref, hb_ref, act_s, cum_s, dtv_s, y_s):
    n_seq = hf_ref.shape[0]
    seq = z_ref.shape[0] // n_seq
    nc = seq // Q
    row = lax.broadcasted_iota(jnp.int32, (Q, Q), 0)
    col = lax.broadcasted_iota(jnp.int32, (Q, Q), 1)
    lower = row >= col
    upper = row <= col
    tril = jnp.where(lower, 1.0, 0.0).astype(BF16)
    triu = jnp.where(upper, 1.0, 0.0).astype(BF16)
    rowc = lax.broadcasted_iota(jnp.int32, (Q, CONV_DIM), 0)
    lane = lax.broadcasted_iota(jnp.int32, (Q, LANES), 1)
    a_neg = -jnp.exp(alog_ref[...])
    cw = cw_ref[...]
    hf_ref[...] = h0f_ref[...]
    hb_ref[...] = h0b_ref[...]

    def fwd_seq(c, s):
        off = s * seq
        base = pl.multiple_of(off + c * Q, Q)
        x0 = xbc_ref[pl.ds(base, Q), :]
        prev = xbc_ref[pl.ds(pl.multiple_of(off + jnp.maximum(c * Q - 8, 0), 8), 8), :][7:8, :]
        nxt = xbc_ref[pl.ds(pl.multiple_of(off + jnp.minimum(c * Q + Q, seq - 8), 8), 8), :][0:1, :]
        prev = jnp.where(c > 0, prev, 0.0)
        nxt = jnp.where(c < nc - 1, nxt, 0.0)
        xm1 = jnp.where(rowc == 0, prev, pltpu.roll(x0, 1, 0))
        xp1 = jnp.where(rowc == Q - 1, nxt, pltpu.roll(x0, Q - 1, 0))
        act = _silu(xm1 * cw[0:1] + x0 * cw[1:2] + xp1 * cw[2:3] + cb_ref[...])
        act_s[pl.ds(base, Q), :] = act
        xs = act[:, 0:256]
        bm = act[:, 256:384]
        cm = act[:, 384:512]

        dtr = dt_ref[pl.ds(base, Q), :] + dtb_ref[...]
        dtv = jnp.maximum(dtr, 0.0) + jnp.log1p(jnp.exp(-jnp.abs(dtr)))
        dta = dtv * a_neg
        cum = jnp.where(lane < D_HEADS, _dot_split_left(tril, dta), _dot_split_left(triu, dta))
        cum_s[pl.ds(base, Q), :] = cum
        dtv_s[pl.ds(base, Q), :] = dtv
        cum_t = cum.T
        dtv_t = dtv.T
        bm_t = bm.T
        ys = []
        for h in range(D_HEADS):
            g = h // (D_HEADS // D_GROUPS)
            cg = cm[:, g * D_STATE:(g + 1) * D_STATE].astype(BF16)
            cb_mat = _dot_nt(cg, bm[:, g * D_STATE:(g + 1) * D_STATE].astype(BF16))
            cf = cum[:, h:h + 1]
            cb = cum[:, D_HEADS + h:D_HEADS + h + 1]
            l_f = jnp.exp(jnp.where(lower, cf - cum_t[h:h + 1, :], NEG_BIG))
            l_b = jnp.exp(jnp.where(upper, cb - cum_t[D_HEADS + h:D_HEADS + h + 1, :], NEG_BIG))
            mix = cb_mat * (l_f * dtv_t[h:h + 1, :] + l_b * dtv_t[D_HEADS + h:D_HEADS + h + 1, :])
            xh = xs[:, h * D_HEADDIM:(h + 1) * D_HEADDIM]
            y = _dot(mix.astype(BF16), xh.astype(BF16))
            state = hf_ref[s, h]
            y = y + _dot(cg, state.astype(BF16)) * jnp.exp(cf)
            y = y + dvec_ref[:, h * D_HEADDIM:(h + 1) * D_HEADDIM] * xh
            ys.append(y)
            last = cum[Q - 1:Q, h:h + 1]
            wgt = jnp.exp(last - cf) * dtv[:, h:h + 1]
            st = _dot(bm_t[g * D_STATE:(g + 1) * D_STATE, :].astype(BF16), (xh * wgt).astype(BF16))
            hf_ref[s, h] = state * jnp.exp(last) + st
        y_s[pl.ds(base, Q), :] = jnp.concatenate(ys, axis=-1)

    def fwd_chunk(c, carry):
        for s in range(n_seq):
            fwd_seq(c, s)
        return carry

    lax.fori_loop(0, nc, fwd_chunk, 0, unroll=SSD_UNROLL)

    def bwd_seq(c, s):
        base = pl.multiple_of(s * seq + c * Q, Q)
        act = act_s[pl.ds(base, Q), :]
        cum = cum_s[pl.ds(base, Q), :]
        dtv = dtv_s[pl.ds(base, Q), :]
        xs = act[:, 0:256]
        bm_t = act[:, 256:384].T
        cm = act[:, 384:512]
        ys = []
        for h in range(D_HEADS):
            g = h // (D_HEADS // D_GROUPS)
            cg = cm[:, g * D_STATE:(g + 1) * D_STATE].astype(BF16)
            cb = cum[:, D_HEADS + h:D_HEADS + h + 1]
            xh = xs[:, h * D_HEADDIM:(h + 1) * D_HEADDIM]
            state = hb_ref[s, h]
            ys.append(_dot(cg, state.astype(BF16)) * jnp.exp(cb))
            first = cum[0:1, D_HEADS + h:D_HEADS + h + 1]
            wgt = jnp.exp(first - cb) * dtv[:, D_HEADS + h:D_HEADS + h + 1]
            st = _dot(bm_t[g * D_STATE:(g + 1) * D_STATE, :].astype(BF16), (xh * wgt).astype(BF16))
            hb_ref[s, h] = state * jnp.exp(first) + st
        y = y_s[pl.ds(base, Q), :] + jnp.concatenate(ys, axis=-1)
        gated = y * _silu(z_ref[pl.ds(base, Q), :])
        ms = jnp.mean(gated * gated, axis=-1, keepdims=True)
        out_ref[pl.ds(base, Q), :] = gated * lax.rsqrt(ms + EPS) * ng_ref[...]

    def bwd_chunk(i, carry):
        for s in range(n_seq):
            bwd_seq(nc - 1 - i, s)
        return carry

    lax.fori_loop(0, nc, bwd_chunk, 0, unroll=SSD_UNROLL)


def _ssd(latent, z, xbc, dt, consts, h0f, h0b, prev_out):
    bsz, seq, n_seq = (DEC_BATCH, DEC_SEQ, 1) if latent else (BATCH, SEQ, 2)
    rows = n_seq * seq
    first = N_CTX // rows if latent else 0
    per_seq = lambda w: pl.BlockSpec((rows, w), lambda b: (first + b, 0))
    const_specs = [pl.BlockSpec(a.shape, lambda b, n=a.ndim: (0,) * n) for a in consts]
    st_spec = pl.BlockSpec((n_seq, D_HEADS, D_STATE, D_HEADDIM), lambda b: (b, 0, 0, 0))
    st_shape = jax.ShapeDtypeStruct((bsz, D_HEADS, D_STATE, D_HEADDIM), F32)
    in_specs = ([per_seq(256), per_seq(512), per_seq(128)] + const_specs + [st_spec, st_spec]
                + [pl.BlockSpec(memory_space=pl.ANY)])
    args = [z, xbc, dt, *consts, h0f, h0b, prev_out]
    aliases = {len(args) - 1: 0}
    return pl.pallas_call(
        _ssd_kernel,
        out_shape=[jax.ShapeDtypeStruct((N_TOK, D_INNER), F32), st_shape, st_shape],
        grid=(bsz // n_seq,),
        in_specs=in_specs,
        out_specs=[per_seq(D_INNER), st_spec, st_spec],
        scratch_shapes=[pltpu.VMEM((rows, CONV_DIM), F32), pltpu.VMEM((rows, LANES), F32),
                        pltpu.VMEM((rows, LANES), F32), pltpu.VMEM((rows, D_INNER), F32)],
        input_output_aliases=aliases,
        compiler_params=_params("parallel"),
        name="ssd_lat" if latent else "ssd_ctx",
    )(*args)


def _out_proj_kernel(x_ref, oa_ref, ob_ref, oc_ref, od_ref, mod_ref, wo_ref, g2_ref, rwh_ref, rwl_ref,
                     rb_ref, x1_ref, h2_ref, te_ref, gt_ref, cnt_ref):
    m = mod_ref[0]
    gate1, sh2, sc2 = m[2:3], m[3:4], m[4:5]
    mixed = (_dot(oa_ref[...].astype(BF16), wo_ref[0:256, :])
             + _dot(ob_ref[...].astype(BF16), wo_ref[256:512, :])
             + _dot(oc_ref[...].astype(BF16), wo_ref[512:768, :])
             + _dot(od_ref[...].astype(BF16), wo_ref[768:1024, :]))
    x1 = x_ref[...] + gate1 * mixed
    x1_ref[...] = x1
    ms = jnp.mean(x1 * x1, axis=-1, keepdims=True)
    h2 = (x1 * lax.rsqrt(ms + EPS) * g2_ref[...]) * (1.0 + sc2) + sh2
    h2_ref[...] = h2.astype(BF16)

    hi = h2.astype(BF16)
    lo = (h2 - hi.astype(F32)).astype(BF16)
    logits = _dot(hi, rwh_ref[...]) + _dot(lo, rwh_ref[...]) + _dot(hi, rwl_ref[...]) + rb_ref[...]
    lane = lax.broadcasted_iota(jnp.int32, logits.shape, 1)
    vals, idxs = [], []
    for _ in range(TOP_K):
        mx = jnp.max(logits, axis=-1, keepdims=True)
        ix = jnp.min(jnp.where(logits == mx, lane, LANES), axis=-1, keepdims=True)
        vals.append(mx)
        idxs.append(ix)
        logits = jnp.where(lane == ix, -3e38, logits)
    es = [jnp.exp(v - vals[0]) for v in vals]
    den = es[0] + es[1] + es[2] + es[3]
    te = jnp.zeros(lane.shape, jnp.int32)
    gt = jnp.zeros(lane.shape, F32)
    member = jnp.zeros(lane.shape, F32)
    for k in range(TOP_K):
        te = jnp.where(lane == k, idxs[k], te)
        gt = jnp.where(lane == k, es[k] / den, gt)
        member = jnp.where(lane == idxs[k], 1.0, member)
    te_ref[...] = te[:, 0:TOP_K]
    gt_ref[...] = gt[:, 0:TOP_K]
    for t in range(PROJ_TM // TM):
        count = jnp.sum(member[t * TM:(t + 1) * TM], axis=0, keepdims=True)
        cnt_ref[t] = jnp.broadcast_to(count, (8, LANES)).astype(jnp.int32)


def _out_proj(x, oa, ob, oc, od, mod3, wo, g2, rwh, rwl, rb):
    row = lambda w: pl.BlockSpec((PROJ_TM, w), lambda i: (i, 0))
    full = lambda a: pl.BlockSpec(a.shape, lambda i: (0,) * a.ndim)
    return pl.pallas_call(
        _out_proj_kernel,
        out_shape=[jax.ShapeDtypeStruct((N_TOK, D_MODEL), F32), jax.ShapeDtypeStruct((N_TOK, D_MODEL), BF16),
                   jax.ShapeDtypeStruct((N_TOK, TOP_K), jnp.int32), jax.ShapeDtypeStruct((N_TOK, TOP_K), F32),
                   jax.ShapeDtypeStruct((N_TILES, 8, LANES), jnp.int32)],
        grid=(N_TOK // PROJ_TM,),
        in_specs=[row(D_MODEL), row(256), row(256), row(256), row(256),
                  pl.BlockSpec((1, N_MOD, D_MODEL), lambda i: (_mod_row(i, PROJ_TM), 0, 0)),
                  full(wo), full(g2), full(rwh), full(rwl), full(rb)],
        out_specs=[row(D_MODEL), row(D_MODEL), row(TOP_K), row(TOP_K),
                   pl.BlockSpec((PROJ_TM // TM, 8, LANES), lambda i: (i, 0, 0))],
        compiler_params=_params("parallel"),
        name="out_proj_router",
    )(x, oa, ob, oc, od, mod3, wo, g2, rwh, rwl, rb)


def _tile_rows(te, tri_ref, off_ref):
    lane = lax.broadcasted_iota(jnp.int32, (TM, LANES), 1)
    hits = [lane == te[:, k:k + 1] for k in range(TOP_K)]
    member = jnp.zeros((TM, LANES), F32)
    for hit in hits:
        member = jnp.where(hit, 1.0, member)
    rank = _dot(tri_ref[...], member.astype(BF16))
    pos = rank + off_ref[0][0:1, :].astype(F32)
    return [jnp.sum(jnp.where(hit, pos, 0.0), axis=-1, keepdims=True) for hit in hits]


def _segment_chunks(seg_ref, len_ref, tile, visit):
    def segment(e, row):
        length = len_ref[tile * N_EXPERTS + e]
        start = seg_ref[tile * N_EXPERTS + e]

        def chunk(c, carry):
            visit(CHUNKS[0], pl.multiple_of(row + c * CHUNKS[0], SEG_ALIGN),
                  pl.multiple_of(start + c * CHUNKS[0], SEG_ALIGN))
            return carry

        lax.fori_loop(0, lax.shift_right_logical(length, CHUNK_SHIFT), chunk, 0)
        for size in CHUNKS[1:]:
            done = jnp.bitwise_and(length, -2 * size)

            @pl.when(jnp.bitwise_and(length, size) != 0)
            def _():
                visit(size, pl.multiple_of(row + done, SEG_ALIGN), pl.multiple_of(start + done, SEG_ALIGN))
        return row + length

    row = 0
    for e in range(N_EXPERTS):
        row = segment(e, row)


def _wait_chunks(nchunk_ref, tile, copy):
    for j, size in enumerate(CHUNKS):
        lax.fori_loop(0, nchunk_ref[tile * len(CHUNKS) + j], lambda c, carry, size=size: (copy(size).wait(), carry)[1], 0)


def _dispatch_kernel(seg_ref, len_ref, nchunk_ref, end_ref, nu_ref, h2_ref, te_ref, off_ref, tri_ref,
                     xb_ref, buf, sems, sem_z):
    i = pl.program_id(0)
    slot = i % 2

    def chunk_copy(size, s, buf_row, xb_row):
        return pltpu.make_async_copy(buf.at[s, pl.ds(buf_row, size)], xb_ref.at[pl.ds(xb_row, size)], sems.at[s])

    def wait_chunks(tile, s):
        _wait_chunks(nchunk_ref, tile, lambda size: chunk_copy(size, s, 0, 0))

    @pl.when(i == 0)
    def _():
        buf[1, 0:ZERO_ROWS, :] = jnp.zeros((ZERO_ROWS, D_MODEL), F32)
        tail = lambda e: pltpu.make_async_copy(
            buf.at[1, pl.ds(0, ZERO_ROWS)],
            xb_ref.at[pl.ds(pl.multiple_of(jnp.maximum(end_ref[e] - ZERO_ROWS, 0), SEG_ALIGN), ZERO_ROWS)], sem_z)
        block = lambda b: pltpu.make_async_copy(
            buf.at[1, pl.ds(0, MOE_BM)], xb_ref.at[pl.ds(pl.multiple_of(b * MOE_BM, MOE_BM), MOE_BM)], sem_z)
        for e in range(N_EXPERTS):
            tail(e).start()
        lax.fori_loop(nu_ref[0], MOE_BLOCKS, lambda b, carry: (block(b).start(), carry)[1], 0)
        for e in range(N_EXPERTS):
            tail(e).wait()
        lax.fori_loop(nu_ref[0], MOE_BLOCKS, lambda b, carry: (block(b).wait(), carry)[1], 0)

    rows = _tile_rows(te_ref[...], tri_ref, off_ref)
    lane = lax.broadcasted_iota(jnp.int32, (TM, LANES), 1)
    packed = jnp.zeros((TM, LANES), F32)
    for k in range(TOP_K):
        packed = jnp.where(lane == k, rows[k], packed)
    rows_t = packed.T
    buf_row = lax.broadcasted_iota(jnp.int32, (TILE_BUF, TM), 0).astype(F32)
    pick = jnp.zeros((TILE_BUF, TM), F32)
    for k in range(TOP_K):
        pick = jnp.where(buf_row == rows_t[k:k + 1, :], 1.0, pick)
    buf[slot] = _dot(pick.astype(BF16), h2_ref[...])

    @pl.when(i > 0)
    def _():
        wait_chunks(i - 1, 1 - slot)

    _segment_chunks(seg_ref, len_ref, i, lambda size, b, x: chunk_copy(size, slot, b, x).start())

    @pl.when(i == N_TILES - 1)
    def _():
        wait_chunks(i, slot)


def _dispatch(seg_start, seg_len, n_chunk, pad_end, n_used, h2, te, seg_off, tri):
    row = lambda w: pl.BlockSpec((TM, w), lambda i, *_: (i, 0))
    grid_spec = pltpu.PrefetchScalarGridSpec(
        num_scalar_prefetch=5,
        grid=(N_TILES,),
        in_specs=[row(D_MODEL), row(TOP_K), pl.BlockSpec((1, 8, LANES), lambda i, *_: (i, 0, 0)),
                  pl.BlockSpec(tri.shape, lambda i, *_: (0, 0))],
        out_specs=pl.BlockSpec(memory_space=pl.ANY),
        scratch_shapes=[pltpu.VMEM((2, TILE_BUF, D_MODEL), F32),
                        pltpu.SemaphoreType.DMA((2,)), pltpu.SemaphoreType.DMA],
    )
    return pl.pallas_call(
        _dispatch_kernel,
        out_shape=jax.ShapeDtypeStruct((YB_ROWS, D_MODEL), F32),
        grid_spec=grid_spec,
        compiler_params=_params("arbitrary"),
        name="moe_dispatch",
    )(seg_start, seg_len, n_chunk, pad_end, n_used, h2, te, seg_off, tri)


def _expert_kernel(be_ref, nu_ref, x_ref, wgu_ref, bgu_ref, wdn_ref, bdn_ref, o_ref, wgu_s, wdn_s):
    i = pl.program_id(0)
    used = i < nu_ref[0]
    prev = be_ref[jnp.maximum(i - 1, 0)]
    fresh = jnp.logical_or(i == 0, be_ref[i] != prev)

    @pl.when(jnp.logical_and(used, fresh))
    def _():
        wgu_s[...] = wgu_ref[0, 0].astype(BF16)
        wdn_s[...] = wdn_ref[0, 0].astype(BF16)

    @pl.when(used)
    def _():
        hgu = _dot(x_ref[...].astype(BF16), wgu_s[...]) + bgu_ref[0, 0]
        gate = jnp.minimum(hgu[:, :D_FF], SWIGLU_LIMIT)
        up = jnp.clip(hgu[:, D_FF:], -SWIGLU_LIMIT, SWIGLU_LIMIT)
        act = (up + 1.0) * gate * jax.nn.sigmoid(SWIGLU_ALPHA * gate)
        o_ref[...] = _dot(act.astype(BF16), wdn_s[...]) + bdn_ref[0, 0]

    @pl.when(jnp.logical_not(used))
    def _():
        o_ref[...] = jnp.zeros_like(o_ref)


def _experts(layer, block_e, n_used, xb, w_gu, b_gu, w_dn, b_dn):
    grid_spec = pltpu.PrefetchScalarGridSpec(
        num_scalar_prefetch=2,
        grid=(MOE_BLOCKS,),
        in_specs=[pl.BlockSpec((MOE_BM, D_MODEL), lambda i, be, nu: (i, 0)),
                  pl.BlockSpec((1, 1, D_MODEL, 2 * D_FF), lambda i, be, nu: (layer, be[i], 0, 0)),
                  pl.BlockSpec((1, 1, 1, 2 * D_FF), lambda i, be, nu: (layer, be[i], 0, 0)),
                  pl.BlockSpec((1, 1, D_FF, D_MODEL), lambda i, be, nu: (layer, be[i], 0, 0)),
                  pl.BlockSpec((1, 1, 1, D_MODEL), lambda i, be, nu: (layer, be[i], 0, 0))],
        out_specs=pl.BlockSpec((MOE_BM, D_MODEL), lambda i, be, nu: (i, 0)),
        scratch_shapes=[pltpu.VMEM((D_MODEL, 2 * D_FF), BF16), pltpu.VMEM((D_FF, D_MODEL), BF16)],
    )
    return pl.pallas_call(
        _expert_kernel,
        out_shape=jax.ShapeDtypeStruct((YB_ROWS, D_MODEL), F32),
        grid_spec=grid_spec,
        compiler_params=_params("arbitrary"),
        name="experts",
    )(block_e, n_used, xb, w_gu, b_gu.reshape(DEPTH, N_EXPERTS, 1, 2 * D_FF), w_dn,
      b_dn.reshape(DEPTH, N_EXPERTS, 1, D_MODEL))


def _combine_kernel(final, seg_ref, len_ref, nchunk_ref, x1_ref, te_ref, gt_ref, off_ref, mod_ref, fg_ref,
                    tri_ref, yb_ref, *rest):
    if final:
        o_ctx_ref, o_lat_ref, buf, sems = rest
    else:
        o_ref, buf, sems = rest
    i = pl.program_id(0)
    slot = i % 2

    def chunk_copy(size, s, buf_row, yb_row):
        return pltpu.make_async_copy(yb_ref.at[pl.ds(yb_row, size)], buf.at[s, pl.ds(buf_row, size)], sems.at[s])

    def request(tile, s):
        _segment_chunks(seg_ref, len_ref, tile, lambda size, b, y: chunk_copy(size, s, b, y).start())

    @pl.when(i == 0)
    def _():
        buf[...] = jnp.zeros_like(buf)
        request(0, 0)

    @pl.when(i + 1 < N_TILES)
    def _():
        request(i + 1, 1 - slot)

    rows = _tile_rows(te_ref[...], tri_ref, off_ref)
    gt = gt_ref[...]
    buf_row = lax.broadcasted_iota(jnp.int32, (TM, TILE_BUF), 1).astype(F32)
    place = jnp.zeros((TM, TILE_BUF), F32)
    for k in range(TOP_K):
        place = jnp.where(buf_row == rows[k], gt[:, k:k + 1], place)

    _wait_chunks(nchunk_ref, i, lambda size: chunk_copy(size, slot, 0, 0))
    y = _dot(place.astype(BF16), buf[slot].astype(BF16))
    x2 = x1_ref[...] + mod_ref[0][5:6] * y
    if not final:
        o_ref[...] = x2
        return
    ms = jnp.mean(x2 * x2, axis=-1, keepdims=True)
    x2 = x2 * lax.rsqrt(ms + EPS) * fg_ref[...]

    @pl.when(i < CTX_TILES)
    def _():
        o_ctx_ref[...] = x2

    @pl.when(i >= CTX_TILES)
    def _():
        o_lat_ref[...] = x2


def _combine(seg_start, seg_len, n_chunk, x1, te, gates, seg_off, mod3, fg, tri, yb, final):
    row = lambda w: pl.BlockSpec((TM, w), lambda i, *_: (i, 0))
    full = lambda a: pl.BlockSpec(a.shape, lambda i, *_: (0,) * a.ndim)
    if final:
        out_shape = [jax.ShapeDtypeStruct((N_CTX, D_MODEL), F32), jax.ShapeDtypeStruct((N_LAT, D_MODEL), F32)]
        out_specs = [pl.BlockSpec((TM, D_MODEL), lambda i, *_: (jnp.minimum(i, CTX_TILES - 1), 0)),
                     pl.BlockSpec((TM, D_MODEL), lambda i, *_: (jnp.maximum(i - CTX_TILES, 0), 0))]
    else:
        out_shape = jax.ShapeDtypeStruct((N_TOK, D_MODEL), F32)
        out_specs = row(D_MODEL)
    grid_spec = pltpu.PrefetchScalarGridSpec(
        num_scalar_prefetch=3,
        grid=(N_TILES,),
        in_specs=[row(D_MODEL), row(TOP_K), row(TOP_K),
                  pl.BlockSpec((1, 8, LANES), lambda i, *_: (i, 0, 0)),
                  pl.BlockSpec((1, N_MOD, D_MODEL), lambda i, *_: (_mod_row(i), 0, 0)),
                  full(fg), full(tri),
                  pl.BlockSpec(memory_space=pl.ANY)],
        out_specs=out_specs,
        scratch_shapes=[pltpu.VMEM((2, TILE_BUF, D_MODEL), F32), pltpu.SemaphoreType.DMA((2,))],
    )
    return pl.pallas_call(
        functools.partial(_combine_kernel, final),
        out_shape=out_shape,
        grid_spec=grid_spec,
        compiler_params=_params("arbitrary"),
        name="moe_combine",
    )(seg_start, seg_len, n_chunk, x1, te, gates, seg_off, mod3, fg, tri, yb)


def _w1_index():
    idx = np.full((W1_COLS,), IN_WIDTH, np.int32)
    for g in range(2):
        for kv in range(2):
            idx[g * 128 + kv * 64:g * 128 + kv * 64 + 64] = np.arange(64) + kv * 128 + g * 64
    idx[256:512] = np.arange(256) + IN_OFF['a_k']
    idx[512:704] = np.arange(192) + IN_OFF['b_cq']
    idx[768:896] = np.arange(128) + IN_OFF['b_ckv']
    idx[896:928] = np.arange(32) + IN_OFF['b_kr']
    idx[1024:1792] = np.arange(768) + IN_OFF['c_q']
    idx[1792:2048] = np.arange(256) + IN_OFF['d_z']
    idx[2048:2560] = np.arange(512) + IN_OFF['d_xbc']
    idx[2560:2568] = np.arange(8) + IN_OFF['d_dtf']
    return idx


def _wout_a_index():
    idx = np.zeros((256,), np.int32)
    for g in range(2):
        for kv in range(2):
            idx[g * 128 + kv * 64:g * 128 + kv * 64 + 64] = np.arange(64) + kv * 128 + g * 64
    return idx


def _rope_tables():
    t = np.arange(DEC_SEQ)
    pos = ((t // GRID_W).astype(np.float32), (t % GRID_W).astype(np.float32))

    def unit(rot_dim):
        quarter, half = rot_dim // 4, rot_dim // 2
        inv = ROPE_THETA ** (-np.arange(0, half, 2, dtype=np.float32) / half)
        cos = np.zeros((DEC_SEQ, rot_dim), np.float32)
        sin = np.zeros((DEC_SEQ, rot_dim), np.float32)
        for seg in range(4):
            ang = pos[seg // 2][:, None] * inv[None, :].astype(np.float32)
            cos[:, seg * quarter:(seg + 1) * quarter] = np.cos(ang)
            sin[:, seg * quarter:(seg + 1) * quarter] = np.sin(ang) * (-1.0 if seg % 2 == 0 else 1.0)
        return cos, sin

    log2e = math.log2(math.e)
    specs = [
        (HEAD_DIM, (0, 64), (), HEAD_DIM ** -0.5 * log2e),
        (HEAD_DIM, (0, 64), (), 1.0),
        (MLA_ROPE, (MLA_NOPE,), (0, MLA_NOPE), (MLA_NOPE + MLA_ROPE) ** -0.5 * log2e),
        (MLA_ROPE, (0,), (), 1.0),
        (DIFF_D, (0, 32, 64, 96), (), DIFF_D ** -0.5 * log2e),
        (DIFF_D, (0, 32, 64, 96), (), 1.0),
    ]
    lat_cols, ident_cols = [], []
    for rot_dim, starts, passthrough, scale in specs:
        ucos, usin = unit(rot_dim)
        cos = np.zeros((DEC_SEQ, LANES), np.float32)
        sin = np.zeros((DEC_SEQ, LANES), np.float32)
        ident = np.zeros((1, LANES), np.float32)
        if passthrough:
            cos[:, passthrough[0]:passthrough[1]] = 1.0
            ident[:, passthrough[0]:passthrough[1]] = 1.0
        for s in starts:
            cos[:, s:s + rot_dim] = ucos
            sin[:, s:s + rot_dim] = usin
            ident[:, s:s + rot_dim] = 1.0
        lat_cols += [cos * scale, sin * scale]
        ident_cols += [ident * scale, np.zeros((1, LANES), np.float32)]
    lat = np.concatenate(lat_cols, axis=1)
    ident_blk = np.broadcast_to(np.concatenate(ident_cols, axis=1), (PROJ_TM, N_TAB * LANES))
    return np.concatenate([ident_blk, lat], axis=0).astype(np.float32)


def _block_diag_ones(n, blk):
    r = np.arange(n)
    return (r[:, None] // blk == r[None, :] // blk).astype(np.float32)


def _chunk_tables(seg_len):
    counts = [jnp.sum(seg_len // CHUNKS[0], axis=1)]
    counts += [jnp.sum(seg_len % (2 * size) // size, axis=1) for size in CHUNKS[1:]]
    return jnp.stack(counts, axis=1).reshape(-1).astype(jnp.int32)


def _strict_lower_ones(n):
    r = np.arange(n)
    return (r[None, :] < r[:, None]).astype(np.float32)


def kernel(x_prompt, x_sample, cache_gqa_k, cache_gqa_v, cache_mla_ckv, cache_mla_krope, cache_diff_k, cache_diff_v, state_ssd_fwd, state_ssd_bwd, c, c_ctx, norm1_g, norm2_g, w_ada, b_ada, w_in, w_out, gqa_qn_g, gqa_kn_g, mla_qa_g, mla_wqb, mla_kva_g, mla_wkvb, diff_lq1, diff_lk1, diff_lq2, diff_lk2, diff_subln_g, ssd_conv_w, ssd_conv_b, ssd_a_log_f, ssd_a_log_b, ssd_dt_bias_f, ssd_dt_bias_b, ssd_d, ssd_norm_g, router_w, router_b, moe_w_gu, moe_b_gu, moe_w_dn, moe_b_dn, final_g):
    tabs = jnp.asarray(_rope_tables())
    bd = jnp.asarray(_block_diag_ones(256, HEAD_DIM), BF16)
    tri = jnp.asarray(_strict_lower_ones(TM), BF16)
    w1_idx = _w1_index()
    woa_idx = _wout_a_index()

    cvec = jnp.zeros((MOD_ROWS, D_MODEL), F32).at[0].set(c_ctx).at[1:1 + DEC_BATCH].set(c)
    mod = _modulation(cvec, w_ada, b_ada).reshape(DEPTH, MOD_ROWS, N_MOD, D_MODEL)

    x = jnp.concatenate([x_prompt.reshape(N_CTX, D_MODEL), x_sample.reshape(N_LAT, D_MODEL)], axis=0)
    new_ctx = []
    for l in range(DEPTH):
        mod3 = mod[l]
        w1 = jnp.concatenate([w_in[l], jnp.zeros((D_MODEL, 1), F32)], axis=1)[:, w1_idx].astype(BF16)
        wqb = mla_wqb[l].reshape(Q_LORA, B_HEADS, MLA_NOPE + MLA_ROPE)
        wqb = jnp.pad(wqb, ((0, 256 - Q_LORA), (0, 0), (0, LANES - MLA_NOPE - MLA_ROPE)))
        wqb = wqb.reshape(256, B_HEADS * LANES).astype(BF16)
        wkvb = mla_wkvb[l].reshape(KV_LORA, B_HEADS, MLA_NOPE + MLA_V)
        wk_nope = jnp.pad(wkvb[:, :, :MLA_NOPE], ((0, 0), (0, 0), (0, LANES - MLA_NOPE)))
        eye_r = jnp.zeros((LANES, B_HEADS, LANES), F32)
        eye_r = eye_r.at[jnp.arange(MLA_ROPE), :, MLA_NOPE + jnp.arange(MLA_ROPE)].set(1.0)
        wk = jnp.concatenate([wk_nope, eye_r], axis=0).reshape(256, B_HEADS * LANES).astype(BF16)
        wv = wkvb[:, :, MLA_NOPE:].reshape(KV_LORA, B_HEADS * MLA_V).astype(BF16)
        wo = jnp.concatenate([w_out[l][woa_idx], w_out[l][256:]], axis=0).astype(BF16)
        gq = jnp.tile(gqa_qn_g[l], 4)[None, :]
        gk = jnp.tile(gqa_kn_g[l], 2)[None, :]
        gqa = jnp.pad(mla_qa_g[l], (0, 256 - Q_LORA))[None, :]
        gkva = mla_kva_g[l][None, :]
        rw = jnp.pad(router_w[l], ((0, 0), (0, LANES - N_EXPERTS)))
        rwh = rw.astype(BF16)
        rwl = (rw - rwh.astype(F32)).astype(BF16)
        rb = jnp.pad(router_b[l], (0, LANES - N_EXPERTS), constant_values=NEG_BIG)[None, :]

        qa, kva, qb, ck, qc, kc, vc, z, xbc, dt = _in_proj(
            x, mod3, norm1_g[l][None, :], w1, bd, gq, gk, gqa, gkva, wqb, tabs)

        past_kva = jnp.concatenate([cache_gqa_k[:, l].reshape(DEC_BATCH, PAST_LEN, 128),
                                    cache_gqa_v[:, l].reshape(DEC_BATCH, PAST_LEN, 128)], axis=-1)
        past_ck = jnp.concatenate([cache_mla_ckv[:, l], cache_mla_krope[:, l],
                                   jnp.zeros((DEC_BATCH, PAST_LEN, LANES - MLA_ROPE), F32)], axis=-1)
        past_kc = cache_diff_k[:, l].reshape(DEC_BATCH, PAST_LEN, 256)
        past_vc = cache_diff_v[:, l].reshape(DEC_BATCH, PAST_LEN, 256)
        lams = [a[l][None, :] for a in (diff_lq1, diff_lk1, diff_lq2, diff_lk2)]
        gsub = jnp.tile(diff_subln_g[l], 2)[None, :]
        lam_init = 0.8 - 0.6 * math.exp(-0.3 * l)
        oa, ob, oc = _attention_mixers(qa, kva, qb, ck, qc, kc, vc, past_kva, past_ck, past_kc, past_vc,
                                       wk, wv, lams, gsub, lam_init)

        pad8 = lambda f, b: jnp.pad(jnp.concatenate([f, b]), (0, LANES - 2 * D_HEADS))[None, :]
        ssd_consts = (ssd_conv_w[l], ssd_conv_b[l][None, :], pad8(ssd_dt_bias_f[l], ssd_dt_bias_b[l]),
                      pad8(ssd_a_log_f[l], ssd_a_log_b[l]), jnp.repeat(ssd_d[l], D_HEADDIM)[None, :],
                      ssd_norm_g[l][None, :])
        zeros_st = jnp.zeros((BATCH, D_HEADS, D_STATE, D_HEADDIM), F32)
        od, hf_c, hb_c = _ssd(False, z, xbc, dt, ssd_consts, zeros_st, zeros_st, jnp.zeros((N_TOK, D_INNER), F32))
        od, _, _ = _ssd(True, z, xbc, dt, ssd_consts, jnp.swapaxes(state_ssd_fwd[:, l], -1, -2),
                        jnp.swapaxes(state_ssd_bwd[:, l], -1, -2), od)

        x1, h2, te, gates, tile_cnt = _out_proj(x, oa, ob, oc, od, mod3, wo, norm2_g[l][None, :], rwh, rwl, rb)

        seg_cnt = tile_cnt[:, 0, :N_EXPERTS]
        seg_len = (seg_cnt + SEG_ALIGN - 1) // SEG_ALIGN * SEG_ALIGN
        region = (jnp.sum(seg_len, axis=0) + MOE_BM - 1) // MOE_BM * MOE_BM
        pad_end = jnp.cumsum(region).astype(jnp.int32)
        pad_start = pad_end - region
        seg_start = (pad_start[None, :] + jnp.cumsum(seg_len, axis=0) - seg_len).astype(jnp.int32)
        n_used = (pad_end[-1] // MOE_BM).astype(jnp.int32).reshape(1)
        blk_start = jnp.arange(MOE_BLOCKS, dtype=jnp.int32) * MOE_BM
        block_e = jnp.minimum(jnp.sum((pad_end[None, :] <= blk_start[:, None]).astype(jnp.int32), axis=1),
                              N_EXPERTS - 1).astype(jnp.int32)
        n_chunk = _chunk_tables(seg_len)
        seg_off = jnp.cumsum(seg_len, axis=1) - seg_len
        seg_off = jnp.broadcast_to(jnp.pad(seg_off, ((0, 0), (0, LANES - N_EXPERTS)))[:, None, :],
                                   (N_TILES, 8, LANES)).astype(jnp.int32)
        seg_start, seg_len = seg_start.reshape(-1), seg_len.reshape(-1).astype(jnp.int32)

        xb = _dispatch(seg_start, seg_len, n_chunk, pad_end, n_used, h2, te, seg_off, tri)
        yb = _experts(l, block_e, n_used, xb, moe_w_gu, moe_b_gu, moe_w_dn, moe_b_dn)
        x = _combine(seg_start, seg_len, n_chunk, x1, te, gates, seg_off, mod3, final_g[None, :], tri, yb,
                     l == DEPTH - 1)

        kva_c, ck_c = kva[:N_CTX], ck[:N_CTX]
        new_ctx.append((kva_c[:, 0:128].reshape(BATCH, SEQ, A_KV_HEADS, HEAD_DIM),
                        kva_c[:, 128:256].reshape(BATCH, SEQ, A_KV_HEADS, HEAD_DIM),
                        ck_c[:, 0:KV_LORA].reshape(BATCH, SEQ, KV_LORA),
                        ck_c[:, KV_LORA:KV_LORA + MLA_ROPE].reshape(BATCH, SEQ, MLA_ROPE),
                        kc[:N_CTX].reshape(BATCH, SEQ, C_HEADS, 2 * DIFF_D),
                        vc[:N_CTX].reshape(BATCH, SEQ, C_HEADS, DIFF_V),
                        jnp.swapaxes(hf_c, -1, -2), jnp.swapaxes(hb_c, -1, -2)))

    y_prompt = x[0].reshape(BATCH, SEQ, D_MODEL)
    y_sample = x[1].reshape(DEC_BATCH, DEC_SEQ, D_MODEL)
    caches = [jnp.stack([cl[i] for cl in new_ctx], axis=1) for i in range(8)]
    return (y_prompt, y_sample, *caches)
```

```python
import functools
import math

import numpy as np
import jax
import jax.numpy as jnp
from jax import lax
from jax.experimental import pallas as pl
from jax.experimental.pallas import tpu as pltpu

F32 = jnp.float32
BF16 = jnp.bfloat16

D_MODEL = 1024
BATCH = 16
SEQ = 256
DEPTH = 2
DEC_BATCH = 8
DEC_SEQ = 2048
PAST_LEN = 256
GRID_W = 64
ROPE_THETA = 10000.0
EPS = 1e-6
HEAD_DIM = 64
A_HEADS = 4
A_KV_HEADS = 2
B_HEADS = 4
MLA_NOPE = 64
MLA_ROPE = 32
MLA_V = 64
Q_LORA = 192
KV_LORA = 128
C_HEADS = 4
DIFF_D = 32
DIFF_V = 64
SUBLN_EPS = 1e-5
D_HEADS = 4
D_HEADDIM = 64
D_INNER = 256
D_GROUPS = 2
D_STATE = 64
CONV_DIM = 512
SSD_CHUNK = 128
N_EXPERTS = 32
TOP_K = 4
D_FF = 1024
SWIGLU_ALPHA = 1.702
SWIGLU_LIMIT = 7.0
N_MOD = 6

N_CTX = BATCH * SEQ
N_LAT = DEC_BATCH * DEC_SEQ
N_TOK = N_CTX + N_LAT

LANES = 128
TM = 256
N_TILES = N_TOK // TM
CTX_TILES = N_CTX // TM
LAT_TQ = 1024
PROJ_TM = 512
MOD_ROWS = 16
MOE_BM = 512
N_PAIRS = N_TOK * TOP_K
SEG_ALIGN = 8
CHUNKS = (32, 16, 8)
CHUNK_SHIFT = 5
MOE_BLOCKS = -(-(N_PAIRS + N_TILES * N_EXPERTS * (SEG_ALIGN - 1) + N_EXPERTS * (MOE_BM - 1)) // MOE_BM)
YB_ROWS = MOE_BLOCKS * MOE_BM
TILE_BUF = -(-(TM * TOP_K + N_EXPERTS * (SEG_ALIGN - 1)) // LANES) * LANES
ZERO_ROWS = MOE_BM
VMEM_LIMIT = 56 * 1024 * 1024
NEG_BIG = -1e30

W1_COLS = 2688
IN_OFF = dict(a_q=0, a_k=256, a_v=384, b_cq=512, b_ckv=704, b_kr=832, c_q=864, c_k=1120,
              c_v=1376, d_z=1632, d_xbc=1888, d_dtf=2400, d_dtb=2404)
IN_WIDTH = 2408
N_TAB = 12


def _mod_row(i, tile=TM):
    return jnp.where(i < N_CTX // tile, 0, 1 + (i - N_CTX // tile) // (DEC_SEQ // tile))


def _tab_block(i, tile):
    return jnp.where(i < N_CTX // tile, 0, 1 + (i - N_CTX // tile) % (DEC_SEQ // tile))


def _pair_specs(width, tile):
    n_ctx = N_CTX // tile
    return [pl.BlockSpec((tile, width), lambda i, *_: (jnp.minimum(i, n_ctx - 1), 0)),
            pl.BlockSpec((tile, width), lambda i, *_: (jnp.maximum(i - n_ctx, 0), 0))]


def _pick(ctx_ref, lat_ref, tile):
    return jnp.where(pl.program_id(0) < N_CTX // tile, ctx_ref[...], lat_ref[...])


def _dot(a, b):
    return jnp.dot(a, b, preferred_element_type=F32)


def _dot_nt(a, b):
    return lax.dot_general(a, b, (((1,), (1,)), ((), ())), preferred_element_type=F32)


def _dot_split(x, m):
    hi = x.astype(BF16)
    lo = (x - hi.astype(F32)).astype(BF16)
    return _dot(hi, m) + _dot(lo, m)


def _dot_split_left(m, x):
    hi = x.astype(BF16)
    lo = (x - hi.astype(F32)).astype(BF16)
    return _dot(m, hi) + _dot(m, lo)


def _silu(x):
    return x * jax.nn.sigmoid(x)


def _params(*semantics):
    return pltpu.CompilerParams(dimension_semantics=semantics, vmem_limit_bytes=VMEM_LIMIT)


MOD_TN = 1536


def _mod_kernel(c_ref, w_ref, b_ref, o_ref):
    c = c_ref[...]
    s = _silu(c).astype(BF16)
    o_ref[0] = _dot(s, w_ref[0].astype(BF16)) + b_ref[0]


def _modulation(cvec, w_ada, b_ada):
    n = N_MOD * D_MODEL
    return pl.pallas_call(
        _mod_kernel,
        out_shape=jax.ShapeDtypeStruct((DEPTH, MOD_ROWS, n), F32),
        grid=(DEPTH, n // MOD_TN),
        in_specs=[pl.BlockSpec((MOD_ROWS, D_MODEL), lambda l, j: (0, 0)),
                  pl.BlockSpec((1, D_MODEL, MOD_TN), lambda l, j: (l, 0, j)),
                  pl.BlockSpec((1, 1, MOD_TN), lambda l, j: (l, 0, j))],
        out_specs=pl.BlockSpec((1, MOD_ROWS, MOD_TN), lambda l, j: (l, 0, j)),
        compiler_params=_params("parallel", "parallel"),
        name="adaln_mod",
    )(cvec, w_ada, b_ada.reshape(DEPTH, 1, n))


def _rope(x, cos, sin, quarter):
    lane = lax.broadcasted_iota(jnp.int32, (x.shape[0], LANES), 1)
    first = (lane // quarter) % 2 == 0
    outs = []
    for t in range(x.shape[1] // LANES):
        xt = x[:, t * LANES:(t + 1) * LANES]
        partner = jnp.where(first, pltpu.roll(xt, LANES - quarter, 1), pltpu.roll(xt, quarter, 1))
        outs.append(xt * cos + partner * sin)
    return outs


def _in_proj_kernel(xc_ref, xl_ref, mod_ref, g1_ref, w1_ref, bd_ref, gq_ref, gk_ref, gqa_ref, gkva_ref,
                    wqb_ref, tab_ref,
                    qa_ref, kva_ref, qb_ref, ck_ref, qc_ref, kc_ref, vc_ref, z_ref, xbc_ref, dt_ref):
    x = _pick(xc_ref, xl_ref, PROJ_TM)
    m = mod_ref[0]
    sh1, sc1 = m[0:1], m[1:2]
    ms = jnp.mean(x * x, axis=-1, keepdims=True)
    h = (x * lax.rsqrt(ms + EPS) * g1_ref[...]) * (1.0 + sc1) + sh1
    u = _dot(h.astype(BF16), w1_ref[...])

    def tab(k):
        return tab_ref[:, k * LANES:(k + 1) * LANES]

    bd = bd_ref[...]

    def head_norm(v, gain):
        w = v.shape[1]
        ss = _dot_split(v * v, bd[:w, :w])
        return v * lax.rsqrt(ss * (1.0 / HEAD_DIM) + EPS) * gain

    qa = _rope(head_norm(u[:, 0:256], gq_ref[...]), tab(0), tab(1), HEAD_DIM // 4)
    for t in range(2):
        qa_ref[:, t * LANES:(t + 1) * LANES] = qa[t].astype(BF16)
    ka = _rope(head_norm(u[:, 256:384], gk_ref[...]), tab(2), tab(3), HEAD_DIM // 4)
    kva_ref[:, 0:128] = ka[0]
    kva_ref[:, 128:256] = u[:, 384:512]

    cq = u[:, 512:768]
    msq = jnp.sum(cq * cq, axis=-1, keepdims=True) * (1.0 / Q_LORA)
    yq = cq * lax.rsqrt(msq + EPS) * gqa_ref[...]
    qb = _rope(_dot(yq.astype(BF16), wqb_ref[...]), tab(4), tab(5), MLA_ROPE // 4)
    for t in range(4):
        qb_ref[:, t * LANES:(t + 1) * LANES] = qb[t].astype(BF16)
    ckv = u[:, 768:896]
    msk = jnp.mean(ckv * ckv, axis=-1, keepdims=True)
    ck_ref[:, 0:128] = ckv * lax.rsqrt(msk + EPS) * gkva_ref[...]
    ck_ref[:, 128:256] = _rope(u[:, 896:1024], tab(6), tab(7), MLA_ROPE // 4)[0]

    qc = _rope(u[:, 1024:1280], tab(8), tab(9), DIFF_D // 4)
    kc = _rope(u[:, 1280:1536], tab(10), tab(11), DIFF_D // 4)
    for t in range(2):
        qc_ref[:, t * LANES:(t + 1) * LANES] = qc[t].astype(BF16)
        kc_ref[:, t * LANES:(t + 1) * LANES] = kc[t]
    vc_ref[...] = u[:, 1536:1792]

    z_ref[...] = u[:, 1792:2048]
    xbc_ref[...] = u[:, 2048:2560]
    dt_ref[...] = u[:, 2560:2688]


def _in_proj(x, mod3, g1, w1, bd, gq, gk, gqa, gkva, wqb, tabs):
    row = lambda w: pl.BlockSpec((PROJ_TM, w), lambda i: (i, 0))
    full = lambda a: pl.BlockSpec(a.shape, lambda i: (0,) * a.ndim)
    outs = [(256, BF16), (256, F32), (512, BF16), (256, F32), (256, BF16), (256, F32), (256, F32),
            (256, F32), (512, F32), (128, F32)]
    return pl.pallas_call(
        _in_proj_kernel,
        out_shape=[jax.ShapeDtypeStruct((N_TOK, w), d) for w, d in outs],
        grid=(N_TOK // PROJ_TM,),
        in_specs=_pair_specs(D_MODEL, PROJ_TM)
                 + [pl.BlockSpec((1, N_MOD, D_MODEL), lambda i: (_mod_row(i, PROJ_TM), 0, 0)),
                    full(g1), full(w1), full(bd), full(gq), full(gk), full(gqa), full(gkva), full(wqb),
                    pl.BlockSpec((PROJ_TM, N_TAB * LANES), lambda i: (_tab_block(i, PROJ_TM), 0))],
        out_specs=[row(w) for w, _ in outs],
        compiler_params=_params("parallel"),
        name="in_proj",
    )(*x, mod3, g1, w1, bd, gq, gk, gqa, gkva, wqb, tabs)


def _attend(q, k, v_ones):
    s = _dot_nt(q, k)
    e = jnp.exp2(s - jnp.max(s, axis=-1, keepdims=True))
    out = _dot(e.astype(BF16), v_ones)
    return out[:, :LANES] / out[:, LANES:]


def _with_ones(v):
    return jnp.concatenate([v, jnp.ones_like(v)], axis=-1)


def _half_mask(rows):
    lane = lax.broadcasted_iota(jnp.int32, (rows, LANES), 1)
    return lane < (LANES // 2)


def _keys(past_ref, new_ref, lo, hi):
    new = new_ref[:, lo:hi].astype(BF16)
    if past_ref is None:
        return new
    return jnp.concatenate([past_ref[0, :, lo:hi].astype(BF16), new], axis=0)


def _attn_a_kernel(latent, *refs):
    if latent:
        q_ref, kv_ref, past_ref, o_ref = refs
    else:
        (q_ref, kv_ref, o_ref), past_ref = refs, None
    q = q_ref[...]
    k = _keys(past_ref, kv_ref, 0, 128)
    v = _with_ones(_keys(past_ref, kv_ref, 128, 256))
    lo = _half_mask(q.shape[0])
    for g in range(2):
        qt = q[:, g * LANES:(g + 1) * LANES].astype(F32)
        res = []
        for half in range(2):
            qm = jnp.where(lo, qt, 0.0) if half == 0 else jnp.where(lo, 0.0, qt)
            res.append(_attend(qm.astype(BF16), k, v))
        o_ref[:, g * LANES:(g + 1) * LANES] = jnp.where(lo, res[0], res[1])


def _attn_b_kernel(latent, *refs):
    if latent:
        q_ref, ck_ref, past_ref, wk_ref, wv_ref, o_ref, k_s, v_s = refs
    else:
        (q_ref, ck_ref, wk_ref, wv_ref, o_ref, k_s, v_s), past_ref = refs, None

    @pl.when(pl.program_id(1) == 0)
    def _():
        ck = _keys(past_ref, ck_ref, 0, 256)
        k_s[...] = _dot(ck, wk_ref[...]).astype(BF16)
        v = _dot(ck[:, 0:128], wv_ref[...]).astype(BF16)
        for j in range(2):
            v_s[:, 2 * j * LANES:2 * (j + 1) * LANES] = _with_ones(v[:, j * LANES:(j + 1) * LANES])

    q = q_ref[...]
    lo = _half_mask(q.shape[0])
    for j in range(2):
        v = v_s[:, 2 * j * LANES:2 * (j + 1) * LANES]
        res = []
        for half in range(2):
            h = 2 * j + half
            res.append(_attend(q[:, h * LANES:(h + 1) * LANES], k_s[:, h * LANES:(h + 1) * LANES], v))
        o_ref[:, j * LANES:(j + 1) * LANES] = jnp.where(lo, res[0], res[1])


def _attn_c_kernel(lam_init, latent, *refs):
    if latent:
        q_ref, k_ref, v_ref, pk_ref, pv_ref, lq1_ref, lk1_ref, lq2_ref, lk2_ref, g_ref, o_ref = refs
    else:
        (q_ref, k_ref, v_ref, lq1_ref, lk1_ref, lq2_ref, lk2_ref, g_ref, o_ref), pk_ref, pv_ref = refs, None, None
    lam = (jnp.exp(jnp.sum(lq1_ref[...] * lk1_ref[...], axis=-1, keepdims=True))
           - jnp.exp(jnp.sum(lq2_ref[...] * lk2_ref[...], axis=-1, keepdims=True)) + lam_init)
    q = q_ref[...]
    rows = q.shape[0]
    lane = lax.broadcasted_iota(jnp.int32, (rows, LANES), 1)
    lo = lane < (LANES // 2)
    for j in range(2):
        qt = q[:, j * LANES:(j + 1) * LANES].astype(F32)
        k = _keys(pk_ref, k_ref, j * LANES, (j + 1) * LANES)
        v = _with_ones(_keys(pv_ref, v_ref, j * LANES, (j + 1) * LANES))
        res = []
        for half in range(2):
            parts = []
            for t in range(2):
                quarter = 2 * half + t
                qm = jnp.where(lane // (LANES // 4) == quarter, qt, 0.0)
                parts.append(_attend(qm.astype(BF16), k, v))
            res.append(parts[0] - lam * parts[1])
        o = jnp.where(lo, res[0], res[1])
        o2 = o * o
        ss_lo = jnp.sum(jnp.where(lo, o2, 0.0), axis=-1, keepdims=True)
        ss_hi = jnp.sum(jnp.where(lo, 0.0, o2), axis=-1, keepdims=True)
        ss = jnp.where(lo, ss_lo, ss_hi) * (1.0 / DIFF_V)
        o_ref[:, j * LANES:(j + 1) * LANES] = (o * lax.rsqrt(ss + SUBLN_EPS) * g_ref[...]) * (1.0 - lam_init)


def _seq_call(body, name, latent, q, news, pasts, consts, scratch=(), q_semantics="parallel"):
    const_specs = [pl.BlockSpec(a.shape, lambda b, i, n=a.ndim: (0,) * n) for a in consts]
    if latent:
        tq, rows = LAT_TQ, N_LAT
        tile = lambda b, i: (N_CTX // LAT_TQ + b * (DEC_SEQ // LAT_TQ) + i, 0)
        out_tile = lambda b, i: (b * (DEC_SEQ // LAT_TQ) + i, 0)
        grid = (DEC_BATCH, DEC_SEQ // LAT_TQ)
        new_specs = [pl.BlockSpec((DEC_SEQ, a.shape[1]), lambda b, i: (N_CTX // DEC_SEQ + b, 0)) for a in news]
        past_specs = [pl.BlockSpec((1, PAST_LEN, a.shape[2]), lambda b, i: (b, 0, 0)) for a in pasts]
    else:
        tq, rows = SEQ, N_CTX
        tile = out_tile = lambda b, i: (b, 0)
        grid = (BATCH, 1)
        new_specs = [pl.BlockSpec((SEQ, a.shape[1]), tile) for a in news]
        past_specs, pasts = [], ()
    return pl.pallas_call(
        functools.partial(body, latent),
        out_shape=jax.ShapeDtypeStruct((rows, 256), F32),
        grid=grid,
        in_specs=[pl.BlockSpec((tq, q.shape[1]), tile)] + new_specs + past_specs + const_specs,
        out_specs=pl.BlockSpec((tq, 256), out_tile),
        scratch_shapes=list(scratch),
        compiler_params=_params("parallel", q_semantics),
        name=name + ("_lat" if latent else "_ctx"),
    )(q, *news, *pasts, *consts)


def _attention_mixers(qa, kva, qb, ck, qc, kc, vc, past_kva, past_ck, past_kc, past_vc, wk, wv,
                      lams, gsub, lam_init):
    def both(body, name, q, news, pasts, consts, scratch_fn=None, q_semantics="parallel"):
        sc = (lambda lk: ()) if scratch_fn is None else scratch_fn
        return (_seq_call(body, name, False, q, news, (), consts, sc(SEQ), q_semantics),
                _seq_call(body, name, True, q, news, pasts, consts, sc(PAST_LEN + DEC_SEQ), q_semantics))

    oa = both(_attn_a_kernel, "attn_gqa", qa, [kva], [past_kva], [])
    ob = both(_attn_b_kernel, "attn_mla", qb, [ck], [past_ck], [wk, wv],
              lambda lk: (pltpu.VMEM((lk, 512), BF16), pltpu.VMEM((lk, 512), BF16)), "arbitrary")
    oc = both(functools.partial(_attn_c_kernel, lam_init), "attn_diff", qc, [kc, vc],
              [past_kc, past_vc], [*lams, gsub])
    return oa, ob, oc


Q = SSD_CHUNK
SSD_UNROLL = 2


def _ssd_kernel(z_ref, xbc_ref, dt_ref, cw_ref, cb_ref, dtb_ref, alog_ref, dvec_ref, ng_ref, h0f_ref, h0b_ref,
                out_ref, hf_ref, hb_ref, act_s, cum_s, dtv_s, y_s):
    n_seq = hf_ref.shape[0]
    seq = z_ref.shape[0] // n_seq
    nc = seq // Q
    row = lax.broadcasted_iota(jnp.int32, (Q, Q), 0)
    col = lax.broadcasted_iota(jnp.int32, (Q, Q), 1)
    lower = row >= col
    upper = row <= col
    tril = jnp.where(lower, 1.0, 0.0).astype(BF16)
    triu = jnp.where(upper, 1.0, 0.0).astype(BF16)
    rowc = lax.broadcasted_iota(jnp.int32, (Q, CONV_DIM), 0)
    lane = lax.broadcasted_iota(jnp.int32, (Q, LANES), 1)
    a_neg = -jnp.exp(alog_ref[...])
    cw = cw_ref[...]
    hf_ref[...] = h0f_ref[...]
    hb_ref[...] = h0b_ref[...]

    def fwd_seq(c, s):
        off = s * seq
        base = pl.multiple_of(off + c * Q, Q)
        x0 = xbc_ref[pl.ds(base, Q), :]
        prev = xbc_ref[pl.ds(pl.multiple_of(off + jnp.maximum(c * Q - 8, 0), 8), 8), :][7:8, :]
        nxt = xbc_ref[pl.ds(pl.multiple_of(off + jnp.minimum(c * Q + Q, seq - 8), 8), 8), :][0:1, :]
        prev = jnp.where(c > 0, prev, 0.0)
        nxt = jnp.where(c < nc - 1, nxt, 0.0)
        xm1 = jnp.where(rowc == 0, prev, pltpu.roll(x0, 1, 0))
        xp1 = jnp.where(rowc == Q - 1, nxt, pltpu.roll(x0, Q - 1, 0))
        act = _silu(xm1 * cw[0:1] + x0 * cw[1:2] + xp1 * cw[2:3] + cb_ref[...])
        act_s[pl.ds(base, Q), :] = act
        xs = act[:, 0:256]
        bm = act[:, 256:384]
        cm = act[:, 384:512]

        dtr = dt_ref[pl.ds(base, Q), :] + dtb_ref[...]
        dtv = jnp.maximum(dtr, 0.0) + jnp.log1p(jnp.exp(-jnp.abs(dtr)))
        dta = dtv * a_neg
        cum = jnp.where(lane < D_HEADS, _dot_split_left(tril, dta), _dot_split_left(triu, dta))
        cum_s[pl.ds(base, Q), :] = cum
        dtv_s[pl.ds(base, Q), :] = dtv
        cum_t = cum.T
        dtv_t = dtv.T
        bm_t = bm.T
        ys = []
        for h in range(D_HEADS):
            g = h // (D_HEADS // D_GROUPS)
            cg = cm[:, g * D_STATE:(g + 1) * D_STATE].astype(BF16)
            cb_mat = _dot_nt(cg, bm[:, g * D_STATE:(g + 1) * D_STATE].astype(BF16))
            cf = cum[:, h:h + 1]
            cb = cum[:, D_HEADS + h:D_HEADS + h + 1]
            l_f = jnp.exp(jnp.where(lower, cf - cum_t[h:h + 1, :], NEG_BIG))
            l_b = jnp.exp(jnp.where(upper, cb - cum_t[D_HEADS + h:D_HEADS + h + 1, :], NEG_BIG))
            mix = cb_mat * (l_f * dtv_t[h:h + 1, :] + l_b * dtv_t[D_HEADS + h:D_HEADS + h + 1, :])
            xh = xs[:, h * D_HEADDIM:(h + 1) * D_HEADDIM]
            y = _dot(mix.astype(BF16), xh.astype(BF16))
            state = hf_ref[s, h]
            y = y + _dot(cg, state.astype(BF16)) * jnp.exp(cf)
            y = y + dvec_ref[:, h * D_HEADDIM:(h + 1) * D_HEADDIM] * xh
            ys.append(y)
            last = cum[Q - 1:Q, h:h + 1]
            wgt = jnp.exp(last - cf) * dtv[:, h:h + 1]
            st = _dot(bm_t[g * D_STATE:(g + 1) * D_STATE, :].astype(BF16), (xh * wgt).astype(BF16))
            hf_ref[s, h] = state * jnp.exp(last) + st
        y_s[pl.ds(base, Q), :] = jnp.concatenate(ys, axis=-1)

    def fwd_chunk(c, carry):
        for s in range(n_seq):
            fwd_seq(c, s)
        return carry

    lax.fori_loop(0, nc, fwd_chunk, 0, unroll=SSD_UNROLL)

    def bwd_seq(c, s):
        base = pl.multiple_of(s * seq + c * Q, Q)
        act = act_s[pl.ds(base, Q), :]
        cum = cum_s[pl.ds(base, Q), :]
        dtv = dtv_s[pl.ds(base, Q), :]
        xs = act[:, 0:256]
        bm_t = act[:, 256:384].T
        cm = act[:, 384:512]
        ys = []
        for h in range(D_HEADS):
            g = h // (D_HEADS // D_GROUPS)
            cg = cm[:, g * D_STATE:(g + 1) * D_STATE].astype(BF16)
            cb = cum[:, D_HEADS + h:D_HEADS + h + 1]
            xh = xs[:, h * D_HEADDIM:(h + 1) * D_HEADDIM]
            state = hb_ref[s, h]
            ys.append(_dot(cg, state.astype(BF16)) * jnp.exp(cb))
            first = cum[0:1, D_HEADS + h:D_HEADS + h + 1]
            wgt = jnp.exp(first - cb) * dtv[:, D_HEADS + h:D_HEADS + h + 1]
            st = _dot(bm_t[g * D_STATE:(g + 1) * D_STATE, :].astype(BF16), (xh * wgt).astype(BF16))
            hb_ref[s, h] = state * jnp.exp(first) + st
        y = y_s[pl.ds(base, Q), :] + jnp.concatenate(ys, axis=-1)
        gated = y * _silu(z_ref[pl.ds(base, Q), :])
        ms = jnp.mean(gated * gated, axis=-1, keepdims=True)
        out_ref[pl.ds(base, Q), :] = gated * lax.rsqrt(ms + EPS) * ng_ref[...]

    def bwd_chunk(i, carry):
        for s in range(n_seq):
            bwd_seq(nc - 1 - i, s)
        return carry

    lax.fori_loop(0, nc, bwd_chunk, 0, unroll=SSD_UNROLL)


def _ssd(latent, z, xbc, dt, consts, h0f, h0b):
    bsz, seq, n_seq = (DEC_BATCH, DEC_SEQ, 1) if latent else (BATCH, SEQ, 2)
    rows = n_seq * seq
    first = N_CTX // rows if latent else 0
    per_seq = lambda w: pl.BlockSpec((rows, w), lambda b: (first + b, 0))
    const_specs = [pl.BlockSpec(a.shape, lambda b, n=a.ndim: (0,) * n) for a in consts]
    st_spec = pl.BlockSpec((n_seq, D_HEADS, D_STATE, D_HEADDIM), lambda b: (b, 0, 0, 0))
    st_shape = jax.ShapeDtypeStruct((bsz, D_HEADS, D_STATE, D_HEADDIM), F32)
    return pl.pallas_call(
        _ssd_kernel,
        out_shape=[jax.ShapeDtypeStruct((bsz * seq, D_INNER), F32), st_shape, st_shape],
        grid=(bsz // n_seq,),
        in_specs=[per_seq(256), per_seq(512), per_seq(128)] + const_specs + [st_spec, st_spec],
        out_specs=[pl.BlockSpec((rows, D_INNER), lambda b: (b, 0)), st_spec, st_spec],
        scratch_shapes=[pltpu.VMEM((rows, CONV_DIM), F32), pltpu.VMEM((rows, LANES), F32),
                        pltpu.VMEM((rows, LANES), F32), pltpu.VMEM((rows, D_INNER), F32)],
        compiler_params=_params("parallel"),
        name="ssd_lat" if latent else "ssd_ctx",
    )(z, xbc, dt, *consts, h0f, h0b)


def _out_proj_kernel(*refs):
    pairs, (mod_ref, wo_ref, g2_ref, rwh_ref, rwl_ref, rb_ref, x1_ref, h2_ref, te_ref, gt_ref, cnt_ref) = \
        refs[:10], refs[10:]
    x, oa, ob, oc, od = [_pick(pairs[2 * j], pairs[2 * j + 1], PROJ_TM) for j in range(5)]
    m = mod_ref[0]
    gate1, sh2, sc2 = m[2:3], m[3:4], m[4:5]
    mixed = (_dot(oa.astype(BF16), wo_ref[0:256, :])
             + _dot(ob.astype(BF16), wo_ref[256:512, :])
             + _dot(oc.astype(BF16), wo_ref[512:768, :])
             + _dot(od.astype(BF16), wo_ref[768:1024, :]))
    x1 = x + gate1 * mixed
    x1_ref[...] = x1
    ms = jnp.mean(x1 * x1, axis=-1, keepdims=True)
    h2 = (x1 * lax.rsqrt(ms + EPS) * g2_ref[...]) * (1.0 + sc2) + sh2
    h2_ref[...] = h2.astype(BF16)

    hi = h2.astype(BF16)
    lo = (h2 - hi.astype(F32)).astype(BF16)
    logits = _dot(hi, rwh_ref[...]) + _dot(lo, rwh_ref[...]) + _dot(hi, rwl_ref[...]) + rb_ref[...]
    lane = lax.broadcasted_iota(jnp.int32, logits.shape, 1)
    vals, idxs = [], []
    for _ in range(TOP_K):
        mx = jnp.max(logits, axis=-1, keepdims=True)
        ix = jnp.min(jnp.where(logits == mx, lane, LANES), axis=-1, keepdims=True)
        vals.append(mx)
        idxs.append(ix)
        logits = jnp.where(lane == ix, -3e38, logits)
    es = [jnp.exp(v - vals[0]) for v in vals]
    den = es[0] + es[1] + es[2] + es[3]
    te = jnp.zeros(lane.shape, jnp.int32)
    gt = jnp.zeros(lane.shape, F32)
    member = jnp.zeros(lane.shape, F32)
    for k in range(TOP_K):
        te = jnp.where(lane == k, idxs[k], te)
        gt = jnp.where(lane == k, es[k] / den, gt)
        member = jnp.where(lane == idxs[k], 1.0, member)
    te_ref[...] = te[:, 0:TOP_K]
    gt_ref[...] = gt[:, 0:TOP_K]
    for t in range(PROJ_TM // TM):
        count = jnp.sum(member[t * TM:(t + 1) * TM], axis=0, keepdims=True)
        cnt_ref[t] = jnp.broadcast_to(count, (8, LANES)).astype(jnp.int32)


def _out_proj(x, oa, ob, oc, od, mod3, wo, g2, rwh, rwl, rb):
    row = lambda w: pl.BlockSpec((PROJ_TM, w), lambda i: (i, 0))
    full = lambda a: pl.BlockSpec(a.shape, lambda i: (0,) * a.ndim)
    return pl.pallas_call(
        _out_proj_kernel,
        out_shape=[jax.ShapeDtypeStruct((N_TOK, D_MODEL), F32), jax.ShapeDtypeStruct((N_TOK, D_MODEL), BF16),
                   jax.ShapeDtypeStruct((N_TOK, TOP_K), jnp.int32), jax.ShapeDtypeStruct((N_TOK, TOP_K), F32),
                   jax.ShapeDtypeStruct((N_TILES, 8, LANES), jnp.int32)],
        grid=(N_TOK // PROJ_TM,),
        in_specs=_pair_specs(D_MODEL, PROJ_TM) + [s for _ in range(4) for s in _pair_specs(256, PROJ_TM)]
                 + [pl.BlockSpec((1, N_MOD, D_MODEL), lambda i: (_mod_row(i, PROJ_TM), 0, 0)),
                    full(wo), full(g2), full(rwh), full(rwl), full(rb)],
        out_specs=[row(D_MODEL), row(D_MODEL), row(TOP_K), row(TOP_K),
                   pl.BlockSpec((PROJ_TM // TM, 8, LANES), lambda i: (i, 0, 0))],
        compiler_params=_params("parallel"),
        name="out_proj_router",
    )(*x, *oa, *ob, *oc, *od, mod3, wo, g2, rwh, rwl, rb)


def _tile_rows(te, tri_ref, off_ref):
    lane = lax.broadcasted_iota(jnp.int32, (TM, LANES), 1)
    hits = [lane == te[:, k:k + 1] for k in range(TOP_K)]
    member = jnp.zeros((TM, LANES), F32)
    for hit in hits:
        member = jnp.where(hit, 1.0, member)
    rank = _dot(tri_ref[...], member.astype(BF16))
    pos = rank + off_ref[0][0:1, :].astype(F32)
    return [jnp.sum(jnp.where(hit, pos, 0.0), axis=-1, keepdims=True) for hit in hits]


def _segment_chunks(seg_ref, len_ref, tile, visit):
    def segment(e, row):
        length = len_ref[tile * N_EXPERTS + e]
        start = seg_ref[tile * N_EXPERTS + e]

        def chunk(c, carry):
            visit(CHUNKS[0], pl.multiple_of(row + c * CHUNKS[0], SEG_ALIGN),
                  pl.multiple_of(start + c * CHUNKS[0], SEG_ALIGN))
            return carry

        lax.fori_loop(0, lax.shift_right_logical(length, CHUNK_SHIFT), chunk, 0)
        for size in CHUNKS[1:]:
            done = jnp.bitwise_and(length, -2 * size)

            @pl.when(jnp.bitwise_and(length, size) != 0)
            def _():
                visit(size, pl.multiple_of(row + done, SEG_ALIGN), pl.multiple_of(start + done, SEG_ALIGN))
        return row + length

    row = 0
    for e in range(N_EXPERTS):
        row = segment(e, row)


def _wait_chunks(nchunk_ref, tile, copy):
    for j, size in enumerate(CHUNKS):
        lax.fori_loop(0, nchunk_ref[tile * len(CHUNKS) + j], lambda c, carry, size=size: (copy(size).wait(), carry)[1], 0)


def _dispatch_kernel(seg_ref, len_ref, nchunk_ref, end_ref, nu_ref, h2_ref, te_ref, off_ref, tri_ref,
                     xb_ref, buf, sems, sem_z):
    i = pl.program_id(0)
    slot = i % 2

    def chunk_copy(size, s, buf_row, xb_row):
        return pltpu.make_async_copy(buf.at[s, pl.ds(buf_row, size)], xb_ref.at[pl.ds(xb_row, size)], sems.at[s])

    def wait_chunks(tile, s):
        _wait_chunks(nchunk_ref, tile, lambda size: chunk_copy(size, s, 0, 0))

    @pl.when(i == 0)
    def _():
        buf[1, 0:ZERO_ROWS, :] = jnp.zeros((ZERO_ROWS, D_MODEL), F32)
        tail = lambda e: pltpu.make_async_copy(
            buf.at[1, pl.ds(0, ZERO_ROWS)],
            xb_ref.at[pl.ds(pl.multiple_of(jnp.maximum(end_ref[e] - ZERO_ROWS, 0), SEG_ALIGN), ZERO_ROWS)], sem_z)
        block = lambda b: pltpu.make_async_copy(
            buf.at[1, pl.ds(0, MOE_BM)], xb_ref.at[pl.ds(pl.multiple_of(b * MOE_BM, MOE_BM), MOE_BM)], sem_z)
        for e in range(N_EXPERTS):
            tail(e).start()
        lax.fori_loop(nu_ref[0], MOE_BLOCKS, lambda b, carry: (block(b).start(), carry)[1], 0)
        for e in range(N_EXPERTS):
            tail(e).wait()
        lax.fori_loop(nu_ref[0], MOE_BLOCKS, lambda b, carry: (block(b).wait(), carry)[1], 0)

    rows = _tile_rows(te_ref[...], tri_ref, off_ref)
    lane = lax.broadcasted_iota(jnp.int32, (TM, LANES), 1)
    packed = jnp.zeros((TM, LANES), F32)
    for k in range(TOP_K):
        packed = jnp.where(lane == k, rows[k], packed)
    rows_t = packed.T
    buf_row = lax.broadcasted_iota(jnp.int32, (TILE_BUF, TM), 0).astype(F32)
    pick = jnp.zeros((TILE_BUF, TM), F32)
    for k in range(TOP_K):
        pick = jnp.where(buf_row == rows_t[k:k + 1, :], 1.0, pick)
    buf[slot] = _dot(pick.astype(BF16), h2_ref[...])

    @pl.when(i > 0)
    def _():
        wait_chunks(i - 1, 1 - slot)

    _segment_chunks(seg_ref, len_ref, i, lambda size, b, x: chunk_copy(size, slot, b, x).start())

    @pl.when(i == N_TILES - 1)
    def _():
        wait_chunks(i, slot)


def _dispatch(seg_start, seg_len, n_chunk, pad_end, n_used, h2, te, seg_off, tri):
    row = lambda w: pl.BlockSpec((TM, w), lambda i, *_: (i, 0))
    grid_spec = pltpu.PrefetchScalarGridSpec(
        num_scalar_prefetch=5,
        grid=(N_TILES,),
        in_specs=[row(D_MODEL), row(TOP_K), pl.BlockSpec((1, 8, LANES), lambda i, *_: (i, 0, 0)),
                  pl.BlockSpec(tri.shape, lambda i, *_: (0, 0))],
        out_specs=pl.BlockSpec(memory_space=pl.ANY),
        scratch_shapes=[pltpu.VMEM((2, TILE_BUF, D_MODEL), F32),
                        pltpu.SemaphoreType.DMA((2,)), pltpu.SemaphoreType.DMA],
    )
    return pl.pallas_call(
        _dispatch_kernel,
        out_shape=jax.ShapeDtypeStruct((YB_ROWS, D_MODEL), F32),
        grid_spec=grid_spec,
        compiler_params=_params("arbitrary"),
        name="moe_dispatch",
    )(seg_start, seg_len, n_chunk, pad_end, n_used, h2, te, seg_off, tri)


def _expert_kernel(be_ref, nu_ref, x_ref, wgu_ref, bgu_ref, wdn_ref, bdn_ref, o_ref, wgu_s, wdn_s):
    i = pl.program_id(0)
    used = i < nu_ref[0]
    prev = be_ref[jnp.maximum(i - 1, 0)]
    fresh = jnp.logical_or(i == 0, be_ref[i] != prev)

    @pl.when(jnp.logical_and(used, fresh))
    def _():
        wgu_s[...] = wgu_ref[0, 0].astype(BF16)
        wdn_s[...] = wdn_ref[0, 0].astype(BF16)

    @pl.when(used)
    def _():
        hgu = _dot(x_ref[...].astype(BF16), wgu_s[...]) + bgu_ref[0, 0]
        gate = jnp.minimum(hgu[:, :D_FF], SWIGLU_LIMIT)
        up = jnp.clip(hgu[:, D_FF:], -SWIGLU_LIMIT, SWIGLU_LIMIT)
        act = (up + 1.0) * gate * jax.nn.sigmoid(SWIGLU_ALPHA * gate)
        o_ref[...] = _dot(act.astype(BF16), wdn_s[...]) + bdn_ref[0, 0]

    @pl.when(jnp.logical_not(used))
    def _():
        o_ref[...] = jnp.zeros_like(o_ref)


def _experts(layer, block_e, n_used, xb, w_gu, b_gu, w_dn, b_dn):
    grid_spec = pltpu.PrefetchScalarGridSpec(
        num_scalar_prefetch=2,
        grid=(MOE_BLOCKS,),
        in_specs=[pl.BlockSpec((MOE_BM, D_MODEL), lambda i, be, nu: (i, 0)),
                  pl.BlockSpec((1, 1, D_MODEL, 2 * D_FF), lambda i, be, nu: (layer, be[i], 0, 0)),
                  pl.BlockSpec((1, 1, 1, 2 * D_FF), lambda i, be, nu: (layer, be[i], 0, 0)),
                  pl.BlockSpec((1, 1, D_FF, D_MODEL), lambda i, be, nu: (layer, be[i], 0, 0)),
                  pl.BlockSpec((1, 1, 1, D_MODEL), lambda i, be, nu: (layer, be[i], 0, 0))],
        out_specs=pl.BlockSpec((MOE_BM, D_MODEL), lambda i, be, nu: (i, 0)),
        scratch_shapes=[pltpu.VMEM((D_MODEL, 2 * D_FF), BF16), pltpu.VMEM((D_FF, D_MODEL), BF16)],
    )
    return pl.pallas_call(
        _expert_kernel,
        out_shape=jax.ShapeDtypeStruct((YB_ROWS, D_MODEL), F32),
        grid_spec=grid_spec,
        compiler_params=_params("arbitrary"),
        name="experts",
    )(block_e, n_used, xb, w_gu, b_gu.reshape(DEPTH, N_EXPERTS, 1, 2 * D_FF), w_dn,
      b_dn.reshape(DEPTH, N_EXPERTS, 1, D_MODEL))


def _combine_kernel(final, seg_ref, len_ref, nchunk_ref, x1_ref, te_ref, gt_ref, off_ref, mod_ref, fg_ref,
                    tri_ref, yb_ref, o_ctx_ref, o_lat_ref, buf, sems):
    i = pl.program_id(0)
    slot = i % 2

    def chunk_copy(size, s, buf_row, yb_row):
        return pltpu.make_async_copy(yb_ref.at[pl.ds(yb_row, size)], buf.at[s, pl.ds(buf_row, size)], sems.at[s])

    def request(tile, s):
        _segment_chunks(seg_ref, len_ref, tile, lambda size, b, y: chunk_copy(size, s, b, y).start())

    @pl.when(i == 0)
    def _():
        buf[...] = jnp.zeros_like(buf)
        request(0, 0)

    @pl.when(i + 1 < N_TILES)
    def _():
        request(i + 1, 1 - slot)

    rows = _tile_rows(te_ref[...], tri_ref, off_ref)
    gt = gt_ref[...]
    buf_row = lax.broadcasted_iota(jnp.int32, (TM, TILE_BUF), 1).astype(F32)
    place = jnp.zeros((TM, TILE_BUF), F32)
    for k in range(TOP_K):
        place = jnp.where(buf_row == rows[k], gt[:, k:k + 1], place)

    _wait_chunks(nchunk_ref, i, lambda size: chunk_copy(size, slot, 0, 0))
    y = _dot(place.astype(BF16), buf[slot].astype(BF16))
    x2 = x1_ref[...] + mod_ref[0][5:6] * y
    if final:
        ms = jnp.mean(x2 * x2, axis=-1, keepdims=True)
        x2 = x2 * lax.rsqrt(ms + EPS) * fg_ref[...]

    @pl.when(i < CTX_TILES)
    def _():
        o_ctx_ref[...] = x2

    @pl.when(i >= CTX_TILES)
    def _():
        o_lat_ref[...] = x2


def _combine(seg_start, seg_len, n_chunk, x1, te, gates, seg_off, mod3, fg, tri, yb, final):
    row = lambda w: pl.BlockSpec((TM, w), lambda i, *_: (i, 0))
    full = lambda a: pl.BlockSpec(a.shape, lambda i, *_: (0,) * a.ndim)
    out_shape = [jax.ShapeDtypeStruct((N_CTX, D_MODEL), F32), jax.ShapeDtypeStruct((N_LAT, D_MODEL), F32)]
    out_specs = _pair_specs(D_MODEL, TM)
    grid_spec = pltpu.PrefetchScalarGridSpec(
        num_scalar_prefetch=3,
        grid=(N_TILES,),
        in_specs=[row(D_MODEL), row(TOP_K), row(TOP_K),
                  pl.BlockSpec((1, 8, LANES), lambda i, *_: (i, 0, 0)),
                  pl.BlockSpec((1, N_MOD, D_MODEL), lambda i, *_: (_mod_row(i), 0, 0)),
                  full(fg), full(tri),
                  pl.BlockSpec(memory_space=pl.ANY)],
        out_specs=out_specs,
        scratch_shapes=[pltpu.VMEM((2, TILE_BUF, D_MODEL), F32), pltpu.SemaphoreType.DMA((2,))],
    )
    return pl.pallas_call(
        functools.partial(_combine_kernel, final),
        out_shape=out_shape,
        grid_spec=grid_spec,
        compiler_params=_params("arbitrary"),
        name="moe_combine",
    )(seg_start, seg_len, n_chunk, x1, te, gates, seg_off, mod3, fg, tri, yb)


def _gqa_head_order(w, axis):
    take = lambda start: lax.slice_in_dim(w, start, start + HEAD_DIM, axis=axis)
    return [take(kv * 128 + g * 64) for g in range(2) for kv in range(2)]


def _w1_layout(w_in):
    cols = lambda name, n: w_in[:, IN_OFF[name]:IN_OFF[name] + n]
    zeros = lambda n: jnp.zeros((D_MODEL, n), F32)
    pieces = (_gqa_head_order(w_in, 1) + [cols('a_k', 256)]
              + [cols('b_cq', 192), zeros(64), cols('b_ckv', 128), cols('b_kr', 32), zeros(96)]
              + [cols('c_q', 768), cols('d_z', 256), cols('d_xbc', 512), cols('d_dtf', 8), zeros(120)])
    return jnp.concatenate(pieces, axis=1).astype(BF16)


def _rope_tables():
    t = np.arange(DEC_SEQ)
    pos = ((t // GRID_W).astype(np.float32), (t % GRID_W).astype(np.float32))

    def unit(rot_dim):
        quarter, half = rot_dim // 4, rot_dim // 2
        inv = ROPE_THETA ** (-np.arange(0, half, 2, dtype=np.float32) / half)
        cos = np.zeros((DEC_SEQ, rot_dim), np.float32)
        sin = np.zeros((DEC_SEQ, rot_dim), np.float32)
        for seg in range(4):
            ang = pos[seg // 2][:, None] * inv[None, :].astype(np.float32)
            cos[:, seg * quarter:(seg + 1) * quarter] = np.cos(ang)
            sin[:, seg * quarter:(seg + 1) * quarter] = np.sin(ang) * (-1.0 if seg % 2 == 0 else 1.0)
        return cos, sin

    log2e = math.log2(math.e)
    specs = [
        (HEAD_DIM, (0, 64), (), HEAD_DIM ** -0.5 * log2e),
        (HEAD_DIM, (0, 64), (), 1.0),
        (MLA_ROPE, (MLA_NOPE,), (0, MLA_NOPE), (MLA_NOPE + MLA_ROPE) ** -0.5 * log2e),
        (MLA_ROPE, (0,), (), 1.0),
        (DIFF_D, (0, 32, 64, 96), (), DIFF_D ** -0.5 * log2e),
        (DIFF_D, (0, 32, 64, 96), (), 1.0),
    ]
    lat_cols, ident_cols = [], []
    for rot_dim, starts, passthrough, scale in specs:
        ucos, usin = unit(rot_dim)
        cos = np.zeros((DEC_SEQ, LANES), np.float32)
        sin = np.zeros((DEC_SEQ, LANES), np.float32)
        ident = np.zeros((1, LANES), np.float32)
        if passthrough:
            cos[:, passthrough[0]:passthrough[1]] = 1.0
            ident[:, passthrough[0]:passthrough[1]] = 1.0
        for s in starts:
            cos[:, s:s + rot_dim] = ucos
            sin[:, s:s + rot_dim] = usin
            ident[:, s:s + rot_dim] = 1.0
        lat_cols += [cos * scale, sin * scale]
        ident_cols += [ident * scale, np.zeros((1, LANES), np.float32)]
    lat = np.concatenate(lat_cols, axis=1)
    ident_blk = np.broadcast_to(np.concatenate(ident_cols, axis=1), (PROJ_TM, N_TAB * LANES))
    return np.concatenate([ident_blk, lat], axis=0).astype(np.float32)


def _block_diag_ones(n, blk):
    r = np.arange(n)
    return (r[:, None] // blk == r[None, :] // blk).astype(np.float32)


def _chunk_tables(seg_len):
    counts = [jnp.sum(seg_len // CHUNKS[0], axis=1)]
    counts += [jnp.sum(seg_len % (2 * size) // size, axis=1) for size in CHUNKS[1:]]
    return jnp.stack(counts, axis=1).reshape(-1).astype(jnp.int32)


def _strict_lower_ones(n):
    r = np.arange(n)
    return (r[None, :] < r[:, None]).astype(np.float32)


def kernel(x_prompt, x_sample, cache_gqa_k, cache_gqa_v, cache_mla_ckv, cache_mla_krope, cache_diff_k, cache_diff_v, state_ssd_fwd, state_ssd_bwd, c, c_ctx, norm1_g, norm2_g, w_ada, b_ada, w_in, w_out, gqa_qn_g, gqa_kn_g, mla_qa_g, mla_wqb, mla_kva_g, mla_wkvb, diff_lq1, diff_lk1, diff_lq2, diff_lk2, diff_subln_g, ssd_conv_w, ssd_conv_b, ssd_a_log_f, ssd_a_log_b, ssd_dt_bias_f, ssd_dt_bias_b, ssd_d, ssd_norm_g, router_w, router_b, moe_w_gu, moe_b_gu, moe_w_dn, moe_b_dn, final_g):
    tabs = jnp.asarray(_rope_tables())
    bd = jnp.asarray(_block_diag_ones(256, HEAD_DIM), BF16)
    tri = jnp.asarray(_strict_lower_ones(TM), BF16)

    cvec = jnp.zeros((MOD_ROWS, D_MODEL), F32).at[0].set(c_ctx).at[1:1 + DEC_BATCH].set(c)
    mod = _modulation(cvec, w_ada, b_ada).reshape(DEPTH, MOD_ROWS, N_MOD, D_MODEL)

    x = (x_prompt.reshape(N_CTX, D_MODEL), x_sample.reshape(N_LAT, D_MODEL))
    new_ctx = []
    for l in range(DEPTH):
        mod3 = mod[l]
        w1 = _w1_layout(w_in[l])
        wqb = mla_wqb[l].reshape(Q_LORA, B_HEADS, MLA_NOPE + MLA_ROPE)
        wqb = jnp.pad(wqb, ((0, 256 - Q_LORA), (0, 0), (0, LANES - MLA_NOPE - MLA_ROPE)))
        wqb = wqb.reshape(256, B_HEADS * LANES).astype(BF16)
        wkvb = mla_wkvb[l].reshape(KV_LORA, B_HEADS, MLA_NOPE + MLA_V)
        wk_nope = jnp.pad(wkvb[:, :, :MLA_NOPE], ((0, 0), (0, 0), (0, LANES - MLA_NOPE)))
        eye_r = jnp.zeros((LANES, B_HEADS, LANES), F32)
        eye_r = eye_r.at[jnp.arange(MLA_ROPE), :, MLA_NOPE + jnp.arange(MLA_ROPE)].set(1.0)
        wk = jnp.concatenate([wk_nope, eye_r], axis=0).reshape(256, B_HEADS * LANES).astype(BF16)
        wv = wkvb[:, :, MLA_NOPE:].reshape(KV_LORA, B_HEADS * MLA_V).astype(BF16)
        wo = jnp.concatenate(_gqa_head_order(w_out[l], 0) + [w_out[l][256:]], axis=0).astype(BF16)
        gq = jnp.tile(gqa_qn_g[l], 4)[None, :]
        gk = jnp.tile(gqa_kn_g[l], 2)[None, :]
        gqa = jnp.pad(mla_qa_g[l], (0, 256 - Q_LORA))[None, :]
        gkva = mla_kva_g[l][None, :]
        rw = jnp.pad(router_w[l], ((0, 0), (0, LANES - N_EXPERTS)))
        rwh = rw.astype(BF16)
        rwl = (rw - rwh.astype(F32)).astype(BF16)
        rb = jnp.pad(router_b[l], (0, LANES - N_EXPERTS), constant_values=NEG_BIG)[None, :]

        qa, kva, qb, ck, qc, kc, vc, z, xbc, dt = _in_proj(
            x, mod3, norm1_g[l][None, :], w1, bd, gq, gk, gqa, gkva, wqb, tabs)

        past_kva = jnp.concatenate([cache_gqa_k[:, l].reshape(DEC_BATCH, PAST_LEN, 128),
                                    cache_gqa_v[:, l].reshape(DEC_BATCH, PAST_LEN, 128)], axis=-1)
        past_ck = jnp.concatenate([cache_mla_ckv[:, l], cache_mla_krope[:, l],
                                   jnp.zeros((DEC_BATCH, PAST_LEN, LANES - MLA_ROPE), F32)], axis=-1)
        past_kc = cache_diff_k[:, l].reshape(DEC_BATCH, PAST_LEN, 256)
        past_vc = cache_diff_v[:, l].reshape(DEC_BATCH, PAST_LEN, 256)
        lams = [a[l][None, :] for a in (diff_lq1, diff_lk1, diff_lq2, diff_lk2)]
        gsub = jnp.tile(diff_subln_g[l], 2)[None, :]
        lam_init = 0.8 - 0.6 * math.exp(-0.3 * l)
        oa, ob, oc = _attention_mixers(qa, kva, qb, ck, qc, kc, vc, past_kva, past_ck, past_kc, past_vc,
                                       wk, wv, lams, gsub, lam_init)

        pad8 = lambda f, b: jnp.pad(jnp.concatenate([f, b]), (0, LANES - 2 * D_HEADS))[None, :]
        ssd_consts = (ssd_conv_w[l], ssd_conv_b[l][None, :], pad8(ssd_dt_bias_f[l], ssd_dt_bias_b[l]),
                      pad8(ssd_a_log_f[l], ssd_a_log_b[l]), jnp.repeat(ssd_d[l], D_HEADDIM)[None, :],
                      ssd_norm_g[l][None, :])
        zeros_st = jnp.zeros((BATCH, D_HEADS, D_STATE, D_HEADDIM), F32)
        od_c, hf_c, hb_c = _ssd(False, z, xbc, dt, ssd_consts, zeros_st, zeros_st)
        od_l, _, _ = _ssd(True, z, xbc, dt, ssd_consts, jnp.swapaxes(state_ssd_fwd[:, l], -1, -2),
                          jnp.swapaxes(state_ssd_bwd[:, l], -1, -2))
        od = (od_c, od_l)

        x1, h2, te, gates, tile_cnt = _out_proj(x, oa, ob, oc, od, mod3, wo, norm2_g[l][None, :], rwh, rwl, rb)

        seg_cnt = tile_cnt[:, 0, :N_EXPERTS]
        seg_len = (seg_cnt + SEG_ALIGN - 1) // SEG_ALIGN * SEG_ALIGN
        region = (jnp.sum(seg_len, axis=0) + MOE_BM - 1) // MOE_BM * MOE_BM
        pad_end = jnp.cumsum(region).astype(jnp.int32)
        pad_start = pad_end - region
        seg_start = (pad_start[None, :] + jnp.cumsum(seg_len, axis=0) - seg_len).astype(jnp.int32)
        n_used = (pad_end[-1] // MOE_BM).astype(jnp.int32).reshape(1)
        blk_start = jnp.arange(MOE_BLOCKS, dtype=jnp.int32) * MOE_BM
        block_e = jnp.minimum(jnp.sum((pad_end[None, :] <= blk_start[:, None]).astype(jnp.int32), axis=1),
                              N_EXPERTS - 1).astype(jnp.int32)
        n_chunk = _chunk_tables(seg_len)
        seg_off = jnp.cumsum(seg_len, axis=1) - seg_len
        seg_off = jnp.broadcast_to(jnp.pad(seg_off, ((0, 0), (0, LANES - N_EXPERTS)))[:, None, :],
                                   (N_TILES, 8, LANES)).astype(jnp.int32)
        seg_start, seg_len = seg_start.reshape(-1), seg_len.reshape(-1).astype(jnp.int32)

        xb = _dispatch(seg_start, seg_len, n_chunk, pad_end, n_used, h2, te, seg_off, tri)
        yb = _experts(l, block_e, n_used, xb, moe_w_gu, moe_b_gu, moe_w_dn, moe_b_dn)
        x = _combine(seg_start, seg_len, n_chunk, x1, te, gates, seg_off, mod3, final_g[None, :], tri, yb,
                     l == DEPTH - 1)

        kva_c, ck_c = kva[:N_CTX], ck[:N_CTX]
        new_ctx.append((kva_c[:, 0:128].reshape(BATCH, SEQ, A_KV_HEADS, HEAD_DIM),
                        kva_c[:, 128:256].reshape(BATCH, SEQ, A_KV_HEADS, HEAD_DIM),
                        ck_c[:, 0:KV_LORA].reshape(BATCH, SEQ, KV_LORA),
                        ck_c[:, KV_LORA:KV_LORA + MLA_ROPE].reshape(BATCH, SEQ, MLA_ROPE),
                        kc[:N_CTX].reshape(BATCH, SEQ, C_HEADS, 2 * DIFF_D),
                        vc[:N_CTX].reshape(BATCH, SEQ, C_HEADS, DIFF_V),
                        jnp.swapaxes(hf_c, -1, -2), jnp.swapaxes(hb_c, -1, -2)))

    y_prompt = x[0].reshape(BATCH, SEQ, D_MODEL)
    y_sample = x[1].reshape(DEC_BATCH, DEC_SEQ, D_MODEL)
    caches = [jnp.stack([cl[i] for cl in new_ctx], axis=1) for i in range(8)]
    return (y_prompt, y_sample, *caches)
```

```python
import functools
import math

import numpy as np
import jax
import jax.numpy as jnp
from jax import lax
from jax.experimental import pallas as pl
from jax.experimental.pallas import tpu as pltpu

F32 = jnp.float32
BF16 = jnp.bfloat16

D_MODEL = 1024
BATCH = 16
SEQ = 256
DEPTH = 2
DEC_BATCH = 8
DEC_SEQ = 2048
PAST_LEN = 256
GRID_W = 64
ROPE_THETA = 10000.0
EPS = 1e-6
HEAD_DIM = 64
A_HEADS = 4
A_KV_HEADS = 2
B_HEADS = 4
MLA_NOPE = 64
MLA_ROPE = 32
MLA_V = 64
Q_LORA = 192
KV_LORA = 128
C_HEADS = 4
DIFF_D = 32
DIFF_V = 64
SUBLN_EPS = 1e-5
D_HEADS = 4
D_HEADDIM = 64
D_INNER = 256
D_GROUPS = 2
D_STATE = 64
CONV_DIM = 512
SSD_CHUNK = 128
N_EXPERTS = 32
TOP_K = 4
D_FF = 1024
SWIGLU_ALPHA = 1.702
SWIGLU_LIMIT = 7.0
N_MOD = 6

N_CTX = BATCH * SEQ
N_LAT = DEC_BATCH * DEC_SEQ
N_TOK = N_CTX + N_LAT

LANES = 128
TM = 256
N_TILES = N_TOK // TM
CTX_TILES = N_CTX // TM
LAT_TQ = 1024
PROJ_TM = 512
MOD_ROWS = 16
MOE_BM = 512
N_PAIRS = N_TOK * TOP_K
SEG_ALIGN = 8
CHUNKS = (32, 16, 8)
CHUNK_SHIFT = 5
MOE_BLOCKS = -(-(N_PAIRS + N_TILES * N_EXPERTS * (SEG_ALIGN - 1) + N_EXPERTS * (MOE_BM - 1)) // MOE_BM)
YB_ROWS = MOE_BLOCKS * MOE_BM
TILE_BUF = -(-(TM * TOP_K + N_EXPERTS * (SEG_ALIGN - 1)) // LANES) * LANES
ZERO_ROWS = MOE_BM
VMEM_LIMIT = 56 * 1024 * 1024
NEG_BIG = -1e30

W1_COLS = 2688
IN_OFF = dict(a_q=0, a_k=256, a_v=384, b_cq=512, b_ckv=704, b_kr=832, c_q=864, c_k=1120,
              c_v=1376, d_z=1632, d_xbc=1888, d_dtf=2400, d_dtb=2404)
IN_WIDTH = 2408
N_TAB = 12


def _mod_row(i, tile=TM):
    return jnp.where(i < N_CTX // tile, 0, 1 + (i - N_CTX // tile) // (DEC_SEQ // tile))


def _tab_block(i, tile):
    return jnp.where(i < N_CTX // tile, 0, 1 + (i - N_CTX // tile) % (DEC_SEQ // tile))


def _pair_specs(width, tile):
    n_ctx = N_CTX // tile
    return [pl.BlockSpec((tile, width), lambda i, *_: (jnp.minimum(i, n_ctx - 1), 0)),
            pl.BlockSpec((tile, width), lambda i, *_: (jnp.maximum(i - n_ctx, 0), 0))]


def _pick(ctx_ref, lat_ref, tile):
    return jnp.where(pl.program_id(0) < N_CTX // tile, ctx_ref[...], lat_ref[...])


def _dot(a, b):
    return jnp.dot(a, b, preferred_element_type=F32)


def _dot_nt(a, b):
    return lax.dot_general(a, b, (((1,), (1,)), ((), ())), preferred_element_type=F32)


def _dot_split(x, m):
    hi = x.astype(BF16)
    lo = (x - hi.astype(F32)).astype(BF16)
    return _dot(hi, m) + _dot(lo, m)


def _dot_split_left(m, x):
    hi = x.astype(BF16)
    lo = (x - hi.astype(F32)).astype(BF16)
    return _dot(m, hi) + _dot(m, lo)


def _silu(x):
    return x * jax.nn.sigmoid(x)


def _params(*semantics):
    return pltpu.CompilerParams(dimension_semantics=semantics, vmem_limit_bytes=VMEM_LIMIT)


MOD_TN = 1536


def _mod_kernel(c_ref, w_ref, b_ref, o_ref):
    c = c_ref[...]
    s = _silu(c).astype(BF16)
    o_ref[0] = _dot(s, w_ref[0].astype(BF16)) + b_ref[0]


def _modulation(cvec, w_ada, b_ada):
    n = N_MOD * D_MODEL
    return pl.pallas_call(
        _mod_kernel,
        out_shape=jax.ShapeDtypeStruct((DEPTH, MOD_ROWS, n), F32),
        grid=(DEPTH, n // MOD_TN),
        in_specs=[pl.BlockSpec((MOD_ROWS, D_MODEL), lambda l, j: (0, 0)),
                  pl.BlockSpec((1, D_MODEL, MOD_TN), lambda l, j: (l, 0, j)),
                  pl.BlockSpec((1, 1, MOD_TN), lambda l, j: (l, 0, j))],
        out_specs=pl.BlockSpec((1, MOD_ROWS, MOD_TN), lambda l, j: (l, 0, j)),
        compiler_params=_params("parallel", "parallel"),
        name="adaln_mod",
    )(cvec, w_ada, b_ada.reshape(DEPTH, 1, n))


def _rope(x, cos, sin, quarter):
    lane = lax.broadcasted_iota(jnp.int32, (x.shape[0], LANES), 1)
    first = (lane // quarter) % 2 == 0
    outs = []
    for t in range(x.shape[1] // LANES):
        xt = x[:, t * LANES:(t + 1) * LANES]
        partner = jnp.where(first, pltpu.roll(xt, LANES - quarter, 1), pltpu.roll(xt, quarter, 1))
        outs.append(xt * cos + partner * sin)
    return outs


def _in_proj_kernel(xc_ref, xl_ref, mod_ref, g1_ref, w1_ref, bd_ref, gq_ref, gk_ref, gqa_ref, gkva_ref,
                    wqb_ref, tab_ref,
                    qa_ref, kva_ref, qb_ref, ck_ref, qc_ref, kc_ref, vc_ref, z_ref, xbc_ref, dt_ref):
    x = _pick(xc_ref, xl_ref, PROJ_TM)
    m = mod_ref[0]
    sh1, sc1 = m[0:1], m[1:2]
    ms = jnp.mean(x * x, axis=-1, keepdims=True)
    h = (x * lax.rsqrt(ms + EPS) * g1_ref[...]) * (1.0 + sc1) + sh1
    u = _dot(h.astype(BF16), w1_ref[...])

    def tab(k):
        return tab_ref[:, k * LANES:(k + 1) * LANES]

    bd = bd_ref[...]

    def head_norm(v, gain):
        w = v.shape[1]
        ss = _dot_split(v * v, bd[:w, :w])
        return v * lax.rsqrt(ss * (1.0 / HEAD_DIM) + EPS) * gain

    qa = _rope(head_norm(u[:, 0:256], gq_ref[...]), tab(0), tab(1), HEAD_DIM // 4)
    for t in range(2):
        qa_ref[:, t * LANES:(t + 1) * LANES] = qa[t].astype(BF16)
    ka = _rope(head_norm(u[:, 256:384], gk_ref[...]), tab(2), tab(3), HEAD_DIM // 4)
    kva_ref[:, 0:128] = ka[0]
    kva_ref[:, 128:256] = u[:, 384:512]

    cq = u[:, 512:768]
    msq = jnp.sum(cq * cq, axis=-1, keepdims=True) * (1.0 / Q_LORA)
    yq = cq * lax.rsqrt(msq + EPS) * gqa_ref[...]
    qb = _rope(_dot(yq.astype(BF16), wqb_ref[...]), tab(4), tab(5), MLA_ROPE // 4)
    for t in range(4):
        qb_ref[:, t * LANES:(t + 1) * LANES] = qb[t].astype(BF16)
    ckv = u[:, 768:896]
    msk = jnp.mean(ckv * ckv, axis=-1, keepdims=True)
    ck_ref[:, 0:128] = ckv * lax.rsqrt(msk + EPS) * gkva_ref[...]
    ck_ref[:, 128:256] = _rope(u[:, 896:1024], tab(6), tab(7), MLA_ROPE // 4)[0]

    qc = _rope(u[:, 1024:1280], tab(8), tab(9), DIFF_D // 4)
    kc = _rope(u[:, 1280:1536], tab(10), tab(11), DIFF_D // 4)
    for t in range(2):
        qc_ref[:, t * LANES:(t + 1) * LANES] = qc[t].astype(BF16)
        kc_ref[:, t * LANES:(t + 1) * LANES] = kc[t]
    vc_ref[...] = u[:, 1536:1792]

    z_ref[...] = u[:, 1792:2048]
    xbc_ref[...] = u[:, 2048:2560]
    dt_ref[...] = u[:, 2560:2688]


def _in_proj(x, mod3, g1, w1, bd, gq, gk, gqa, gkva, wqb, tabs):
    row = lambda w: pl.BlockSpec((PROJ_TM, w), lambda i: (i, 0))
    full = lambda a: pl.BlockSpec(a.shape, lambda i: (0,) * a.ndim)
    outs = [(256, BF16), (256, F32), (512, BF16), (256, F32), (256, BF16), (256, F32), (256, F32),
            (256, F32), (512, F32), (128, F32)]
    return pl.pallas_call(
        _in_proj_kernel,
        out_shape=[jax.ShapeDtypeStruct((N_TOK, w), d) for w, d in outs],
        grid=(N_TOK // PROJ_TM,),
        in_specs=_pair_specs(D_MODEL, PROJ_TM)
                 + [pl.BlockSpec((1, N_MOD, D_MODEL), lambda i: (_mod_row(i, PROJ_TM), 0, 0)),
                    full(g1), full(w1), full(bd), full(gq), full(gk), full(gqa), full(gkva), full(wqb),
                    pl.BlockSpec((PROJ_TM, N_TAB * LANES), lambda i: (_tab_block(i, PROJ_TM), 0))],
        out_specs=[row(w) for w, _ in outs],
        compiler_params=_params("parallel"),
        name="in_proj",
    )(*x, mod3, g1, w1, bd, gq, gk, gqa, gkva, wqb, tabs)


def _attend(q, k, v_ones):
    s = _dot_nt(q, k)
    e = jnp.exp2(s - jnp.max(s, axis=-1, keepdims=True))
    out = _dot(e.astype(BF16), v_ones)
    return out[:, :LANES] / out[:, LANES:]


def _with_ones(v):
    return jnp.concatenate([v, jnp.ones_like(v)], axis=-1)


def _half_mask(rows):
    lane = lax.broadcasted_iota(jnp.int32, (rows, LANES), 1)
    return lane < (LANES // 2)


def _keys(past_ref, new_ref, lo, hi):
    new = new_ref[:, lo:hi].astype(BF16)
    if past_ref is None:
        return new
    return jnp.concatenate([past_ref[0, :, lo:hi].astype(BF16), new], axis=0)


def _attn_a_kernel(latent, *refs):
    if latent:
        q_ref, kv_ref, past_ref, o_ref = refs
    else:
        (q_ref, kv_ref, o_ref), past_ref = refs, None
    q = q_ref[...]
    k = _keys(past_ref, kv_ref, 0, 128)
    v = _with_ones(_keys(past_ref, kv_ref, 128, 256))
    lo = _half_mask(q.shape[0])
    for g in range(2):
        qt = q[:, g * LANES:(g + 1) * LANES].astype(F32)
        res = []
        for half in range(2):
            qm = jnp.where(lo, qt, 0.0) if half == 0 else jnp.where(lo, 0.0, qt)
            res.append(_attend(qm.astype(BF16), k, v))
        o_ref[:, g * LANES:(g + 1) * LANES] = jnp.where(lo, res[0], res[1])


def _attn_b_kernel(latent, *refs):
    if latent:
        q_ref, ck_ref, past_ref, wk_ref, wv_ref, o_ref, k_s, v_s = refs
    else:
        (q_ref, ck_ref, wk_ref, wv_ref, o_ref, k_s, v_s), past_ref = refs, None

    @pl.when(pl.program_id(1) == 0)
    def _():
        ck = _keys(past_ref, ck_ref, 0, 256)
        k_s[...] = _dot(ck, wk_ref[...]).astype(BF16)
        v = _dot(ck[:, 0:128], wv_ref[...]).astype(BF16)
        for j in range(2):
            v_s[:, 2 * j * LANES:2 * (j + 1) * LANES] = _with_ones(v[:, j * LANES:(j + 1) * LANES])

    q = q_ref[...]
    lo = _half_mask(q.shape[0])
    for j in range(2):
        v = v_s[:, 2 * j * LANES:2 * (j + 1) * LANES]
        res = []
        for half in range(2):
            h = 2 * j + half
            res.append(_attend(q[:, h * LANES:(h + 1) * LANES], k_s[:, h * LANES:(h + 1) * LANES], v))
        o_ref[:, j * LANES:(j + 1) * LANES] = jnp.where(lo, res[0], res[1])


def _attn_c_kernel(lam_init, latent, *refs):
    if latent:
        q_ref, k_ref, v_ref, pk_ref, pv_ref, lq1_ref, lk1_ref, lq2_ref, lk2_ref, g_ref, o_ref = refs
    else:
        (q_ref, k_ref, v_ref, lq1_ref, lk1_ref, lq2_ref, lk2_ref, g_ref, o_ref), pk_ref, pv_ref = refs, None, None
    lam = (jnp.exp(jnp.sum(lq1_ref[...] * lk1_ref[...], axis=-1, keepdims=True))
           - jnp.exp(jnp.sum(lq2_ref[...] * lk2_ref[...], axis=-1, keepdims=True)) + lam_init)
    q = q_ref[...]
    rows = q.shape[0]
    lane = lax.broadcasted_iota(jnp.int32, (rows, LANES), 1)
    lo = lane < (LANES // 2)
    for j in range(2):
        qt = q[:, j * LANES:(j + 1) * LANES].astype(F32)
        k = _keys(pk_ref, k_ref, j * LANES, (j + 1) * LANES)
        v = _with_ones(_keys(pv_ref, v_ref, j * LANES, (j + 1) * LANES))
        res = []
        for half in range(2):
            parts = []
            for t in range(2):
                quarter = 2 * half + t
                qm = jnp.where(lane // (LANES // 4) == quarter, qt, 0.0)
                parts.append(_attend(qm.astype(BF16), k, v))
            res.append(parts[0] - lam * parts[1])
        o = jnp.where(lo, res[0], res[1])
        o2 = o * o
        ss_lo = jnp.sum(jnp.where(lo, o2, 0.0), axis=-1, keepdims=True)
        ss_hi = jnp.sum(jnp.where(lo, 0.0, o2), axis=-1, keepdims=True)
        ss = jnp.where(lo, ss_lo, ss_hi) * (1.0 / DIFF_V)
        o_ref[:, j * LANES:(j + 1) * LANES] = (o * lax.rsqrt(ss + SUBLN_EPS) * g_ref[...]) * (1.0 - lam_init)


def _seq_call(body, name, latent, q, news, pasts, consts, scratch=(), q_semantics="parallel"):
    const_specs = [pl.BlockSpec(a.shape, lambda b, i, n=a.ndim: (0,) * n) for a in consts]
    if latent:
        tq, rows = LAT_TQ, N_LAT
        tile = lambda b, i: (N_CTX // LAT_TQ + b * (DEC_SEQ // LAT_TQ) + i, 0)
        out_tile = lambda b, i: (b * (DEC_SEQ // LAT_TQ) + i, 0)
        grid = (DEC_BATCH, DEC_SEQ // LAT_TQ)
        new_specs = [pl.BlockSpec((DEC_SEQ, a.shape[1]), lambda b, i: (N_CTX // DEC_SEQ + b, 0)) for a in news]
        past_specs = [pl.BlockSpec((1, PAST_LEN, a.shape[2]), lambda b, i: (b, 0, 0)) for a in pasts]
    else:
        tq, rows = SEQ, N_CTX
        tile = out_tile = lambda b, i: (b, 0)
        grid = (BATCH, 1)
        new_specs = [pl.BlockSpec((SEQ, a.shape[1]), tile) for a in news]
        past_specs, pasts = [], ()
    return pl.pallas_call(
        functools.partial(body, latent),
        out_shape=jax.ShapeDtypeStruct((rows, 256), F32),
        grid=grid,
        in_specs=[pl.BlockSpec((tq, q.shape[1]), tile)] + new_specs + past_specs + const_specs,
        out_specs=pl.BlockSpec((tq, 256), out_tile),
        scratch_shapes=list(scratch),
        compiler_params=_params("parallel", q_semantics),
        name=name + ("_lat" if latent else "_ctx"),
    )(q, *news, *pasts, *consts)


def _attention_mixers(qa, kva, qb, ck, qc, kc, vc, past_kva, past_ck, past_kc, past_vc, wk, wv,
                      lams, gsub, lam_init):
    def both(body, name, q, news, pasts, consts, scratch_fn=None, q_semantics="parallel"):
        sc = (lambda lk: ()) if scratch_fn is None else scratch_fn
        return (_seq_call(body, name, False, q, news, (), consts, sc(SEQ), q_semantics),
                _seq_call(body, name, True, q, news, pasts, consts, sc(PAST_LEN + DEC_SEQ), q_semantics))

    oa = both(_attn_a_kernel, "attn_gqa", qa, [kva], [past_kva], [])
    ob = both(_attn_b_kernel, "attn_mla", qb, [ck], [past_ck], [wk, wv],
              lambda lk: (pltpu.VMEM((lk, 512), BF16), pltpu.VMEM((lk, 512), BF16)), "arbitrary")
    oc = both(functools.partial(_attn_c_kernel, lam_init), "attn_diff", qc, [kc, vc],
              [past_kc, past_vc], [*lams, gsub])
    return oa, ob, oc


Q = SSD_CHUNK
SSD_UNROLL = 2


def _ssd_kernel(z_ref, xbc_ref, dt_ref, cw_ref, cb_ref, dtb_ref, alog_ref, dvec_ref, ng_ref, h0f_ref, h0b_ref,
                out_ref, hf_ref, hb_ref, act_s, cum_s, dtv_s, y_s):
    n_seq = hf_ref.shape[0]
    seq = z_ref.shape[0] // n_seq
    nc = seq // Q
    row = lax.broadcasted_iota(jnp.int32, (Q, Q), 0)
    col = lax.broadcasted_iota(jnp.int32, (Q, Q), 1)
    lower = row >= col
    upper = row <= col
    tril = jnp.where(lower, 1.0, 0.0).astype(BF16)
    triu = jnp.where(upper, 1.0, 0.0).astype(BF16)
    rowc = lax.broadcasted_iota(jnp.int32, (Q, CONV_DIM), 0)
    lane = lax.broadcasted_iota(jnp.int32, (Q, LANES), 1)
    a_neg = -jnp.exp(alog_ref[...])
    cw = cw_ref[...]
    hf_ref[...] = h0f_ref[...]
    hb_ref[...] = h0b_ref[...]

    def fwd_seq(c, s):
        off = s * seq
        base = pl.multiple_of(off + c * Q, Q)
        x0 = xbc_ref[pl.ds(base, Q), :]
        prev = xbc_ref[pl.ds(pl.multiple_of(off + jnp.maximum(c * Q - 8, 0), 8), 8), :][7:8, :]
        nxt = xbc_ref[pl.ds(pl.multiple_of(off + jnp.minimum(c * Q + Q, seq - 8), 8), 8), :][0:1, :]
        prev = jnp.where(c > 0, prev, 0.0)
        nxt = jnp.where(c < nc - 1, nxt, 0.0)
        xm1 = jnp.where(rowc == 0, prev, pltpu.roll(x0, 1, 0))
        xp1 = jnp.where(rowc == Q - 1, nxt, pltpu.roll(x0, Q - 1, 0))
        act = _silu(xm1 * cw[0:1] + x0 * cw[1:2] + xp1 * cw[2:3] + cb_ref[...])
        act_s[pl.ds(base, Q), :] = act
        xs = act[:, 0:256]
        bm = act[:, 256:384]
        cm = act[:, 384:512]

        dtr = dt_ref[pl.ds(base, Q), :] + dtb_ref[...]
        dtv = jnp.maximum(dtr, 0.0) + jnp.log1p(jnp.exp(-jnp.abs(dtr)))
        dta = dtv * a_neg
        cum = jnp.where(lane < D_HEADS, _dot_split_left(tril, dta), _dot_split_left(triu, dta))
        cum_s[pl.ds(base, Q), :] = cum
        dtv_s[pl.ds(base, Q), :] = dtv
        cum_t = cum.T
        dtv_t = dtv.T
        bm_t = bm.T
        ys = []
        for h in range(D_HEADS):
            g = h // (D_HEADS // D_GROUPS)
            cg = cm[:, g * D_STATE:(g + 1) * D_STATE].astype(BF16)
            cb_mat = _dot_nt(cg, bm[:, g * D_STATE:(g + 1) * D_STATE].astype(BF16))
            cf = cum[:, h:h + 1]
            cb = cum[:, D_HEADS + h:D_HEADS + h + 1]
            l_f = jnp.exp(jnp.where(lower, cf - cum_t[h:h + 1, :], NEG_BIG))
            l_b = jnp.exp(jnp.where(upper, cb - cum_t[D_HEADS + h:D_HEADS + h + 1, :], NEG_BIG))
            mix = cb_mat * (l_f * dtv_t[h:h + 1, :] + l_b * dtv_t[D_HEADS + h:D_HEADS + h + 1, :])
            xh = xs[:, h * D_HEADDIM:(h + 1) * D_HEADDIM]
            y = _dot(mix.astype(BF16), xh.astype(BF16))
            state = hf_ref[s, h]
            y = y + _dot(cg, state.astype(BF16)) * jnp.exp(cf)
            y = y + dvec_ref[:, h * D_HEADDIM:(h + 1) * D_HEADDIM] * xh
            ys.append(y)
            last = cum[Q - 1:Q, h:h + 1]
            wgt = jnp.exp(last - cf) * dtv[:, h:h + 1]
            st = _dot(bm_t[g * D_STATE:(g + 1) * D_STATE, :].astype(BF16), (xh * wgt).astype(BF16))
            hf_ref[s, h] = state * jnp.exp(last) + st
        y_s[pl.ds(base, Q), :] = jnp.concatenate(ys, axis=-1)

    def fwd_chunk(c, carry):
        for s in range(n_seq):
            fwd_seq(c, s)
        return carry

    lax.fori_loop(0, nc, fwd_chunk, 0, unroll=SSD_UNROLL)

    def bwd_seq(c, s):
        base = pl.multiple_of(s * seq + c * Q, Q)
        act = act_s[pl.ds(base, Q), :]
        cum = cum_s[pl.ds(base, Q), :]
        dtv = dtv_s[pl.ds(base, Q), :]
        xs = act[:, 0:256]
        bm_t = act[:, 256:384].T
        cm = act[:, 384:512]
        ys = []
        for h in range(D_HEADS):
            g = h // (D_HEADS // D_GROUPS)
            cg = cm[:, g * D_STATE:(g + 1) * D_STATE].astype(BF16)
            cb = cum[:, D_HEADS + h:D_HEADS + h + 1]
            xh = xs[:, h * D_HEADDIM:(h + 1) * D_HEADDIM]
            state = hb_ref[s, h]
            ys.append(_dot(cg, state.astype(BF16)) * jnp.exp(cb))
            first = cum[0:1, D_HEADS + h:D_HEADS + h + 1]
            wgt = jnp.exp(first - cb) * dtv[:, D_HEADS + h:D_HEADS + h + 1]
            st = _dot(bm_t[g * D_STATE:(g + 1) * D_STATE, :].astype(BF16), (xh * wgt).astype(BF16))
            hb_ref[s, h] = state * jnp.exp(first) + st
        y = y_s[pl.ds(base, Q), :] + jnp.concatenate(ys, axis=-1)
        gated = y * _silu(z_ref[pl.ds(base, Q), :])
        ms = jnp.mean(gated * gated, axis=-1, keepdims=True)
        out_ref[pl.ds(base, Q), :] = gated * lax.rsqrt(ms + EPS) * ng_ref[...]

    def bwd_chunk(i, carry):
        for s in range(n_seq):
            bwd_seq(nc - 1 - i, s)
        return carry

    lax.fori_loop(0, nc, bwd_chunk, 0, unroll=SSD_UNROLL)


def _ssd(latent, z, xbc, dt, consts, h0f, h0b):
    bsz, seq, n_seq = (DEC_BATCH, DEC_SEQ, 1) if latent else (BATCH, SEQ, 2)
    rows = n_seq * seq
    first = N_CTX // rows if latent else 0
    per_seq = lambda w: pl.BlockSpec((rows, w), lambda b: (first + b, 0))
    const_specs = [pl.BlockSpec(a.shape, lambda b, n=a.ndim: (0,) * n) for a in consts]
    st_spec = pl.BlockSpec((n_seq, D_HEADS, D_STATE, D_HEADDIM), lambda b: (b, 0, 0, 0))
    st_shape = jax.ShapeDtypeStruct((bsz, D_HEADS, D_STATE, D_HEADDIM), F32)
    return pl.pallas_call(
        _ssd_kernel,
        out_shape=[jax.ShapeDtypeStruct((bsz * seq, D_INNER), F32), st_shape, st_shape],
        grid=(bsz // n_seq,),
        in_specs=[per_seq(256), per_seq(512), per_seq(128)] + const_specs + [st_spec, st_spec],
        out_specs=[pl.BlockSpec((rows, D_INNER), lambda b: (b, 0)), st_spec, st_spec],
        scratch_shapes=[pltpu.VMEM((rows, CONV_DIM), F32), pltpu.VMEM((rows, LANES), F32),
                        pltpu.VMEM((rows, LANES), F32), pltpu.VMEM((rows, D_INNER), F32)],
        compiler_params=_params("parallel"),
        name="ssd_lat" if latent else "ssd_ctx",
    )(z, xbc, dt, *consts, h0f, h0b)


def _out_proj_kernel(*refs):
    pairs, (mod_ref, wo_ref, g2_ref, rwh_ref, rwl_ref, rb_ref, x1_ref, h2_ref, te_ref, gt_ref, cnt_ref) = \
        refs[:10], refs[10:]
    x, oa, ob, oc, od = [_pick(pairs[2 * j], pairs[2 * j + 1], PROJ_TM) for j in range(5)]
    m = mod_ref[0]
    gate1, sh2, sc2 = m[2:3], m[3:4], m[4:5]
    mixed = (_dot(oa.astype(BF16), wo_ref[0:256, :])
             + _dot(ob.astype(BF16), wo_ref[256:512, :])
             + _dot(oc.astype(BF16), wo_ref[512:768, :])
             + _dot(od.astype(BF16), wo_ref[768:1024, :]))
    x1 = x + gate1 * mixed
    x1_ref[...] = x1
    ms = jnp.mean(x1 * x1, axis=-1, keepdims=True)
    h2 = (x1 * lax.rsqrt(ms + EPS) * g2_ref[...]) * (1.0 + sc2) + sh2
    h2_ref[...] = h2.astype(BF16)

    hi = h2.astype(BF16)
    lo = (h2 - hi.astype(F32)).astype(BF16)
    logits = _dot(hi, rwh_ref[...]) + _dot(lo, rwh_ref[...]) + _dot(hi, rwl_ref[...]) + rb_ref[...]
    lane = lax.broadcasted_iota(jnp.int32, logits.shape, 1)
    vals, idxs = [], []
    for _ in range(TOP_K):
        mx = jnp.max(logits, axis=-1, keepdims=True)
        ix = jnp.min(jnp.where(logits == mx, lane, LANES), axis=-1, keepdims=True)
        vals.append(mx)
        idxs.append(ix)
        logits = jnp.where(lane == ix, -3e38, logits)
    es = [jnp.exp(v - vals[0]) for v in vals]
    den = es[0] + es[1] + es[2] + es[3]
    te = jnp.zeros(lane.shape, jnp.int32)
    gt = jnp.zeros(lane.shape, F32)
    member = jnp.zeros(lane.shape, F32)
    for k in range(TOP_K):
        te = jnp.where(lane == k, idxs[k], te)
        gt = jnp.where(lane == k, es[k] / den, gt)
        member = jnp.where(lane == idxs[k], 1.0, member)
    te_ref[...] = te[:, 0:TOP_K]
    gt_ref[...] = gt[:, 0:TOP_K]
    for t in range(PROJ_TM // TM):
        count = jnp.sum(member[t * TM:(t + 1) * TM], axis=0, keepdims=True)
        cnt_ref[t] = jnp.broadcast_to(count, (8, LANES)).astype(jnp.int32)


def _out_proj(x, oa, ob, oc, od, mod3, wo, g2, rwh, rwl, rb):
    row = lambda w: pl.BlockSpec((PROJ_TM, w), lambda i: (i, 0))
    full = lambda a: pl.BlockSpec(a.shape, lambda i: (0,) * a.ndim)
    return pl.pallas_call(
        _out_proj_kernel,
        out_shape=[jax.ShapeDtypeStruct((N_TOK, D_MODEL), F32), jax.ShapeDtypeStruct((N_TOK, D_MODEL), BF16),
                   jax.ShapeDtypeStruct((N_TOK, TOP_K), jnp.int32), jax.ShapeDtypeStruct((N_TOK, TOP_K), F32),
                   jax.ShapeDtypeStruct((N_TILES, 8, LANES), jnp.int32)],
        grid=(N_TOK // PROJ_TM,),
        in_specs=_pair_specs(D_MODEL, PROJ_TM) + [s for _ in range(4) for s in _pair_specs(256, PROJ_TM)]
                 + [pl.BlockSpec((1, N_MOD, D_MODEL), lambda i: (_mod_row(i, PROJ_TM), 0, 0)),
                    full(wo), full(g2), full(rwh), full(rwl), full(rb)],
        out_specs=[row(D_MODEL), row(D_MODEL), row(TOP_K), row(TOP_K),
                   pl.BlockSpec((PROJ_TM // TM, 8, LANES), lambda i: (i, 0, 0))],
        compiler_params=_params("parallel"),
        name="out_proj_router",
    )(*x, *oa, *ob, *oc, *od, mod3, wo, g2, rwh, rwl, rb)


def _tile_rows(te, tri_ref, off_ref):
    lane = lax.broadcasted_iota(jnp.int32, (TM, LANES), 1)
    hits = [lane == te[:, k:k + 1] for k in range(TOP_K)]
    member = jnp.zeros((TM, LANES), F32)
    for hit in hits:
        member = jnp.where(hit, 1.0, member)
    rank = _dot(tri_ref[...], member.astype(BF16))
    pos = rank + off_ref[0][0:1, :].astype(F32)
    return [jnp.sum(jnp.where(hit, pos, 0.0), axis=-1, keepdims=True) for hit in hits]


def _segment_chunks(seg_ref, len_ref, tile, visit):
    def segment(e, row):
        length = len_ref[tile * N_EXPERTS + e]
        start = seg_ref[tile * N_EXPERTS + e]

        def chunk(c, carry):
            visit(CHUNKS[0], pl.multiple_of(row + c * CHUNKS[0], SEG_ALIGN),
                  pl.multiple_of(start + c * CHUNKS[0], SEG_ALIGN))
            return carry

        lax.fori_loop(0, lax.shift_right_logical(length, CHUNK_SHIFT), chunk, 0)
        for size in CHUNKS[1:]:
            done = jnp.bitwise_and(length, -2 * size)

            @pl.when(jnp.bitwise_and(length, size) != 0)
            def _():
                visit(size, pl.multiple_of(row + done, SEG_ALIGN), pl.multiple_of(start + done, SEG_ALIGN))
        return row + length

    row = 0
    for e in range(N_EXPERTS):
        row = segment(e, row)


def _wait_chunks(nchunk_ref, tile, copy):
    for j, size in enumerate(CHUNKS):
        lax.fori_loop(0, nchunk_ref[tile * len(CHUNKS) + j], lambda c, carry, size=size: (copy(size).wait(), carry)[1], 0)


def _dispatch_kernel(seg_ref, len_ref, nchunk_ref, end_ref, nu_ref, h2_ref, te_ref, off_ref, tri_ref,
                     xb_ref, buf, sems, sem_z):
    i = pl.program_id(0)
    slot = i % 2

    def chunk_copy(size, s, buf_row, xb_row):
        return pltpu.make_async_copy(buf.at[s, pl.ds(buf_row, size)], xb_ref.at[pl.ds(xb_row, size)], sems.at[s])

    def wait_chunks(tile, s):
        _wait_chunks(nchunk_ref, tile, lambda size: chunk_copy(size, s, 0, 0))

    @pl.when(i == 0)
    def _():
        buf[1, 0:ZERO_ROWS, :] = jnp.zeros((ZERO_ROWS, D_MODEL), F32)
        tail = lambda e: pltpu.make_async_copy(
            buf.at[1, pl.ds(0, ZERO_ROWS)],
            xb_ref.at[pl.ds(pl.multiple_of(jnp.maximum(end_ref[e] - ZERO_ROWS, 0), SEG_ALIGN), ZERO_ROWS)], sem_z)
        block = lambda b: pltpu.make_async_copy(
            buf.at[1, pl.ds(0, MOE_BM)], xb_ref.at[pl.ds(pl.multiple_of(b * MOE_BM, MOE_BM), MOE_BM)], sem_z)
        for e in range(N_EXPERTS):
            tail(e).start()
        lax.fori_loop(nu_ref[0], MOE_BLOCKS, lambda b, carry: (block(b).start(), carry)[1], 0)
        for e in range(N_EXPERTS):
            tail(e).wait()
        lax.fori_loop(nu_ref[0], MOE_BLOCKS, lambda b, carry: (block(b).wait(), carry)[1], 0)

    rows = _tile_rows(te_ref[...], tri_ref, off_ref)
    lane = lax.broadcasted_iota(jnp.int32, (TM, LANES), 1)
    packed = jnp.zeros((TM, LANES), F32)
    for k in range(TOP_K):
        packed = jnp.where(lane == k, rows[k], packed)
    rows_t = packed.T
    buf_row = lax.broadcasted_iota(jnp.int32, (TILE_BUF, TM), 0).astype(F32)
    pick = jnp.zeros((TILE_BUF, TM), F32)
    for k in range(TOP_K):
        pick = jnp.where(buf_row == rows_t[k:k + 1, :], 1.0, pick)
    buf[slot] = _dot(pick.astype(BF16), h2_ref[...])

    @pl.when(i > 0)
    def _():
        wait_chunks(i - 1, 1 - slot)

    _segment_chunks(seg_ref, len_ref, i, lambda size, b, x: chunk_copy(size, slot, b, x).start())

    @pl.when(i == N_TILES - 1)
    def _():
        wait_chunks(i, slot)


def _dispatch(seg_start, seg_len, n_chunk, pad_end, n_used, h2, te, seg_off, tri):
    row = lambda w: pl.BlockSpec((TM, w), lambda i, *_: (i, 0))
    grid_spec = pltpu.PrefetchScalarGridSpec(
        num_scalar_prefetch=5,
        grid=(N_TILES,),
        in_specs=[row(D_MODEL), row(TOP_K), pl.BlockSpec((1, 8, LANES), lambda i, *_: (i, 0, 0)),
                  pl.BlockSpec(tri.shape, lambda i, *_: (0, 0))],
        out_specs=pl.BlockSpec(memory_space=pl.ANY),
        scratch_shapes=[pltpu.VMEM((2, TILE_BUF, D_MODEL), F32),
                        pltpu.SemaphoreType.DMA((2,)), pltpu.SemaphoreType.DMA],
    )
    return pl.pallas_call(
        _dispatch_kernel,
        out_shape=jax.ShapeDtypeStruct((YB_ROWS, D_MODEL), F32),
        grid_spec=grid_spec,
        compiler_params=_params("arbitrary"),
        name="moe_dispatch",
    )(seg_start, seg_len, n_chunk, pad_end, n_used, h2, te, seg_off, tri)


def _expert_kernel(layer, be_ref, nu_ref, slot_ref, next_ref, x_ref, wgu_hbm, bgu_ref, wdn_hbm, bdn_ref,
                   o_ref, wgu_f, wdn_f, wgu_s, wdn_s, sems):
    i = pl.program_id(0)
    used = i < nu_ref[0]
    expert = be_ref[i]
    fresh = jnp.logical_or(i == 0, expert != be_ref[jnp.maximum(i - 1, 0)])
    slot = slot_ref[expert]

    def fetch(e, s):
        return (pltpu.make_async_copy(wgu_hbm.at[layer, e], wgu_f.at[s], sems.at[0, s]),
                pltpu.make_async_copy(wdn_hbm.at[layer, e], wdn_f.at[s], sems.at[1, s]))

    @pl.when(i == 0)
    def _():
        for cp in fetch(expert, slot):
            cp.start()

    @pl.when(jnp.logical_and(used, fresh))
    def _():
        for cp in fetch(expert, slot):
            cp.wait()
        nxt = next_ref[expert]

        @pl.when(nxt >= 0)
        def _():
            for cp in fetch(nxt, 1 - slot):
                cp.start()

        wgu_s[...] = wgu_f[slot].astype(BF16)
        wdn_s[...] = wdn_f[slot].astype(BF16)

    @pl.when(used)
    def _():
        hgu = _dot(x_ref[...].astype(BF16), wgu_s[...]) + bgu_ref[0, 0]
        gate = jnp.minimum(hgu[:, :D_FF], SWIGLU_LIMIT)
        up = jnp.clip(hgu[:, D_FF:], -SWIGLU_LIMIT, SWIGLU_LIMIT)
        act = (up + 1.0) * gate * jax.nn.sigmoid(SWIGLU_ALPHA * gate)
        o_ref[...] = _dot(act.astype(BF16), wdn_s[...]) + bdn_ref[0, 0]

    @pl.when(jnp.logical_not(used))
    def _():
        o_ref[...] = jnp.zeros_like(o_ref)


def _experts(layer, block_e, n_used, w_slot, next_e, xb, w_gu, b_gu, w_dn, b_dn):
    grid_spec = pltpu.PrefetchScalarGridSpec(
        num_scalar_prefetch=4,
        grid=(MOE_BLOCKS,),
        in_specs=[pl.BlockSpec((MOE_BM, D_MODEL), lambda i, be, *_: (i, 0)),
                  pl.BlockSpec(memory_space=pl.ANY),
                  pl.BlockSpec((1, 1, 1, 2 * D_FF), lambda i, be, *_: (layer, be[i], 0, 0)),
                  pl.BlockSpec(memory_space=pl.ANY),
                  pl.BlockSpec((1, 1, 1, D_MODEL), lambda i, be, *_: (layer, be[i], 0, 0))],
        out_specs=pl.BlockSpec((MOE_BM, D_MODEL), lambda i, be, *_: (i, 0)),
        scratch_shapes=[pltpu.VMEM((2, D_MODEL, 2 * D_FF), F32), pltpu.VMEM((2, D_FF, D_MODEL), F32),
                        pltpu.VMEM((D_MODEL, 2 * D_FF), BF16), pltpu.VMEM((D_FF, D_MODEL), BF16),
                        pltpu.SemaphoreType.DMA((2, 2))],
    )
    return pl.pallas_call(
        functools.partial(_expert_kernel, layer),
        out_shape=jax.ShapeDtypeStruct((YB_ROWS, D_MODEL), F32),
        grid_spec=grid_spec,
        compiler_params=_params("arbitrary"),
        name="experts",
    )(block_e, n_used, w_slot, next_e, xb, w_gu, b_gu.reshape(DEPTH, N_EXPERTS, 1, 2 * D_FF), w_dn,
      b_dn.reshape(DEPTH, N_EXPERTS, 1, D_MODEL))


def _combine_kernel(final, seg_ref, len_ref, nchunk_ref, x1_ref, te_ref, gt_ref, off_ref, mod_ref, fg_ref,
                    tri_ref, yb_ref, o_ctx_ref, o_lat_ref, buf, sems):
    i = pl.program_id(0)
    slot = i % 2

    def chunk_copy(size, s, buf_row, yb_row):
        return pltpu.make_async_copy(yb_ref.at[pl.ds(yb_row, size)], buf.at[s, pl.ds(buf_row, size)], sems.at[s])

    def request(tile, s):
        _segment_chunks(seg_ref, len_ref, tile, lambda size, b, y: chunk_copy(size, s, b, y).start())

    @pl.when(i == 0)
    def _():
        buf[...] = jnp.zeros_like(buf)
        request(0, 0)

    @pl.when(i + 1 < N_TILES)
    def _():
        request(i + 1, 1 - slot)

    rows = _tile_rows(te_ref[...], tri_ref, off_ref)
    gt = gt_ref[...]
    buf_row = lax.broadcasted_iota(jnp.int32, (TM, TILE_BUF), 1).astype(F32)
    place = jnp.zeros((TM, TILE_BUF), F32)
    for k in range(TOP_K):
        place = jnp.where(buf_row == rows[k], gt[:, k:k + 1], place)

    _wait_chunks(nchunk_ref, i, lambda size: chunk_copy(size, slot, 0, 0))
    y = _dot(place.astype(BF16), buf[slot].astype(BF16))
    x2 = x1_ref[...] + mod_ref[0][5:6] * y
    if final:
        ms = jnp.mean(x2 * x2, axis=-1, keepdims=True)
        x2 = x2 * lax.rsqrt(ms + EPS) * fg_ref[...]

    @pl.when(i < CTX_TILES)
    def _():
        o_ctx_ref[...] = x2

    @pl.when(i >= CTX_TILES)
    def _():
        o_lat_ref[...] = x2


def _combine(seg_start, seg_len, n_chunk, x1, te, gates, seg_off, mod3, fg, tri, yb, final):
    row = lambda w: pl.BlockSpec((TM, w), lambda i, *_: (i, 0))
    full = lambda a: pl.BlockSpec(a.shape, lambda i, *_: (0,) * a.ndim)
    out_shape = [jax.ShapeDtypeStruct((N_CTX, D_MODEL), F32), jax.ShapeDtypeStruct((N_LAT, D_MODEL), F32)]
    out_specs = _pair_specs(D_MODEL, TM)
    grid_spec = pltpu.PrefetchScalarGridSpec(
        num_scalar_prefetch=3,
        grid=(N_TILES,),
        in_specs=[row(D_MODEL), row(TOP_K), row(TOP_K),
                  pl.BlockSpec((1, 8, LANES), lambda i, *_: (i, 0, 0)),
                  pl.BlockSpec((1, N_MOD, D_MODEL), lambda i, *_: (_mod_row(i), 0, 0)),
                  full(fg), full(tri),
                  pl.BlockSpec(memory_space=pl.ANY)],
        out_specs=out_specs,
        scratch_shapes=[pltpu.VMEM((2, TILE_BUF, D_MODEL), F32), pltpu.SemaphoreType.DMA((2,))],
    )
    return pl.pallas_call(
        functools.partial(_combine_kernel, final),
        out_shape=out_shape,
        grid_spec=grid_spec,
        compiler_params=_params("arbitrary"),
        name="moe_combine",
    )(seg_start, seg_len, n_chunk, x1, te, gates, seg_off, mod3, fg, tri, yb)


def _gqa_head_order(w, axis):
    take = lambda start: lax.slice_in_dim(w, start, start + HEAD_DIM, axis=axis)
    return [take(kv * 128 + g * 64) for g in range(2) for kv in range(2)]


def _w1_layout(w_in):
    cols = lambda name, n: w_in[:, IN_OFF[name]:IN_OFF[name] + n]
    zeros = lambda n: jnp.zeros((D_MODEL, n), F32)
    pieces = (_gqa_head_order(w_in, 1) + [cols('a_k', 256)]
              + [cols('b_cq', 192), zeros(64), cols('b_ckv', 128), cols('b_kr', 32), zeros(96)]
              + [cols('c_q', 768), cols('d_z', 256), cols('d_xbc', 512), cols('d_dtf', 8), zeros(120)])
    return jnp.concatenate(pieces, axis=1).astype(BF16)


def _rope_tables():
    t = np.arange(DEC_SEQ)
    pos = ((t // GRID_W).astype(np.float32), (t % GRID_W).astype(np.float32))

    def unit(rot_dim):
        quarter, half = rot_dim // 4, rot_dim // 2
        inv = ROPE_THETA ** (-np.arange(0, half, 2, dtype=np.float32) / half)
        cos = np.zeros((DEC_SEQ, rot_dim), np.float32)
        sin = np.zeros((DEC_SEQ, rot_dim), np.float32)
        for seg in range(4):
            ang = pos[seg // 2][:, None] * inv[None, :].astype(np.float32)
            cos[:, seg * quarter:(seg + 1) * quarter] = np.cos(ang)
            sin[:, seg * quarter:(seg + 1) * quarter] = np.sin(ang) * (-1.0 if seg % 2 == 0 else 1.0)
        return cos, sin

    log2e = math.log2(math.e)
    specs = [
        (HEAD_DIM, (0, 64), (), HEAD_DIM ** -0.5 * log2e),
        (HEAD_DIM, (0, 64), (), 1.0),
        (MLA_ROPE, (MLA_NOPE,), (0, MLA_NOPE), (MLA_NOPE + MLA_ROPE) ** -0.5 * log2e),
        (MLA_ROPE, (0,), (), 1.0),
        (DIFF_D, (0, 32, 64, 96), (), DIFF_D ** -0.5 * log2e),
        (DIFF_D, (0, 32, 64, 96), (), 1.0),
    ]
    lat_cols, ident_cols = [], []
    for rot_dim, starts, passthrough, scale in specs:
        ucos, usin = unit(rot_dim)
        cos = np.zeros((DEC_SEQ, LANES), np.float32)
        sin = np.zeros((DEC_SEQ, LANES), np.float32)
        ident = np.zeros((1, LANES), np.float32)
        if passthrough:
            cos[:, passthrough[0]:passthrough[1]] = 1.0
            ident[:, passthrough[0]:passthrough[1]] = 1.0
        for s in starts:
            cos[:, s:s + rot_dim] = ucos
            sin[:, s:s + rot_dim] = usin
            ident[:, s:s + rot_dim] = 1.0
        lat_cols += [cos * scale, sin * scale]
        ident_cols += [ident * scale, np.zeros((1, LANES), np.float32)]
    lat = np.concatenate(lat_cols, axis=1)
    ident_blk = np.broadcast_to(np.concatenate(ident_cols, axis=1), (PROJ_TM, N_TAB * LANES))
    return np.concatenate([ident_blk, lat], axis=0).astype(np.float32)


def _block_diag_ones(n, blk):
    r = np.arange(n)
    return (r[:, None] // blk == r[None, :] // blk).astype(np.float32)


def _chunk_tables(seg_len):
    counts = [jnp.sum(seg_len // CHUNKS[0], axis=1)]
    counts += [jnp.sum(seg_len % (2 * size) // size, axis=1) for size in CHUNKS[1:]]
    return jnp.stack(counts, axis=1).reshape(-1).astype(jnp.int32)


def _strict_lower_ones(n):
    r = np.arange(n)
    return (r[None, :] < r[:, None]).astype(np.float32)


def kernel(x_prompt, x_sample, cache_gqa_k, cache_gqa_v, cache_mla_ckv, cache_mla_krope, cache_diff_k, cache_diff_v, state_ssd_fwd, state_ssd_bwd, c, c_ctx, norm1_g, norm2_g, w_ada, b_ada, w_in, w_out, gqa_qn_g, gqa_kn_g, mla_qa_g, mla_wqb, mla_kva_g, mla_wkvb, diff_lq1, diff_lk1, diff_lq2, diff_lk2, diff_subln_g, ssd_conv_w, ssd_conv_b, ssd_a_log_f, ssd_a_log_b, ssd_dt_bias_f, ssd_dt_bias_b, ssd_d, ssd_norm_g, router_w, router_b, moe_w_gu, moe_b_gu, moe_w_dn, moe_b_dn, final_g):
    tabs = jnp.asarray(_rope_tables())
    bd = jnp.asarray(_block_diag_ones(256, HEAD_DIM), BF16)
    tri = jnp.asarray(_strict_lower_ones(TM), BF16)

    cvec = jnp.zeros((MOD_ROWS, D_MODEL), F32).at[0].set(c_ctx).at[1:1 + DEC_BATCH].set(c)
    mod = _modulation(cvec, w_ada, b_ada).reshape(DEPTH, MOD_ROWS, N_MOD, D_MODEL)

    x = (x_prompt.reshape(N_CTX, D_MODEL), x_sample.reshape(N_LAT, D_MODEL))
    new_ctx = []
    for l in range(DEPTH):
        mod3 = mod[l]
        w1 = _w1_layout(w_in[l])
        wqb = mla_wqb[l].reshape(Q_LORA, B_HEADS, MLA_NOPE + MLA_ROPE)
        wqb = jnp.pad(wqb, ((0, 256 - Q_LORA), (0, 0), (0, LANES - MLA_NOPE - MLA_ROPE)))
        wqb = wqb.reshape(256, B_HEADS * LANES).astype(BF16)
        wkvb = mla_wkvb[l].reshape(KV_LORA, B_HEADS, MLA_NOPE + MLA_V)
        wk_nope = jnp.pad(wkvb[:, :, :MLA_NOPE], ((0, 0), (0, 0), (0, LANES - MLA_NOPE)))
        eye_r = jnp.zeros((LANES, B_HEADS, LANES), F32)
        eye_r = eye_r.at[jnp.arange(MLA_ROPE), :, MLA_NOPE + jnp.arange(MLA_ROPE)].set(1.0)
        wk = jnp.concatenate([wk_nope, eye_r], axis=0).reshape(256, B_HEADS * LANES).astype(BF16)
        wv = wkvb[:, :, MLA_NOPE:].reshape(KV_LORA, B_HEADS * MLA_V).astype(BF16)
        wo = jnp.concatenate(_gqa_head_order(w_out[l], 0) + [w_out[l][256:]], axis=0).astype(BF16)
        gq = jnp.tile(gqa_qn_g[l], 4)[None, :]
        gk = jnp.tile(gqa_kn_g[l], 2)[None, :]
        gqa = jnp.pad(mla_qa_g[l], (0, 256 - Q_LORA))[None, :]
        gkva = mla_kva_g[l][None, :]
        rw = jnp.pad(router_w[l], ((0, 0), (0, LANES - N_EXPERTS)))
        rwh = rw.astype(BF16)
        rwl = (rw - rwh.astype(F32)).astype(BF16)
        rb = jnp.pad(router_b[l], (0, LANES - N_EXPERTS), constant_values=NEG_BIG)[None, :]

        qa, kva, qb, ck, qc, kc, vc, z, xbc, dt = _in_proj(
            x, mod3, norm1_g[l][None, :], w1, bd, gq, gk, gqa, gkva, wqb, tabs)

        past_kva = jnp.concatenate([cache_gqa_k[:, l].reshape(DEC_BATCH, PAST_LEN, 128),
                                    cache_gqa_v[:, l].reshape(DEC_BATCH, PAST_LEN, 128)], axis=-1)
        past_ck = jnp.concatenate([cache_mla_ckv[:, l], cache_mla_krope[:, l],
                                   jnp.zeros((DEC_BATCH, PAST_LEN, LANES - MLA_ROPE), F32)], axis=-1)
        past_kc = cache_diff_k[:, l].reshape(DEC_BATCH, PAST_LEN, 256)
        past_vc = cache_diff_v[:, l].reshape(DEC_BATCH, PAST_LEN, 256)
        lams = [a[l][None, :] for a in (diff_lq1, diff_lk1, diff_lq2, diff_lk2)]
        gsub = jnp.tile(diff_subln_g[l], 2)[None, :]
        lam_init = 0.8 - 0.6 * math.exp(-0.3 * l)
        oa, ob, oc = _attention_mixers(qa, kva, qb, ck, qc, kc, vc, past_kva, past_ck, past_kc, past_vc,
                                       wk, wv, lams, gsub, lam_init)

        pad8 = lambda f, b: jnp.pad(jnp.concatenate([f, b]), (0, LANES - 2 * D_HEADS))[None, :]
        ssd_consts = (ssd_conv_w[l], ssd_conv_b[l][None, :], pad8(ssd_dt_bias_f[l], ssd_dt_bias_b[l]),
                      pad8(ssd_a_log_f[l], ssd_a_log_b[l]), jnp.repeat(ssd_d[l], D_HEADDIM)[None, :],
                      ssd_norm_g[l][None, :])
        zeros_st = jnp.zeros((BATCH, D_HEADS, D_STATE, D_HEADDIM), F32)
        od_c, hf_c, hb_c = _ssd(False, z, xbc, dt, ssd_consts, zeros_st, zeros_st)
        od_l, _, _ = _ssd(True, z, xbc, dt, ssd_consts, jnp.swapaxes(state_ssd_fwd[:, l], -1, -2),
                          jnp.swapaxes(state_ssd_bwd[:, l], -1, -2))
        od = (od_c, od_l)

        x1, h2, te, gates, tile_cnt = _out_proj(x, oa, ob, oc, od, mod3, wo, norm2_g[l][None, :], rwh, rwl, rb)

        seg_cnt = tile_cnt[:, 0, :N_EXPERTS]
        seg_len = (seg_cnt + SEG_ALIGN - 1) // SEG_ALIGN * SEG_ALIGN
        region = (jnp.sum(seg_len, axis=0) + MOE_BM - 1) // MOE_BM * MOE_BM
        pad_end = jnp.cumsum(region).astype(jnp.int32)
        pad_start = pad_end - region
        seg_start = (pad_start[None, :] + jnp.cumsum(seg_len, axis=0) - seg_len).astype(jnp.int32)
        n_used = (pad_end[-1] // MOE_BM).astype(jnp.int32).reshape(1)
        blk_start = jnp.arange(MOE_BLOCKS, dtype=jnp.int32) * MOE_BM
        block_e = jnp.minimum(jnp.sum((pad_end[None, :] <= blk_start[:, None]).astype(jnp.int32), axis=1),
                              N_EXPERTS - 1).astype(jnp.int32)
        n_chunk = _chunk_tables(seg_len)
        seg_off = jnp.cumsum(seg_len, axis=1) - seg_len
        seg_off = jnp.broadcast_to(jnp.pad(seg_off, ((0, 0), (0, LANES - N_EXPERTS)))[:, None, :],
                                   (N_TILES, 8, LANES)).astype(jnp.int32)
        seg_start, seg_len = seg_start.reshape(-1), seg_len.reshape(-1).astype(jnp.int32)

        xb = _dispatch(seg_start, seg_len, n_chunk, pad_end, n_used, h2, te, seg_off, tri)
        has_rows = region > 0
        w_slot = ((jnp.cumsum(has_rows.astype(jnp.int32)) - 1) % 2).astype(jnp.int32)
        later = lax.cummin(jnp.where(has_rows, jnp.arange(N_EXPERTS, dtype=jnp.int32), N_EXPERTS), reverse=True)
        later = jnp.concatenate([later[1:], jnp.full((1,), N_EXPERTS, jnp.int32)])
        next_e = jnp.where(later < N_EXPERTS, later, -1).astype(jnp.int32)
        yb = _experts(l, block_e, n_used, w_slot, next_e, xb, moe_w_gu, moe_b_gu, moe_w_dn, moe_b_dn)
        x = _combine(seg_start, seg_len, n_chunk, x1, te, gates, seg_off, mod3, final_g[None, :], tri, yb,
                     l == DEPTH - 1)

        kva_c, ck_c = kva[:N_CTX], ck[:N_CTX]
        new_ctx.append((kva_c[:, 0:128].reshape(BATCH, SEQ, A_KV_HEADS, HEAD_DIM),
                        kva_c[:, 128:256].reshape(BATCH, SEQ, A_KV_HEADS, HEAD_DIM),
                        ck_c[:, 0:KV_LORA].reshape(BATCH, SEQ, KV_LORA),
                        ck_c[:, KV_LORA:KV_LORA + MLA_ROPE].reshape(BATCH, SEQ, MLA_ROPE),
                        kc[:N_CTX].reshape(BATCH, SEQ, C_HEADS, 2 * DIFF_D),
                        vc[:N_CTX].reshape(BATCH, SEQ, C_HEADS, DIFF_V),
                        jnp.swapaxes(hf_c, -1, -2), jnp.swapaxes(hb_c, -1, -2)))

    y_prompt = x[0].reshape(BATCH, SEQ, D_MODEL)
    y_sample = x[1].reshape(DEC_BATCH, DEC_SEQ, D_MODEL)
    caches = [jnp.stack([cl[i] for cl in new_ctx], axis=1) for i in range(8)]
    return (y_prompt, y_sample, *caches)
```

```python
import functools
import math

import numpy as np
import jax
import jax.numpy as jnp
from jax import lax
from jax.experimental import pallas as pl
from jax.experimental.pallas import tpu as pltpu

F32 = jnp.float32
BF16 = jnp.bfloat16

D_MODEL = 1024
BATCH = 16
SEQ = 256
DEPTH = 2
DEC_BATCH = 8
DEC_SEQ = 2048
PAST_LEN = 256
GRID_W = 64
ROPE_THETA = 10000.0
EPS = 1e-6
HEAD_DIM = 64
A_HEADS = 4
A_KV_HEADS = 2
B_HEADS = 4
MLA_NOPE = 64
MLA_ROPE = 32
MLA_V = 64
Q_LORA = 192
KV_LORA = 128
C_HEADS = 4
DIFF_D = 32
DIFF_V = 64
SUBLN_EPS = 1e-5
D_HEADS = 4
D_HEADDIM = 64
D_INNER = 256
D_GROUPS = 2
D_STATE = 64
CONV_DIM = 512
SSD_CHUNK = 128
N_EXPERTS = 32
TOP_K = 4
D_FF = 1024
SWIGLU_ALPHA = 1.702
SWIGLU_LIMIT = 7.0
N_MOD = 6

N_CTX = BATCH * SEQ
N_LAT = DEC_BATCH * DEC_SEQ
N_TOK = N_CTX + N_LAT

LANES = 128
TM = 256
N_TILES = N_TOK // TM
CTX_TILES = N_CTX // TM
LAT_TQ = 1024
PROJ_TM = 512
MOD_ROWS = 16
MOE_BM = 512
N_PAIRS = N_TOK * TOP_K
SEG_ALIGN = 8
CHUNKS = (32, 16, 8)
CHUNK_SHIFT = 5
MOE_BLOCKS = -(-(N_PAIRS + N_TILES * N_EXPERTS * (SEG_ALIGN - 1) + N_EXPERTS * (MOE_BM - 1)) // MOE_BM)
YB_ROWS = MOE_BLOCKS * MOE_BM
TILE_BUF = -(-(TM * TOP_K + N_EXPERTS * (SEG_ALIGN - 1)) // LANES) * LANES
ZERO_ROWS = MOE_BM
VMEM_LIMIT = 56 * 1024 * 1024
NEG_BIG = -1e30

W1_COLS = 2688
IN_OFF = dict(a_q=0, a_k=256, a_v=384, b_cq=512, b_ckv=704, b_kr=832, c_q=864, c_k=1120,
              c_v=1376, d_z=1632, d_xbc=1888, d_dtf=2400, d_dtb=2404)
IN_WIDTH = 2408
N_TAB = 12


def _mod_row(i, tile=TM):
    return jnp.where(i < N_CTX // tile, 0, 1 + (i - N_CTX // tile) // (DEC_SEQ // tile))


def _tab_block(i, tile):
    return jnp.where(i < N_CTX // tile, 0, 1 + (i - N_CTX // tile) % (DEC_SEQ // tile))


def _pair_specs(width, tile):
    n_ctx = N_CTX // tile
    return [pl.BlockSpec((tile, width), lambda i, *_: (jnp.minimum(i, n_ctx - 1), 0)),
            pl.BlockSpec((tile, width), lambda i, *_: (jnp.maximum(i - n_ctx, 0), 0))]


def _pick(ctx_ref, lat_ref, tile):
    return jnp.where(pl.program_id(0) < N_CTX // tile, ctx_ref[...], lat_ref[...])


def _dot(a, b):
    return jnp.dot(a, b, preferred_element_type=F32)


def _dot_nt(a, b):
    return lax.dot_general(a, b, (((1,), (1,)), ((), ())), preferred_element_type=F32)


def _dot_split(x, m):
    hi = x.astype(BF16)
    lo = (x - hi.astype(F32)).astype(BF16)
    return _dot(hi, m) + _dot(lo, m)


def _dot_split_left(m, x):
    hi = x.astype(BF16)
    lo = (x - hi.astype(F32)).astype(BF16)
    return _dot(m, hi) + _dot(m, lo)


def _silu(x):
    return x * jax.nn.sigmoid(x)


def _params(*semantics):
    return pltpu.CompilerParams(dimension_semantics=semantics, vmem_limit_bytes=VMEM_LIMIT)


MOD_TN = 1536


def _mod_kernel(c_ref, w_ref, b_ref, o_ref):
    c = c_ref[...]
    s = _silu(c).astype(BF16)
    o_ref[0] = _dot(s, w_ref[0].astype(BF16)) + b_ref[0]


def _modulation(cvec, w_ada, b_ada):
    n = N_MOD * D_MODEL
    return pl.pallas_call(
        _mod_kernel,
        out_shape=jax.ShapeDtypeStruct((DEPTH, MOD_ROWS, n), F32),
        grid=(DEPTH, n // MOD_TN),
        in_specs=[pl.BlockSpec((MOD_ROWS, D_MODEL), lambda l, j: (0, 0)),
                  pl.BlockSpec((1, D_MODEL, MOD_TN), lambda l, j: (l, 0, j)),
                  pl.BlockSpec((1, 1, MOD_TN), lambda l, j: (l, 0, j))],
        out_specs=pl.BlockSpec((1, MOD_ROWS, MOD_TN), lambda l, j: (l, 0, j)),
        compiler_params=_params("parallel", "parallel"),
        name="adaln_mod",
    )(cvec, w_ada, b_ada.reshape(DEPTH, 1, n))


def _rope(x, cos, sin, quarter):
    lane = lax.broadcasted_iota(jnp.int32, (x.shape[0], LANES), 1)
    first = (lane // quarter) % 2 == 0
    outs = []
    for t in range(x.shape[1] // LANES):
        xt = x[:, t * LANES:(t + 1) * LANES]
        partner = jnp.where(first, pltpu.roll(xt, LANES - quarter, 1), pltpu.roll(xt, quarter, 1))
        outs.append(xt * cos + partner * sin)
    return outs


def _in_proj_kernel(xc_ref, xl_ref, mod_ref, g1_ref, w1_ref, bd_ref, gq_ref, gk_ref, gqa_ref, gkva_ref,
                    wqb_ref, tab_ref,
                    qa_ref, kva_ref, qb_ref, ck_ref, qc_ref, kc_ref, vc_ref, z_ref, xbc_ref, dt_ref):
    x = _pick(xc_ref, xl_ref, PROJ_TM)
    m = mod_ref[0]
    sh1, sc1 = m[0:1], m[1:2]
    ms = jnp.mean(x * x, axis=-1, keepdims=True)
    h = (x * lax.rsqrt(ms + EPS) * g1_ref[...]) * (1.0 + sc1) + sh1
    u = _dot(h.astype(BF16), w1_ref[...])

    def tab(k):
        return tab_ref[:, k * LANES:(k + 1) * LANES]

    bd = bd_ref[...]

    def head_norm(v, gain):
        w = v.shape[1]
        ss = _dot_split(v * v, bd[:w, :w])
        return v * lax.rsqrt(ss * (1.0 / HEAD_DIM) + EPS) * gain

    qa = _rope(head_norm(u[:, 0:256], gq_ref[...]), tab(0), tab(1), HEAD_DIM // 4)
    for t in range(2):
        qa_ref[:, t * LANES:(t + 1) * LANES] = qa[t].astype(BF16)
    ka = _rope(head_norm(u[:, 256:384], gk_ref[...]), tab(2), tab(3), HEAD_DIM // 4)
    kva_ref[:, 0:128] = ka[0]
    kva_ref[:, 128:256] = u[:, 384:512]

    cq = u[:, 512:768]
    msq = jnp.sum(cq * cq, axis=-1, keepdims=True) * (1.0 / Q_LORA)
    yq = cq * lax.rsqrt(msq + EPS) * gqa_ref[...]
    qb = _rope(_dot(yq.astype(BF16), wqb_ref[...]), tab(4), tab(5), MLA_ROPE // 4)
    for t in range(4):
        qb_ref[:, t * LANES:(t + 1) * LANES] = qb[t].astype(BF16)
    ckv = u[:, 768:896]
    msk = jnp.mean(ckv * ckv, axis=-1, keepdims=True)
    ck_ref[:, 0:128] = ckv * lax.rsqrt(msk + EPS) * gkva_ref[...]
    ck_ref[:, 128:256] = _rope(u[:, 896:1024], tab(6), tab(7), MLA_ROPE // 4)[0]

    qc = _rope(u[:, 1024:1280], tab(8), tab(9), DIFF_D // 4)
    kc = _rope(u[:, 1280:1536], tab(10), tab(11), DIFF_D // 4)
    for t in range(2):
        qc_ref[:, t * LANES:(t + 1) * LANES] = qc[t].astype(BF16)
        kc_ref[:, t * LANES:(t + 1) * LANES] = kc[t]
    vc_ref[...] = u[:, 1536:1792]

    z_ref[...] = u[:, 1792:2048]
    xbc_ref[...] = u[:, 2048:2560]
    dt_ref[...] = u[:, 2560:2688]


def _in_proj(x, mod3, g1, w1, bd, gq, gk, gqa, gkva, wqb, tabs):
    row = lambda w: pl.BlockSpec((PROJ_TM, w), lambda i: (i, 0))
    full = lambda a: pl.BlockSpec(a.shape, lambda i: (0,) * a.ndim)
    outs = [(256, BF16), (256, F32), (512, BF16), (256, F32), (256, BF16), (256, F32), (256, F32),
            (256, F32), (512, F32), (128, F32)]
    return pl.pallas_call(
        _in_proj_kernel,
        out_shape=[jax.ShapeDtypeStruct((N_TOK, w), d) for w, d in outs],
        grid=(N_TOK // PROJ_TM,),
        in_specs=_pair_specs(D_MODEL, PROJ_TM)
                 + [pl.BlockSpec((1, N_MOD, D_MODEL), lambda i: (_mod_row(i, PROJ_TM), 0, 0)),
                    full(g1), full(w1), full(bd), full(gq), full(gk), full(gqa), full(gkva), full(wqb),
                    pl.BlockSpec((PROJ_TM, N_TAB * LANES), lambda i: (_tab_block(i, PROJ_TM), 0))],
        out_specs=[row(w) for w, _ in outs],
        compiler_params=_params("parallel"),
        name="in_proj",
    )(*x, mod3, g1, w1, bd, gq, gk, gqa, gkva, wqb, tabs)


def _attend(q, k, v_ones):
    s = _dot_nt(q, k)
    e = jnp.exp2(s - jnp.max(s, axis=-1, keepdims=True))
    out = _dot(e.astype(BF16), v_ones)
    return out[:, :LANES] / out[:, LANES:]


def _with_ones(v):
    return jnp.concatenate([v, jnp.ones_like(v)], axis=-1)


def _half_mask(rows):
    lane = lax.broadcasted_iota(jnp.int32, (rows, LANES), 1)
    return lane < (LANES // 2)


def _keys(past_ref, new_ref, lo, hi):
    new = new_ref[:, lo:hi].astype(BF16)
    if past_ref is None:
        return new
    return jnp.concatenate([past_ref[0, :, lo:hi].astype(BF16), new], axis=0)


def _attn_a_kernel(latent, *refs):
    if latent:
        q_ref, kv_ref, past_ref, o_ref = refs
    else:
        (q_ref, kv_ref, o_ref), past_ref = refs, None
    q = q_ref[...]
    k = _keys(past_ref, kv_ref, 0, 128)
    v = _with_ones(_keys(past_ref, kv_ref, 128, 256))
    lo = _half_mask(q.shape[0])
    for g in range(2):
        qt = q[:, g * LANES:(g + 1) * LANES].astype(F32)
        res = []
        for half in range(2):
            qm = jnp.where(lo, qt, 0.0) if half == 0 else jnp.where(lo, 0.0, qt)
            res.append(_attend(qm.astype(BF16), k, v))
        o_ref[:, g * LANES:(g + 1) * LANES] = jnp.where(lo, res[0], res[1])


def _attn_b_kernel(latent, *refs):
    if latent:
        q_ref, ck_ref, past_ref, wk_ref, wv_ref, o_ref, k_s, v_s = refs
    else:
        (q_ref, ck_ref, wk_ref, wv_ref, o_ref, k_s, v_s), past_ref = refs, None

    @pl.when(pl.program_id(1) == 0)
    def _():
        ck = _keys(past_ref, ck_ref, 0, 256)
        k_s[...] = _dot(ck, wk_ref[...]).astype(BF16)
        v = _dot(ck[:, 0:128], wv_ref[...]).astype(BF16)
        for j in range(2):
            v_s[:, 2 * j * LANES:2 * (j + 1) * LANES] = _with_ones(v[:, j * LANES:(j + 1) * LANES])

    q = q_ref[...]
    lo = _half_mask(q.shape[0])
    for j in range(2):
        v = v_s[:, 2 * j * LANES:2 * (j + 1) * LANES]
        res = []
        for half in range(2):
            h = 2 * j + half
            res.append(_attend(q[:, h * LANES:(h + 1) * LANES], k_s[:, h * LANES:(h + 1) * LANES], v))
        o_ref[:, j * LANES:(j + 1) * LANES] = jnp.where(lo, res[0], res[1])


def _attn_c_kernel(lam_init, latent, *refs):
    if latent:
        q_ref, k_ref, v_ref, pk_ref, pv_ref, lq1_ref, lk1_ref, lq2_ref, lk2_ref, g_ref, o_ref = refs
    else:
        (q_ref, k_ref, v_ref, lq1_ref, lk1_ref, lq2_ref, lk2_ref, g_ref, o_ref), pk_ref, pv_ref = refs, None, None
    lam = (jnp.exp(jnp.sum(lq1_ref[...] * lk1_ref[...], axis=-1, keepdims=True))
           - jnp.exp(jnp.sum(lq2_ref[...] * lk2_ref[...], axis=-1, keepdims=True)) + lam_init)
    q = q_ref[...]
    rows = q.shape[0]
    lane = lax.broadcasted_iota(jnp.int32, (rows, LANES), 1)
    lo = lane < (LANES // 2)
    for j in range(2):
        qt = q[:, j * LANES:(j + 1) * LANES].astype(F32)
        k = _keys(pk_ref, k_ref, j * LANES, (j + 1) * LANES)
        v = _with_ones(_keys(pv_ref, v_ref, j * LANES, (j + 1) * LANES))
        res = []
        for half in range(2):
            parts = []
            for t in range(2):
                quarter = 2 * half + t
                qm = jnp.where(lane // (LANES // 4) == quarter, qt, 0.0)
                parts.append(_attend(qm.astype(BF16), k, v))
            res.append(parts[0] - lam * parts[1])
        o = jnp.where(lo, res[0], res[1])
        o2 = o * o
        ss_lo = jnp.sum(jnp.where(lo, o2, 0.0), axis=-1, keepdims=True)
        ss_hi = jnp.sum(jnp.where(lo, 0.0, o2), axis=-1, keepdims=True)
        ss = jnp.where(lo, ss_lo, ss_hi) * (1.0 / DIFF_V)
        o_ref[:, j * LANES:(j + 1) * LANES] = (o * lax.rsqrt(ss + SUBLN_EPS) * g_ref[...]) * (1.0 - lam_init)


def _seq_call(body, name, latent, q, news, pasts, consts, scratch=(), q_semantics="parallel"):
    const_specs = [pl.BlockSpec(a.shape, lambda b, i, n=a.ndim: (0,) * n) for a in consts]
    if latent:
        tq, rows = LAT_TQ, N_LAT
        tile = lambda b, i: (N_CTX // LAT_TQ + b * (DEC_SEQ // LAT_TQ) + i, 0)
        out_tile = lambda b, i: (b * (DEC_SEQ // LAT_TQ) + i, 0)
        grid = (DEC_BATCH, DEC_SEQ // LAT_TQ)
        new_specs = [pl.BlockSpec((DEC_SEQ, a.shape[1]), lambda b, i: (N_CTX // DEC_SEQ + b, 0)) for a in news]
        past_specs = [pl.BlockSpec((1, PAST_LEN, a.shape[2]), lambda b, i: (b, 0, 0)) for a in pasts]
    else:
        tq, rows = SEQ, N_CTX
        tile = out_tile = lambda b, i: (b, 0)
        grid = (BATCH, 1)
        new_specs = [pl.BlockSpec((SEQ, a.shape[1]), tile) for a in news]
        past_specs, pasts = [], ()
    return pl.pallas_call(
        functools.partial(body, latent),
        out_shape=jax.ShapeDtypeStruct((rows, 256), F32),
        grid=grid,
        in_specs=[pl.BlockSpec((tq, q.shape[1]), tile)] + new_specs + past_specs + const_specs,
        out_specs=pl.BlockSpec((tq, 256), out_tile),
        scratch_shapes=list(scratch),
        compiler_params=_params("parallel", q_semantics),
        name=name + ("_lat" if latent else "_ctx"),
    )(q, *news, *pasts, *consts)


def _attention_mixers(qa, kva, qb, ck, qc, kc, vc, past_kva, past_ck, past_kc, past_vc, wk, wv,
                      lams, gsub, lam_init):
    def both(body, name, q, news, pasts, consts, scratch_fn=None, q_semantics="parallel"):
        sc = (lambda lk: ()) if scratch_fn is None else scratch_fn
        return (_seq_call(body, name, False, q, news, (), consts, sc(SEQ), q_semantics),
                _seq_call(body, name, True, q, news, pasts, consts, sc(PAST_LEN + DEC_SEQ), q_semantics))

    oa = both(_attn_a_kernel, "attn_gqa", qa, [kva], [past_kva], [])
    ob = both(_attn_b_kernel, "attn_mla", qb, [ck], [past_ck], [wk, wv],
              lambda lk: (pltpu.VMEM((lk, 512), BF16), pltpu.VMEM((lk, 512), BF16)), "arbitrary")
    oc = both(functools.partial(_attn_c_kernel, lam_init), "attn_diff", qc, [kc, vc],
              [past_kc, past_vc], [*lams, gsub])
    return oa, ob, oc


Q = SSD_CHUNK
SSD_UNROLL = 2


def _ssd_kernel(z_ref, xbc_ref, dt_ref, cw_ref, cb_ref, dtb_ref, alog_ref, dvec_ref, ng_ref, h0f_ref, h0b_ref,
                out_ref, hf_ref, hb_ref, act_s, cum_s, dtv_s, y_s):
    n_seq = hf_ref.shape[0]
    seq = z_ref.shape[0] // n_seq
    nc = seq // Q
    row = lax.broadcasted_iota(jnp.int32, (Q, Q), 0)
    col = lax.broadcasted_iota(jnp.int32, (Q, Q), 1)
    lower = row >= col
    upper = row <= col
    tril = jnp.where(lower, 1.0, 0.0).astype(BF16)
    triu = jnp.where(upper, 1.0, 0.0).astype(BF16)
    rowc = lax.broadcasted_iota(jnp.int32, (Q, CONV_DIM), 0)
    lane = lax.broadcasted_iota(jnp.int32, (Q, LANES), 1)
    a_neg = -jnp.exp(alog_ref[...])
    cw = cw_ref[...]
    hf_ref[...] = h0f_ref[...]
    hb_ref[...] = h0b_ref[...]

    def fwd_seq(c, s):
        off = s * seq
        base = pl.multiple_of(off + c * Q, Q)
        x0 = xbc_ref[pl.ds(base, Q), :]
        prev = xbc_ref[pl.ds(pl.multiple_of(off + jnp.maximum(c * Q - 8, 0), 8), 8), :][7:8, :]
        nxt = xbc_ref[pl.ds(pl.multiple_of(off + jnp.minimum(c * Q + Q, seq - 8), 8), 8), :][0:1, :]
        prev = jnp.where(c > 0, prev, 0.0)
        nxt = jnp.where(c < nc - 1, nxt, 0.0)
        xm1 = jnp.where(rowc == 0, prev, pltpu.roll(x0, 1, 0))
        xp1 = jnp.where(rowc == Q - 1, nxt, pltpu.roll(x0, Q - 1, 0))
        act = _silu(xm1 * cw[0:1] + x0 * cw[1:2] + xp1 * cw[2:3] + cb_ref[...])
        act_s[pl.ds(base, Q), :] = act
        xs = act[:, 0:256]
        bm = act[:, 256:384]
        cm = act[:, 384:512]

        dtr = dt_ref[pl.ds(base, Q), :] + dtb_ref[...]
        dtv = jnp.maximum(dtr, 0.0) + jnp.log1p(jnp.exp(-jnp.abs(dtr)))
        dta = dtv * a_neg
        cum = jnp.where(lane < D_HEADS, _dot_split_left(tril, dta), _dot_split_left(triu, dta))
        cum_s[pl.ds(base, Q), :] = cum
        dtv_s[pl.ds(base, Q), :] = dtv
        cum_t = cum.T
        dtv_t = dtv.T
        bm_t = bm.T
        ys = []
        for h in range(D_HEADS):
            g = h // (D_HEADS // D_GROUPS)
            cg = cm[:, g * D_STATE:(g + 1) * D_STATE].astype(BF16)
            cb_mat = _dot_nt(cg, bm[:, g * D_STATE:(g + 1) * D_STATE].astype(BF16))
            cf = cum[:, h:h + 1]
            cb = cum[:, D_HEADS + h:D_HEADS + h + 1]
            l_f = jnp.exp(jnp.where(lower, cf - cum_t[h:h + 1, :], NEG_BIG))
            l_b = jnp.exp(jnp.where(upper, cb - cum_t[D_HEADS + h:D_HEADS + h + 1, :], NEG_BIG))
            mix = cb_mat * (l_f * dtv_t[h:h + 1, :] + l_b * dtv_t[D_HEADS + h:D_HEADS + h + 1, :])
            xh = xs[:, h * D_HEADDIM:(h + 1) * D_HEADDIM]
            y = _dot(mix.astype(BF16), xh.astype(BF16))
            state = hf_ref[s, h]
            y = y + _dot(cg, state.astype(BF16)) * jnp.exp(cf)
            y = y + dvec_ref[:, h * D_HEADDIM:(h + 1) * D_HEADDIM] * xh
            ys.append(y)
            last = cum[Q - 1:Q, h:h + 1]
            wgt = jnp.exp(last - cf) * dtv[:, h:h + 1]
            st = _dot(bm_t[g * D_STATE:(g + 1) * D_STATE, :].astype(BF16), (xh * wgt).astype(BF16))
            hf_ref[s, h] = state * jnp.exp(last) + st
        y_s[pl.ds(base, Q), :] = jnp.concatenate(ys, axis=-1)

    def fwd_chunk(c, carry):
        for s in range(n_seq):
            fwd_seq(c, s)
        return carry

    lax.fori_loop(0, nc, fwd_chunk, 0, unroll=SSD_UNROLL)

    def bwd_seq(c, s):
        base = pl.multiple_of(s * seq + c * Q, Q)
        act = act_s[pl.ds(base, Q), :]
        cum = cum_s[pl.ds(base, Q), :]
        dtv = dtv_s[pl.ds(base, Q), :]
        xs = act[:, 0:256]
        bm_t = act[:, 256:384].T
        cm = act[:, 384:512]
        ys = []
        for h in range(D_HEADS):
            g = h // (D_HEADS // D_GROUPS)
            cg = cm[:, g * D_STATE:(g + 1) * D_STATE].astype(BF16)
            cb = cum[:, D_HEADS + h:D_HEADS + h + 1]
            xh = xs[:, h * D_HEADDIM:(h + 1) * D_HEADDIM]
            state = hb_ref[s, h]
            ys.append(_dot(cg, state.astype(BF16)) * jnp.exp(cb))
            first = cum[0:1, D_HEADS + h:D_HEADS + h + 1]
            wgt = jnp.exp(first - cb) * dtv[:, D_HEADS + h:D_HEADS + h + 1]
            st = _dot(bm_t[g * D_STATE:(g + 1) * D_STATE, :].astype(BF16), (xh * wgt).astype(BF16))
            hb_ref[s, h] = state * jnp.exp(first) + st
        y = y_s[pl.ds(base, Q), :] + jnp.concatenate(ys, axis=-1)
        gated = y * _silu(z_ref[pl.ds(base, Q), :])
        ms = jnp.mean(gated * gated, axis=-1, keepdims=True)
        out_ref[pl.ds(base, Q), :] = gated * lax.rsqrt(ms + EPS) * ng_ref[...]

    def bwd_chunk(i, carry):
        for s in range(n_seq):
            bwd_seq(nc - 1 - i, s)
        return carry

    lax.fori_loop(0, nc, bwd_chunk, 0, unroll=SSD_UNROLL)


def _ssd(latent, z, xbc, dt, consts, h0f, h0b):
    bsz, seq, n_seq = (DEC_BATCH, DEC_SEQ, 1) if latent else (BATCH, SEQ, 2)
    rows = n_seq * seq
    first = N_CTX // rows if latent else 0
    per_seq = lambda w: pl.BlockSpec((rows, w), lambda b: (first + b, 0))
    const_specs = [pl.BlockSpec(a.shape, lambda b, n=a.ndim: (0,) * n) for a in consts]
    st_spec = pl.BlockSpec((n_seq, D_HEADS, D_STATE, D_HEADDIM), lambda b: (b, 0, 0, 0))
    st_shape = jax.ShapeDtypeStruct((bsz, D_HEADS, D_STATE, D_HEADDIM), F32)
    return pl.pallas_call(
        _ssd_kernel,
        out_shape=[jax.ShapeDtypeStruct((bsz * seq, D_INNER), F32), st_shape, st_shape],
        grid=(bsz // n_seq,),
        in_specs=[per_seq(256), per_seq(512), per_seq(128)] + const_specs + [st_spec, st_spec],
        out_specs=[pl.BlockSpec((rows, D_INNER), lambda b: (b, 0)), st_spec, st_spec],
        scratch_shapes=[pltpu.VMEM((rows, CONV_DIM), F32), pltpu.VMEM((rows, LANES), F32),
                        pltpu.VMEM((rows, LANES), F32), pltpu.VMEM((rows, D_INNER), F32)],
        compiler_params=_params("parallel"),
        name="ssd_lat" if latent else "ssd_ctx",
    )(z, xbc, dt, *consts, h0f, h0b)


def _out_proj_kernel(*refs):
    pairs, (mod_ref, wo_ref, g2_ref, rwh_ref, rwl_ref, rb_ref, x1_ref, h2_ref, te_ref, gt_ref, cnt_ref) = \
        refs[:10], refs[10:]
    x, oa, ob, oc, od = [_pick(pairs[2 * j], pairs[2 * j + 1], PROJ_TM) for j in range(5)]
    m = mod_ref[0]
    gate1, sh2, sc2 = m[2:3], m[3:4], m[4:5]
    mixed = (_dot(oa.astype(BF16), wo_ref[0:256, :])
             + _dot(ob.astype(BF16), wo_ref[256:512, :])
             + _dot(oc.astype(BF16), wo_ref[512:768, :])
             + _dot(od.astype(BF16), wo_ref[768:1024, :]))
    x1 = x + gate1 * mixed
    x1_ref[...] = x1
    ms = jnp.mean(x1 * x1, axis=-1, keepdims=True)
    h2 = (x1 * lax.rsqrt(ms + EPS) * g2_ref[...]) * (1.0 + sc2) + sh2
    h2_ref[...] = h2.astype(BF16)

    hi = h2.astype(BF16)
    lo = (h2 - hi.astype(F32)).astype(BF16)
    logits = _dot(hi, rwh_ref[...]) + _dot(lo, rwh_ref[...]) + _dot(hi, rwl_ref[...]) + rb_ref[...]
    lane = lax.broadcasted_iota(jnp.int32, logits.shape, 1)
    vals, idxs = [], []
    for _ in range(TOP_K):
        mx = jnp.max(logits, axis=-1, keepdims=True)
        ix = jnp.min(jnp.where(logits == mx, lane, LANES), axis=-1, keepdims=True)
        vals.append(mx)
        idxs.append(ix)
        logits = jnp.where(lane == ix, -3e38, logits)
    es = [jnp.exp(v - vals[0]) for v in vals]
    den = es[0] + es[1] + es[2] + es[3]
    te = jnp.zeros(lane.shape, jnp.int32)
    gt = jnp.zeros(lane.shape, F32)
    member = jnp.zeros(lane.shape, F32)
    for k in range(TOP_K):
        te = jnp.where(lane == k, idxs[k], te)
        gt = jnp.where(lane == k, es[k] / den, gt)
        member = jnp.where(lane == idxs[k], 1.0, member)
    te_ref[...] = te[:, 0:TOP_K]
    gt_ref[...] = gt[:, 0:TOP_K]
    for t in range(PROJ_TM // TM):
        count = jnp.sum(member[t * TM:(t + 1) * TM], axis=0, keepdims=True)
        cnt_ref[t] = jnp.broadcast_to(count, (8, LANES)).astype(jnp.int32)


def _out_proj(x, oa, ob, oc, od, mod3, wo, g2, rwh, rwl, rb):
    row = lambda w: pl.BlockSpec((PROJ_TM, w), lambda i: (i, 0))
    full = lambda a: pl.BlockSpec(a.shape, lambda i: (0,) * a.ndim)
    return pl.pallas_call(
        _out_proj_kernel,
        out_shape=[jax.ShapeDtypeStruct((N_TOK, D_MODEL), F32), jax.ShapeDtypeStruct((N_TOK, D_MODEL), BF16),
                   jax.ShapeDtypeStruct((N_TOK, TOP_K), jnp.int32), jax.ShapeDtypeStruct((N_TOK, TOP_K), F32),
                   jax.ShapeDtypeStruct((N_TILES, 8, LANES), jnp.int32)],
        grid=(N_TOK // PROJ_TM,),
        in_specs=_pair_specs(D_MODEL, PROJ_TM) + [s for _ in range(4) for s in _pair_specs(256, PROJ_TM)]
                 + [pl.BlockSpec((1, N_MOD, D_MODEL), lambda i: (_mod_row(i, PROJ_TM), 0, 0)),
                    full(wo), full(g2), full(rwh), full(rwl), full(rb)],
        out_specs=[row(D_MODEL), row(D_MODEL), row(TOP_K), row(TOP_K),
                   pl.BlockSpec((PROJ_TM // TM, 8, LANES), lambda i: (i, 0, 0))],
        compiler_params=_params("parallel"),
        name="out_proj_router",
    )(*x, *oa, *ob, *oc, *od, mod3, wo, g2, rwh, rwl, rb)


def _tile_rows(te, tri_ref, off_ref):
    lane = lax.broadcasted_iota(jnp.int32, (TM, LANES), 1)
    hits = [lane == te[:, k:k + 1] for k in range(TOP_K)]
    member = jnp.zeros((TM, LANES), F32)
    for hit in hits:
        member = jnp.where(hit, 1.0, member)
    rank = _dot(tri_ref[...], member.astype(BF16))
    pos = rank + off_ref[0][0:1, :].astype(F32)
    return [jnp.sum(jnp.where(hit, pos, 0.0), axis=-1, keepdims=True) for hit in hits]


def _segment_chunks(seg_ref, len_ref, tile, visit):
    def segment(e, row):
        length = len_ref[tile * N_EXPERTS + e]
        start = seg_ref[tile * N_EXPERTS + e]
        queue = e % 2

        def chunk(c, carry):
            visit(CHUNKS[0], pl.multiple_of(row + c * CHUNKS[0], SEG_ALIGN),
                  pl.multiple_of(start + c * CHUNKS[0], SEG_ALIGN), queue)
            return carry

        lax.fori_loop(0, lax.shift_right_logical(length, CHUNK_SHIFT), chunk, 0)
        for size in CHUNKS[1:]:
            done = jnp.bitwise_and(length, -2 * size)

            @pl.when(jnp.bitwise_and(length, size) != 0)
            def _():
                visit(size, pl.multiple_of(row + done, SEG_ALIGN), pl.multiple_of(start + done, SEG_ALIGN), queue)
        return row + length

    row = 0
    for e in range(N_EXPERTS):
        row = segment(e, row)


def _wait_chunks(nchunk_ref, tile, copy):
    for j, size in enumerate(CHUNKS):
        lax.fori_loop(0, nchunk_ref[tile * len(CHUNKS) + j], lambda c, carry, size=size: (copy(size).wait(), carry)[1], 0)


def _dispatch_kernel(seg_ref, len_ref, nchunk_ref, end_ref, nu_ref, h2_ref, te_ref, off_ref, tri_ref,
                     xb_ref, rows_ref, buf, sems, sem_z):
    i = pl.program_id(0)
    slot = i % 2

    def chunk_copy(size, s, buf_row, xb_row):
        return pltpu.make_async_copy(buf.at[s, pl.ds(buf_row, size)], xb_ref.at[pl.ds(xb_row, size)], sems.at[s])

    def wait_chunks(tile, s):
        _wait_chunks(nchunk_ref, tile, lambda size: chunk_copy(size, s, 0, 0))

    @pl.when(i == 0)
    def _():
        buf[1, 0:ZERO_ROWS, :] = jnp.zeros((ZERO_ROWS, D_MODEL), F32)
        tail = lambda e: pltpu.make_async_copy(
            buf.at[1, pl.ds(0, ZERO_ROWS)],
            xb_ref.at[pl.ds(pl.multiple_of(jnp.maximum(end_ref[e] - ZERO_ROWS, 0), SEG_ALIGN), ZERO_ROWS)], sem_z)
        block = lambda b: pltpu.make_async_copy(
            buf.at[1, pl.ds(0, MOE_BM)], xb_ref.at[pl.ds(pl.multiple_of(b * MOE_BM, MOE_BM), MOE_BM)], sem_z)
        for e in range(N_EXPERTS):
            tail(e).start()
        lax.fori_loop(nu_ref[0], MOE_BLOCKS, lambda b, carry: (block(b).start(), carry)[1], 0)
        for e in range(N_EXPERTS):
            tail(e).wait()
        lax.fori_loop(nu_ref[0], MOE_BLOCKS, lambda b, carry: (block(b).wait(), carry)[1], 0)

    rows = _tile_rows(te_ref[...], tri_ref, off_ref)
    lane = lax.broadcasted_iota(jnp.int32, (TM, LANES), 1)
    packed = jnp.zeros((TM, LANES), F32)
    for k in range(TOP_K):
        packed = jnp.where(lane == k, rows[k], packed)
    rows_ref[...] = packed[:, 0:TOP_K]
    rows_t = packed.T
    buf_row = lax.broadcasted_iota(jnp.int32, (TILE_BUF, TM), 0).astype(F32)
    pick = jnp.zeros((TILE_BUF, TM), F32)
    for k in range(TOP_K):
        pick = jnp.where(buf_row == rows_t[k:k + 1, :], 1.0, pick)
    buf[slot] = _dot(pick.astype(BF16), h2_ref[...])

    @pl.when(i > 0)
    def _():
        wait_chunks(i - 1, 1 - slot)

    _segment_chunks(seg_ref, len_ref, i, lambda size, b, x, q: chunk_copy(size, slot, b, x).start(priority=q))

    @pl.when(i == N_TILES - 1)
    def _():
        wait_chunks(i, slot)


def _dispatch(seg_start, seg_len, n_chunk, pad_end, n_used, h2, te, seg_off, tri):
    row = lambda w: pl.BlockSpec((TM, w), lambda i, *_: (i, 0))
    grid_spec = pltpu.PrefetchScalarGridSpec(
        num_scalar_prefetch=5,
        grid=(N_TILES,),
        in_specs=[row(D_MODEL), row(TOP_K), pl.BlockSpec((1, 8, LANES), lambda i, *_: (i, 0, 0)),
                  pl.BlockSpec(tri.shape, lambda i, *_: (0, 0))],
        out_specs=[pl.BlockSpec(memory_space=pl.ANY), row(TOP_K)],
        scratch_shapes=[pltpu.VMEM((2, TILE_BUF, D_MODEL), F32),
                        pltpu.SemaphoreType.DMA((2,)), pltpu.SemaphoreType.DMA],
    )
    return pl.pallas_call(
        _dispatch_kernel,
        out_shape=[jax.ShapeDtypeStruct((YB_ROWS, D_MODEL), F32), jax.ShapeDtypeStruct((N_TOK, TOP_K), F32)],
        grid_spec=grid_spec,
        compiler_params=_params("arbitrary"),
        name="moe_dispatch",
    )(seg_start, seg_len, n_chunk, pad_end, n_used, h2, te, seg_off, tri)


def _expert_kernel(layer, be_ref, nu_ref, slot_ref, next_ref, x_ref, wgu_hbm, bgu_ref, wdn_hbm, bdn_ref,
                   o_ref, wgu_f, wdn_f, wgu_s, wdn_s, sems):
    i = pl.program_id(0)
    used = i < nu_ref[0]
    expert = be_ref[i]
    fresh = jnp.logical_or(i == 0, expert != be_ref[jnp.maximum(i - 1, 0)])
    slot = slot_ref[expert]

    def fetch(e, s):
        return (pltpu.make_async_copy(wgu_hbm.at[layer, e], wgu_f.at[s], sems.at[0, s]),
                pltpu.make_async_copy(wdn_hbm.at[layer, e], wdn_f.at[s], sems.at[1, s]))

    @pl.when(i == 0)
    def _():
        for cp in fetch(expert, slot):
            cp.start()

    @pl.when(jnp.logical_and(used, fresh))
    def _():
        for cp in fetch(expert, slot):
            cp.wait()
        nxt = next_ref[expert]

        @pl.when(nxt >= 0)
        def _():
            for cp in fetch(nxt, 1 - slot):
                cp.start()

        wgu_s[...] = wgu_f[slot].astype(BF16)
        wdn_s[...] = wdn_f[slot].astype(BF16)

    @pl.when(used)
    def _():
        hgu = _dot(x_ref[...].astype(BF16), wgu_s[...]) + bgu_ref[0, 0]
        gate = jnp.minimum(hgu[:, :D_FF], SWIGLU_LIMIT)
        up = jnp.clip(hgu[:, D_FF:], -SWIGLU_LIMIT, SWIGLU_LIMIT)
        act = (up + 1.0) * gate * jax.nn.sigmoid(SWIGLU_ALPHA * gate)
        o_ref[...] = _dot(act.astype(BF16), wdn_s[...]) + bdn_ref[0, 0]

    @pl.when(jnp.logical_not(used))
    def _():
        o_ref[...] = jnp.zeros_like(o_ref)


def _experts(layer, block_e, n_used, w_slot, next_e, xb, w_gu, b_gu, w_dn, b_dn):
    grid_spec = pltpu.PrefetchScalarGridSpec(
        num_scalar_prefetch=4,
        grid=(MOE_BLOCKS,),
        in_specs=[pl.BlockSpec((MOE_BM, D_MODEL), lambda i, be, *_: (i, 0)),
                  pl.BlockSpec(memory_space=pl.ANY),
                  pl.BlockSpec((1, 1, 1, 2 * D_FF), lambda i, be, *_: (layer, be[i], 0, 0)),
                  pl.BlockSpec(memory_space=pl.ANY),
                  pl.BlockSpec((1, 1, 1, D_MODEL), lambda i, be, *_: (layer, be[i], 0, 0))],
        out_specs=pl.BlockSpec((MOE_BM, D_MODEL), lambda i, be, *_: (i, 0)),
        scratch_shapes=[pltpu.VMEM((2, D_MODEL, 2 * D_FF), F32), pltpu.VMEM((2, D_FF, D_MODEL), F32),
                        pltpu.VMEM((D_MODEL, 2 * D_FF), BF16), pltpu.VMEM((D_FF, D_MODEL), BF16),
                        pltpu.SemaphoreType.DMA((2, 2))],
    )
    return pl.pallas_call(
        functools.partial(_expert_kernel, layer),
        out_shape=jax.ShapeDtypeStruct((YB_ROWS, D_MODEL), F32),
        grid_spec=grid_spec,
        compiler_params=_params("arbitrary"),
        name="experts",
    )(block_e, n_used, w_slot, next_e, xb, w_gu, b_gu.reshape(DEPTH, N_EXPERTS, 1, 2 * D_FF), w_dn,
      b_dn.reshape(DEPTH, N_EXPERTS, 1, D_MODEL))


def _combine_kernel(final, seg_ref, len_ref, nchunk_ref, x1_ref, rows_ref, gt_ref, mod_ref, fg_ref,
                    yb_ref, o_ctx_ref, o_lat_ref, buf, sems):
    i = pl.program_id(0)
    slot = i % 2

    def chunk_copy(size, s, buf_row, yb_row):
        return pltpu.make_async_copy(yb_ref.at[pl.ds(yb_row, size)], buf.at[s, pl.ds(buf_row, size)], sems.at[s])

    def request(tile, s):
        _segment_chunks(seg_ref, len_ref, tile, lambda size, b, y, q: chunk_copy(size, s, b, y).start(priority=q))

    @pl.when(i == 0)
    def _():
        buf[...] = jnp.zeros_like(buf)
        request(0, 0)

    @pl.when(i + 1 < N_TILES)
    def _():
        request(i + 1, 1 - slot)

    rows = rows_ref[...]
    gt = gt_ref[...]
    buf_row = lax.broadcasted_iota(jnp.int32, (TM, TILE_BUF), 1).astype(F32)
    place = jnp.zeros((TM, TILE_BUF), F32)
    for k in range(TOP_K):
        place = jnp.where(buf_row == rows[:, k:k + 1], gt[:, k:k + 1], place)

    _wait_chunks(nchunk_ref, i, lambda size: chunk_copy(size, slot, 0, 0))
    y = _dot(place.astype(BF16), buf[slot].astype(BF16))
    x2 = x1_ref[...] + mod_ref[0][5:6] * y
    if final:
        ms = jnp.mean(x2 * x2, axis=-1, keepdims=True)
        x2 = x2 * lax.rsqrt(ms + EPS) * fg_ref[...]

    @pl.when(i < CTX_TILES)
    def _():
        o_ctx_ref[...] = x2

    @pl.when(i >= CTX_TILES)
    def _():
        o_lat_ref[...] = x2


def _combine(seg_start, seg_len, n_chunk, x1, rows, gates, mod3, fg, yb, final):
    row = lambda w: pl.BlockSpec((TM, w), lambda i, *_: (i, 0))
    full = lambda a: pl.BlockSpec(a.shape, lambda i, *_: (0,) * a.ndim)
    out_shape = [jax.ShapeDtypeStruct((N_CTX, D_MODEL), F32), jax.ShapeDtypeStruct((N_LAT, D_MODEL), F32)]
    out_specs = _pair_specs(D_MODEL, TM)
    grid_spec = pltpu.PrefetchScalarGridSpec(
        num_scalar_prefetch=3,
        grid=(N_TILES,),
        in_specs=[row(D_MODEL), row(TOP_K), row(TOP_K),
                  pl.BlockSpec((1, N_MOD, D_MODEL), lambda i, *_: (_mod_row(i), 0, 0)),
                  full(fg),
                  pl.BlockSpec(memory_space=pl.ANY)],
        out_specs=out_specs,
        scratch_shapes=[pltpu.VMEM((2, TILE_BUF, D_MODEL), F32), pltpu.SemaphoreType.DMA((2,))],
    )
    return pl.pallas_call(
        functools.partial(_combine_kernel, final),
        out_shape=out_shape,
        grid_spec=grid_spec,
        compiler_params=_params("arbitrary"),
        name="moe_combine",
    )(seg_start, seg_len, n_chunk, x1, rows, gates, mod3, fg, yb)


def _gqa_head_order(w, axis):
    take = lambda start: lax.slice_in_dim(w, start, start + HEAD_DIM, axis=axis)
    return [take(kv * 128 + g * 64) for g in range(2) for kv in range(2)]


def _w1_layout(w_in):
    cols = lambda name, n: w_in[:, IN_OFF[name]:IN_OFF[name] + n]
    zeros = lambda n: jnp.zeros((D_MODEL, n), F32)
    pieces = (_gqa_head_order(w_in, 1) + [cols('a_k', 256)]
              + [cols('b_cq', 192), zeros(64), cols('b_ckv', 128), cols('b_kr', 32), zeros(96)]
              + [cols('c_q', 768), cols('d_z', 256), cols('d_xbc', 512), cols('d_dtf', 8), zeros(120)])
    return jnp.concatenate(pieces, axis=1).astype(BF16)


def _rope_tables():
    t = np.arange(DEC_SEQ)
    pos = ((t // GRID_W).astype(np.float32), (t % GRID_W).astype(np.float32))

    def unit(rot_dim):
        quarter, half = rot_dim // 4, rot_dim // 2
        inv = ROPE_THETA ** (-np.arange(0, half, 2, dtype=np.float32) / half)
        cos = np.zeros((DEC_SEQ, rot_dim), np.float32)
        sin = np.zeros((DEC_SEQ, rot_dim), np.float32)
        for seg in range(4):
            ang = pos[seg // 2][:, None] * inv[None, :].astype(np.float32)
            cos[:, seg * quarter:(seg + 1) * quarter] = np.cos(ang)
            sin[:, seg * quarter:(seg + 1) * quarter] = np.sin(ang) * (-1.0 if seg % 2 == 0 else 1.0)
        return cos, sin

    log2e = math.log2(math.e)
    specs = [
        (HEAD_DIM, (0, 64), (), HEAD_DIM ** -0.5 * log2e),
        (HEAD_DIM, (0, 64), (), 1.0),
        (MLA_ROPE, (MLA_NOPE,), (0, MLA_NOPE), (MLA_NOPE + MLA_ROPE) ** -0.5 * log2e),
        (MLA_ROPE, (0,), (), 1.0),
        (DIFF_D, (0, 32, 64, 96), (), DIFF_D ** -0.5 * log2e),
        (DIFF_D, (0, 32, 64, 96), (), 1.0),
    ]
    lat_cols, ident_cols = [], []
    for rot_dim, starts, passthrough, scale in specs:
        ucos, usin = unit(rot_dim)
        cos = np.zeros((DEC_SEQ, LANES), np.float32)
        sin = np.zeros((DEC_SEQ, LANES), np.float32)
        ident = np.zeros((1, LANES), np.float32)
        if passthrough:
            cos[:, passthrough[0]:passthrough[1]] = 1.0
            ident[:, passthrough[0]:passthrough[1]] = 1.0
        for s in starts:
            cos[:, s:s + rot_dim] = ucos
            sin[:, s:s + rot_dim] = usin
            ident[:, s:s + rot_dim] = 1.0
        lat_cols += [cos * scale, sin * scale]
        ident_cols += [ident * scale, np.zeros((1, LANES), np.float32)]
    lat = np.concatenate(lat_cols, axis=1)
    ident_blk = np.broadcast_to(np.concatenate(ident_cols, axis=1), (PROJ_TM, N_TAB * LANES))
    return np.concatenate([ident_blk, lat], axis=0).astype(np.float32)


def _block_diag_ones(n, blk):
    r = np.arange(n)
    return (r[:, None] // blk == r[None, :] // blk).astype(np.float32)


def _chunk_tables(seg_len):
    counts = [jnp.sum(seg_len // CHUNKS[0], axis=1)]
    counts += [jnp.sum(seg_len % (2 * size) // size, axis=1) for size in CHUNKS[1:]]
    return jnp.stack(counts, axis=1).reshape(-1).astype(jnp.int32)


def _strict_lower_ones(n):
    r = np.arange(n)
    return (r[None, :] < r[:, None]).astype(np.float32)


def kernel(x_prompt, x_sample, cache_gqa_k, cache_gqa_v, cache_mla_ckv, cache_mla_krope, cache_diff_k, cache_diff_v, state_ssd_fwd, state_ssd_bwd, c, c_ctx, norm1_g, norm2_g, w_ada, b_ada, w_in, w_out, gqa_qn_g, gqa_kn_g, mla_qa_g, mla_wqb, mla_kva_g, mla_wkvb, diff_lq1, diff_lk1, diff_lq2, diff_lk2, diff_subln_g, ssd_conv_w, ssd_conv_b, ssd_a_log_f, ssd_a_log_b, ssd_dt_bias_f, ssd_dt_bias_b, ssd_d, ssd_norm_g, router_w, router_b, moe_w_gu, moe_b_gu, moe_w_dn, moe_b_dn, final_g):
    tabs = jnp.asarray(_rope_tables())
    bd = jnp.asarray(_block_diag_ones(256, HEAD_DIM), BF16)
    tri = jnp.asarray(_strict_lower_ones(TM), BF16)

    cvec = jnp.zeros((MOD_ROWS, D_MODEL), F32).at[0].set(c_ctx).at[1:1 + DEC_BATCH].set(c)
    mod = _modulation(cvec, w_ada, b_ada).reshape(DEPTH, MOD_ROWS, N_MOD, D_MODEL)

    x = (x_prompt.reshape(N_CTX, D_MODEL), x_sample.reshape(N_LAT, D_MODEL))
    new_ctx = []
    for l in range(DEPTH):
        mod3 = mod[l]
        w1 = _w1_layout(w_in[l])
        wqb = mla_wqb[l].reshape(Q_LORA, B_HEADS, MLA_NOPE + MLA_ROPE)
        wqb = jnp.pad(wqb, ((0, 256 - Q_LORA), (0, 0), (0, LANES - MLA_NOPE - MLA_ROPE)))
        wqb = wqb.reshape(256, B_HEADS * LANES).astype(BF16)
        wkvb = mla_wkvb[l].reshape(KV_LORA, B_HEADS, MLA_NOPE + MLA_V)
        wk_nope = jnp.pad(wkvb[:, :, :MLA_NOPE], ((0, 0), (0, 0), (0, LANES - MLA_NOPE)))
        eye_r = jnp.zeros((LANES, B_HEADS, LANES), F32)
        eye_r = eye_r.at[jnp.arange(MLA_ROPE), :, MLA_NOPE + jnp.arange(MLA_ROPE)].set(1.0)
        wk = jnp.concatenate([wk_nope, eye_r], axis=0).reshape(256, B_HEADS * LANES).astype(BF16)
        wv = wkvb[:, :, MLA_NOPE:].reshape(KV_LORA, B_HEADS * MLA_V).astype(BF16)
        wo = jnp.concatenate(_gqa_head_order(w_out[l], 0) + [w_out[l][256:]], axis=0).astype(BF16)
        gq = jnp.tile(gqa_qn_g[l], 4)[None, :]
        gk = jnp.tile(gqa_kn_g[l], 2)[None, :]
        gqa = jnp.pad(mla_qa_g[l], (0, 256 - Q_LORA))[None, :]
        gkva = mla_kva_g[l][None, :]
        rw = jnp.pad(router_w[l], ((0, 0), (0, LANES - N_EXPERTS)))
        rwh = rw.astype(BF16)
        rwl = (rw - rwh.astype(F32)).astype(BF16)
        rb = jnp.pad(router_b[l], (0, LANES - N_EXPERTS), constant_values=NEG_BIG)[None, :]

        qa, kva, qb, ck, qc, kc, vc, z, xbc, dt = _in_proj(
            x, mod3, norm1_g[l][None, :], w1, bd, gq, gk, gqa, gkva, wqb, tabs)

        past_kva = jnp.concatenate([cache_gqa_k[:, l].reshape(DEC_BATCH, PAST_LEN, 128),
                                    cache_gqa_v[:, l].reshape(DEC_BATCH, PAST_LEN, 128)], axis=-1)
        past_ck = jnp.concatenate([cache_mla_ckv[:, l], cache_mla_krope[:, l],
                                   jnp.zeros((DEC_BATCH, PAST_LEN, LANES - MLA_ROPE), F32)], axis=-1)
        past_kc = cache_diff_k[:, l].reshape(DEC_BATCH, PAST_LEN, 256)
        past_vc = cache_diff_v[:, l].reshape(DEC_BATCH, PAST_LEN, 256)
        lams = [a[l][None, :] for a in (diff_lq1, diff_lk1, diff_lq2, diff_lk2)]
        gsub = jnp.tile(diff_subln_g[l], 2)[None, :]
        lam_init = 0.8 - 0.6 * math.exp(-0.3 * l)
        oa, ob, oc = _attention_mixers(qa, kva, qb, ck, qc, kc, vc, past_kva, past_ck, past_kc, past_vc,
                                       wk, wv, lams, gsub, lam_init)

        pad8 = lambda f, b: jnp.pad(jnp.concatenate([f, b]), (0, LANES - 2 * D_HEADS))[None, :]
        ssd_consts = (ssd_conv_w[l], ssd_conv_b[l][None, :], pad8(ssd_dt_bias_f[l], ssd_dt_bias_b[l]),
                      pad8(ssd_a_log_f[l], ssd_a_log_b[l]), jnp.repeat(ssd_d[l], D_HEADDIM)[None, :],
                      ssd_norm_g[l][None, :])
        zeros_st = jnp.zeros((BATCH, D_HEADS, D_STATE, D_HEADDIM), F32)
        od_c, hf_c, hb_c = _ssd(False, z, xbc, dt, ssd_consts, zeros_st, zeros_st)
        od_l, _, _ = _ssd(True, z, xbc, dt, ssd_consts, jnp.swapaxes(state_ssd_fwd[:, l], -1, -2),
                          jnp.swapaxes(state_ssd_bwd[:, l], -1, -2))
        od = (od_c, od_l)

        x1, h2, te, gates, tile_cnt = _out_proj(x, oa, ob, oc, od, mod3, wo, norm2_g[l][None, :], rwh, rwl, rb)

        seg_cnt = tile_cnt[:, 0, :N_EXPERTS]
        seg_len = (seg_cnt + SEG_ALIGN - 1) // SEG_ALIGN * SEG_ALIGN
        region = (jnp.sum(seg_len, axis=0) + MOE_BM - 1) // MOE_BM * MOE_BM
        pad_end = jnp.cumsum(region).astype(jnp.int32)
        pad_start = pad_end - region
        seg_start = (pad_start[None, :] + jnp.cumsum(seg_len, axis=0) - seg_len).astype(jnp.int32)
        n_used = (pad_end[-1] // MOE_BM).astype(jnp.int32).reshape(1)
        blk_start = jnp.arange(MOE_BLOCKS, dtype=jnp.int32) * MOE_BM
        block_e = jnp.minimum(jnp.sum((pad_end[None, :] <= blk_start[:, None]).astype(jnp.int32), axis=1),
                              N_EXPERTS - 1).astype(jnp.int32)
        n_chunk = _chunk_tables(seg_len)
        seg_off = jnp.cumsum(seg_len, axis=1) - seg_len
        seg_off = jnp.broadcast_to(jnp.pad(seg_off, ((0, 0), (0, LANES - N_EXPERTS)))[:, None, :],
                                   (N_TILES, 8, LANES)).astype(jnp.int32)
        seg_start, seg_len = seg_start.reshape(-1), seg_len.reshape(-1).astype(jnp.int32)

        xb, pair_rows = _dispatch(seg_start, seg_len, n_chunk, pad_end, n_used, h2, te, seg_off, tri)
        has_rows = region > 0
        w_slot = ((jnp.cumsum(has_rows.astype(jnp.int32)) - 1) % 2).astype(jnp.int32)
        later = lax.cummin(jnp.where(has_rows, jnp.arange(N_EXPERTS, dtype=jnp.int32), N_EXPERTS), reverse=True)
        later = jnp.concatenate([later[1:], jnp.full((1,), N_EXPERTS, jnp.int32)])
        next_e = jnp.where(later < N_EXPERTS, later, -1).astype(jnp.int32)
        yb = _experts(l, block_e, n_used, w_slot, next_e, xb, moe_w_gu, moe_b_gu, moe_w_dn, moe_b_dn)
        x = _combine(seg_start, seg_len, n_chunk, x1, pair_rows, gates, mod3, final_g[None, :], yb,
                     l == DEPTH - 1)

        kva_c, ck_c = kva[:N_CTX], ck[:N_CTX]
        new_ctx.append((kva_c[:, 0:128].reshape(BATCH, SEQ, A_KV_HEADS, HEAD_DIM),
                        kva_c[:, 128:256].reshape(BATCH, SEQ, A_KV_HEADS, HEAD_DIM),
                        ck_c[:, 0:KV_LORA].reshape(BATCH, SEQ, KV_LORA),
                        ck_c[:, KV_LORA:KV_LORA + MLA_ROPE].reshape(BATCH, SEQ, MLA_ROPE),
                        kc[:N_CTX].reshape(BATCH, SEQ, C_HEADS, 2 * DIFF_D),
                        vc[:N_CTX].reshape(BATCH, SEQ, C_HEADS, DIFF_V),
                        jnp.swapaxes(hf_c, -1, -2), jnp.swapaxes(hb_c, -1, -2)))

    y_prompt = x[0].reshape(BATCH, SEQ, D_MODEL)
    y_sample = x[1].reshape(DEC_BATCH, DEC_SEQ, D_MODEL)
    caches = [jnp.stack([cl[i] for cl in new_ctx], axis=1) for i in range(8)]
    return (y_prompt, y_sample, *caches)
```

```python
import functools
import math

import numpy as np
import jax
import jax.numpy as jnp
from jax import lax
from jax.experimental import pallas as pl
from jax.experimental.pallas import tpu as pltpu

F32 = jnp.float32
BF16 = jnp.bfloat16

D_MODEL = 1024
BATCH = 16
SEQ = 256
DEPTH = 2
DEC_BATCH = 8
DEC_SEQ = 2048
PAST_LEN = 256
GRID_W = 64
ROPE_THETA = 10000.0
EPS = 1e-6
HEAD_DIM = 64
A_HEADS = 4
A_KV_HEADS = 2
B_HEADS = 4
MLA_NOPE = 64
MLA_ROPE = 32
MLA_V = 64
Q_LORA = 192
KV_LORA = 128
C_HEADS = 4
DIFF_D = 32
DIFF_V = 64
SUBLN_EPS = 1e-5
D_HEADS = 4
D_HEADDIM = 64
D_INNER = 256
D_GROUPS = 2
D_STATE = 64
CONV_DIM = 512
SSD_CHUNK = 128
N_EXPERTS = 32
TOP_K = 4
D_FF = 1024
SWIGLU_ALPHA = 1.702
SWIGLU_LIMIT = 7.0
N_MOD = 6

N_CTX = BATCH * SEQ
N_LAT = DEC_BATCH * DEC_SEQ
N_TOK = N_CTX + N_LAT

LANES = 128
TM = 256
N_TILES = N_TOK // TM
CTX_TILES = N_CTX // TM
LAT_TQ = 1024
PROJ_TM = 512
MOD_ROWS = 16
MOE_BM = 512
N_PAIRS = N_TOK * TOP_K
SEG_ALIGN = 8
CHUNKS = (32, 16, 8)
CHUNK_SHIFT = 5
MOE_BLOCKS = -(-(N_PAIRS + N_TILES * N_EXPERTS * (SEG_ALIGN - 1) + N_EXPERTS * (MOE_BM - 1)) // MOE_BM)
YB_ROWS = MOE_BLOCKS * MOE_BM
TILE_BUF = -(-(TM * TOP_K + N_EXPERTS * (SEG_ALIGN - 1)) // LANES) * LANES
ZERO_CHUNKS = tuple(MOE_BM >> k for k in range(1, (MOE_BM // SEG_ALIGN).bit_length()))
VMEM_LIMIT = 56 * 1024 * 1024
NEG_BIG = -1e30

W1_COLS = 2688
IN_OFF = dict(a_q=0, a_k=256, a_v=384, b_cq=512, b_ckv=704, b_kr=832, c_q=864, c_k=1120,
              c_v=1376, d_z=1632, d_xbc=1888, d_dtf=2400, d_dtb=2404)
IN_WIDTH = 2408
N_TAB = 12


def _mod_row(i, tile=TM):
    return jnp.where(i < N_CTX // tile, 0, 1 + (i - N_CTX // tile) // (DEC_SEQ // tile))


def _tab_block(i, tile):
    return jnp.where(i < N_CTX // tile, 0, 1 + (i - N_CTX // tile) % (DEC_SEQ // tile))


def _pair_specs(width, tile):
    n_ctx = N_CTX // tile
    return [pl.BlockSpec((tile, width), lambda i, *_: (jnp.minimum(i, n_ctx - 1), 0)),
            pl.BlockSpec((tile, width), lambda i, *_: (jnp.maximum(i - n_ctx, 0), 0))]


def _pick(ctx_ref, lat_ref, tile):
    return jnp.where(pl.program_id(0) < N_CTX // tile, ctx_ref[...], lat_ref[...])


def _dot(a, b):
    return jnp.dot(a, b, preferred_element_type=F32)


def _dot_nt(a, b):
    return lax.dot_general(a, b, (((1,), (1,)), ((), ())), preferred_element_type=F32)


def _dot_split(x, m):
    hi = x.astype(BF16)
    lo = (x - hi.astype(F32)).astype(BF16)
    return _dot(hi, m) + _dot(lo, m)


def _dot_split_left(m, x):
    hi = x.astype(BF16)
    lo = (x - hi.astype(F32)).astype(BF16)
    return _dot(m, hi) + _dot(m, lo)


def _silu(x):
    return x * jax.nn.sigmoid(x)


def _params(*semantics):
    return pltpu.CompilerParams(dimension_semantics=semantics, vmem_limit_bytes=VMEM_LIMIT)


MOD_TN = 1536


def _mod_kernel(c_ref, w_ref, b_ref, o_ref):
    c = c_ref[...]
    s = _silu(c).astype(BF16)
    o_ref[0] = _dot(s, w_ref[0].astype(BF16)) + b_ref[0]


def _modulation(cvec, w_ada, b_ada):
    n = N_MOD * D_MODEL
    return pl.pallas_call(
        _mod_kernel,
        out_shape=jax.ShapeDtypeStruct((DEPTH, MOD_ROWS, n), F32),
        grid=(DEPTH, n // MOD_TN),
        in_specs=[pl.BlockSpec((MOD_ROWS, D_MODEL), lambda l, j: (0, 0)),
                  pl.BlockSpec((1, D_MODEL, MOD_TN), lambda l, j: (l, 0, j)),
                  pl.BlockSpec((1, 1, MOD_TN), lambda l, j: (l, 0, j))],
        out_specs=pl.BlockSpec((1, MOD_ROWS, MOD_TN), lambda l, j: (l, 0, j)),
        compiler_params=_params("parallel", "parallel"),
        name="adaln_mod",
    )(cvec, w_ada, b_ada.reshape(DEPTH, 1, n))


def _rope(x, cos, sin, quarter):
    lane = lax.broadcasted_iota(jnp.int32, (x.shape[0], LANES), 1)
    first = (lane // quarter) % 2 == 0
    outs = []
    for t in range(x.shape[1] // LANES):
        xt = x[:, t * LANES:(t + 1) * LANES]
        partner = jnp.where(first, pltpu.roll(xt, LANES - quarter, 1), pltpu.roll(xt, quarter, 1))
        outs.append(xt * cos + partner * sin)
    return outs


def _in_proj_kernel(xc_ref, xl_ref, mod_ref, g1_ref, w1_ref, bd_ref, gq_ref, gk_ref, gqa_ref, gkva_ref,
                    wqb_ref, tab_ref,
                    qa_ref, kva_ref, qb_ref, ck_ref, qc_ref, kc_ref, vc_ref, z_ref, xbc_ref, dt_ref):
    x = _pick(xc_ref, xl_ref, PROJ_TM)
    m = mod_ref[0]
    sh1, sc1 = m[0:1], m[1:2]
    ms = jnp.mean(x * x, axis=-1, keepdims=True)
    h = (x * lax.rsqrt(ms + EPS) * g1_ref[...]) * (1.0 + sc1) + sh1
    u = _dot(h.astype(BF16), w1_ref[...])

    def tab(k):
        return tab_ref[:, k * LANES:(k + 1) * LANES]

    bd = bd_ref[...]

    def head_norm(v, gain):
        w = v.shape[1]
        ss = _dot_split(v * v, bd[:w, :w])
        return v * lax.rsqrt(ss * (1.0 / HEAD_DIM) + EPS) * gain

    qa = _rope(head_norm(u[:, 0:256], gq_ref[...]), tab(0), tab(1), HEAD_DIM // 4)
    for t in range(2):
        qa_ref[:, t * LANES:(t + 1) * LANES] = qa[t].astype(BF16)
    ka = _rope(head_norm(u[:, 256:384], gk_ref[...]), tab(2), tab(3), HEAD_DIM // 4)
    kva_ref[:, 0:128] = ka[0]
    kva_ref[:, 128:256] = u[:, 384:512]

    cq = u[:, 512:768]
    msq = jnp.sum(cq * cq, axis=-1, keepdims=True) * (1.0 / Q_LORA)
    yq = cq * lax.rsqrt(msq + EPS) * gqa_ref[...]
    qb = _rope(_dot(yq.astype(BF16), wqb_ref[...]), tab(4), tab(5), MLA_ROPE // 4)
    for t in range(4):
        qb_ref[:, t * LANES:(t + 1) * LANES] = qb[t].astype(BF16)
    ckv = u[:, 768:896]
    msk = jnp.mean(ckv * ckv, axis=-1, keepdims=True)
    ck_ref[:, 0:128] = ckv * lax.rsqrt(msk + EPS) * gkva_ref[...]
    ck_ref[:, 128:256] = _rope(u[:, 896:1024], tab(6), tab(7), MLA_ROPE // 4)[0]

    qc = _rope(u[:, 1024:1280], tab(8), tab(9), DIFF_D // 4)
    kc = _rope(u[:, 1280:1536], tab(10), tab(11), DIFF_D // 4)
    for t in range(2):
        qc_ref[:, t * LANES:(t + 1) * LANES] = qc[t].astype(BF16)
        kc_ref[:, t * LANES:(t + 1) * LANES] = kc[t]
    vc_ref[...] = u[:, 1536:1792]

    z_ref[...] = u[:, 1792:2048]
    xbc_ref[...] = u[:, 2048:2560]
    dt_ref[...] = u[:, 2560:2688]


def _in_proj(x, mod3, g1, w1, bd, gq, gk, gqa, gkva, wqb, tabs):
    row = lambda w: pl.BlockSpec((PROJ_TM, w), lambda i: (i, 0))
    full = lambda a: pl.BlockSpec(a.shape, lambda i: (0,) * a.ndim)
    outs = [(256, BF16), (256, F32), (512, BF16), (256, F32), (256, BF16), (256, F32), (256, F32),
            (256, F32), (512, F32), (128, F32)]
    return pl.pallas_call(
        _in_proj_kernel,
        out_shape=[jax.ShapeDtypeStruct((N_TOK, w), d) for w, d in outs],
        grid=(N_TOK // PROJ_TM,),
        in_specs=_pair_specs(D_MODEL, PROJ_TM)
                 + [pl.BlockSpec((1, N_MOD, D_MODEL), lambda i: (_mod_row(i, PROJ_TM), 0, 0)),
                    full(g1), full(w1), full(bd), full(gq), full(gk), full(gqa), full(gkva), full(wqb),
                    pl.BlockSpec((PROJ_TM, N_TAB * LANES), lambda i: (_tab_block(i, PROJ_TM), 0))],
        out_specs=[row(w) for w, _ in outs],
        compiler_params=_params("parallel"),
        name="in_proj",
    )(*x, mod3, g1, w1, bd, gq, gk, gqa, gkva, wqb, tabs)


def _attend(q, k, v_ones):
    s = _dot_nt(q, k)
    e = jnp.exp2(s - jnp.max(s, axis=-1, keepdims=True))
    out = _dot(e.astype(BF16), v_ones)
    return out[:, :LANES] / out[:, LANES:]


def _with_ones(v):
    return jnp.concatenate([v, jnp.ones_like(v)], axis=-1)


def _half_mask(rows):
    lane = lax.broadcasted_iota(jnp.int32, (rows, LANES), 1)
    return lane < (LANES // 2)


def _keys(past_ref, new_ref, lo, hi):
    new = new_ref[:, lo:hi].astype(BF16)
    if past_ref is None:
        return new
    return jnp.concatenate([past_ref[0, :, lo:hi].astype(BF16), new], axis=0)


def _attn_a_kernel(latent, *refs):
    if latent:
        q_ref, kv_ref, past_ref, o_ref = refs
    else:
        (q_ref, kv_ref, o_ref), past_ref = refs, None
    q = q_ref[...]
    k = _keys(past_ref, kv_ref, 0, 128)
    v = _with_ones(_keys(past_ref, kv_ref, 128, 256))
    lo = _half_mask(q.shape[0])
    for g in range(2):
        qt = q[:, g * LANES:(g + 1) * LANES].astype(F32)
        res = []
        for half in range(2):
            qm = jnp.where(lo, qt, 0.0) if half == 0 else jnp.where(lo, 0.0, qt)
            res.append(_attend(qm.astype(BF16), k, v))
        o_ref[:, g * LANES:(g + 1) * LANES] = jnp.where(lo, res[0], res[1])


def _attn_b_kernel(latent, *refs):
    if latent:
        q_ref, ck_ref, past_ref, wk_ref, wv_ref, o_ref, k_s, v_s = refs
    else:
        (q_ref, ck_ref, wk_ref, wv_ref, o_ref, k_s, v_s), past_ref = refs, None

    @pl.when(pl.program_id(1) == 0)
    def _():
        ck = _keys(past_ref, ck_ref, 0, 256)
        k_s[...] = _dot(ck, wk_ref[...]).astype(BF16)
        v = _dot(ck[:, 0:128], wv_ref[...]).astype(BF16)
        for j in range(2):
            v_s[:, 2 * j * LANES:2 * (j + 1) * LANES] = _with_ones(v[:, j * LANES:(j + 1) * LANES])

    q = q_ref[...]
    lo = _half_mask(q.shape[0])
    for j in range(2):
        v = v_s[:, 2 * j * LANES:2 * (j + 1) * LANES]
        res = []
        for half in range(2):
            h = 2 * j + half
            res.append(_attend(q[:, h * LANES:(h + 1) * LANES], k_s[:, h * LANES:(h + 1) * LANES], v))
        o_ref[:, j * LANES:(j + 1) * LANES] = jnp.where(lo, res[0], res[1])


def _attn_c_kernel(lam_init, latent, *refs):
    if latent:
        q_ref, k_ref, v_ref, pk_ref, pv_ref, lq1_ref, lk1_ref, lq2_ref, lk2_ref, g_ref, o_ref = refs
    else:
        (q_ref, k_ref, v_ref, lq1_ref, lk1_ref, lq2_ref, lk2_ref, g_ref, o_ref), pk_ref, pv_ref = refs, None, None
    lam = (jnp.exp(jnp.sum(lq1_ref[...] * lk1_ref[...], axis=-1, keepdims=True))
           - jnp.exp(jnp.sum(lq2_ref[...] * lk2_ref[...], axis=-1, keepdims=True)) + lam_init)
    q = q_ref[...]
    rows = q.shape[0]
    lane = lax.broadcasted_iota(jnp.int32, (rows, LANES), 1)
    lo = lane < (LANES // 2)
    for j in range(2):
        qt = q[:, j * LANES:(j + 1) * LANES].astype(F32)
        k = _keys(pk_ref, k_ref, j * LANES, (j + 1) * LANES)
        v = _with_ones(_keys(pv_ref, v_ref, j * LANES, (j + 1) * LANES))
        res = []
        for half in range(2):
            parts = []
            for t in range(2):
                quarter = 2 * half + t
                qm = jnp.where(lane // (LANES // 4) == quarter, qt, 0.0)
                parts.append(_attend(qm.astype(BF16), k, v))
            res.append(parts[0] - lam * parts[1])
        o = jnp.where(lo, res[0], res[1])
        o2 = o * o
        ss_lo = jnp.sum(jnp.where(lo, o2, 0.0), axis=-1, keepdims=True)
        ss_hi = jnp.sum(jnp.where(lo, 0.0, o2), axis=-1, keepdims=True)
        ss = jnp.where(lo, ss_lo, ss_hi) * (1.0 / DIFF_V)
        o_ref[:, j * LANES:(j + 1) * LANES] = (o * lax.rsqrt(ss + SUBLN_EPS) * g_ref[...]) * (1.0 - lam_init)


def _seq_call(body, name, latent, q, news, pasts, consts, scratch=(), q_semantics="parallel"):
    const_specs = [pl.BlockSpec(a.shape, lambda b, i, n=a.ndim: (0,) * n) for a in consts]
    if latent:
        tq, rows = LAT_TQ, N_LAT
        tile = lambda b, i: (N_CTX // LAT_TQ + b * (DEC_SEQ // LAT_TQ) + i, 0)
        out_tile = lambda b, i: (b * (DEC_SEQ // LAT_TQ) + i, 0)
        grid = (DEC_BATCH, DEC_SEQ // LAT_TQ)
        new_specs = [pl.BlockSpec((DEC_SEQ, a.shape[1]), lambda b, i: (N_CTX // DEC_SEQ + b, 0)) for a in news]
        past_specs = [pl.BlockSpec((1, PAST_LEN, a.shape[2]), lambda b, i: (b, 0, 0)) for a in pasts]
    else:
        tq, rows = SEQ, N_CTX
        tile = out_tile = lambda b, i: (b, 0)
        grid = (BATCH, 1)
        new_specs = [pl.BlockSpec((SEQ, a.shape[1]), tile) for a in news]
        past_specs, pasts = [], ()
    return pl.pallas_call(
        functools.partial(body, latent),
        out_shape=jax.ShapeDtypeStruct((rows, 256), F32),
        grid=grid,
        in_specs=[pl.BlockSpec((tq, q.shape[1]), tile)] + new_specs + past_specs + const_specs,
        out_specs=pl.BlockSpec((tq, 256), out_tile),
        scratch_shapes=list(scratch),
        compiler_params=_params("parallel", q_semantics),
        name=name + ("_lat" if latent else "_ctx"),
    )(q, *news, *pasts, *consts)


def _attention_mixers(qa, kva, qb, ck, qc, kc, vc, past_kva, past_ck, past_kc, past_vc, wk, wv,
                      lams, gsub, lam_init):
    def both(body, name, q, news, pasts, consts, scratch_fn=None, q_semantics="parallel"):
        sc = (lambda lk: ()) if scratch_fn is None else scratch_fn
        return (_seq_call(body, name, False, q, news, (), consts, sc(SEQ), q_semantics),
                _seq_call(body, name, True, q, news, pasts, consts, sc(PAST_LEN + DEC_SEQ), q_semantics))

    oa = both(_attn_a_kernel, "attn_gqa", qa, [kva], [past_kva], [])
    ob = both(_attn_b_kernel, "attn_mla", qb, [ck], [past_ck], [wk, wv],
              lambda lk: (pltpu.VMEM((lk, 512), BF16), pltpu.VMEM((lk, 512), BF16)), "arbitrary")
    oc = both(functools.partial(_attn_c_kernel, lam_init), "attn_diff", qc, [kc, vc],
              [past_kc, past_vc], [*lams, gsub])
    return oa, ob, oc


Q = SSD_CHUNK
SSD_UNROLL = 2


def _ssd_kernel(z_ref, xbc_ref, dt_ref, cw_ref, cb_ref, dtb_ref, alog_ref, dvec_ref, ng_ref, h0f_ref, h0b_ref,
                out_ref, hf_ref, hb_ref, act_s, cum_s, dtv_s, y_s):
    n_seq = hf_ref.shape[0]
    seq = z_ref.shape[0] // n_seq
    nc = seq // Q
    row = lax.broadcasted_iota(jnp.int32, (Q, Q), 0)
    col = lax.broadcasted_iota(jnp.int32, (Q, Q), 1)
    lower = row >= col
    upper = row <= col
    tril = jnp.where(lower, 1.0, 0.0).astype(BF16)
    triu = jnp.where(upper, 1.0, 0.0).astype(BF16)
    rowc = lax.broadcasted_iota(jnp.int32, (Q, CONV_DIM), 0)
    lane = lax.broadcasted_iota(jnp.int32, (Q, LANES), 1)
    a_neg = -jnp.exp(alog_ref[...])
    cw = cw_ref[...]
    hf_ref[...] = h0f_ref[...]
    hb_ref[...] = h0b_ref[...]

    def fwd_seq(c, s):
        off = s * seq
        base = pl.multiple_of(off + c * Q, Q)
        x0 = xbc_ref[pl.ds(base, Q), :]
        prev = xbc_ref[pl.ds(pl.multiple_of(off + jnp.maximum(c * Q - 8, 0), 8), 8), :][7:8, :]
        nxt = xbc_ref[pl.ds(pl.multiple_of(off + jnp.minimum(c * Q + Q, seq - 8), 8), 8), :][0:1, :]
        prev = jnp.where(c > 0, prev, 0.0)
        nxt = jnp.where(c < nc - 1, nxt, 0.0)
        xm1 = jnp.where(rowc == 0, prev, pltpu.roll(x0, 1, 0))
        xp1 = jnp.where(rowc == Q - 1, nxt, pltpu.roll(x0, Q - 1, 0))
        act = _silu(xm1 * cw[0:1] + x0 * cw[1:2] + xp1 * cw[2:3] + cb_ref[...])
        act_s[pl.ds(base, Q), :] = act
        xs = act[:, 0:256]
        bm = act[:, 256:384]
        cm = act[:, 384:512]

        dtr = dt_ref[pl.ds(base, Q), :] + dtb_ref[...]
        dtv = jnp.maximum(dtr, 0.0) + jnp.log1p(jnp.exp(-jnp.abs(dtr)))
        dta = dtv * a_neg
        cum = jnp.where(lane < D_HEADS, _dot_split_left(tril, dta), _dot_split_left(triu, dta))
        cum_s[pl.ds(base, Q), :] = cum
        dtv_s[pl.ds(base, Q), :] = dtv
        cum_t = cum.T
        dtv_t = dtv.T
        bm_t = bm.T
        ys = []
        for h in range(D_HEADS):
            g = h // (D_HEADS // D_GROUPS)
            cg = cm[:, g * D_STATE:(g + 1) * D_STATE].astype(BF16)
            cb_mat = _dot_nt(cg, bm[:, g * D_STATE:(g + 1) * D_STATE].astype(BF16))
            cf = cum[:, h:h + 1]
            cb = cum[:, D_HEADS + h:D_HEADS + h + 1]
            l_f = jnp.exp(jnp.where(lower, cf - cum_t[h:h + 1, :], NEG_BIG))
            l_b = jnp.exp(jnp.where(upper, cb - cum_t[D_HEADS + h:D_HEADS + h + 1, :], NEG_BIG))
            mix = cb_mat * (l_f * dtv_t[h:h + 1, :] + l_b * dtv_t[D_HEADS + h:D_HEADS + h + 1, :])
            xh = xs[:, h * D_HEADDIM:(h + 1) * D_HEADDIM]
            y = _dot(mix.astype(BF16), xh.astype(BF16))
            state = hf_ref[s, h]
            y = y + _dot(cg, state.astype(BF16)) * jnp.exp(cf)
            y = y + dvec_ref[:, h * D_HEADDIM:(h + 1) * D_HEADDIM] * xh
            ys.append(y)
            last = cum[Q - 1:Q, h:h + 1]
            wgt = jnp.exp(last - cf) * dtv[:, h:h + 1]
            st = _dot(bm_t[g * D_STATE:(g + 1) * D_STATE, :].astype(BF16), (xh * wgt).astype(BF16))
            hf_ref[s, h] = state * jnp.exp(last) + st
        y_s[pl.ds(base, Q), :] = jnp.concatenate(ys, axis=-1)

    def fwd_chunk(c, carry):
        for s in range(n_seq):
            fwd_seq(c, s)
        return carry

    lax.fori_loop(0, nc, fwd_chunk, 0, unroll=SSD_UNROLL)

    def bwd_seq(c, s):
        base = pl.multiple_of(s * seq + c * Q, Q)
        act = act_s[pl.ds(base, Q), :]
        cum = cum_s[pl.ds(base, Q), :]
        dtv = dtv_s[pl.ds(base, Q), :]
        xs = act[:, 0:256]
        bm_t = act[:, 256:384].T
        cm = act[:, 384:512]
        ys = []
        for h in range(D_HEADS):
            g = h // (D_HEADS // D_GROUPS)
            cg = cm[:, g * D_STATE:(g + 1) * D_STATE].astype(BF16)
            cb = cum[:, D_HEADS + h:D_HEADS + h + 1]
            xh = xs[:, h * D_HEADDIM:(h + 1) * D_HEADDIM]
            state = hb_ref[s, h]
            ys.append(_dot(cg, state.astype(BF16)) * jnp.exp(cb))
            first = cum[0:1, D_HEADS + h:D_HEADS + h + 1]
            wgt = jnp.exp(first - cb) * dtv[:, D_HEADS + h:D_HEADS + h + 1]
            st = _dot(bm_t[g * D_STATE:(g + 1) * D_STATE, :].astype(BF16), (xh * wgt).astype(BF16))
            hb_ref[s, h] = state * jnp.exp(first) + st
        y = y_s[pl.ds(base, Q), :] + jnp.concatenate(ys, axis=-1)
        gated = y * _silu(z_ref[pl.ds(base, Q), :])
        ms = jnp.mean(gated * gated, axis=-1, keepdims=True)
        out_ref[pl.ds(base, Q), :] = gated * lax.rsqrt(ms + EPS) * ng_ref[...]

    def bwd_chunk(i, carry):
        for s in range(n_seq):
            bwd_seq(nc - 1 - i, s)
        return carry

    lax.fori_loop(0, nc, bwd_chunk, 0, unroll=SSD_UNROLL)


def _ssd(latent, z, xbc, dt, consts, h0f, h0b):
    bsz, seq, n_seq = (DEC_BATCH, DEC_SEQ, 1) if latent else (BATCH, SEQ, 2)
    rows = n_seq * seq
    first = N_CTX // rows if latent else 0
    per_seq = lambda w: pl.BlockSpec((rows, w), lambda b: (first + b, 0))
    const_specs = [pl.BlockSpec(a.shape, lambda b, n=a.ndim: (0,) * n) for a in consts]
    st_spec = pl.BlockSpec((n_seq, D_HEADS, D_STATE, D_HEADDIM), lambda b: (b, 0, 0, 0))
    st_shape = jax.ShapeDtypeStruct((bsz, D_HEADS, D_STATE, D_HEADDIM), F32)
    return pl.pallas_call(
        _ssd_kernel,
        out_shape=[jax.ShapeDtypeStruct((bsz * seq, D_INNER), F32), st_shape, st_shape],
        grid=(bsz // n_seq,),
        in_specs=[per_seq(256), per_seq(512), per_seq(128)] + const_specs + [st_spec, st_spec],
        out_specs=[pl.BlockSpec((rows, D_INNER), lambda b: (b, 0)), st_spec, st_spec],
        scratch_shapes=[pltpu.VMEM((rows, CONV_DIM), F32), pltpu.VMEM((rows, LANES), F32),
                        pltpu.VMEM((rows, LANES), F32), pltpu.VMEM((rows, D_INNER), F32)],
        compiler_params=_params("parallel"),
        name="ssd_lat" if latent else "ssd_ctx",
    )(z, xbc, dt, *consts, h0f, h0b)


def _out_proj_kernel(*refs):
    pairs, (mod_ref, wo_ref, g2_ref, rwh_ref, rwl_ref, rb_ref, x1_ref, h2_ref, te_ref, gt_ref, cnt_ref) = \
        refs[:10], refs[10:]
    x, oa, ob, oc, od = [_pick(pairs[2 * j], pairs[2 * j + 1], PROJ_TM) for j in range(5)]
    m = mod_ref[0]
    gate1, sh2, sc2 = m[2:3], m[3:4], m[4:5]
    mixed = (_dot(oa.astype(BF16), wo_ref[0:256, :])
             + _dot(ob.astype(BF16), wo_ref[256:512, :])
             + _dot(oc.astype(BF16), wo_ref[512:768, :])
             + _dot(od.astype(BF16), wo_ref[768:1024, :]))
    x1 = x + gate1 * mixed
    x1_ref[...] = x1
    ms = jnp.mean(x1 * x1, axis=-1, keepdims=True)
    h2 = (x1 * lax.rsqrt(ms + EPS) * g2_ref[...]) * (1.0 + sc2) + sh2
    h2_ref[...] = h2.astype(BF16)

    hi = h2.astype(BF16)
    lo = (h2 - hi.astype(F32)).astype(BF16)
    logits = _dot(hi, rwh_ref[...]) + _dot(lo, rwh_ref[...]) + _dot(hi, rwl_ref[...]) + rb_ref[...]
    lane = lax.broadcasted_iota(jnp.int32, logits.shape, 1)
    vals, idxs = [], []
    for _ in range(TOP_K):
        mx = jnp.max(logits, axis=-1, keepdims=True)
        ix = jnp.min(jnp.where(logits == mx, lane, LANES), axis=-1, keepdims=True)
        vals.append(mx)
        idxs.append(ix)
        logits = jnp.where(lane == ix, -3e38, logits)
    es = [jnp.exp(v - vals[0]) for v in vals]
    den = es[0] + es[1] + es[2] + es[3]
    te = jnp.zeros(lane.shape, jnp.int32)
    gt = jnp.zeros(lane.shape, F32)
    member = jnp.zeros(lane.shape, F32)
    for k in range(TOP_K):
        te = jnp.where(lane == k, idxs[k], te)
        gt = jnp.where(lane == k, es[k] / den, gt)
        member = jnp.where(lane == idxs[k], 1.0, member)
    te_ref[...] = te[:, 0:TOP_K]
    gt_ref[...] = gt[:, 0:TOP_K]
    for t in range(PROJ_TM // TM):
        count = jnp.sum(member[t * TM:(t + 1) * TM], axis=0, keepdims=True)
        cnt_ref[t] = jnp.broadcast_to(count, (8, LANES)).astype(jnp.int32)


def _out_proj(x, oa, ob, oc, od, mod3, wo, g2, rwh, rwl, rb):
    row = lambda w: pl.BlockSpec((PROJ_TM, w), lambda i: (i, 0))
    full = lambda a: pl.BlockSpec(a.shape, lambda i: (0,) * a.ndim)
    return pl.pallas_call(
        _out_proj_kernel,
        out_shape=[jax.ShapeDtypeStruct((N_TOK, D_MODEL), F32), jax.ShapeDtypeStruct((N_TOK, D_MODEL), BF16),
                   jax.ShapeDtypeStruct((N_TOK, TOP_K), jnp.int32), jax.ShapeDtypeStruct((N_TOK, TOP_K), F32),
                   jax.ShapeDtypeStruct((N_TILES, 8, LANES), jnp.int32)],
        grid=(N_TOK // PROJ_TM,),
        in_specs=_pair_specs(D_MODEL, PROJ_TM) + [s for _ in range(4) for s in _pair_specs(256, PROJ_TM)]
                 + [pl.BlockSpec((1, N_MOD, D_MODEL), lambda i: (_mod_row(i, PROJ_TM), 0, 0)),
                    full(wo), full(g2), full(rwh), full(rwl), full(rb)],
        out_specs=[row(D_MODEL), row(D_MODEL), row(TOP_K), row(TOP_K),
                   pl.BlockSpec((PROJ_TM // TM, 8, LANES), lambda i: (i, 0, 0))],
        compiler_params=_params("parallel"),
        name="out_proj_router",
    )(*x, *oa, *ob, *oc, *od, mod3, wo, g2, rwh, rwl, rb)


def _tile_rows(te, tri_ref, off_ref):
    lane = lax.broadcasted_iota(jnp.int32, (TM, LANES), 1)
    hits = [lane == te[:, k:k + 1] for k in range(TOP_K)]
    member = jnp.zeros((TM, LANES), F32)
    for hit in hits:
        member = jnp.where(hit, 1.0, member)
    rank = _dot(tri_ref[...], member.astype(BF16))
    pos = rank + off_ref[0][0:1, :].astype(F32)
    return [jnp.sum(jnp.where(hit, pos, 0.0), axis=-1, keepdims=True) for hit in hits]


def _segment_chunks(seg_ref, len_ref, tile, visit):
    def segment(e, row):
        length = len_ref[tile * N_EXPERTS + e]
        start = seg_ref[tile * N_EXPERTS + e]
        queue = e % 2

        def chunk(c, carry):
            visit(CHUNKS[0], pl.multiple_of(row + c * CHUNKS[0], SEG_ALIGN),
                  pl.multiple_of(start + c * CHUNKS[0], SEG_ALIGN), queue)
            return carry

        lax.fori_loop(0, lax.shift_right_logical(length, CHUNK_SHIFT), chunk, 0)
        for size in CHUNKS[1:]:
            done = jnp.bitwise_and(length, -2 * size)

            @pl.when(jnp.bitwise_and(length, size) != 0)
            def _():
                visit(size, pl.multiple_of(row + done, SEG_ALIGN), pl.multiple_of(start + done, SEG_ALIGN), queue)
        return row + length

    row = 0
    for e in range(N_EXPERTS):
        row = segment(e, row)


def _wait_chunks(nchunk_ref, tile, copy):
    for j, size in enumerate(CHUNKS):
        lax.fori_loop(0, nchunk_ref[tile * len(CHUNKS) + j], lambda c, carry, size=size: (copy(size).wait(), carry)[1], 0)


def _dispatch_kernel(seg_ref, len_ref, nchunk_ref, tail_ref, tlen_ref, nu_ref, h2_ref, te_ref, off_ref, tri_ref,
                     xb_ref, rows_ref, buf, zeros, sems, sem_z):
    i = pl.program_id(0)
    slot = i % 2

    def chunk_copy(size, s, buf_row, xb_row):
        return pltpu.make_async_copy(buf.at[s, pl.ds(buf_row, size)], xb_ref.at[pl.ds(xb_row, size)], sems.at[s])

    def wait_chunks(tile, s):
        _wait_chunks(nchunk_ref, tile, lambda size: chunk_copy(size, s, 0, 0))

    def zero_fills(visit):
        for e in range(N_EXPERTS):
            start, length = tail_ref[e], tlen_ref[e]
            for size in ZERO_CHUNKS:
                done = jnp.bitwise_and(length, -2 * size)

                @pl.when(jnp.bitwise_and(length, size) != 0)
                def _():
                    visit(pltpu.make_async_copy(
                        zeros.at[pl.ds(0, size)],
                        xb_ref.at[pl.ds(pl.multiple_of(start + done, SEG_ALIGN), size)], sem_z))
        block = lambda b: pltpu.make_async_copy(
            zeros, xb_ref.at[pl.ds(pl.multiple_of(b * MOE_BM, MOE_BM), MOE_BM)], sem_z)
        lax.fori_loop(nu_ref[0], MOE_BLOCKS, lambda b, carry: (visit(block(b)), carry)[1], 0)

    @pl.when(i == 0)
    def _():
        zeros[...] = jnp.zeros_like(zeros)
        zero_fills(lambda cp: cp.start())

    rows = _tile_rows(te_ref[...], tri_ref, off_ref)
    lane = lax.broadcasted_iota(jnp.int32, (TM, LANES), 1)
    packed = jnp.zeros((TM, LANES), F32)
    for k in range(TOP_K):
        packed = jnp.where(lane == k, rows[k], packed)
    rows_ref[...] = packed[:, 0:TOP_K]
    rows_t = packed.T
    buf_row = lax.broadcasted_iota(jnp.int32, (TILE_BUF, TM), 0).astype(F32)
    pick = jnp.zeros((TILE_BUF, TM), F32)
    for k in range(TOP_K):
        pick = jnp.where(buf_row == rows_t[k:k + 1, :], 1.0, pick)
    buf[slot] = _dot(pick.astype(BF16), h2_ref[...])

    @pl.when(i > 0)
    def _():
        wait_chunks(i - 1, 1 - slot)

    _segment_chunks(seg_ref, len_ref, i, lambda size, b, x, q: chunk_copy(size, slot, b, x).start(priority=q))

    @pl.when(i == N_TILES - 1)
    def _():
        wait_chunks(i, slot)
        zero_fills(lambda cp: cp.wait())


def _dispatch(seg_start, seg_len, n_chunk, tail_start, tail_len, n_used, h2, te, seg_off, tri):
    row = lambda w: pl.BlockSpec((TM, w), lambda i, *_: (i, 0))
    grid_spec = pltpu.PrefetchScalarGridSpec(
        num_scalar_prefetch=6,
        grid=(N_TILES,),
        in_specs=[row(D_MODEL), row(TOP_K), pl.BlockSpec((1, 8, LANES), lambda i, *_: (i, 0, 0)),
                  pl.BlockSpec(tri.shape, lambda i, *_: (0, 0))],
        out_specs=[pl.BlockSpec(memory_space=pl.ANY), row(TOP_K)],
        scratch_shapes=[pltpu.VMEM((2, TILE_BUF, D_MODEL), F32), pltpu.VMEM((MOE_BM, D_MODEL), F32),
                        pltpu.SemaphoreType.DMA((2,)), pltpu.SemaphoreType.DMA],
    )
    return pl.pallas_call(
        _dispatch_kernel,
        out_shape=[jax.ShapeDtypeStruct((YB_ROWS, D_MODEL), F32), jax.ShapeDtypeStruct((N_TOK, TOP_K), F32)],
        grid_spec=grid_spec,
        compiler_params=_params("arbitrary"),
        name="moe_dispatch",
    )(seg_start, seg_len, n_chunk, tail_start, tail_len, n_used, h2, te, seg_off, tri)


def _expert_kernel(layer, be_ref, nu_ref, slot_ref, next_ref, x_ref, wgu_hbm, bgu_ref, wdn_hbm, bdn_ref,
                   o_ref, wgu_f, wdn_f, wgu_s, wdn_s, sems):
    i = pl.program_id(0)
    used = i < nu_ref[0]
    expert = be_ref[i]
    fresh = jnp.logical_or(i == 0, expert != be_ref[jnp.maximum(i - 1, 0)])
    slot = slot_ref[expert]

    def fetch(e, s):
        return (pltpu.make_async_copy(wgu_hbm.at[layer, e], wgu_f.at[s], sems.at[0, s]),
                pltpu.make_async_copy(wdn_hbm.at[layer, e], wdn_f.at[s], sems.at[1, s]))

    @pl.when(i == 0)
    def _():
        for cp in fetch(expert, slot):
            cp.start()

    @pl.when(jnp.logical_and(used, fresh))
    def _():
        for cp in fetch(expert, slot):
            cp.wait()
        nxt = next_ref[expert]

        @pl.when(nxt >= 0)
        def _():
            for cp in fetch(nxt, 1 - slot):
                cp.start()

        wgu_s[...] = wgu_f[slot].astype(BF16)
        wdn_s[...] = wdn_f[slot].astype(BF16)

    @pl.when(used)
    def _():
        hgu = _dot(x_ref[...].astype(BF16), wgu_s[...]) + bgu_ref[0, 0]
        gate = jnp.minimum(hgu[:, :D_FF], SWIGLU_LIMIT)
        up = jnp.clip(hgu[:, D_FF:], -SWIGLU_LIMIT, SWIGLU_LIMIT)
        act = (up + 1.0) * gate * jax.nn.sigmoid(SWIGLU_ALPHA * gate)
        o_ref[...] = _dot(act.astype(BF16), wdn_s[...]) + bdn_ref[0, 0]

    @pl.when(jnp.logical_not(used))
    def _():
        o_ref[...] = jnp.zeros_like(o_ref)


def _experts(layer, block_e, n_used, w_slot, next_e, xb, w_gu, b_gu, w_dn, b_dn):
    grid_spec = pltpu.PrefetchScalarGridSpec(
        num_scalar_prefetch=4,
        grid=(MOE_BLOCKS,),
        in_specs=[pl.BlockSpec((MOE_BM, D_MODEL), lambda i, be, *_: (i, 0)),
                  pl.BlockSpec(memory_space=pl.ANY),
                  pl.BlockSpec((1, 1, 1, 2 * D_FF), lambda i, be, *_: (layer, be[i], 0, 0)),
                  pl.BlockSpec(memory_space=pl.ANY),
                  pl.BlockSpec((1, 1, 1, D_MODEL), lambda i, be, *_: (layer, be[i], 0, 0))],
        out_specs=pl.BlockSpec((MOE_BM, D_MODEL), lambda i, be, *_: (i, 0)),
        scratch_shapes=[pltpu.VMEM((2, D_MODEL, 2 * D_FF), F32), pltpu.VMEM((2, D_FF, D_MODEL), F32),
                        pltpu.VMEM((D_MODEL, 2 * D_FF), BF16), pltpu.VMEM((D_FF, D_MODEL), BF16),
                        pltpu.SemaphoreType.DMA((2, 2))],
    )
    return pl.pallas_call(
        functools.partial(_expert_kernel, layer),
        out_shape=jax.ShapeDtypeStruct((YB_ROWS, D_MODEL), F32),
        grid_spec=grid_spec,
        compiler_params=_params("arbitrary"),
        name="experts",
    )(block_e, n_used, w_slot, next_e, xb, w_gu, b_gu.reshape(DEPTH, N_EXPERTS, 1, 2 * D_FF), w_dn,
      b_dn.reshape(DEPTH, N_EXPERTS, 1, D_MODEL))


def _combine_kernel(final, seg_ref, len_ref, nchunk_ref, x1_ref, rows_ref, gt_ref, mod_ref, fg_ref,
                    yb_ref, o_ctx_ref, o_lat_ref, buf, sems):
    i = pl.program_id(0)
    slot = i % 2

    def chunk_copy(size, s, buf_row, yb_row):
        return pltpu.make_async_copy(yb_ref.at[pl.ds(yb_row, size)], buf.at[s, pl.ds(buf_row, size)], sems.at[s])

    def request(tile, s):
        _segment_chunks(seg_ref, len_ref, tile, lambda size, b, y, q: chunk_copy(size, s, b, y).start(priority=q))

    @pl.when(i == 0)
    def _():
        buf[...] = jnp.zeros_like(buf)
        request(0, 0)

    @pl.when(i + 1 < N_TILES)
    def _():
        request(i + 1, 1 - slot)

    rows = rows_ref[...]
    gt = gt_ref[...]
    buf_row = lax.broadcasted_iota(jnp.int32, (TM, TILE_BUF), 1).astype(F32)
    place = jnp.zeros((TM, TILE_BUF), F32)
    for k in range(TOP_K):
        place = jnp.where(buf_row == rows[:, k:k + 1], gt[:, k:k + 1], place)

    _wait_chunks(nchunk_ref, i, lambda size: chunk_copy(size, slot, 0, 0))
    y = _dot(place.astype(BF16), buf[slot].astype(BF16))
    x2 = x1_ref[...] + mod_ref[0][5:6] * y
    if final:
        ms = jnp.mean(x2 * x2, axis=-1, keepdims=True)
        x2 = x2 * lax.rsqrt(ms + EPS) * fg_ref[...]

    @pl.when(i < CTX_TILES)
    def _():
        o_ctx_ref[...] = x2

    @pl.when(i >= CTX_TILES)
    def _():
        o_lat_ref[...] = x2


def _combine(seg_start, seg_len, n_chunk, x1, rows, gates, mod3, fg, yb, final):
    row = lambda w: pl.BlockSpec((TM, w), lambda i, *_: (i, 0))
    full = lambda a: pl.BlockSpec(a.shape, lambda i, *_: (0,) * a.ndim)
    out_shape = [jax.ShapeDtypeStruct((N_CTX, D_MODEL), F32), jax.ShapeDtypeStruct((N_LAT, D_MODEL), F32)]
    out_specs = _pair_specs(D_MODEL, TM)
    grid_spec = pltpu.PrefetchScalarGridSpec(
        num_scalar_prefetch=3,
        grid=(N_TILES,),
        in_specs=[row(D_MODEL), row(TOP_K), row(TOP_K),
                  pl.BlockSpec((1, N_MOD, D_MODEL), lambda i, *_: (_mod_row(i), 0, 0)),
                  full(fg),
                  pl.BlockSpec(memory_space=pl.ANY)],
        out_specs=out_specs,
        scratch_shapes=[pltpu.VMEM((2, TILE_BUF, D_MODEL), F32), pltpu.SemaphoreType.DMA((2,))],
    )
    return pl.pallas_call(
        functools.partial(_combine_kernel, final),
        out_shape=out_shape,
        grid_spec=grid_spec,
        compiler_params=_params("arbitrary"),
        name="moe_combine",
    )(seg_start, seg_len, n_chunk, x1, rows, gates, mod3, fg, yb)


def _gqa_head_order(w, axis):
    take = lambda start: lax.slice_in_dim(w, start, start + HEAD_DIM, axis=axis)
    return [take(kv * 128 + g * 64) for g in range(2) for kv in range(2)]


def _w1_layout(w_in):
    cols = lambda name, n: w_in[:, IN_OFF[name]:IN_OFF[name] + n]
    zeros = lambda n: jnp.zeros((D_MODEL, n), F32)
    pieces = (_gqa_head_order(w_in, 1) + [cols('a_k', 256)]
              + [cols('b_cq', 192), zeros(64), cols('b_ckv', 128), cols('b_kr', 32), zeros(96)]
              + [cols('c_q', 768), cols('d_z', 256), cols('d_xbc', 512), cols('d_dtf', 8), zeros(120)])
    return jnp.concatenate(pieces, axis=1).astype(BF16)


def _rope_tables():
    t = np.arange(DEC_SEQ)
    pos = ((t // GRID_W).astype(np.float32), (t % GRID_W).astype(np.float32))

    def unit(rot_dim):
        quarter, half = rot_dim // 4, rot_dim // 2
        inv = ROPE_THETA ** (-np.arange(0, half, 2, dtype=np.float32) / half)
        cos = np.zeros((DEC_SEQ, rot_dim), np.float32)
        sin = np.zeros((DEC_SEQ, rot_dim), np.float32)
        for seg in range(4):
            ang = pos[seg // 2][:, None] * inv[None, :].astype(np.float32)
            cos[:, seg * quarter:(seg + 1) * quarter] = np.cos(ang)
            sin[:, seg * quarter:(seg + 1) * quarter] = np.sin(ang) * (-1.0 if seg % 2 == 0 else 1.0)
        return cos, sin

    log2e = math.log2(math.e)
    specs = [
        (HEAD_DIM, (0, 64), (), HEAD_DIM ** -0.5 * log2e),
        (HEAD_DIM, (0, 64), (), 1.0),
        (MLA_ROPE, (MLA_NOPE,), (0, MLA_NOPE), (MLA_NOPE + MLA_ROPE) ** -0.5 * log2e),
        (MLA_ROPE, (0,), (), 1.0),
        (DIFF_D, (0, 32, 64, 96), (), DIFF_D ** -0.5 * log2e),
        (DIFF_D, (0, 32, 64, 96), (), 1.0),
    ]
    lat_cols, ident_cols = [], []
    for rot_dim, starts, passthrough, scale in specs:
        ucos, usin = unit(rot_dim)
        cos = np.zeros((DEC_SEQ, LANES), np.float32)
        sin = np.zeros((DEC_SEQ, LANES), np.float32)
        ident = np.zeros((1, LANES), np.float32)
        if passthrough:
            cos[:, passthrough[0]:passthrough[1]] = 1.0
            ident[:, passthrough[0]:passthrough[1]] = 1.0
        for s in starts:
            cos[:, s:s + rot_dim] = ucos
            sin[:, s:s + rot_dim] = usin
            ident[:, s:s + rot_dim] = 1.0
        lat_cols += [cos * scale, sin * scale]
        ident_cols += [ident * scale, np.zeros((1, LANES), np.float32)]
    lat = np.concatenate(lat_cols, axis=1)
    ident_blk = np.broadcast_to(np.concatenate(ident_cols, axis=1), (PROJ_TM, N_TAB * LANES))
    return np.concatenate([ident_blk, lat], axis=0).astype(np.float32)


def _block_diag_ones(n, blk):
    r = np.arange(n)
    return (r[:, None] // blk == r[None, :] // blk).astype(np.float32)


def _chunk_tables(seg_len):
    counts = [jnp.sum(seg_len // CHUNKS[0], axis=1)]
    counts += [jnp.sum(seg_len % (2 * size) // size, axis=1) for size in CHUNKS[1:]]
    return jnp.stack(counts, axis=1).reshape(-1).astype(jnp.int32)


def _strict_lower_ones(n):
    r = np.arange(n)
    return (r[None, :] < r[:, None]).astype(np.float32)


def kernel(x_prompt, x_sample, cache_gqa_k, cache_gqa_v, cache_mla_ckv, cache_mla_krope, cache_diff_k, cache_diff_v, state_ssd_fwd, state_ssd_bwd, c, c_ctx, norm1_g, norm2_g, w_ada, b_ada, w_in, w_out, gqa_qn_g, gqa_kn_g, mla_qa_g, mla_wqb, mla_kva_g, mla_wkvb, diff_lq1, diff_lk1, diff_lq2, diff_lk2, diff_subln_g, ssd_conv_w, ssd_conv_b, ssd_a_log_f, ssd_a_log_b, ssd_dt_bias_f, ssd_dt_bias_b, ssd_d, ssd_norm_g, router_w, router_b, moe_w_gu, moe_b_gu, moe_w_dn, moe_b_dn, final_g):
    tabs = jnp.asarray(_rope_tables())
    bd = jnp.asarray(_block_diag_ones(256, HEAD_DIM), BF16)
    tri = jnp.asarray(_strict_lower_ones(TM), BF16)

    cvec = jnp.zeros((MOD_ROWS, D_MODEL), F32).at[0].set(c_ctx).at[1:1 + DEC_BATCH].set(c)
    mod = _modulation(cvec, w_ada, b_ada).reshape(DEPTH, MOD_ROWS, N_MOD, D_MODEL)

    x = (x_prompt.reshape(N_CTX, D_MODEL), x_sample.reshape(N_LAT, D_MODEL))
    new_ctx = []
    for l in range(DEPTH):
        mod3 = mod[l]
        w1 = _w1_layout(w_in[l])
        wqb = mla_wqb[l].reshape(Q_LORA, B_HEADS, MLA_NOPE + MLA_ROPE)
        wqb = jnp.pad(wqb, ((0, 256 - Q_LORA), (0, 0), (0, LANES - MLA_NOPE - MLA_ROPE)))
        wqb = wqb.reshape(256, B_HEADS * LANES).astype(BF16)
        wkvb = mla_wkvb[l].reshape(KV_LORA, B_HEADS, MLA_NOPE + MLA_V)
        wk_nope = jnp.pad(wkvb[:, :, :MLA_NOPE], ((0, 0), (0, 0), (0, LANES - MLA_NOPE)))
        eye_r = jnp.zeros((LANES, B_HEADS, LANES), F32)
        eye_r = eye_r.at[jnp.arange(MLA_ROPE), :, MLA_NOPE + jnp.arange(MLA_ROPE)].set(1.0)
        wk = jnp.concatenate([wk_nope, eye_r], axis=0).reshape(256, B_HEADS * LANES).astype(BF16)
        wv = wkvb[:, :, MLA_NOPE:].reshape(KV_LORA, B_HEADS * MLA_V).astype(BF16)
        wo = jnp.concatenate(_gqa_head_order(w_out[l], 0) + [w_out[l][256:]], axis=0).astype(BF16)
        gq = jnp.tile(gqa_qn_g[l], 4)[None, :]
        gk = jnp.tile(gqa_kn_g[l], 2)[None, :]
        gqa = jnp.pad(mla_qa_g[l], (0, 256 - Q_LORA))[None, :]
        gkva = mla_kva_g[l][None, :]
        rw = jnp.pad(router_w[l], ((0, 0), (0, LANES - N_EXPERTS)))
        rwh = rw.astype(BF16)
        rwl = (rw - rwh.astype(F32)).astype(BF16)
        rb = jnp.pad(router_b[l], (0, LANES - N_EXPERTS), constant_values=NEG_BIG)[None, :]

        qa, kva, qb, ck, qc, kc, vc, z, xbc, dt = _in_proj(
            x, mod3, norm1_g[l][None, :], w1, bd, gq, gk, gqa, gkva, wqb, tabs)

        past_kva = jnp.concatenate([cache_gqa_k[:, l].reshape(DEC_BATCH, PAST_LEN, 128),
                                    cache_gqa_v[:, l].reshape(DEC_BATCH, PAST_LEN, 128)], axis=-1)
        past_ck = jnp.concatenate([cache_mla_ckv[:, l], cache_mla_krope[:, l],
                                   jnp.zeros((DEC_BATCH, PAST_LEN, LANES - MLA_ROPE), F32)], axis=-1)
        past_kc = cache_diff_k[:, l].reshape(DEC_BATCH, PAST_LEN, 256)
        past_vc = cache_diff_v[:, l].reshape(DEC_BATCH, PAST_LEN, 256)
        lams = [a[l][None, :] for a in (diff_lq1, diff_lk1, diff_lq2, diff_lk2)]
        gsub = jnp.tile(diff_subln_g[l], 2)[None, :]
        lam_init = 0.8 - 0.6 * math.exp(-0.3 * l)
        oa, ob, oc = _attention_mixers(qa, kva, qb, ck, qc, kc, vc, past_kva, past_ck, past_kc, past_vc,
                                       wk, wv, lams, gsub, lam_init)

        pad8 = lambda f, b: jnp.pad(jnp.concatenate([f, b]), (0, LANES - 2 * D_HEADS))[None, :]
        ssd_consts = (ssd_conv_w[l], ssd_conv_b[l][None, :], pad8(ssd_dt_bias_f[l], ssd_dt_bias_b[l]),
                      pad8(ssd_a_log_f[l], ssd_a_log_b[l]), jnp.repeat(ssd_d[l], D_HEADDIM)[None, :],
                      ssd_norm_g[l][None, :])
        zeros_st = jnp.zeros((BATCH, D_HEADS, D_STATE, D_HEADDIM), F32)
        od_c, hf_c, hb_c = _ssd(False, z, xbc, dt, ssd_consts, zeros_st, zeros_st)
        od_l, _, _ = _ssd(True, z, xbc, dt, ssd_consts, jnp.swapaxes(state_ssd_fwd[:, l], -1, -2),
                          jnp.swapaxes(state_ssd_bwd[:, l], -1, -2))
        od = (od_c, od_l)

        x1, h2, te, gates, tile_cnt = _out_proj(x, oa, ob, oc, od, mod3, wo, norm2_g[l][None, :], rwh, rwl, rb)

        seg_cnt = tile_cnt[:, 0, :N_EXPERTS]
        seg_len = (seg_cnt + SEG_ALIGN - 1) // SEG_ALIGN * SEG_ALIGN
        region = (jnp.sum(seg_len, axis=0) + MOE_BM - 1) // MOE_BM * MOE_BM
        pad_end = jnp.cumsum(region).astype(jnp.int32)
        pad_start = pad_end - region
        seg_start = (pad_start[None, :] + jnp.cumsum(seg_len, axis=0) - seg_len).astype(jnp.int32)
        n_used = (pad_end[-1] // MOE_BM).astype(jnp.int32).reshape(1)
        blk_start = jnp.arange(MOE_BLOCKS, dtype=jnp.int32) * MOE_BM
        block_e = jnp.minimum(jnp.sum((pad_end[None, :] <= blk_start[:, None]).astype(jnp.int32), axis=1),
                              N_EXPERTS - 1).astype(jnp.int32)
        n_chunk = _chunk_tables(seg_len)
        seg_off = jnp.cumsum(seg_len, axis=1) - seg_len
        seg_off = jnp.broadcast_to(jnp.pad(seg_off, ((0, 0), (0, LANES - N_EXPERTS)))[:, None, :],
                                   (N_TILES, 8, LANES)).astype(jnp.int32)
        seg_start, seg_len = seg_start.reshape(-1), seg_len.reshape(-1).astype(jnp.int32)

        tail_start = (pad_end - region + jnp.sum(seg_len.reshape(N_TILES, N_EXPERTS), axis=0)).astype(jnp.int32)
        tail_len = (pad_end - tail_start).astype(jnp.int32)
        xb, pair_rows = _dispatch(seg_start, seg_len, n_chunk, tail_start, tail_len, n_used, h2, te, seg_off, tri)
        has_rows = region > 0
        w_slot = ((jnp.cumsum(has_rows.astype(jnp.int32)) - 1) % 2).astype(jnp.int32)
        later = lax.cummin(jnp.where(has_rows, jnp.arange(N_EXPERTS, dtype=jnp.int32), N_EXPERTS), reverse=True)
        later = jnp.concatenate([later[1:], jnp.full((1,), N_EXPERTS, jnp.int32)])
        next_e = jnp.where(later < N_EXPERTS, later, -1).astype(jnp.int32)
        yb = _experts(l, block_e, n_used, w_slot, next_e, xb, moe_w_gu, moe_b_gu, moe_w_dn, moe_b_dn)
        x = _combine(seg_start, seg_len, n_chunk, x1, pair_rows, gates, mod3, final_g[None, :], yb,
                     l == DEPTH - 1)

        kva_c, ck_c = kva[:N_CTX], ck[:N_CTX]
        new_ctx.append((kva_c[:, 0:128].reshape(BATCH, SEQ, A_KV_HEADS, HEAD_DIM),
                        kva_c[:, 128:256].reshape(BATCH, SEQ, A_KV_HEADS, HEAD_DIM),
                        ck_c[:, 0:KV_LORA].reshape(BATCH, SEQ, KV_LORA),
                        ck_c[:, KV_LORA:KV_LORA + MLA_ROPE].reshape(BATCH, SEQ, MLA_ROPE),
                        kc[:N_CTX].reshape(BATCH, SEQ, C_HEADS, 2 * DIFF_D),
                        vc[:N_CTX].reshape(BATCH, SEQ, C_HEADS, DIFF_V),
                        jnp.swapaxes(hf_c, -1, -2), jnp.swapaxes(hb_c, -1, -2)))

    y_prompt = x[0].reshape(BATCH, SEQ, D_MODEL)
    y_sample = x[1].reshape(DEC_BATCH, DEC_SEQ, D_MODEL)
    caches = [jnp.stack([cl[i] for cl in new_ctx], axis=1) for i in range(8)]
    return (y_prompt, y_sample, *caches)
```

```python
import functools
import math

import numpy as np
import jax
import jax.numpy as jnp
from jax import lax
from jax.experimental import pallas as pl
from jax.experimental.pallas import tpu as pltpu

F32 = jnp.float32
BF16 = jnp.bfloat16

D_MODEL = 1024
BATCH = 16
SEQ = 256
DEPTH = 2
DEC_BATCH = 8
DEC_SEQ = 2048
PAST_LEN = 256
GRID_W = 64
ROPE_THETA = 10000.0
EPS = 1e-6
HEAD_DIM = 64
A_HEADS = 4
A_KV_HEADS = 2
B_HEADS = 4
MLA_NOPE = 64
MLA_ROPE = 32
MLA_V = 64
Q_LORA = 192
KV_LORA = 128
C_HEADS = 4
DIFF_D = 32
DIFF_V = 64
SUBLN_EPS = 1e-5
D_HEADS = 4
D_HEADDIM = 64
D_INNER = 256
D_GROUPS = 2
D_STATE = 64
CONV_DIM = 512
SSD_CHUNK = 128
N_EXPERTS = 32
TOP_K = 4
D_FF = 1024
SWIGLU_ALPHA = 1.702
SWIGLU_LIMIT = 7.0
N_MOD = 6

N_CTX = BATCH * SEQ
N_LAT = DEC_BATCH * DEC_SEQ
N_TOK = N_CTX + N_LAT

LANES = 128
TM = 256
N_TILES = N_TOK // TM
CTX_TILES = N_CTX // TM
LAT_TQ = 1024
PROJ_TM = 512
MOD_ROWS = 16
MOE_BM = 512
N_PAIRS = N_TOK * TOP_K
SEG_ALIGN = 8
CHUNKS = (32, 16, 8)
CHUNK_SHIFT = 5
MOE_BLOCKS = -(-(N_PAIRS + N_TILES * N_EXPERTS * (SEG_ALIGN - 1) + N_EXPERTS * (MOE_BM - 1)) // MOE_BM)
YB_ROWS = MOE_BLOCKS * MOE_BM
TILE_BUF = -(-(TM * TOP_K + N_EXPERTS * (SEG_ALIGN - 1)) // LANES) * LANES
ZERO_CHUNKS = tuple(MOE_BM >> k for k in range(1, (MOE_BM // SEG_ALIGN).bit_length()))
VMEM_LIMIT = 56 * 1024 * 1024
NEG_BIG = -1e30

W1_COLS = 2688
IN_OFF = dict(a_q=0, a_k=256, a_v=384, b_cq=512, b_ckv=704, b_kr=832, c_q=864, c_k=1120,
              c_v=1376, d_z=1632, d_xbc=1888, d_dtf=2400, d_dtb=2404)
IN_WIDTH = 2408
N_TAB = 12


def _mod_row(i, tile=TM):
    return jnp.where(i < N_CTX // tile, 0, 1 + (i - N_CTX // tile) // (DEC_SEQ // tile))


def _tab_block(i, tile):
    return jnp.where(i < N_CTX // tile, 0, 1 + (i - N_CTX // tile) % (DEC_SEQ // tile))


def _pair_specs(width, tile):
    n_ctx = N_CTX // tile
    return [pl.BlockSpec((tile, width), lambda i, *_: (jnp.minimum(i, n_ctx - 1), 0)),
            pl.BlockSpec((tile, width), lambda i, *_: (jnp.maximum(i - n_ctx, 0), 0))]


def _pick(ctx_ref, lat_ref, tile):
    return jnp.where(pl.program_id(0) < N_CTX // tile, ctx_ref[...], lat_ref[...])


def _dot(a, b):
    return jnp.dot(a, b, preferred_element_type=F32)


def _dot_nt(a, b):
    return lax.dot_general(a, b, (((1,), (1,)), ((), ())), preferred_element_type=F32)


def _dot_split(x, m):
    hi = x.astype(BF16)
    lo = (x - hi.astype(F32)).astype(BF16)
    return _dot(hi, m) + _dot(lo, m)


def _dot_split_left(m, x):
    hi = x.astype(BF16)
    lo = (x - hi.astype(F32)).astype(BF16)
    return _dot(m, hi) + _dot(m, lo)


def _silu(x):
    return x * jax.nn.sigmoid(x)


def _params(*semantics):
    return pltpu.CompilerParams(dimension_semantics=semantics, vmem_limit_bytes=VMEM_LIMIT)


MOD_TN = 1536


def _mod_kernel(c_ref, w_ref, b_ref, o_ref):
    c = c_ref[...]
    s = _silu(c).astype(BF16)
    o_ref[0] = _dot(s, w_ref[0].astype(BF16)) + b_ref[0]


def _modulation(cvec, w_ada, b_ada):
    n = N_MOD * D_MODEL
    return pl.pallas_call(
        _mod_kernel,
        out_shape=jax.ShapeDtypeStruct((DEPTH, MOD_ROWS, n), F32),
        grid=(DEPTH, n // MOD_TN),
        in_specs=[pl.BlockSpec((MOD_ROWS, D_MODEL), lambda l, j: (0, 0)),
                  pl.BlockSpec((1, D_MODEL, MOD_TN), lambda l, j: (l, 0, j)),
                  pl.BlockSpec((1, 1, MOD_TN), lambda l, j: (l, 0, j))],
        out_specs=pl.BlockSpec((1, MOD_ROWS, MOD_TN), lambda l, j: (l, 0, j)),
        compiler_params=_params("parallel", "parallel"),
        name="adaln_mod",
    )(cvec, w_ada, b_ada.reshape(DEPTH, 1, n))


def _rope(x, cos, sin, quarter):
    lane = lax.broadcasted_iota(jnp.int32, (x.shape[0], LANES), 1)
    first = (lane // quarter) % 2 == 0
    outs = []
    for t in range(x.shape[1] // LANES):
        xt = x[:, t * LANES:(t + 1) * LANES]
        partner = jnp.where(first, pltpu.roll(xt, LANES - quarter, 1), pltpu.roll(xt, quarter, 1))
        outs.append(xt * cos + partner * sin)
    return outs


def _in_proj_kernel(xc_ref, xl_ref, mod_ref, g1_ref, w1_ref, bd_ref, gq_ref, gk_ref, gqa_ref, gkva_ref,
                    wqb_ref, tab_ref,
                    qa_ref, kva_ref, qb_ref, ck_ref, qc_ref, kc_ref, vc_ref, z_ref, xbc_ref, dt_ref):
    x = _pick(xc_ref, xl_ref, PROJ_TM)
    m = mod_ref[0]
    sh1, sc1 = m[0:1], m[1:2]
    ms = jnp.mean(x * x, axis=-1, keepdims=True)
    h = (x * lax.rsqrt(ms + EPS) * g1_ref[...]) * (1.0 + sc1) + sh1
    u = _dot(h.astype(BF16), w1_ref[...])

    def tab(k):
        return tab_ref[:, k * LANES:(k + 1) * LANES]

    bd = bd_ref[...]

    def head_norm(v, gain):
        w = v.shape[1]
        ss = _dot_split(v * v, bd[:w, :w])
        return v * lax.rsqrt(ss * (1.0 / HEAD_DIM) + EPS) * gain

    qa = _rope(head_norm(u[:, 0:256], gq_ref[...]), tab(0), tab(1), HEAD_DIM // 4)
    for t in range(2):
        qa_ref[:, t * LANES:(t + 1) * LANES] = qa[t].astype(BF16)
    ka = _rope(head_norm(u[:, 256:384], gk_ref[...]), tab(2), tab(3), HEAD_DIM // 4)
    kva_ref[:, 0:128] = ka[0]
    kva_ref[:, 128:256] = u[:, 384:512]

    cq = u[:, 512:768]
    msq = jnp.sum(cq * cq, axis=-1, keepdims=True) * (1.0 / Q_LORA)
    yq = cq * lax.rsqrt(msq + EPS) * gqa_ref[...]
    qb = _rope(_dot(yq.astype(BF16), wqb_ref[...]), tab(4), tab(5), MLA_ROPE // 4)
    for t in range(4):
        qb_ref[:, t * LANES:(t + 1) * LANES] = qb[t].astype(BF16)
    ckv = u[:, 768:896]
    msk = jnp.mean(ckv * ckv, axis=-1, keepdims=True)
    ck_ref[:, 0:128] = ckv * lax.rsqrt(msk + EPS) * gkva_ref[...]
    ck_ref[:, 128:256] = _rope(u[:, 896:1024], tab(6), tab(7), MLA_ROPE // 4)[0]

    qc = _rope(u[:, 1024:1280], tab(8), tab(9), DIFF_D // 4)
    kc = _rope(u[:, 1280:1536], tab(10), tab(11), DIFF_D // 4)
    for t in range(2):
        qc_ref[:, t * LANES:(t + 1) * LANES] = qc[t].astype(BF16)
        kc_ref[:, t * LANES:(t + 1) * LANES] = kc[t]
    vc_ref[...] = u[:, 1536:1792]

    z_ref[...] = u[:, 1792:2048]
    xbc_ref[...] = u[:, 2048:2560]
    dt_ref[...] = u[:, 2560:2688]


def _in_proj(x, mod3, g1, w1, bd, gq, gk, gqa, gkva, wqb, tabs):
    row = lambda w: pl.BlockSpec((PROJ_TM, w), lambda i: (i, 0))
    full = lambda a: pl.BlockSpec(a.shape, lambda i: (0,) * a.ndim)
    outs = [(256, BF16), (256, F32), (512, BF16), (256, F32), (256, BF16), (256, F32), (256, F32),
            (256, F32), (512, F32), (128, F32)]
    return pl.pallas_call(
        _in_proj_kernel,
        out_shape=[jax.ShapeDtypeStruct((N_TOK, w), d) for w, d in outs],
        grid=(N_TOK // PROJ_TM,),
        in_specs=_pair_specs(D_MODEL, PROJ_TM)
                 + [pl.BlockSpec((1, N_MOD, D_MODEL), lambda i: (_mod_row(i, PROJ_TM), 0, 0)),
                    full(g1), full(w1), full(bd), full(gq), full(gk), full(gqa), full(gkva), full(wqb),
                    pl.BlockSpec((PROJ_TM, N_TAB * LANES), lambda i: (_tab_block(i, PROJ_TM), 0))],
        out_specs=[row(w) for w, _ in outs],
        compiler_params=_params("parallel"),
        name="in_proj",
    )(*x, mod3, g1, w1, bd, gq, gk, gqa, gkva, wqb, tabs)


def _attend(q, k, v_ones):
    s = _dot_nt(q, k)
    e = jnp.exp2(s - jnp.max(s, axis=-1, keepdims=True))
    out = _dot(e.astype(BF16), v_ones)
    return out[:, :LANES] / out[:, LANES:]


def _with_ones(v):
    return jnp.concatenate([v, jnp.ones_like(v)], axis=-1)


def _half_mask(rows):
    lane = lax.broadcasted_iota(jnp.int32, (rows, LANES), 1)
    return lane < (LANES // 2)


def _keys(past_ref, new_ref, lo, hi):
    new = new_ref[:, lo:hi].astype(BF16)
    if past_ref is None:
        return new
    return jnp.concatenate([past_ref[0, :, lo:hi].astype(BF16), new], axis=0)


def _attn_a_kernel(latent, *refs):
    if latent:
        q_ref, kv_ref, past_ref, o_ref, k_s, v_s = refs
    else:
        (q_ref, kv_ref, o_ref, k_s, v_s), past_ref = refs, None

    @pl.when(pl.program_id(1) == 0)
    def _():
        k_s[...] = _keys(past_ref, kv_ref, 0, 128)
        v_s[...] = _with_ones(_keys(past_ref, kv_ref, 128, 256))

    q = q_ref[...]
    k = k_s[...]
    v = v_s[...]
    lo = _half_mask(q.shape[0])
    for g in range(2):
        qt = q[:, g * LANES:(g + 1) * LANES].astype(F32)
        res = []
        for half in range(2):
            qm = jnp.where(lo, qt, 0.0) if half == 0 else jnp.where(lo, 0.0, qt)
            res.append(_attend(qm.astype(BF16), k, v))
        o_ref[:, g * LANES:(g + 1) * LANES] = jnp.where(lo, res[0], res[1])


def _attn_b_kernel(latent, *refs):
    if latent:
        q_ref, ck_ref, past_ref, wk_ref, wv_ref, o_ref, k_s, v_s = refs
    else:
        (q_ref, ck_ref, wk_ref, wv_ref, o_ref, k_s, v_s), past_ref = refs, None

    @pl.when(pl.program_id(1) == 0)
    def _():
        ck = _keys(past_ref, ck_ref, 0, 256)
        k_s[...] = _dot(ck, wk_ref[...]).astype(BF16)
        v = _dot(ck[:, 0:128], wv_ref[...]).astype(BF16)
        for j in range(2):
            v_s[:, 2 * j * LANES:2 * (j + 1) * LANES] = _with_ones(v[:, j * LANES:(j + 1) * LANES])

    q = q_ref[...]
    lo = _half_mask(q.shape[0])
    for j in range(2):
        v = v_s[:, 2 * j * LANES:2 * (j + 1) * LANES]
        res = []
        for half in range(2):
            h = 2 * j + half
            res.append(_attend(q[:, h * LANES:(h + 1) * LANES], k_s[:, h * LANES:(h + 1) * LANES], v))
        o_ref[:, j * LANES:(j + 1) * LANES] = jnp.where(lo, res[0], res[1])


def _attn_c_kernel(lam_init, latent, *refs):
    if latent:
        q_ref, k_ref, v_ref, pk_ref, pv_ref, lq1_ref, lk1_ref, lq2_ref, lk2_ref, g_ref, o_ref, k_s, v_s = refs
    else:
        (q_ref, k_ref, v_ref, lq1_ref, lk1_ref, lq2_ref, lk2_ref, g_ref, o_ref, k_s, v_s), pk_ref, pv_ref = \
            refs, None, None

    @pl.when(pl.program_id(1) == 0)
    def _():
        for j in range(2):
            k_s[:, j * LANES:(j + 1) * LANES] = _keys(pk_ref, k_ref, j * LANES, (j + 1) * LANES)
            v_s[:, 2 * j * LANES:2 * (j + 1) * LANES] = _with_ones(_keys(pv_ref, v_ref, j * LANES, (j + 1) * LANES))

    lam = (jnp.exp(jnp.sum(lq1_ref[...] * lk1_ref[...], axis=-1, keepdims=True))
           - jnp.exp(jnp.sum(lq2_ref[...] * lk2_ref[...], axis=-1, keepdims=True)) + lam_init)
    q = q_ref[...]
    rows = q.shape[0]
    lane = lax.broadcasted_iota(jnp.int32, (rows, LANES), 1)
    lo = lane < (LANES // 2)
    for j in range(2):
        qt = q[:, j * LANES:(j + 1) * LANES].astype(F32)
        k = k_s[:, j * LANES:(j + 1) * LANES]
        v = v_s[:, 2 * j * LANES:2 * (j + 1) * LANES]
        res = []
        for half in range(2):
            parts = []
            for t in range(2):
                quarter = 2 * half + t
                qm = jnp.where(lane // (LANES // 4) == quarter, qt, 0.0)
                parts.append(_attend(qm.astype(BF16), k, v))
            res.append(parts[0] - lam * parts[1])
        o = jnp.where(lo, res[0], res[1])
        o2 = o * o
        ss_lo = jnp.sum(jnp.where(lo, o2, 0.0), axis=-1, keepdims=True)
        ss_hi = jnp.sum(jnp.where(lo, 0.0, o2), axis=-1, keepdims=True)
        ss = jnp.where(lo, ss_lo, ss_hi) * (1.0 / DIFF_V)
        o_ref[:, j * LANES:(j + 1) * LANES] = (o * lax.rsqrt(ss + SUBLN_EPS) * g_ref[...]) * (1.0 - lam_init)


def _seq_call(body, name, latent, q, news, pasts, consts, scratch=(), q_semantics="parallel"):
    const_specs = [pl.BlockSpec(a.shape, lambda b, i, n=a.ndim: (0,) * n) for a in consts]
    if latent:
        tq, rows = LAT_TQ, N_LAT
        tile = lambda b, i: (N_CTX // LAT_TQ + b * (DEC_SEQ // LAT_TQ) + i, 0)
        out_tile = lambda b, i: (b * (DEC_SEQ // LAT_TQ) + i, 0)
        grid = (DEC_BATCH, DEC_SEQ // LAT_TQ)
        new_specs = [pl.BlockSpec((DEC_SEQ, a.shape[1]), lambda b, i: (N_CTX // DEC_SEQ + b, 0)) for a in news]
        past_specs = [pl.BlockSpec((1, PAST_LEN, a.shape[2]), lambda b, i: (b, 0, 0)) for a in pasts]
    else:
        tq, rows = SEQ, N_CTX
        tile = out_tile = lambda b, i: (b, 0)
        grid = (BATCH, 1)
        new_specs = [pl.BlockSpec((SEQ, a.shape[1]), tile) for a in news]
        past_specs, pasts = [], ()
    return pl.pallas_call(
        functools.partial(body, latent),
        out_shape=jax.ShapeDtypeStruct((rows, 256), F32),
        grid=grid,
        in_specs=[pl.BlockSpec((tq, q.shape[1]), tile)] + new_specs + past_specs + const_specs,
        out_specs=pl.BlockSpec((tq, 256), out_tile),
        scratch_shapes=list(scratch),
        compiler_params=_params("parallel", q_semantics),
        name=name + ("_lat" if latent else "_ctx"),
    )(q, *news, *pasts, *consts)


def _attention_mixers(qa, kva, qb, ck, qc, kc, vc, past_kva, past_ck, past_kc, past_vc, wk, wv,
                      lams, gsub, lam_init):
    def both(body, name, q, news, pasts, consts, scratch_fn=None, q_semantics="parallel"):
        sc = (lambda lk: ()) if scratch_fn is None else scratch_fn
        return (_seq_call(body, name, False, q, news, (), consts, sc(SEQ), q_semantics),
                _seq_call(body, name, True, q, news, pasts, consts, sc(PAST_LEN + DEC_SEQ), q_semantics))

    kv_scratch = lambda wk_, wv_: (lambda lk: (pltpu.VMEM((lk, wk_), BF16), pltpu.VMEM((lk, wv_), BF16)))
    oa = both(_attn_a_kernel, "attn_gqa", qa, [kva], [past_kva], [], kv_scratch(128, 256), "arbitrary")
    ob = both(_attn_b_kernel, "attn_mla", qb, [ck], [past_ck], [wk, wv], kv_scratch(512, 512), "arbitrary")
    oc = both(functools.partial(_attn_c_kernel, lam_init), "attn_diff", qc, [kc, vc],
              [past_kc, past_vc], [*lams, gsub], kv_scratch(256, 512), "arbitrary")
    return oa, ob, oc


Q = SSD_CHUNK
SSD_UNROLL = 2


def _ssd_kernel(z_ref, xbc_ref, dt_ref, cw_ref, cb_ref, dtb_ref, alog_ref, dvec_ref, ng_ref, h0f_ref, h0b_ref,
                out_ref, hf_ref, hb_ref, act_s, cum_s, dtv_s, y_s):
    n_seq = hf_ref.shape[0]
    seq = z_ref.shape[0] // n_seq
    nc = seq // Q
    row = lax.broadcasted_iota(jnp.int32, (Q, Q), 0)
    col = lax.broadcasted_iota(jnp.int32, (Q, Q), 1)
    lower = row >= col
    upper = row <= col
    tril = jnp.where(lower, 1.0, 0.0).astype(BF16)
    triu = jnp.where(upper, 1.0, 0.0).astype(BF16)
    rowc = lax.broadcasted_iota(jnp.int32, (Q, CONV_DIM), 0)
    lane = lax.broadcasted_iota(jnp.int32, (Q, LANES), 1)
    a_neg = -jnp.exp(alog_ref[...])
    cw = cw_ref[...]
    hf_ref[...] = h0f_ref[...]
    hb_ref[...] = h0b_ref[...]

    def fwd_seq(c, s):
        off = s * seq
        base = pl.multiple_of(off + c * Q, Q)
        x0 = xbc_ref[pl.ds(base, Q), :]
        prev = xbc_ref[pl.ds(pl.multiple_of(off + jnp.maximum(c * Q - 8, 0), 8), 8), :][7:8, :]
        nxt = xbc_ref[pl.ds(pl.multiple_of(off + jnp.minimum(c * Q + Q, seq - 8), 8), 8), :][0:1, :]
        prev = jnp.where(c > 0, prev, 0.0)
        nxt = jnp.where(c < nc - 1, nxt, 0.0)
        xm1 = jnp.where(rowc == 0, prev, pltpu.roll(x0, 1, 0))
        xp1 = jnp.where(rowc == Q - 1, nxt, pltpu.roll(x0, Q - 1, 0))
        act = _silu(xm1 * cw[0:1] + x0 * cw[1:2] + xp1 * cw[2:3] + cb_ref[...])
        act_s[pl.ds(base, Q), :] = act
        xs = act[:, 0:256]
        bm = act[:, 256:384]
        cm = act[:, 384:512]

        dtr = dt_ref[pl.ds(base, Q), :] + dtb_ref[...]
        dtv = jnp.maximum(dtr, 0.0) + jnp.log1p(jnp.exp(-jnp.abs(dtr)))
        dta = dtv * a_neg
        cum = jnp.where(lane < D_HEADS, _dot_split_left(tril, dta), _dot_split_left(triu, dta))
        cum_s[pl.ds(base, Q), :] = cum
        dtv_s[pl.ds(base, Q), :] = dtv
        cum_t = cum.T
        dtv_t = dtv.T
        bm_t = bm.T
        ys = []
        for h in range(D_HEADS):
            g = h // (D_HEADS // D_GROUPS)
            cg = cm[:, g * D_STATE:(g + 1) * D_STATE].astype(BF16)
            cb_mat = _dot_nt(cg, bm[:, g * D_STATE:(g + 1) * D_STATE].astype(BF16))
            cf = cum[:, h:h + 1]
            cb = cum[:, D_HEADS + h:D_HEADS + h + 1]
            l_f = jnp.exp(jnp.where(lower, cf - cum_t[h:h + 1, :], NEG_BIG))
            l_b = jnp.exp(jnp.where(upper, cb - cum_t[D_HEADS + h:D_HEADS + h + 1, :], NEG_BIG))
            mix = cb_mat * (l_f * dtv_t[h:h + 1, :] + l_b * dtv_t[D_HEADS + h:D_HEADS + h + 1, :])
            xh = xs[:, h * D_HEADDIM:(h + 1) * D_HEADDIM]
            y = _dot(mix.astype(BF16), xh.astype(BF16))
            state = hf_ref[s, h]
            y = y + _dot(cg, state.astype(BF16)) * jnp.exp(cf)
            y = y + dvec_ref[:, h * D_HEADDIM:(h + 1) * D_HEADDIM] * xh
            ys.append(y)
            last = cum[Q - 1:Q, h:h + 1]
            wgt = jnp.exp(last - cf) * dtv[:, h:h + 1]
            st = _dot(bm_t[g * D_STATE:(g + 1) * D_STATE, :].astype(BF16), (xh * wgt).astype(BF16))
            hf_ref[s, h] = state * jnp.exp(last) + st
        y_s[pl.ds(base, Q), :] = jnp.concatenate(ys, axis=-1)

    def fwd_chunk(c, carry):
        for s in range(n_seq):
            fwd_seq(c, s)
        return carry

    lax.fori_loop(0, nc, fwd_chunk, 0, unroll=SSD_UNROLL)

    def bwd_seq(c, s):
        base = pl.multiple_of(s * seq + c * Q, Q)
        act = act_s[pl.ds(base, Q), :]
        cum = cum_s[pl.ds(base, Q), :]
        dtv = dtv_s[pl.ds(base, Q), :]
        xs = act[:, 0:256]
        bm_t = act[:, 256:384].T
        cm = act[:, 384:512]
        ys = []
        for h in range(D_HEADS):
            g = h // (D_HEADS // D_GROUPS)
            cg = cm[:, g * D_STATE:(g + 1) * D_STATE].astype(BF16)
            cb = cum[:, D_HEADS + h:D_HEADS + h + 1]
            xh = xs[:, h * D_HEADDIM:(h + 1) * D_HEADDIM]
            state = hb_ref[s, h]
            ys.append(_dot(cg, state.astype(BF16)) * jnp.exp(cb))
            first = cum[0:1, D_HEADS + h:D_HEADS + h + 1]
            wgt = jnp.exp(first - cb) * dtv[:, D_HEADS + h:D_HEADS + h + 1]
            st = _dot(bm_t[g * D_STATE:(g + 1) * D_STATE, :].astype(BF16), (xh * wgt).astype(BF16))
            hb_ref[s, h] = state * jnp.exp(first) + st
        y = y_s[pl.ds(base, Q), :] + jnp.concatenate(ys, axis=-1)
        gated = y * _silu(z_ref[pl.ds(base, Q), :])
        ms = jnp.mean(gated * gated, axis=-1, keepdims=True)
        out_ref[pl.ds(base, Q), :] = gated * lax.rsqrt(ms + EPS) * ng_ref[...]

    def bwd_chunk(i, carry):
        for s in range(n_seq):
            bwd_seq(nc - 1 - i, s)
        return carry

    lax.fori_loop(0, nc, bwd_chunk, 0, unroll=SSD_UNROLL)


def _ssd(latent, z, xbc, dt, consts, h0f, h0b):
    bsz, seq, n_seq = (DEC_BATCH, DEC_SEQ, 1) if latent else (BATCH, SEQ, 2)
    rows = n_seq * seq
    first = N_CTX // rows if latent else 0
    per_seq = lambda w: pl.BlockSpec((rows, w), lambda b: (first + b, 0))
    const_specs = [pl.BlockSpec(a.shape, lambda b, n=a.ndim: (0,) * n) for a in consts]
    st_spec = pl.BlockSpec((n_seq, D_HEADS, D_STATE, D_HEADDIM), lambda b: (b, 0, 0, 0))
    st_shape = jax.ShapeDtypeStruct((bsz, D_HEADS, D_STATE, D_HEADDIM), F32)
    return pl.pallas_call(
        _ssd_kernel,
        out_shape=[jax.ShapeDtypeStruct((bsz * seq, D_INNER), F32), st_shape, st_shape],
        grid=(bsz // n_seq,),
        in_specs=[per_seq(256), per_seq(512), per_seq(128)] + const_specs + [st_spec, st_spec],
        out_specs=[pl.BlockSpec((rows, D_INNER), lambda b: (b, 0)), st_spec, st_spec],
        scratch_shapes=[pltpu.VMEM((rows, CONV_DIM), F32), pltpu.VMEM((rows, LANES), F32),
                        pltpu.VMEM((rows, LANES), F32), pltpu.VMEM((rows, D_INNER), F32)],
        compiler_params=_params("parallel"),
        name="ssd_lat" if latent else "ssd_ctx",
    )(z, xbc, dt, *consts, h0f, h0b)


def _out_proj_kernel(*refs):
    pairs, (mod_ref, wo_ref, g2_ref, rwh_ref, rwl_ref, rb_ref, x1_ref, h2_ref, te_ref, gt_ref, cnt_ref) = \
        refs[:10], refs[10:]
    x, oa, ob, oc, od = [_pick(pairs[2 * j], pairs[2 * j + 1], PROJ_TM) for j in range(5)]
    m = mod_ref[0]
    gate1, sh2, sc2 = m[2:3], m[3:4], m[4:5]
    mixed = (_dot(oa.astype(BF16), wo_ref[0:256, :])
             + _dot(ob.astype(BF16), wo_ref[256:512, :])
             + _dot(oc.astype(BF16), wo_ref[512:768, :])
             + _dot(od.astype(BF16), wo_ref[768:1024, :]))
    x1 = x + gate1 * mixed
    x1_ref[...] = x1
    ms = jnp.mean(x1 * x1, axis=-1, keepdims=True)
    h2 = (x1 * lax.rsqrt(ms + EPS) * g2_ref[...]) * (1.0 + sc2) + sh2
    h2_ref[...] = h2.astype(BF16)

    hi = h2.astype(BF16)
    lo = (h2 - hi.astype(F32)).astype(BF16)
    logits = _dot(hi, rwh_ref[...]) + _dot(lo, rwh_ref[...]) + _dot(hi, rwl_ref[...]) + rb_ref[...]
    lane = lax.broadcasted_iota(jnp.int32, logits.shape, 1)
    vals, idxs = [], []
    for _ in range(TOP_K):
        mx = jnp.max(logits, axis=-1, keepdims=True)
        ix = jnp.min(jnp.where(logits == mx, lane, LANES), axis=-1, keepdims=True)
        vals.append(mx)
        idxs.append(ix)
        logits = jnp.where(lane == ix, -3e38, logits)
    es = [jnp.exp(v - vals[0]) for v in vals]
    den = es[0] + es[1] + es[2] + es[3]
    te = jnp.zeros(lane.shape, jnp.int32)
    gt = jnp.zeros(lane.shape, F32)
    member = jnp.zeros(lane.shape, F32)
    for k in range(TOP_K):
        te = jnp.where(lane == k, idxs[k], te)
        gt = jnp.where(lane == k, es[k] / den, gt)
        member = jnp.where(lane == idxs[k], 1.0, member)
    te_ref[...] = te[:, 0:TOP_K]
    gt_ref[...] = gt[:, 0:TOP_K]
    for t in range(PROJ_TM // TM):
        count = jnp.sum(member[t * TM:(t + 1) * TM], axis=0, keepdims=True)
        cnt_ref[t] = jnp.broadcast_to(count, (8, LANES)).astype(jnp.int32)


def _out_proj(x, oa, ob, oc, od, mod3, wo, g2, rwh, rwl, rb):
    row = lambda w: pl.BlockSpec((PROJ_TM, w), lambda i: (i, 0))
    full = lambda a: pl.BlockSpec(a.shape, lambda i: (0,) * a.ndim)
    return pl.pallas_call(
        _out_proj_kernel,
        out_shape=[jax.ShapeDtypeStruct((N_TOK, D_MODEL), F32), jax.ShapeDtypeStruct((N_TOK, D_MODEL), BF16),
                   jax.ShapeDtypeStruct((N_TOK, TOP_K), jnp.int32), jax.ShapeDtypeStruct((N_TOK, TOP_K), F32),
                   jax.ShapeDtypeStruct((N_TILES, 8, LANES), jnp.int32)],
        grid=(N_TOK // PROJ_TM,),
        in_specs=_pair_specs(D_MODEL, PROJ_TM) + [s for _ in range(4) for s in _pair_specs(256, PROJ_TM)]
                 + [pl.BlockSpec((1, N_MOD, D_MODEL), lambda i: (_mod_row(i, PROJ_TM), 0, 0)),
                    full(wo), full(g2), full(rwh), full(rwl), full(rb)],
        out_specs=[row(D_MODEL), row(D_MODEL), row(TOP_K), row(TOP_K),
                   pl.BlockSpec((PROJ_TM // TM, 8, LANES), lambda i: (i, 0, 0))],
        compiler_params=_params("parallel"),
        name="out_proj_router",
    )(*x, *oa, *ob, *oc, *od, mod3, wo, g2, rwh, rwl, rb)


def _tile_rows(te, tri_ref, off_ref):
    lane = lax.broadcasted_iota(jnp.int32, (TM, LANES), 1)
    hits = [lane == te[:, k:k + 1] for k in range(TOP_K)]
    member = jnp.zeros((TM, LANES), F32)
    for hit in hits:
        member = jnp.where(hit, 1.0, member)
    rank = _dot(tri_ref[...], member.astype(BF16))
    pos = rank + off_ref[0][0:1, :].astype(F32)
    return [jnp.sum(jnp.where(hit, pos, 0.0), axis=-1, keepdims=True) for hit in hits]


def _segment_chunks(seg_ref, len_ref, tile, visit):
    def segment(e, row):
        length = len_ref[tile * N_EXPERTS + e]
        start = seg_ref[tile * N_EXPERTS + e]
        queue = e % 2

        def chunk(c, carry):
            visit(CHUNKS[0], pl.multiple_of(row + c * CHUNKS[0], SEG_ALIGN),
                  pl.multiple_of(start + c * CHUNKS[0], SEG_ALIGN), queue)
            return carry

        lax.fori_loop(0, lax.shift_right_logical(length, CHUNK_SHIFT), chunk, 0)
        for size in CHUNKS[1:]:
            done = jnp.bitwise_and(length, -2 * size)

            @pl.when(jnp.bitwise_and(length, size) != 0)
            def _():
                visit(size, pl.multiple_of(row + done, SEG_ALIGN), pl.multiple_of(start + done, SEG_ALIGN), queue)
        return row + length

    row = 0
    for e in range(N_EXPERTS):
        row = segment(e, row)


def _wait_chunks(nchunk_ref, tile, copy):
    for j, size in enumerate(CHUNKS):
        lax.fori_loop(0, nchunk_ref[tile * len(CHUNKS) + j], lambda c, carry, size=size: (copy(size).wait(), carry)[1], 0)


def _dispatch_kernel(seg_ref, len_ref, nchunk_ref, tail_ref, tlen_ref, nu_ref, h2_ref, te_ref, off_ref, tri_ref,
                     xb_ref, rows_ref, buf, zeros, sems, sem_z):
    i = pl.program_id(0)
    slot = i % 2

    def chunk_copy(size, s, buf_row, xb_row):
        return pltpu.make_async_copy(buf.at[s, pl.ds(buf_row, size)], xb_ref.at[pl.ds(xb_row, size)], sems.at[s])

    def wait_chunks(tile, s):
        _wait_chunks(nchunk_ref, tile, lambda size: chunk_copy(size, s, 0, 0))

    def zero_fills(visit):
        for e in range(N_EXPERTS):
            start, length = tail_ref[e], tlen_ref[e]
            for size in ZERO_CHUNKS:
                done = jnp.bitwise_and(length, -2 * size)

                @pl.when(jnp.bitwise_and(length, size) != 0)
                def _():
                    visit(pltpu.make_async_copy(
                        zeros.at[pl.ds(0, size)],
                        xb_ref.at[pl.ds(pl.multiple_of(start + done, SEG_ALIGN), size)], sem_z))
        block = lambda b: pltpu.make_async_copy(
            zeros, xb_ref.at[pl.ds(pl.multiple_of(b * MOE_BM, MOE_BM), MOE_BM)], sem_z)
        lax.fori_loop(nu_ref[0], MOE_BLOCKS, lambda b, carry: (visit(block(b)), carry)[1], 0)

    @pl.when(i == 0)
    def _():
        zeros[...] = jnp.zeros_like(zeros)
        zero_fills(lambda cp: cp.start())

    rows = _tile_rows(te_ref[...], tri_ref, off_ref)
    lane = lax.broadcasted_iota(jnp.int32, (TM, LANES), 1)
    packed = jnp.zeros((TM, LANES), F32)
    for k in range(TOP_K):
        packed = jnp.where(lane == k, rows[k], packed)
    rows_ref[...] = packed[:, 0:TOP_K]
    rows_t = packed.T
    buf_row = lax.broadcasted_iota(jnp.int32, (TILE_BUF, TM), 0).astype(F32)
    pick = jnp.zeros((TILE_BUF, TM), F32)
    for k in range(TOP_K):
        pick = jnp.where(buf_row == rows_t[k:k + 1, :], 1.0, pick)
    buf[slot] = _dot(pick.astype(BF16), h2_ref[...])

    @pl.when(i > 0)
    def _():
        wait_chunks(i - 1, 1 - slot)

    _segment_chunks(seg_ref, len_ref, i, lambda size, b, x, q: chunk_copy(size, slot, b, x).start(priority=q))

    @pl.when(i == N_TILES - 1)
    def _():
        wait_chunks(i, slot)
        zero_fills(lambda cp: cp.wait())


def _dispatch(seg_start, seg_len, n_chunk, tail_start, tail_len, n_used, h2, te, seg_off, tri):
    row = lambda w: pl.BlockSpec((TM, w), lambda i, *_: (i, 0))
    grid_spec = pltpu.PrefetchScalarGridSpec(
        num_scalar_prefetch=6,
        grid=(N_TILES,),
        in_specs=[row(D_MODEL), row(TOP_K), pl.BlockSpec((1, 8, LANES), lambda i, *_: (i, 0, 0)),
                  pl.BlockSpec(tri.shape, lambda i, *_: (0, 0))],
        out_specs=[pl.BlockSpec(memory_space=pl.ANY), row(TOP_K)],
        scratch_shapes=[pltpu.VMEM((2, TILE_BUF, D_MODEL), F32), pltpu.VMEM((MOE_BM, D_MODEL), F32),
                        pltpu.SemaphoreType.DMA((2,)), pltpu.SemaphoreType.DMA],
    )
    return pl.pallas_call(
        _dispatch_kernel,
        out_shape=[jax.ShapeDtypeStruct((YB_ROWS, D_MODEL), F32), jax.ShapeDtypeStruct((N_TOK, TOP_K), F32)],
        grid_spec=grid_spec,
        compiler_params=_params("arbitrary"),
        name="moe_dispatch",
    )(seg_start, seg_len, n_chunk, tail_start, tail_len, n_used, h2, te, seg_off, tri)


def _expert_kernel(layer, be_ref, nu_ref, slot_ref, next_ref, x_ref, wgu_hbm, bgu_ref, wdn_hbm, bdn_ref,
                   o_ref, wgu_f, wdn_f, wgu_s, wdn_s, sems):
    i = pl.program_id(0)
    used = i < nu_ref[0]
    expert = be_ref[i]
    fresh = jnp.logical_or(i == 0, expert != be_ref[jnp.maximum(i - 1, 0)])
    slot = slot_ref[expert]

    def fetch(e, s):
        return (pltpu.make_async_copy(wgu_hbm.at[layer, e], wgu_f.at[s], sems.at[0, s]),
                pltpu.make_async_copy(wdn_hbm.at[layer, e], wdn_f.at[s], sems.at[1, s]))

    @pl.when(i == 0)
    def _():
        for cp in fetch(expert, slot):
            cp.start()

    @pl.when(jnp.logical_and(used, fresh))
    def _():
        for cp in fetch(expert, slot):
            cp.wait()
        nxt = next_ref[expert]

        @pl.when(nxt >= 0)
        def _():
            for cp in fetch(nxt, 1 - slot):
                cp.start()

        wgu_s[...] = wgu_f[slot].astype(BF16)
        wdn_s[...] = wdn_f[slot].astype(BF16)

    @pl.when(used)
    def _():
        hgu = _dot(x_ref[...].astype(BF16), wgu_s[...]) + bgu_ref[0, 0]
        gate = jnp.minimum(hgu[:, :D_FF], SWIGLU_LIMIT)
        up = jnp.clip(hgu[:, D_FF:], -SWIGLU_LIMIT, SWIGLU_LIMIT)
        act = (up + 1.0) * gate * jax.nn.sigmoid(SWIGLU_ALPHA * gate)
        o_ref[...] = _dot(act.astype(BF16), wdn_s[...]) + bdn_ref[0, 0]

    @pl.when(jnp.logical_not(used))
    def _():
        o_ref[...] = jnp.zeros_like(o_ref)


def _experts(layer, block_e, n_used, w_slot, next_e, xb, w_gu, b_gu, w_dn, b_dn):
    grid_spec = pltpu.PrefetchScalarGridSpec(
        num_scalar_prefetch=4,
        grid=(MOE_BLOCKS,),
        in_specs=[pl.BlockSpec((MOE_BM, D_MODEL), lambda i, be, *_: (i, 0)),
                  pl.BlockSpec(memory_space=pl.ANY),
                  pl.BlockSpec((1, 1, 1, 2 * D_FF), lambda i, be, *_: (layer, be[i], 0, 0)),
                  pl.BlockSpec(memory_space=pl.ANY),
                  pl.BlockSpec((1, 1, 1, D_MODEL), lambda i, be, *_: (layer, be[i], 0, 0))],
        out_specs=pl.BlockSpec((MOE_BM, D_MODEL), lambda i, be, *_: (i, 0)),
        scratch_shapes=[pltpu.VMEM((2, D_MODEL, 2 * D_FF), F32), pltpu.VMEM((2, D_FF, D_MODEL), F32),
                        pltpu.VMEM((D_MODEL, 2 * D_FF), BF16), pltpu.VMEM((D_FF, D_MODEL), BF16),
                        pltpu.SemaphoreType.DMA((2, 2))],
    )
    return pl.pallas_call(
        functools.partial(_expert_kernel, layer),
        out_shape=jax.ShapeDtypeStruct((YB_ROWS, D_MODEL), F32),
        grid_spec=grid_spec,
        compiler_params=_params("arbitrary"),
        name="experts",
    )(block_e, n_used, w_slot, next_e, xb, w_gu, b_gu.reshape(DEPTH, N_EXPERTS, 1, 2 * D_FF), w_dn,
      b_dn.reshape(DEPTH, N_EXPERTS, 1, D_MODEL))


def _combine_kernel(final, seg_ref, len_ref, nchunk_ref, x1_ref, rows_ref, gt_ref, mod_ref, fg_ref,
                    yb_ref, o_ctx_ref, o_lat_ref, buf, sems):
    i = pl.program_id(0)
    slot = i % 2

    def chunk_copy(size, s, buf_row, yb_row):
        return pltpu.make_async_copy(yb_ref.at[pl.ds(yb_row, size)], buf.at[s, pl.ds(buf_row, size)], sems.at[s])

    def request(tile, s):
        _segment_chunks(seg_ref, len_ref, tile, lambda size, b, y, q: chunk_copy(size, s, b, y).start(priority=q))

    @pl.when(i == 0)
    def _():
        buf[...] = jnp.zeros_like(buf)
        request(0, 0)

    @pl.when(i + 1 < N_TILES)
    def _():
        request(i + 1, 1 - slot)

    rows = rows_ref[...]
    gt = gt_ref[...]
    buf_row = lax.broadcasted_iota(jnp.int32, (TM, TILE_BUF), 1).astype(F32)
    place = jnp.zeros((TM, TILE_BUF), F32)
    for k in range(TOP_K):
        place = jnp.where(buf_row == rows[:, k:k + 1], gt[:, k:k + 1], place)

    _wait_chunks(nchunk_ref, i, lambda size: chunk_copy(size, slot, 0, 0))
    y = _dot(place.astype(BF16), buf[slot].astype(BF16))
    x2 = x1_ref[...] + mod_ref[0][5:6] * y
    if final:
        ms = jnp.mean(x2 * x2, axis=-1, keepdims=True)
        x2 = x2 * lax.rsqrt(ms + EPS) * fg_ref[...]

    @pl.when(i < CTX_TILES)
    def _():
        o_ctx_ref[...] = x2

    @pl.when(i >= CTX_TILES)
    def _():
        o_lat_ref[...] = x2


def _combine(seg_start, seg_len, n_chunk, x1, rows, gates, mod3, fg, yb, final):
    row = lambda w: pl.BlockSpec((TM, w), lambda i, *_: (i, 0))
    full = lambda a: pl.BlockSpec(a.shape, lambda i, *_: (0,) * a.ndim)
    out_shape = [jax.ShapeDtypeStruct((N_CTX, D_MODEL), F32), jax.ShapeDtypeStruct((N_LAT, D_MODEL), F32)]
    out_specs = _pair_specs(D_MODEL, TM)
    grid_spec = pltpu.PrefetchScalarGridSpec(
        num_scalar_prefetch=3,
        grid=(N_TILES,),
        in_specs=[row(D_MODEL), row(TOP_K), row(TOP_K),
                  pl.BlockSpec((1, N_MOD, D_MODEL), lambda i, *_: (_mod_row(i), 0, 0)),
                  full(fg),
                  pl.BlockSpec(memory_space=pl.ANY)],
        out_specs=out_specs,
        scratch_shapes=[pltpu.VMEM((2, TILE_BUF, D_MODEL), F32), pltpu.SemaphoreType.DMA((2,))],
    )
    return pl.pallas_call(
        functools.partial(_combine_kernel, final),
        out_shape=out_shape,
        grid_spec=grid_spec,
        compiler_params=_params("arbitrary"),
        name="moe_combine",
    )(seg_start, seg_len, n_chunk, x1, rows, gates, mod3, fg, yb)


def _gqa_head_order(w, axis):
    take = lambda start: lax.slice_in_dim(w, start, start + HEAD_DIM, axis=axis)
    return [take(kv * 128 + g * 64) for g in range(2) for kv in range(2)]


def _w1_layout(w_in):
    cols = lambda name, n: w_in[:, IN_OFF[name]:IN_OFF[name] + n]
    zeros = lambda n: jnp.zeros((D_MODEL, n), F32)
    pieces = (_gqa_head_order(w_in, 1) + [cols('a_k', 256)]
              + [cols('b_cq', 192), zeros(64), cols('b_ckv', 128), cols('b_kr', 32), zeros(96)]
              + [cols('c_q', 768), cols('d_z', 256), cols('d_xbc', 512), cols('d_dtf', 8), zeros(120)])
    return jnp.concatenate(pieces, axis=1).astype(BF16)


def _rope_tables():
    t = np.arange(DEC_SEQ)
    pos = ((t // GRID_W).astype(np.float32), (t % GRID_W).astype(np.float32))

    def unit(rot_dim):
        quarter, half = rot_dim // 4, rot_dim // 2
        inv = ROPE_THETA ** (-np.arange(0, half, 2, dtype=np.float32) / half)
        cos = np.zeros((DEC_SEQ, rot_dim), np.float32)
        sin = np.zeros((DEC_SEQ, rot_dim), np.float32)
        for seg in range(4):
            ang = pos[seg // 2][:, None] * inv[None, :].astype(np.float32)
            cos[:, seg * quarter:(seg + 1) * quarter] = np.cos(ang)
            sin[:, seg * quarter:(seg + 1) * quarter] = np.sin(ang) * (-1.0 if seg % 2 == 0 else 1.0)
        return cos, sin

    log2e = math.log2(math.e)
    specs = [
        (HEAD_DIM, (0, 64), (), HEAD_DIM ** -0.5 * log2e),
        (HEAD_DIM, (0, 64), (), 1.0),
        (MLA_ROPE, (MLA_NOPE,), (0, MLA_NOPE), (MLA_NOPE + MLA_ROPE) ** -0.5 * log2e),
        (MLA_ROPE, (0,), (), 1.0),
        (DIFF_D, (0, 32, 64, 96), (), DIFF_D ** -0.5 * log2e),
        (DIFF_D, (0, 32, 64, 96), (), 1.0),
    ]
    lat_cols, ident_cols = [], []
    for rot_dim, starts, passthrough, scale in specs:
        ucos, usin = unit(rot_dim)
        cos = np.zeros((DEC_SEQ, LANES), np.float32)
        sin = np.zeros((DEC_SEQ, LANES), np.float32)
        ident = np.zeros((1, LANES), np.float32)
        if passthrough:
            cos[:, passthrough[0]:passthrough[1]] = 1.0
            ident[:, passthrough[0]:passthrough[1]] = 1.0
        for s in starts:
            cos[:, s:s + rot_dim] = ucos
            sin[:, s:s + rot_dim] = usin
            ident[:, s:s + rot_dim] = 1.0
        lat_cols += [cos * scale, sin * scale]
        ident_cols += [ident * scale, np.zeros((1, LANES), np.float32)]
    lat = np.concatenate(lat_cols, axis=1)
    ident_blk = np.broadcast_to(np.concatenate(ident_cols, axis=1), (PROJ_TM, N_TAB * LANES))
    return np.concatenate([ident_blk, lat], axis=0).astype(np.float32)


def _block_diag_ones(n, blk):
    r = np.arange(n)
    return (r[:, None] // blk == r[None, :] // blk).astype(np.float32)


def _chunk_tables(seg_len):
    counts = [jnp.sum(seg_len // CHUNKS[0], axis=1)]
    counts += [jnp.sum(seg_len % (2 * size) // size, axis=1) for size in CHUNKS[1:]]
    return jnp.stack(counts, axis=1).reshape(-1).astype(jnp.int32)


def _strict_lower_ones(n):
    r = np.arange(n)
    return (r[None, :] < r[:, None]).astype(np.float32)


def kernel(x_prompt, x_sample, cache_gqa_k, cache_gqa_v, cache_mla_ckv, cache_mla_krope, cache_diff_k, cache_diff_v, state_ssd_fwd, state_ssd_bwd, c, c_ctx, norm1_g, norm2_g, w_ada, b_ada, w_in, w_out, gqa_qn_g, gqa_kn_g, mla_qa_g, mla_wqb, mla_kva_g, mla_wkvb, diff_lq1, diff_lk1, diff_lq2, diff_lk2, diff_subln_g, ssd_conv_w, ssd_conv_b, ssd_a_log_f, ssd_a_log_b, ssd_dt_bias_f, ssd_dt_bias_b, ssd_d, ssd_norm_g, router_w, router_b, moe_w_gu, moe_b_gu, moe_w_dn, moe_b_dn, final_g):
    tabs = jnp.asarray(_rope_tables())
    bd = jnp.asarray(_block_diag_ones(256, HEAD_DIM), BF16)
    tri = jnp.asarray(_strict_lower_ones(TM), BF16)

    cvec = jnp.zeros((MOD_ROWS, D_MODEL), F32).at[0].set(c_ctx).at[1:1 + DEC_BATCH].set(c)
    mod = _modulation(cvec, w_ada, b_ada).reshape(DEPTH, MOD_ROWS, N_MOD, D_MODEL)

    x = (x_prompt.reshape(N_CTX, D_MODEL), x_sample.reshape(N_LAT, D_MODEL))
    new_ctx = []
    for l in range(DEPTH):
        mod3 = mod[l]
        w1 = _w1_layout(w_in[l])
        wqb = mla_wqb[l].reshape(Q_LORA, B_HEADS, MLA_NOPE + MLA_ROPE)
        wqb = jnp.pad(wqb, ((0, 256 - Q_LORA), (0, 0), (0, LANES - MLA_NOPE - MLA_ROPE)))
        wqb = wqb.reshape(256, B_HEADS * LANES).astype(BF16)
        wkvb = mla_wkvb[l].reshape(KV_LORA, B_HEADS, MLA_NOPE + MLA_V)
        wk_nope = jnp.pad(wkvb[:, :, :MLA_NOPE], ((0, 0), (0, 0), (0, LANES - MLA_NOPE)))
        eye_r = jnp.zeros((LANES, B_HEADS, LANES), F32)
        eye_r = eye_r.at[jnp.arange(MLA_ROPE), :, MLA_NOPE + jnp.arange(MLA_ROPE)].set(1.0)
        wk = jnp.concatenate([wk_nope, eye_r], axis=0).reshape(256, B_HEADS * LANES).astype(BF16)
        wv = wkvb[:, :, MLA_NOPE:].reshape(KV_LORA, B_HEADS * MLA_V).astype(BF16)
        wo = jnp.concatenate(_gqa_head_order(w_out[l], 0) + [w_out[l][256:]], axis=0).astype(BF16)
        gq = jnp.tile(gqa_qn_g[l], 4)[None, :]
        gk = jnp.tile(gqa_kn_g[l], 2)[None, :]
        gqa = jnp.pad(mla_qa_g[l], (0, 256 - Q_LORA))[None, :]
        gkva = mla_kva_g[l][None, :]
        rw = jnp.pad(router_w[l], ((0, 0), (0, LANES - N_EXPERTS)))
        rwh = rw.astype(BF16)
        rwl = (rw - rwh.astype(F32)).astype(BF16)
        rb = jnp.pad(router_b[l], (0, LANES - N_EXPERTS), constant_values=NEG_BIG)[None, :]

        qa, kva, qb, ck, qc, kc, vc, z, xbc, dt = _in_proj(
            x, mod3, norm1_g[l][None, :], w1, bd, gq, gk, gqa, gkva, wqb, tabs)

        past_kva = jnp.concatenate([cache_gqa_k[:, l].reshape(DEC_BATCH, PAST_LEN, 128),
                                    cache_gqa_v[:, l].reshape(DEC_BATCH, PAST_LEN, 128)], axis=-1)
        past_ck = jnp.concatenate([cache_mla_ckv[:, l], cache_mla_krope[:, l],
                                   jnp.zeros((DEC_BATCH, PAST_LEN, LANES - MLA_ROPE), F32)], axis=-1)
        past_kc = cache_diff_k[:, l].reshape(DEC_BATCH, PAST_LEN, 256)
        past_vc = cache_diff_v[:, l].reshape(DEC_BATCH, PAST_LEN, 256)
        lams = [a[l][None, :] for a in (diff_lq1, diff_lk1, diff_lq2, diff_lk2)]
        gsub = jnp.tile(diff_subln_g[l], 2)[None, :]
        lam_init = 0.8 - 0.6 * math.exp(-0.3 * l)
        oa, ob, oc = _attention_mixers(qa, kva, qb, ck, qc, kc, vc, past_kva, past_ck, past_kc, past_vc,
                                       wk, wv, lams, gsub, lam_init)

        pad8 = lambda f, b: jnp.pad(jnp.concatenate([f, b]), (0, LANES - 2 * D_HEADS))[None, :]
        ssd_consts = (ssd_conv_w[l], ssd_conv_b[l][None, :], pad8(ssd_dt_bias_f[l], ssd_dt_bias_b[l]),
                      pad8(ssd_a_log_f[l], ssd_a_log_b[l]), jnp.repeat(ssd_d[l], D_HEADDIM)[None, :],
                      ssd_norm_g[l][None, :])
        zeros_st = jnp.zeros((BATCH, D_HEADS, D_STATE, D_HEADDIM), F32)
        od_c, hf_c, hb_c = _ssd(False, z, xbc, dt, ssd_consts, zeros_st, zeros_st)
        od_l, _, _ = _ssd(True, z, xbc, dt, ssd_consts, jnp.swapaxes(state_ssd_fwd[:, l], -1, -2),
                          jnp.swapaxes(state_ssd_bwd[:, l], -1, -2))
        od = (od_c, od_l)

        x1, h2, te, gates, tile_cnt = _out_proj(x, oa, ob, oc, od, mod3, wo, norm2_g[l][None, :], rwh, rwl, rb)

        seg_cnt = tile_cnt[:, 0, :N_EXPERTS]
        seg_len = (seg_cnt + SEG_ALIGN - 1) // SEG_ALIGN * SEG_ALIGN
        region = (jnp.sum(seg_len, axis=0) + MOE_BM - 1) // MOE_BM * MOE_BM
        pad_end = jnp.cumsum(region).astype(jnp.int32)
        pad_start = pad_end - region
        seg_start = (pad_start[None, :] + jnp.cumsum(seg_len, axis=0) - seg_len).astype(jnp.int32)
        n_used = (pad_end[-1] // MOE_BM).astype(jnp.int32).reshape(1)
        blk_start = jnp.arange(MOE_BLOCKS, dtype=jnp.int32) * MOE_BM
        block_e = jnp.minimum(jnp.sum((pad_end[None, :] <= blk_start[:, None]).astype(jnp.int32), axis=1),
                              N_EXPERTS - 1).astype(jnp.int32)
        n_chunk = _chunk_tables(seg_len)
        seg_off = jnp.cumsum(seg_len, axis=1) - seg_len
        seg_off = jnp.broadcast_to(jnp.pad(seg_off, ((0, 0), (0, LANES - N_EXPERTS)))[:, None, :],
                                   (N_TILES, 8, LANES)).astype(jnp.int32)
        seg_start, seg_len = seg_start.reshape(-1), seg_len.reshape(-1).astype(jnp.int32)

        tail_start = (pad_end - region + jnp.sum(seg_len.reshape(N_TILES, N_EXPERTS), axis=0)).astype(jnp.int32)
        tail_len = (pad_end - tail_start).astype(jnp.int32)
        xb, pair_rows = _dispatch(seg_start, seg_len, n_chunk, tail_start, tail_len, n_used, h2, te, seg_off, tri)
        has_rows = region > 0
        w_slot = ((jnp.cumsum(has_rows.astype(jnp.int32)) - 1) % 2).astype(jnp.int32)
        later = lax.cummin(jnp.where(has_rows, jnp.arange(N_EXPERTS, dtype=jnp.int32), N_EXPERTS), reverse=True)
        later = jnp.concatenate([later[1:], jnp.full((1,), N_EXPERTS, jnp.int32)])
        next_e = jnp.where(later < N_EXPERTS, later, -1).astype(jnp.int32)
        yb = _experts(l, block_e, n_used, w_slot, next_e, xb, moe_w_gu, moe_b_gu, moe_w_dn, moe_b_dn)
        x = _combine(seg_start, seg_len, n_chunk, x1, pair_rows, gates, mod3, final_g[None, :], yb,
                     l == DEPTH - 1)

        kva_c, ck_c = kva[:N_CTX], ck[:N_CTX]
        new_ctx.append((kva_c[:, 0:128].reshape(BATCH, SEQ, A_KV_HEADS, HEAD_DIM),
                        kva_c[:, 128:256].reshape(BATCH, SEQ, A_KV_HEADS, HEAD_DIM),
                        ck_c[:, 0:KV_LORA].reshape(BATCH, SEQ, KV_LORA),
                        ck_c[:, KV_LORA:KV_LORA + MLA_ROPE].reshape(BATCH, SEQ, MLA_ROPE),
                        kc[:N_CTX].reshape(BATCH, SEQ, C_HEADS, 2 * DIFF_D),
                        vc[:N_CTX].reshape(BATCH, SEQ, C_HEADS, DIFF_V),
                        jnp.swapaxes(hf_c, -1, -2), jnp.swapaxes(hb_c, -1, -2)))

    y_prompt = x[0].reshape(BATCH, SEQ, D_MODEL)
    y_sample = x[1].reshape(DEC_BATCH, DEC_SEQ, D_MODEL)
    caches = [jnp.stack([cl[i] for cl in new_ctx], axis=1) for i in range(8)]
    return (y_prompt, y_sample, *caches)
```

```python
import functools
import math

import numpy as np
import jax
import jax.numpy as jnp
from jax import lax
from jax.experimental import pallas as pl
from jax.experimental.pallas import tpu as pltpu

F32 = jnp.float32
BF16 = jnp.bfloat16

D_MODEL = 1024
BATCH = 16
SEQ = 256
DEPTH = 2
DEC_BATCH = 8
DEC_SEQ = 2048
PAST_LEN = 256
GRID_W = 64
ROPE_THETA = 10000.0
EPS = 1e-6
HEAD_DIM = 64
A_HEADS = 4
A_KV_HEADS = 2
B_HEADS = 4
MLA_NOPE = 64
MLA_ROPE = 32
MLA_V = 64
Q_LORA = 192
KV_LORA = 128
C_HEADS = 4
DIFF_D = 32
DIFF_V = 64
SUBLN_EPS = 1e-5
D_HEADS = 4
D_HEADDIM = 64
D_INNER = 256
D_GROUPS = 2
D_STATE = 64
CONV_DIM = 512
SSD_CHUNK = 128
N_EXPERTS = 32
TOP_K = 4
D_FF = 1024
SWIGLU_ALPHA = 1.702
SWIGLU_LIMIT = 7.0
N_MOD = 6

N_CTX = BATCH * SEQ
N_LAT = DEC_BATCH * DEC_SEQ
N_TOK = N_CTX + N_LAT

LANES = 128
TM = 256
N_TILES = N_TOK // TM
CTX_TILES = N_CTX // TM
LAT_TQ = 1024
PROJ_TM = 512
MOD_ROWS = 16
MOE_BM = 256
N_PAIRS = N_TOK * TOP_K
SEG_ALIGN = 8
CHUNKS = (32, 16, 8)
CHUNK_SHIFT = 5
MOE_BLOCKS = -(-(N_PAIRS + N_TILES * N_EXPERTS * (SEG_ALIGN - 1) + N_EXPERTS * (MOE_BM - 1)) // MOE_BM)
YB_ROWS = MOE_BLOCKS * MOE_BM
TILE_BUF = -(-(TM * TOP_K + N_EXPERTS * (SEG_ALIGN - 1)) // LANES) * LANES
ZERO_CHUNKS = tuple(MOE_BM >> k for k in range(1, (MOE_BM // SEG_ALIGN).bit_length()))
VMEM_LIMIT = 56 * 1024 * 1024
NEG_BIG = -1e30

W1_COLS = 2688
IN_OFF = dict(a_q=0, a_k=256, a_v=384, b_cq=512, b_ckv=704, b_kr=832, c_q=864, c_k=1120,
              c_v=1376, d_z=1632, d_xbc=1888, d_dtf=2400, d_dtb=2404)
IN_WIDTH = 2408
N_TAB = 12


def _mod_row(i, tile=TM):
    return jnp.where(i < N_CTX // tile, 0, 1 + (i - N_CTX // tile) // (DEC_SEQ // tile))


def _tab_block(i, tile):
    return jnp.where(i < N_CTX // tile, 0, 1 + (i - N_CTX // tile) % (DEC_SEQ // tile))


def _pair_specs(width, tile):
    n_ctx = N_CTX // tile
    return [pl.BlockSpec((tile, width), lambda i, *_: (jnp.minimum(i, n_ctx - 1), 0)),
            pl.BlockSpec((tile, width), lambda i, *_: (jnp.maximum(i - n_ctx, 0), 0))]


def _pick(ctx_ref, lat_ref, tile):
    return jnp.where(pl.program_id(0) < N_CTX // tile, ctx_ref[...], lat_ref[...])


def _dot(a, b):
    return jnp.dot(a, b, preferred_element_type=F32)


def _dot_nt(a, b):
    return lax.dot_general(a, b, (((1,), (1,)), ((), ())), preferred_element_type=F32)


def _dot_split(x, m):
    hi = x.astype(BF16)
    lo = (x - hi.astype(F32)).astype(BF16)
    return _dot(hi, m) + _dot(lo, m)


def _dot_split_left(m, x):
    hi = x.astype(BF16)
    lo = (x - hi.astype(F32)).astype(BF16)
    return _dot(m, hi) + _dot(m, lo)


def _silu(x):
    return x * jax.nn.sigmoid(x)


def _params(*semantics):
    return pltpu.CompilerParams(dimension_semantics=semantics, vmem_limit_bytes=VMEM_LIMIT)


MOD_TN = 1536


def _mod_kernel(c_ref, w_ref, b_ref, o_ref):
    c = c_ref[...]
    s = _silu(c).astype(BF16)
    o_ref[0] = _dot(s, w_ref[0].astype(BF16)) + b_ref[0]


def _modulation(cvec, w_ada, b_ada):
    n = N_MOD * D_MODEL
    return pl.pallas_call(
        _mod_kernel,
        out_shape=jax.ShapeDtypeStruct((DEPTH, MOD_ROWS, n), F32),
        grid=(DEPTH, n // MOD_TN),
        in_specs=[pl.BlockSpec((MOD_ROWS, D_MODEL), lambda l, j: (0, 0)),
                  pl.BlockSpec((1, D_MODEL, MOD_TN), lambda l, j: (l, 0, j)),
                  pl.BlockSpec((1, 1, MOD_TN), lambda l, j: (l, 0, j))],
        out_specs=pl.BlockSpec((1, MOD_ROWS, MOD_TN), lambda l, j: (l, 0, j)),
        compiler_params=_params("parallel", "parallel"),
        name="adaln_mod",
    )(cvec, w_ada, b_ada.reshape(DEPTH, 1, n))


def _rope(x, cos, sin, quarter):
    lane = lax.broadcasted_iota(jnp.int32, (x.shape[0], LANES), 1)
    first = (lane // quarter) % 2 == 0
    outs = []
    for t in range(x.shape[1] // LANES):
        xt = x[:, t * LANES:(t + 1) * LANES]
        partner = jnp.where(first, pltpu.roll(xt, LANES - quarter, 1), pltpu.roll(xt, quarter, 1))
        outs.append(xt * cos + partner * sin)
    return outs


def _in_proj_kernel(xc_ref, xl_ref, mod_ref, g1_ref, w1_ref, bd_ref, gq_ref, gk_ref, gqa_ref, gkva_ref,
                    wqb_ref, tab_ref,
                    qa_ref, kva_ref, qb_ref, ck_ref, qc_ref, kc_ref, vc_ref, z_ref, xbc_ref, dt_ref):
    x = _pick(xc_ref, xl_ref, PROJ_TM)
    m = mod_ref[0]
    sh1, sc1 = m[0:1], m[1:2]
    ms = jnp.mean(x * x, axis=-1, keepdims=True)
    h = (x * lax.rsqrt(ms + EPS) * g1_ref[...]) * (1.0 + sc1) + sh1
    u = _dot(h.astype(BF16), w1_ref[...])

    def tab(k):
        return tab_ref[:, k * LANES:(k + 1) * LANES]

    bd = bd_ref[...]

    def head_norm(v, gain):
        w = v.shape[1]
        ss = _dot_split(v * v, bd[:w, :w])
        return v * lax.rsqrt(ss * (1.0 / HEAD_DIM) + EPS) * gain

    qa = _rope(head_norm(u[:, 0:256], gq_ref[...]), tab(0), tab(1), HEAD_DIM // 4)
    for t in range(2):
        qa_ref[:, t * LANES:(t + 1) * LANES] = qa[t].astype(BF16)
    ka = _rope(head_norm(u[:, 256:384], gk_ref[...]), tab(2), tab(3), HEAD_DIM // 4)
    kva_ref[:, 0:128] = ka[0]
    kva_ref[:, 128:256] = u[:, 384:512]

    cq = u[:, 512:768]
    msq = jnp.sum(cq * cq, axis=-1, keepdims=True) * (1.0 / Q_LORA)
    yq = cq * lax.rsqrt(msq + EPS) * gqa_ref[...]
    qb = _rope(_dot(yq.astype(BF16), wqb_ref[...]), tab(4), tab(5), MLA_ROPE // 4)
    for t in range(4):
        qb_ref[:, t * LANES:(t + 1) * LANES] = qb[t].astype(BF16)
    ckv = u[:, 768:896]
    msk = jnp.mean(ckv * ckv, axis=-1, keepdims=True)
    ck_ref[:, 0:128] = ckv * lax.rsqrt(msk + EPS) * gkva_ref[...]
    ck_ref[:, 128:256] = _rope(u[:, 896:1024], tab(6), tab(7), MLA_ROPE // 4)[0]

    qc = _rope(u[:, 1024:1280], tab(8), tab(9), DIFF_D // 4)
    kc = _rope(u[:, 1280:1536], tab(10), tab(11), DIFF_D // 4)
    for t in range(2):
        qc_ref[:, t * LANES:(t + 1) * LANES] = qc[t].astype(BF16)
        kc_ref[:, t * LANES:(t + 1) * LANES] = kc[t]
    vc_ref[...] = u[:, 1536:1792]

    z_ref[...] = u[:, 1792:2048]
    xbc_ref[...] = u[:, 2048:2560]
    dt_ref[...] = u[:, 2560:2688]


def _in_proj(x, mod3, g1, w1, bd, gq, gk, gqa, gkva, wqb, tabs):
    row = lambda w: pl.BlockSpec((PROJ_TM, w), lambda i: (i, 0))
    full = lambda a: pl.BlockSpec(a.shape, lambda i: (0,) * a.ndim)
    outs = [(256, BF16), (256, F32), (512, BF16), (256, F32), (256, BF16), (256, F32), (256, F32),
            (256, F32), (512, F32), (128, F32)]
    return pl.pallas_call(
        _in_proj_kernel,
        out_shape=[jax.ShapeDtypeStruct((N_TOK, w), d) for w, d in outs],
        grid=(N_TOK // PROJ_TM,),
        in_specs=_pair_specs(D_MODEL, PROJ_TM)
                 + [pl.BlockSpec((1, N_MOD, D_MODEL), lambda i: (_mod_row(i, PROJ_TM), 0, 0)),
                    full(g1), full(w1), full(bd), full(gq), full(gk), full(gqa), full(gkva), full(wqb),
                    pl.BlockSpec((PROJ_TM, N_TAB * LANES), lambda i: (_tab_block(i, PROJ_TM), 0))],
        out_specs=[row(w) for w, _ in outs],
        compiler_params=_params("parallel"),
        name="in_proj",
    )(*x, mod3, g1, w1, bd, gq, gk, gqa, gkva, wqb, tabs)


def _attend(q, k, v_ones):
    s = _dot_nt(q, k)
    e = jnp.exp2(s - jnp.max(s, axis=-1, keepdims=True))
    out = _dot(e.astype(BF16), v_ones)
    return out[:, :LANES] / out[:, LANES:]


def _with_ones(v):
    return jnp.concatenate([v, jnp.ones_like(v)], axis=-1)


def _half_mask(rows):
    lane = lax.broadcasted_iota(jnp.int32, (rows, LANES), 1)
    return lane < (LANES // 2)


def _keys(past_ref, new_ref, lo, hi):
    new = new_ref[:, lo:hi].astype(BF16)
    if past_ref is None:
        return new
    return jnp.concatenate([past_ref[0, :, lo:hi].astype(BF16), new], axis=0)


def _attn_a_kernel(latent, *refs):
    if latent:
        q_ref, kv_ref, past_ref, o_ref = refs
    else:
        (q_ref, kv_ref, o_ref), past_ref = refs, None
    q = q_ref[...]
    k = _keys(past_ref, kv_ref, 0, 128)
    v = _with_ones(_keys(past_ref, kv_ref, 128, 256))
    lo = _half_mask(q.shape[0])
    for g in range(2):
        qt = q[:, g * LANES:(g + 1) * LANES].astype(F32)
        res = []
        for half in range(2):
            qm = jnp.where(lo, qt, 0.0) if half == 0 else jnp.where(lo, 0.0, qt)
            res.append(_attend(qm.astype(BF16), k, v))
        o_ref[:, g * LANES:(g + 1) * LANES] = jnp.where(lo, res[0], res[1])


def _attn_b_kernel(latent, *refs):
    if latent:
        q_ref, ck_ref, past_ref, wk_ref, wv_ref, o_ref, k_s, v_s = refs
    else:
        (q_ref, ck_ref, wk_ref, wv_ref, o_ref, k_s, v_s), past_ref = refs, None

    @pl.when(pl.program_id(1) == 0)
    def _():
        ck = _keys(past_ref, ck_ref, 0, 256)
        k_s[...] = _dot(ck, wk_ref[...]).astype(BF16)
        v = _dot(ck[:, 0:128], wv_ref[...]).astype(BF16)
        for j in range(2):
            v_s[:, 2 * j * LANES:2 * (j + 1) * LANES] = _with_ones(v[:, j * LANES:(j + 1) * LANES])

    q = q_ref[...]
    lo = _half_mask(q.shape[0])
    for j in range(2):
        v = v_s[:, 2 * j * LANES:2 * (j + 1) * LANES]
        res = []
        for half in range(2):
            h = 2 * j + half
            res.append(_attend(q[:, h * LANES:(h + 1) * LANES], k_s[:, h * LANES:(h + 1) * LANES], v))
        o_ref[:, j * LANES:(j + 1) * LANES] = jnp.where(lo, res[0], res[1])


def _attn_c_kernel(lam_init, latent, *refs):
    if latent:
        q_ref, k_ref, v_ref, pk_ref, pv_ref, lq1_ref, lk1_ref, lq2_ref, lk2_ref, g_ref, o_ref = refs
    else:
        (q_ref, k_ref, v_ref, lq1_ref, lk1_ref, lq2_ref, lk2_ref, g_ref, o_ref), pk_ref, pv_ref = refs, None, None
    lam = (jnp.exp(jnp.sum(lq1_ref[...] * lk1_ref[...], axis=-1, keepdims=True))
           - jnp.exp(jnp.sum(lq2_ref[...] * lk2_ref[...], axis=-1, keepdims=True)) + lam_init)
    q = q_ref[...]
    rows = q.shape[0]
    lane = lax.broadcasted_iota(jnp.int32, (rows, LANES), 1)
    lo = lane < (LANES // 2)
    for j in range(2):
        qt = q[:, j * LANES:(j + 1) * LANES].astype(F32)
        k = _keys(pk_ref, k_ref, j * LANES, (j + 1) * LANES)
        v = _with_ones(_keys(pv_ref, v_ref, j * LANES, (j + 1) * LANES))
        res = []
        for half in range(2):
            parts = []
            for t in range(2):
                quarter = 2 * half + t
                qm = jnp.where(lane // (LANES // 4) == quarter, qt, 0.0)
                parts.append(_attend(qm.astype(BF16), k, v))
            res.append(parts[0] - lam * parts[1])
        o = jnp.where(lo, res[0], res[1])
        o2 = o * o
        ss_lo = jnp.sum(jnp.where(lo, o2, 0.0), axis=-1, keepdims=True)
        ss_hi = jnp.sum(jnp.where(lo, 0.0, o2), axis=-1, keepdims=True)
        ss = jnp.where(lo, ss_lo, ss_hi) * (1.0 / DIFF_V)
        o_ref[:, j * LANES:(j + 1) * LANES] = (o * lax.rsqrt(ss + SUBLN_EPS) * g_ref[...]) * (1.0 - lam_init)


def _seq_call(body, name, latent, q, news, pasts, consts, scratch=(), q_semantics="parallel"):
    const_specs = [pl.BlockSpec(a.shape, lambda b, i, n=a.ndim: (0,) * n) for a in consts]
    if latent:
        tq, rows = LAT_TQ, N_LAT
        tile = lambda b, i: (N_CTX // LAT_TQ + b * (DEC_SEQ // LAT_TQ) + i, 0)
        out_tile = lambda b, i: (b * (DEC_SEQ // LAT_TQ) + i, 0)
        grid = (DEC_BATCH, DEC_SEQ // LAT_TQ)
        new_specs = [pl.BlockSpec((DEC_SEQ, a.shape[1]), lambda b, i: (N_CTX // DEC_SEQ + b, 0)) for a in news]
        past_specs = [pl.BlockSpec((1, PAST_LEN, a.shape[2]), lambda b, i: (b, 0, 0)) for a in pasts]
    else:
        tq, rows = SEQ, N_CTX
        tile = out_tile = lambda b, i: (b, 0)
        grid = (BATCH, 1)
        new_specs = [pl.BlockSpec((SEQ, a.shape[1]), tile) for a in news]
        past_specs, pasts = [], ()
    return pl.pallas_call(
        functools.partial(body, latent),
        out_shape=jax.ShapeDtypeStruct((rows, 256), F32),
        grid=grid,
        in_specs=[pl.BlockSpec((tq, q.shape[1]), tile)] + new_specs + past_specs + const_specs,
        out_specs=pl.BlockSpec((tq, 256), out_tile),
        scratch_shapes=list(scratch),
        compiler_params=_params("parallel", q_semantics),
        name=name + ("_lat" if latent else "_ctx"),
    )(q, *news, *pasts, *consts)


def _attention_mixers(qa, kva, qb, ck, qc, kc, vc, past_kva, past_ck, past_kc, past_vc, wk, wv,
                      lams, gsub, lam_init):
    def both(body, name, q, news, pasts, consts, scratch_fn=None, q_semantics="parallel"):
        sc = (lambda lk: ()) if scratch_fn is None else scratch_fn
        return (_seq_call(body, name, False, q, news, (), consts, sc(SEQ), q_semantics),
                _seq_call(body, name, True, q, news, pasts, consts, sc(PAST_LEN + DEC_SEQ), q_semantics))

    oa = both(_attn_a_kernel, "attn_gqa", qa, [kva], [past_kva], [])
    ob = both(_attn_b_kernel, "attn_mla", qb, [ck], [past_ck], [wk, wv],
              lambda lk: (pltpu.VMEM((lk, 512), BF16), pltpu.VMEM((lk, 512), BF16)), "arbitrary")
    oc = both(functools.partial(_attn_c_kernel, lam_init), "attn_diff", qc, [kc, vc],
              [past_kc, past_vc], [*lams, gsub])
    return oa, ob, oc


Q = SSD_CHUNK
SSD_UNROLL = 2


def _ssd_kernel(z_ref, xbc_ref, dt_ref, cw_ref, cb_ref, dtb_ref, alog_ref, dvec_ref, ng_ref, h0f_ref, h0b_ref,
                out_ref, hf_ref, hb_ref, act_s, cum_s, dtv_s, y_s):
    n_seq = hf_ref.shape[0]
    seq = z_ref.shape[0] // n_seq
    nc = seq // Q
    row = lax.broadcasted_iota(jnp.int32, (Q, Q), 0)
    col = lax.broadcasted_iota(jnp.int32, (Q, Q), 1)
    lower = row >= col
    upper = row <= col
    tril = jnp.where(lower, 1.0, 0.0).astype(BF16)
    triu = jnp.where(upper, 1.0, 0.0).astype(BF16)
    rowc = lax.broadcasted_iota(jnp.int32, (Q, CONV_DIM), 0)
    lane = lax.broadcasted_iota(jnp.int32, (Q, LANES), 1)
    a_neg = -jnp.exp(alog_ref[...])
    cw = cw_ref[...]
    hf_ref[...] = h0f_ref[...]
    hb_ref[...] = h0b_ref[...]

    def fwd_seq(c, s):
        off = s * seq
        base = pl.multiple_of(off + c * Q, Q)
        x0 = xbc_ref[pl.ds(base, Q), :]
        prev = xbc_ref[pl.ds(pl.multiple_of(off + jnp.maximum(c * Q - 8, 0), 8), 8), :][7:8, :]
        nxt = xbc_ref[pl.ds(pl.multiple_of(off + jnp.minimum(c * Q + Q, seq - 8), 8), 8), :][0:1, :]
        prev = jnp.where(c > 0, prev, 0.0)
        nxt = jnp.where(c < nc - 1, nxt, 0.0)
        xm1 = jnp.where(rowc == 0, prev, pltpu.roll(x0, 1, 0))
        xp1 = jnp.where(rowc == Q - 1, nxt, pltpu.roll(x0, Q - 1, 0))
        act = _silu(xm1 * cw[0:1] + x0 * cw[1:2] + xp1 * cw[2:3] + cb_ref[...])
        act_s[pl.ds(base, Q), :] = act
        xs = act[:, 0:256]
        bm = act[:, 256:384]
        cm = act[:, 384:512]

        dtr = dt_ref[pl.ds(base, Q), :] + dtb_ref[...]
        dtv = jnp.maximum(dtr, 0.0) + jnp.log1p(jnp.exp(-jnp.abs(dtr)))
        dta = dtv * a_neg
        cum = jnp.where(lane < D_HEADS, _dot_split_left(tril, dta), _dot_split_left(triu, dta))
        cum_s[pl.ds(base, Q), :] = cum
        dtv_s[pl.ds(base, Q), :] = dtv
        cum_t = cum.T
        dtv_t = dtv.T
        bm_t = bm.T
        ys = []
        for h in range(D_HEADS):
            g = h // (D_HEADS // D_GROUPS)
            cg = cm[:, g * D_STATE:(g + 1) * D_STATE].astype(BF16)
            cb_mat = _dot_nt(cg, bm[:, g * D_STATE:(g + 1) * D_STATE].astype(BF16))
            cf = cum[:, h:h + 1]
            cb = cum[:, D_HEADS + h:D_HEADS + h + 1]
            l_f = jnp.exp(jnp.where(lower, cf - cum_t[h:h + 1, :], NEG_BIG))
            l_b = jnp.exp(jnp.where(upper, cb - cum_t[D_HEADS + h:D_HEADS + h + 1, :], NEG_BIG))
            mix = cb_mat * (l_f * dtv_t[h:h + 1, :] + l_b * dtv_t[D_HEADS + h:D_HEADS + h + 1, :])
            xh = xs[:, h * D_HEADDIM:(h + 1) * D_HEADDIM]
            y = _dot(mix.astype(BF16), xh.astype(BF16))
            state = hf_ref[s, h]
            y = y + _dot(cg, state.astype(BF16)) * jnp.exp(cf)
            y = y + dvec_ref[:, h * D_HEADDIM:(h + 1) * D_HEADDIM] * xh
            ys.append(y)
            last = cum[Q - 1:Q, h:h + 1]
            wgt = jnp.exp(last - cf) * dtv[:, h:h + 1]
            st = _dot(bm_t[g * D_STATE:(g + 1) * D_STATE, :].astype(BF16), (xh * wgt).astype(BF16))
            hf_ref[s, h] = state * jnp.exp(last) + st
        y_s[pl.ds(base, Q), :] = jnp.concatenate(ys, axis=-1)

    def fwd_chunk(c, carry):
        for s in range(n_seq):
            fwd_seq(c, s)
        return carry

    lax.fori_loop(0, nc, fwd_chunk, 0, unroll=SSD_UNROLL)

    def bwd_seq(c, s):
        base = pl.multiple_of(s * seq + c * Q, Q)
        act = act_s[pl.ds(base, Q), :]
        cum = cum_s[pl.ds(base, Q), :]
        dtv = dtv_s[pl.ds(base, Q), :]
        xs = act[:, 0:256]
        bm_t = act[:, 256:384].T
        cm = act[:, 384:512]
        ys = []
        for h in range(D_HEADS):
            g = h // (D_HEADS // D_GROUPS)
            cg = cm[:, g * D_STATE:(g + 1) * D_STATE].astype(BF16)
            cb = cum[:, D_HEADS + h:D_HEADS + h + 1]
            xh = xs[:, h * D_HEADDIM:(h + 1) * D_HEADDIM]
            state = hb_ref[s, h]
            ys.append(_dot(cg, state.astype(BF16)) * jnp.exp(cb))
            first = cum[0:1, D_HEADS + h:D_HEADS + h + 1]
            wgt = jnp.exp(first - cb) * dtv[:, D_HEADS + h:D_HEADS + h + 1]
            st = _dot(bm_t[g * D_STATE:(g + 1) * D_STATE, :].astype(BF16), (xh * wgt).astype(BF16))
            hb_ref[s, h] = state * jnp.exp(first) + st
        y = y_s[pl.ds(base, Q), :] + jnp.concatenate(ys, axis=-1)
        gated = y * _silu(z_ref[pl.ds(base, Q), :])
        ms = jnp.mean(gated * gated, axis=-1, keepdims=True)
        out_ref[pl.ds(base, Q), :] = gated * lax.rsqrt(ms + EPS) * ng_ref[...]

    def bwd_chunk(i, carry):
        for s in range(n_seq):
            bwd_seq(nc - 1 - i, s)
        return carry

    lax.fori_loop(0, nc, bwd_chunk, 0, unroll=SSD_UNROLL)


def _ssd(latent, z, xbc, dt, consts, h0f, h0b):
    bsz, seq, n_seq = (DEC_BATCH, DEC_SEQ, 1) if latent else (BATCH, SEQ, 2)
    rows = n_seq * seq
    first = N_CTX // rows if latent else 0
    per_seq = lambda w: pl.BlockSpec((rows, w), lambda b: (first + b, 0))
    const_specs = [pl.BlockSpec(a.shape, lambda b, n=a.ndim: (0,) * n) for a in consts]
    st_spec = pl.BlockSpec((n_seq, D_HEADS, D_STATE, D_HEADDIM), lambda b: (b, 0, 0, 0))
    st_shape = jax.ShapeDtypeStruct((bsz, D_HEADS, D_STATE, D_HEADDIM), F32)
    return pl.pallas_call(
        _ssd_kernel,
        out_shape=[jax.ShapeDtypeStruct((bsz * seq, D_INNER), F32), st_shape, st_shape],
        grid=(bsz // n_seq,),
        in_specs=[per_seq(256), per_seq(512), per_seq(128)] + const_specs + [st_spec, st_spec],
        out_specs=[pl.BlockSpec((rows, D_INNER), lambda b: (b, 0)), st_spec, st_spec],
        scratch_shapes=[pltpu.VMEM((rows, CONV_DIM), F32), pltpu.VMEM((rows, LANES), F32),
                        pltpu.VMEM((rows, LANES), F32), pltpu.VMEM((rows, D_INNER), F32)],
        compiler_params=_params("parallel"),
        name="ssd_lat" if latent else "ssd_ctx",
    )(z, xbc, dt, *consts, h0f, h0b)


def _out_proj_kernel(*refs):
    pairs, (mod_ref, wo_ref, g2_ref, rwh_ref, rwl_ref, rb_ref, x1_ref, h2_ref, te_ref, gt_ref, cnt_ref) = \
        refs[:10], refs[10:]
    x, oa, ob, oc, od = [_pick(pairs[2 * j], pairs[2 * j + 1], PROJ_TM) for j in range(5)]
    m = mod_ref[0]
    gate1, sh2, sc2 = m[2:3], m[3:4], m[4:5]
    mixed = (_dot(oa.astype(BF16), wo_ref[0:256, :])
             + _dot(ob.astype(BF16), wo_ref[256:512, :])
             + _dot(oc.astype(BF16), wo_ref[512:768, :])
             + _dot(od.astype(BF16), wo_ref[768:1024, :]))
    x1 = x + gate1 * mixed
    x1_ref[...] = x1
    ms = jnp.mean(x1 * x1, axis=-1, keepdims=True)
    h2 = (x1 * lax.rsqrt(ms + EPS) * g2_ref[...]) * (1.0 + sc2) + sh2
    h2_ref[...] = h2.astype(BF16)

    hi = h2.astype(BF16)
    lo = (h2 - hi.astype(F32)).astype(BF16)
    logits = _dot(hi, rwh_ref[...]) + _dot(lo, rwh_ref[...]) + _dot(hi, rwl_ref[...]) + rb_ref[...]
    lane = lax.broadcasted_iota(jnp.int32, logits.shape, 1)
    vals, idxs = [], []
    for _ in range(TOP_K):
        mx = jnp.max(logits, axis=-1, keepdims=True)
        ix = jnp.min(jnp.where(logits == mx, lane, LANES), axis=-1, keepdims=True)
        vals.append(mx)
        idxs.append(ix)
        logits = jnp.where(lane == ix, -3e38, logits)
    es = [jnp.exp(v - vals[0]) for v in vals]
    den = es[0] + es[1] + es[2] + es[3]
    te = jnp.zeros(lane.shape, jnp.int32)
    gt = jnp.zeros(lane.shape, F32)
    member = jnp.zeros(lane.shape, F32)
    for k in range(TOP_K):
        te = jnp.where(lane == k, idxs[k], te)
        gt = jnp.where(lane == k, es[k] / den, gt)
        member = jnp.where(lane == idxs[k], 1.0, member)
    te_ref[...] = te[:, 0:TOP_K]
    gt_ref[...] = gt[:, 0:TOP_K]
    for t in range(PROJ_TM // TM):
        count = jnp.sum(member[t * TM:(t + 1) * TM], axis=0, keepdims=True)
        cnt_ref[t] = jnp.broadcast_to(count, (8, LANES)).astype(jnp.int32)


def _out_proj(x, oa, ob, oc, od, mod3, wo, g2, rwh, rwl, rb):
    row = lambda w: pl.BlockSpec((PROJ_TM, w), lambda i: (i, 0))
    full = lambda a: pl.BlockSpec(a.shape, lambda i: (0,) * a.ndim)
    return pl.pallas_call(
        _out_proj_kernel,
        out_shape=[jax.ShapeDtypeStruct((N_TOK, D_MODEL), F32), jax.ShapeDtypeStruct((N_TOK, D_MODEL), BF16),
                   jax.ShapeDtypeStruct((N_TOK, TOP_K), jnp.int32), jax.ShapeDtypeStruct((N_TOK, TOP_K), F32),
                   jax.ShapeDtypeStruct((N_TILES, 8, LANES), jnp.int32)],
        grid=(N_TOK // PROJ_TM,),
        in_specs=_pair_specs(D_MODEL, PROJ_TM) + [s for _ in range(4) for s in _pair_specs(256, PROJ_TM)]
                 + [pl.BlockSpec((1, N_MOD, D_MODEL), lambda i: (_mod_row(i, PROJ_TM), 0, 0)),
                    full(wo), full(g2), full(rwh), full(rwl), full(rb)],
        out_specs=[row(D_MODEL), row(D_MODEL), row(TOP_K), row(TOP_K),
                   pl.BlockSpec((PROJ_TM // TM, 8, LANES), lambda i: (i, 0, 0))],
        compiler_params=_params("parallel"),
        name="out_proj_router",
    )(*x, *oa, *ob, *oc, *od, mod3, wo, g2, rwh, rwl, rb)


def _tile_rows(te, tri_ref, off_ref):
    lane = lax.broadcasted_iota(jnp.int32, (TM, LANES), 1)
    hits = [lane == te[:, k:k + 1] for k in range(TOP_K)]
    member = jnp.zeros((TM, LANES), F32)
    for hit in hits:
        member = jnp.where(hit, 1.0, member)
    rank = _dot(tri_ref[...], member.astype(BF16))
    pos = rank + off_ref[0][0:1, :].astype(F32)
    return [jnp.sum(jnp.where(hit, pos, 0.0), axis=-1, keepdims=True) for hit in hits]


def _segment_chunks(seg_ref, len_ref, tile, visit):
    def segment(e, row):
        length = len_ref[tile * N_EXPERTS + e]
        start = seg_ref[tile * N_EXPERTS + e]
        queue = e % 2

        def chunk(c, carry):
            visit(CHUNKS[0], pl.multiple_of(row + c * CHUNKS[0], SEG_ALIGN),
                  pl.multiple_of(start + c * CHUNKS[0], SEG_ALIGN), queue)
            return carry

        lax.fori_loop(0, lax.shift_right_logical(length, CHUNK_SHIFT), chunk, 0)
        for size in CHUNKS[1:]:
            done = jnp.bitwise_and(length, -2 * size)

            @pl.when(jnp.bitwise_and(length, size) != 0)
            def _():
                visit(size, pl.multiple_of(row + done, SEG_ALIGN), pl.multiple_of(start + done, SEG_ALIGN), queue)
        return row + length

    row = 0
    for e in range(N_EXPERTS):
        row = segment(e, row)


def _wait_chunks(nchunk_ref, tile, copy):
    for j, size in enumerate(CHUNKS):
        lax.fori_loop(0, nchunk_ref[tile * len(CHUNKS) + j], lambda c, carry, size=size: (copy(size).wait(), carry)[1], 0)


def _dispatch_kernel(seg_ref, len_ref, nchunk_ref, tail_ref, tlen_ref, nu_ref, h2_ref, te_ref, off_ref, tri_ref,
                     xb_ref, rows_ref, buf, zeros, sems, sem_z):
    i = pl.program_id(0)
    slot = i % 2

    def chunk_copy(size, s, buf_row, xb_row):
        return pltpu.make_async_copy(buf.at[s, pl.ds(buf_row, size)], xb_ref.at[pl.ds(xb_row, size)], sems.at[s])

    def wait_chunks(tile, s):
        _wait_chunks(nchunk_ref, tile, lambda size: chunk_copy(size, s, 0, 0))

    def zero_fills(visit):
        for e in range(N_EXPERTS):
            start, length = tail_ref[e], tlen_ref[e]
            for size in ZERO_CHUNKS:
                done = jnp.bitwise_and(length, -2 * size)

                @pl.when(jnp.bitwise_and(length, size) != 0)
                def _():
                    visit(pltpu.make_async_copy(
                        zeros.at[pl.ds(0, size)],
                        xb_ref.at[pl.ds(pl.multiple_of(start + done, SEG_ALIGN), size)], sem_z))
        block = lambda b: pltpu.make_async_copy(
            zeros, xb_ref.at[pl.ds(pl.multiple_of(b * MOE_BM, MOE_BM), MOE_BM)], sem_z)
        lax.fori_loop(nu_ref[0], MOE_BLOCKS, lambda b, carry: (visit(block(b)), carry)[1], 0)

    @pl.when(i == 0)
    def _():
        zeros[...] = jnp.zeros_like(zeros)
        zero_fills(lambda cp: cp.start())

    rows = _tile_rows(te_ref[...], tri_ref, off_ref)
    lane = lax.broadcasted_iota(jnp.int32, (TM, LANES), 1)
    packed = jnp.zeros((TM, LANES), F32)
    for k in range(TOP_K):
        packed = jnp.where(lane == k, rows[k], packed)
    rows_ref[...] = packed[:, 0:TOP_K]
    rows_t = packed.T
    buf_row = lax.broadcasted_iota(jnp.int32, (TILE_BUF, TM), 0).astype(F32)
    pick = jnp.zeros((TILE_BUF, TM), F32)
    for k in range(TOP_K):
        pick = jnp.where(buf_row == rows_t[k:k + 1, :], 1.0, pick)
    buf[slot] = _dot(pick.astype(BF16), h2_ref[...])

    @pl.when(i > 0)
    def _():
        wait_chunks(i - 1, 1 - slot)

    _segment_chunks(seg_ref, len_ref, i, lambda size, b, x, q: chunk_copy(size, slot, b, x).start(priority=q))

    @pl.when(i == N_TILES - 1)
    def _():
        wait_chunks(i, slot)
        zero_fills(lambda cp: cp.wait())


def _dispatch(seg_start, seg_len, n_chunk, tail_start, tail_len, n_used, h2, te, seg_off, tri):
    row = lambda w: pl.BlockSpec((TM, w), lambda i, *_: (i, 0))
    grid_spec = pltpu.PrefetchScalarGridSpec(
        num_scalar_prefetch=6,
        grid=(N_TILES,),
        in_specs=[row(D_MODEL), row(TOP_K), pl.BlockSpec((1, 8, LANES), lambda i, *_: (i, 0, 0)),
                  pl.BlockSpec(tri.shape, lambda i, *_: (0, 0))],
        out_specs=[pl.BlockSpec(memory_space=pl.ANY), row(TOP_K)],
        scratch_shapes=[pltpu.VMEM((2, TILE_BUF, D_MODEL), F32), pltpu.VMEM((MOE_BM, D_MODEL), F32),
                        pltpu.SemaphoreType.DMA((2,)), pltpu.SemaphoreType.DMA],
    )
    return pl.pallas_call(
        _dispatch_kernel,
        out_shape=[jax.ShapeDtypeStruct((YB_ROWS, D_MODEL), F32), jax.ShapeDtypeStruct((N_TOK, TOP_K), F32)],
        grid_spec=grid_spec,
        compiler_params=_params("arbitrary"),
        name="moe_dispatch",
    )(seg_start, seg_len, n_chunk, tail_start, tail_len, n_used, h2, te, seg_off, tri)


def _expert_kernel(layer, be_ref, nu_ref, slot_ref, next_ref, x_ref, wgu_hbm, bgu_ref, wdn_hbm, bdn_ref,
                   o_ref, wgu_f, wdn_f, wgu_s, wdn_s, sems):
    i = pl.program_id(0)
    used = i < nu_ref[0]
    expert = be_ref[i]
    fresh = jnp.logical_or(i == 0, expert != be_ref[jnp.maximum(i - 1, 0)])
    slot = slot_ref[expert]

    def fetch(e, s):
        return (pltpu.make_async_copy(wgu_hbm.at[layer, e], wgu_f.at[s], sems.at[0, s]),
                pltpu.make_async_copy(wdn_hbm.at[layer, e], wdn_f.at[s], sems.at[1, s]))

    @pl.when(i == 0)
    def _():
        for cp in fetch(expert, slot):
            cp.start()

    @pl.when(jnp.logical_and(used, fresh))
    def _():
        for cp in fetch(expert, slot):
            cp.wait()
        nxt = next_ref[expert]

        @pl.when(nxt >= 0)
        def _():
            for cp in fetch(nxt, 1 - slot):
                cp.start()

        wgu_s[...] = wgu_f[slot].astype(BF16)
        wdn_s[...] = wdn_f[slot].astype(BF16)

    @pl.when(used)
    def _():
        hgu = _dot(x_ref[...].astype(BF16), wgu_s[...]) + bgu_ref[0, 0]
        gate = jnp.minimum(hgu[:, :D_FF], SWIGLU_LIMIT)
        up = jnp.clip(hgu[:, D_FF:], -SWIGLU_LIMIT, SWIGLU_LIMIT)
        act = (up + 1.0) * gate * jax.nn.sigmoid(SWIGLU_ALPHA * gate)
        o_ref[...] = _dot(act.astype(BF16), wdn_s[...]) + bdn_ref[0, 0]

    @pl.when(jnp.logical_not(used))
    def _():
        o_ref[...] = jnp.zeros_like(o_ref)


def _experts(layer, block_e, n_used, w_slot, next_e, xb, w_gu, b_gu, w_dn, b_dn):
    grid_spec = pltpu.PrefetchScalarGridSpec(
        num_scalar_prefetch=4,
        grid=(MOE_BLOCKS,),
        in_specs=[pl.BlockSpec((MOE_BM, D_MODEL), lambda i, be, *_: (i, 0)),
                  pl.BlockSpec(memory_space=pl.ANY),
                  pl.BlockSpec((1, 1, 1, 2 * D_FF), lambda i, be, *_: (layer, be[i], 0, 0)),
                  pl.BlockSpec(memory_space=pl.ANY),
                  pl.BlockSpec((1, 1, 1, D_MODEL), lambda i, be, *_: (layer, be[i], 0, 0))],
        out_specs=pl.BlockSpec((MOE_BM, D_MODEL), lambda i, be, *_: (i, 0)),
        scratch_shapes=[pltpu.VMEM((2, D_MODEL, 2 * D_FF), F32), pltpu.VMEM((2, D_FF, D_MODEL), F32),
                        pltpu.VMEM((D_MODEL, 2 * D_FF), BF16), pltpu.VMEM((D_FF, D_MODEL), BF16),
                        pltpu.SemaphoreType.DMA((2, 2))],
    )
    return pl.pallas_call(
        functools.partial(_expert_kernel, layer),
        out_shape=jax.ShapeDtypeStruct((YB_ROWS, D_MODEL), F32),
        grid_spec=grid_spec,
        compiler_params=_params("arbitrary"),
        name="experts",
    )(block_e, n_used, w_slot, next_e, xb, w_gu, b_gu.reshape(DEPTH, N_EXPERTS, 1, 2 * D_FF), w_dn,
      b_dn.reshape(DEPTH, N_EXPERTS, 1, D_MODEL))


def _combine_kernel(final, seg_ref, len_ref, nchunk_ref, x1_ref, rows_ref, gt_ref, mod_ref, fg_ref,
                    yb_ref, o_ctx_ref, o_lat_ref, buf, sems):
    i = pl.program_id(0)
    slot = i % 2

    def chunk_copy(size, s, buf_row, yb_row):
        return pltpu.make_async_copy(yb_ref.at[pl.ds(yb_row, size)], buf.at[s, pl.ds(buf_row, size)], sems.at[s])

    def request(tile, s):
        _segment_chunks(seg_ref, len_ref, tile, lambda size, b, y, q: chunk_copy(size, s, b, y).start(priority=q))

    @pl.when(i == 0)
    def _():
        buf[...] = jnp.zeros_like(buf)
        request(0, 0)

    @pl.when(i + 1 < N_TILES)
    def _():
        request(i + 1, 1 - slot)

    rows = rows_ref[...]
    gt = gt_ref[...]
    buf_row = lax.broadcasted_iota(jnp.int32, (TM, TILE_BUF), 1).astype(F32)
    place = jnp.zeros((TM, TILE_BUF), F32)
    for k in range(TOP_K):
        place = jnp.where(buf_row == rows[:, k:k + 1], gt[:, k:k + 1], place)

    _wait_chunks(nchunk_ref, i, lambda size: chunk_copy(size, slot, 0, 0))
    y = _dot(place.astype(BF16), buf[slot].astype(BF16))
    x2 = x1_ref[...] + mod_ref[0][5:6] * y
    if final:
        ms = jnp.mean(x2 * x2, axis=-1, keepdims=True)
        x2 = x2 * lax.rsqrt(ms + EPS) * fg_ref[...]

    @pl.when(i < CTX_TILES)
    def _():
        o_ctx_ref[...] = x2

    @pl.when(i >= CTX_TILES)
    def _():
        o_lat_ref[...] = x2


def _combine(seg_start, seg_len, n_chunk, x1, rows, gates, mod3, fg, yb, final):
    row = lambda w: pl.BlockSpec((TM, w), lambda i, *_: (i, 0))
    full = lambda a: pl.BlockSpec(a.shape, lambda i, *_: (0,) * a.ndim)
    out_shape = [jax.ShapeDtypeStruct((N_CTX, D_MODEL), F32), jax.ShapeDtypeStruct((N_LAT, D_MODEL), F32)]
    out_specs = _pair_specs(D_MODEL, TM)
    grid_spec = pltpu.PrefetchScalarGridSpec(
        num_scalar_prefetch=3,
        grid=(N_TILES,),
        in_specs=[row(D_MODEL), row(TOP_K), row(TOP_K),
                  pl.BlockSpec((1, N_MOD, D_MODEL), lambda i, *_: (_mod_row(i), 0, 0)),
                  full(fg),
                  pl.BlockSpec(memory_space=pl.ANY)],
        out_specs=out_specs,
        scratch_shapes=[pltpu.VMEM((2, TILE_BUF, D_MODEL), F32), pltpu.SemaphoreType.DMA((2,))],
    )
    return pl.pallas_call(
        functools.partial(_combine_kernel, final),
        out_shape=out_shape,
        grid_spec=grid_spec,
        compiler_params=_params("arbitrary"),
        name="moe_combine",
    )(seg_start, seg_len, n_chunk, x1, rows, gates, mod3, fg, yb)


def _gqa_head_order(w, axis):
    take = lambda start: lax.slice_in_dim(w, start, start + HEAD_DIM, axis=axis)
    return [take(kv * 128 + g * 64) for g in range(2) for kv in range(2)]


def _w1_layout(w_in):
    cols = lambda name, n: w_in[:, IN_OFF[name]:IN_OFF[name] + n]
    zeros = lambda n: jnp.zeros((D_MODEL, n), F32)
    pieces = (_gqa_head_order(w_in, 1) + [cols('a_k', 256)]
              + [cols('b_cq', 192), zeros(64), cols('b_ckv', 128), cols('b_kr', 32), zeros(96)]
              + [cols('c_q', 768), cols('d_z', 256), cols('d_xbc', 512), cols('d_dtf', 8), zeros(120)])
    return jnp.concatenate(pieces, axis=1).astype(BF16)


def _rope_tables():
    t = np.arange(DEC_SEQ)
    pos = ((t // GRID_W).astype(np.float32), (t % GRID_W).astype(np.float32))

    def unit(rot_dim):
        quarter, half = rot_dim // 4, rot_dim // 2
        inv = ROPE_THETA ** (-np.arange(0, half, 2, dtype=np.float32) / half)
        cos = np.zeros((DEC_SEQ, rot_dim), np.float32)
        sin = np.zeros((DEC_SEQ, rot_dim), np.float32)
        for seg in range(4):
            ang = pos[seg // 2][:, None] * inv[None, :].astype(np.float32)
            cos[:, seg * quarter:(seg + 1) * quarter] = np.cos(ang)
            sin[:, seg * quarter:(seg + 1) * quarter] = np.sin(ang) * (-1.0 if seg % 2 == 0 else 1.0)
        return cos, sin

    log2e = math.log2(math.e)
    specs = [
        (HEAD_DIM, (0, 64), (), HEAD_DIM ** -0.5 * log2e),
        (HEAD_DIM, (0, 64), (), 1.0),
        (MLA_ROPE, (MLA_NOPE,), (0, MLA_NOPE), (MLA_NOPE + MLA_ROPE) ** -0.5 * log2e),
        (MLA_ROPE, (0,), (), 1.0),
        (DIFF_D, (0, 32, 64, 96), (), DIFF_D ** -0.5 * log2e),
        (DIFF_D, (0, 32, 64, 96), (), 1.0),
    ]
    lat_cols, ident_cols = [], []
    for rot_dim, starts, passthrough, scale in specs:
        ucos, usin = unit(rot_dim)
        cos = np.zeros((DEC_SEQ, LANES), np.float32)
        sin = np.zeros((DEC_SEQ, LANES), np.float32)
        ident = np.zeros((1, LANES), np.float32)
        if passthrough:
            cos[:, passthrough[0]:passthrough[1]] = 1.0
            ident[:, passthrough[0]:passthrough[1]] = 1.0
        for s in starts:
            cos[:, s:s + rot_dim] = ucos
            sin[:, s:s + rot_dim] = usin
            ident[:, s:s + rot_dim] = 1.0
        lat_cols += [cos * scale, sin * scale]
        ident_cols += [ident * scale, np.zeros((1, LANES), np.float32)]
    lat = np.concatenate(lat_cols, axis=1)
    ident_blk = np.broadcast_to(np.concatenate(ident_cols, axis=1), (PROJ_TM, N_TAB * LANES))
    return np.concatenate([ident_blk, lat], axis=0).astype(np.float32)


def _block_diag_ones(n, blk):
    r = np.arange(n)
    return (r[:, None] // blk == r[None, :] // blk).astype(np.float32)


def _chunk_tables(seg_len):
    counts = [jnp.sum(seg_len // CHUNKS[0], axis=1)]
    counts += [jnp.sum(seg_len % (2 * size) // size, axis=1) for size in CHUNKS[1:]]
    return jnp.stack(counts, axis=1).reshape(-1).astype(jnp.int32)


def _strict_lower_ones(n):
    r = np.arange(n)
    return (r[None, :] < r[:, None]).astype(np.float32)


def kernel(x_prompt, x_sample, cache_gqa_k, cache_gqa_v, cache_mla_ckv, cache_mla_krope, cache_diff_k, cache_diff_v, state_ssd_fwd, state_ssd_bwd, c, c_ctx, norm1_g, norm2_g, w_ada, b_ada, w_in, w_out, gqa_qn_g, gqa_kn_g, mla_qa_g, mla_wqb, mla_kva_g, mla_wkvb, diff_lq1, diff_lk1, diff_lq2, diff_lk2, diff_subln_g, ssd_conv_w, ssd_conv_b, ssd_a_log_f, ssd_a_log_b, ssd_dt_bias_f, ssd_dt_bias_b, ssd_d, ssd_norm_g, router_w, router_b, moe_w_gu, moe_b_gu, moe_w_dn, moe_b_dn, final_g):
    tabs = jnp.asarray(_rope_tables())
    bd = jnp.asarray(_block_diag_ones(256, HEAD_DIM), BF16)
    tri = jnp.asarray(_strict_lower_ones(TM), BF16)

    cvec = jnp.zeros((MOD_ROWS, D_MODEL), F32).at[0].set(c_ctx).at[1:1 + DEC_BATCH].set(c)
    mod = _modulation(cvec, w_ada, b_ada).reshape(DEPTH, MOD_ROWS, N_MOD, D_MODEL)

    x = (x_prompt.reshape(N_CTX, D_MODEL), x_sample.reshape(N_LAT, D_MODEL))
    new_ctx = []
    for l in range(DEPTH):
        mod3 = mod[l]
        w1 = _w1_layout(w_in[l])
        wqb = mla_wqb[l].reshape(Q_LORA, B_HEADS, MLA_NOPE + MLA_ROPE)
        wqb = jnp.pad(wqb, ((0, 256 - Q_LORA), (0, 0), (0, LANES - MLA_NOPE - MLA_ROPE)))
        wqb = wqb.reshape(256, B_HEADS * LANES).astype(BF16)
        wkvb = mla_wkvb[l].reshape(KV_LORA, B_HEADS, MLA_NOPE + MLA_V)
        wk_nope = jnp.pad(wkvb[:, :, :MLA_NOPE], ((0, 0), (0, 0), (0, LANES - MLA_NOPE)))
        eye_r = jnp.zeros((LANES, B_HEADS, LANES), F32)
        eye_r = eye_r.at[jnp.arange(MLA_ROPE), :, MLA_NOPE + jnp.arange(MLA_ROPE)].set(1.0)
        wk = jnp.concatenate([wk_nope, eye_r], axis=0).reshape(256, B_HEADS * LANES).astype(BF16)
        wv = wkvb[:, :, MLA_NOPE:].reshape(KV_LORA, B_HEADS * MLA_V).astype(BF16)
        wo = jnp.concatenate(_gqa_head_order(w_out[l], 0) + [w_out[l][256:]], axis=0).astype(BF16)
        gq = jnp.tile(gqa_qn_g[l], 4)[None, :]
        gk = jnp.tile(gqa_kn_g[l], 2)[None, :]
        gqa = jnp.pad(mla_qa_g[l], (0, 256 - Q_LORA))[None, :]
        gkva = mla_kva_g[l][None, :]
        rw = jnp.pad(router_w[l], ((0, 0), (0, LANES - N_EXPERTS)))
        rwh = rw.astype(BF16)
        rwl = (rw - rwh.astype(F32)).astype(BF16)
        rb = jnp.pad(router_b[l], (0, LANES - N_EXPERTS), constant_values=NEG_BIG)[None, :]

        qa, kva, qb, ck, qc, kc, vc, z, xbc, dt = _in_proj(
            x, mod3, norm1_g[l][None, :], w1, bd, gq, gk, gqa, gkva, wqb, tabs)

        past_kva = jnp.concatenate([cache_gqa_k[:, l].reshape(DEC_BATCH, PAST_LEN, 128),
                                    cache_gqa_v[:, l].reshape(DEC_BATCH, PAST_LEN, 128)], axis=-1)
        past_ck = jnp.concatenate([cache_mla_ckv[:, l], cache_mla_krope[:, l],
                                   jnp.zeros((DEC_BATCH, PAST_LEN, LANES - MLA_ROPE), F32)], axis=-1)
        past_kc = cache_diff_k[:, l].reshape(DEC_BATCH, PAST_LEN, 256)
        past_vc = cache_diff_v[:, l].reshape(DEC_BATCH, PAST_LEN, 256)
        lams = [a[l][None, :] for a in (diff_lq1, diff_lk1, diff_lq2, diff_lk2)]
        gsub = jnp.tile(diff_subln_g[l], 2)[None, :]
        lam_init = 0.8 - 0.6 * math.exp(-0.3 * l)
        oa, ob, oc = _attention_mixers(qa, kva, qb, ck, qc, kc, vc, past_kva, past_ck, past_kc, past_vc,
                                       wk, wv, lams, gsub, lam_init)

        pad8 = lambda f, b: jnp.pad(jnp.concatenate([f, b]), (0, LANES - 2 * D_HEADS))[None, :]
        ssd_consts = (ssd_conv_w[l], ssd_conv_b[l][None, :], pad8(ssd_dt_bias_f[l], ssd_dt_bias_b[l]),
                      pad8(ssd_a_log_f[l], ssd_a_log_b[l]), jnp.repeat(ssd_d[l], D_HEADDIM)[None, :],
                      ssd_norm_g[l][None, :])
        zeros_st = jnp.zeros((BATCH, D_HEADS, D_STATE, D_HEADDIM), F32)
        od_c, hf_c, hb_c = _ssd(False, z, xbc, dt, ssd_consts, zeros_st, zeros_st)
        od_l, _, _ = _ssd(True, z, xbc, dt, ssd_consts, jnp.swapaxes(state_ssd_fwd[:, l], -1, -2),
                          jnp.swapaxes(state_ssd_bwd[:, l], -1, -2))
        od = (od_c, od_l)

        x1, h2, te, gates, tile_cnt = _out_proj(x, oa, ob, oc, od, mod3, wo, norm2_g[l][None, :], rwh, rwl, rb)

        seg_cnt = tile_cnt[:, 0, :N_EXPERTS]
        seg_len = (seg_cnt + SEG_ALIGN - 1) // SEG_ALIGN * SEG_ALIGN
        region = (jnp.sum(seg_len, axis=0) + MOE_BM - 1) // MOE_BM * MOE_BM
        pad_end = jnp.cumsum(region).astype(jnp.int32)
        pad_start = pad_end - region
        seg_start = (pad_start[None, :] + jnp.cumsum(seg_len, axis=0) - seg_len).astype(jnp.int32)
        n_used = (pad_end[-1] // MOE_BM).astype(jnp.int32).reshape(1)
        blk_start = jnp.arange(MOE_BLOCKS, dtype=jnp.int32) * MOE_BM
        block_e = jnp.minimum(jnp.sum((pad_end[None, :] <= blk_start[:, None]).astype(jnp.int32), axis=1),
                              N_EXPERTS - 1).astype(jnp.int32)
        n_chunk = _chunk_tables(seg_len)
        seg_off = jnp.cumsum(seg_len, axis=1) - seg_len
        seg_off = jnp.broadcast_to(jnp.pad(seg_off, ((0, 0), (0, LANES - N_EXPERTS)))[:, None, :],
                                   (N_TILES, 8, LANES)).astype(jnp.int32)
        seg_start, seg_len = seg_start.reshape(-1), seg_len.reshape(-1).astype(jnp.int32)

        tail_start = (pad_end - region + jnp.sum(seg_len.reshape(N_TILES, N_EXPERTS), axis=0)).astype(jnp.int32)
        tail_len = (pad_end - tail_start).astype(jnp.int32)
        xb, pair_rows = _dispatch(seg_start, seg_len, n_chunk, tail_start, tail_len, n_used, h2, te, seg_off, tri)
        has_rows = region > 0
        w_slot = ((jnp.cumsum(has_rows.astype(jnp.int32)) - 1) % 2).astype(jnp.int32)
        later = lax.cummin(jnp.where(has_rows, jnp.arange(N_EXPERTS, dtype=jnp.int32), N_EXPERTS), reverse=True)
        later = jnp.concatenate([later[1:], jnp.full((1,), N_EXPERTS, jnp.int32)])
        next_e = jnp.where(later < N_EXPERTS, later, -1).astype(jnp.int32)
        yb = _experts(l, block_e, n_used, w_slot, next_e, xb, moe_w_gu, moe_b_gu, moe_w_dn, moe_b_dn)
        x = _combine(seg_start, seg_len, n_chunk, x1, pair_rows, gates, mod3, final_g[None, :], yb,
                     l == DEPTH - 1)

        kva_c, ck_c = kva[:N_CTX], ck[:N_CTX]
        new_ctx.append((kva_c[:, 0:128].reshape(BATCH, SEQ, A_KV_HEADS, HEAD_DIM),
                        kva_c[:, 128:256].reshape(BATCH, SEQ, A_KV_HEADS, HEAD_DIM),
                        ck_c[:, 0:KV_LORA].reshape(BATCH, SEQ, KV_LORA),
                        ck_c[:, KV_LORA:KV_LORA + MLA_ROPE].reshape(BATCH, SEQ, MLA_ROPE),
                        kc[:N_CTX].reshape(BATCH, SEQ, C_HEADS, 2 * DIFF_D),
                        vc[:N_CTX].reshape(BATCH, SEQ, C_HEADS, DIFF_V),
                        jnp.swapaxes(hf_c, -1, -2), jnp.swapaxes(hb_c, -1, -2)))

    y_prompt = x[0].reshape(BATCH, SEQ, D_MODEL)
    y_sample = x[1].reshape(DEC_BATCH, DEC_SEQ, D_MODEL)
    caches = [jnp.stack([cl[i] for cl in new_ctx], axis=1) for i in range(8)]
    return (y_prompt, y_sample, *caches)
```
